```python
import math
import jax, jax.numpy as jnp
from jax import lax
import numpy as np

D_MODEL = 1024
BATCH = 16
SEQ = 2048
DEPTH = 1

GRID_W = 64
N_FOURIER_GROUPS = 8
FOURIER_GROUP_DIM = 64
FOURIER_WIDTH = N_FOURIER_GROUPS * FOURIER_GROUP_DIM
N_Q_HEADS = 16
N_KV_HEADS = 4
HEAD_DIM = 64
Q_GROUP = N_Q_HEADS // N_KV_HEADS
ATTN_WIDTH = N_Q_HEADS * HEAD_DIM
KV_WIDTH = N_KV_HEADS * HEAD_DIM
Q_BLOCK = 128
ROPE_THETA = 10000.0
QK_EPS = 1e-6
N_BRANCHES = 2
OFF_Q = FOURIER_WIDTH
OFF_K = OFF_Q + ATTN_WIDTH
OFF_V = OFF_K + KV_WIDTH
OFF_G = OFF_V + KV_WIDTH
IN_WIDTH = OFF_G + N_BRANCHES * D_MODEL
N_EXPERTS = 256
TOP_K = 8
N_EXPERT_GROUPS = 8
TOPK_GROUPS = 4
EXPERT_DIM = 256
SHARED_DIM = 256
ROUTED_SCALE = 2.5
DISPATCH_BLOCK = 128
LN_EPS = 1e-5
DEEPNORM_ALPHA = (2 * DEPTH) ** 0.25
DEEPNORM_BETA = (8 * DEPTH) ** -0.25

kernel_name = "hybrid_fourier_axial_gqa_moe_deepnorm"


def layer_norm(x, g, b):
    xf = x.astype(jnp.float32)
    mu = jnp.mean(xf, axis=-1, keepdims=True)
    var = jnp.mean(jnp.square(xf - mu), axis=-1, keepdims=True)
    y = (xf - mu) * lax.rsqrt(var + LN_EPS)
    return (y * g.astype(jnp.float32) + b.astype(jnp.float32)).astype(x.dtype)


def rms_norm(x, g):
    xf = x.astype(jnp.float32)
    y = xf * lax.rsqrt(jnp.mean(jnp.square(xf), axis=-1, keepdims=True) + QK_EPS)
    return (y * g.astype(jnp.float32)).astype(x.dtype)


def rope_1d(x, pos):
    half = x.shape[-1] // 2
    freqs = ROPE_THETA ** (-jnp.arange(half, dtype=jnp.float32) / half)
    ang = pos.astype(jnp.float32)[:, None] * freqs
    cos = jnp.cos(ang)[:, None, :].astype(x.dtype)
    sin = jnp.sin(ang)[:, None, :].astype(x.dtype)
    x1, x2 = x[..., :half], x[..., half:]
    return jnp.concatenate([x1 * cos - x2 * sin, x2 * cos + x1 * sin], axis=-1)


def rope_axial(x, row, col):
    h = x.shape[-1] // 2
    return jnp.concatenate([rope_1d(x[..., :h], row), rope_1d(x[..., h:], col)], axis=-1)


def fourier_branch(u):
    B, S, _ = u.shape
    ug = u.reshape(B, S, N_FOURIER_GROUPS, FOURIER_GROUP_DIM).astype(jnp.float32)
    f = jnp.fft.fftn(ug, axes=(1, 3), norm="ortho").real
    return f.reshape(B, S, FOURIER_WIDTH).astype(u.dtype)


def attention_branch(q, k, v, q_g, k_g, row, col):
    B, S, _ = q.shape
    q = rope_axial(rms_norm(q.reshape(B, S, N_Q_HEADS, HEAD_DIM), q_g), row, col)
    k = rope_axial(rms_norm(k.reshape(B, S, N_KV_HEADS, HEAD_DIM), k_g), row, col)
    v = v.reshape(B, S, N_KV_HEADS, HEAD_DIM)
    n_blk = S // Q_BLOCK
    qb = q.reshape(B, n_blk, Q_BLOCK, N_KV_HEADS, Q_GROUP, HEAD_DIM).transpose(1, 0, 2, 3, 4, 5)
    scale = HEAD_DIM ** -0.5

    def attend(q_blk):
        s = jnp.einsum("bqhgd,bkhd->bhgqk", q_blk, k, preferred_element_type=jnp.float32) * scale
        p = jax.nn.softmax(s, axis=-1)
        return jnp.einsum("bhgqk,bkhd->bqhgd", p.astype(v.dtype), v)

    o = lax.map(attend, qb)
    return o.transpose(1, 0, 2, 3, 4, 5).reshape(B, S, ATTN_WIDTH)


def hybrid_mixer(x, w_in, b_gate, q_norm_g, k_norm_g, w_four_proj, w_attn_proj, w_o):
    B, S, _ = x.shape
    rows = S // GRID_W
    row = jnp.repeat(jnp.arange(rows, dtype=jnp.int32), GRID_W)
    col = jnp.tile(jnp.arange(GRID_W, dtype=jnp.int32), rows)
    z = x @ w_in
    y_four = fourier_branch(z[..., :OFF_Q]) @ w_four_proj
    y_attn = attention_branch(z[..., OFF_Q:OFF_K], z[..., OFF_K:OFF_V], z[..., OFF_V:OFF_G],
                              q_norm_g, k_norm_g, row, col) @ w_attn_proj
    gates = jax.nn.sigmoid(z[..., OFF_G:] + b_gate)
    merged = gates[..., :D_MODEL] * y_four + gates[..., D_MODEL:] * y_attn
    return merged @ w_o


def swiglu(x, w_in, w_down):
    g, u = jnp.split(x @ w_in, 2, axis=-1)
    return (jax.nn.silu(g) * u) @ w_down


def route(xt, w_router, e_bias):
    N = xt.shape[0]
    scores = jax.nn.sigmoid(jnp.dot(xt, w_router, preferred_element_type=jnp.float32))
    biased = scores + e_bias.astype(jnp.float32)
    grp = biased.reshape(N, N_EXPERT_GROUPS, N_EXPERTS // N_EXPERT_GROUPS)
    grp_score = lax.top_k(grp, 2)[0].sum(-1)
    _, gidx = lax.top_k(grp_score, TOPK_GROUPS)
    gmask = jax.nn.one_hot(gidx, N_EXPERT_GROUPS, dtype=jnp.float32).sum(-2) > 0
    emask = jnp.repeat(gmask, N_EXPERTS // N_EXPERT_GROUPS, axis=-1)
    _, eidx = lax.top_k(jnp.where(emask, biased, -jnp.inf), TOP_K)
    w = jnp.take_along_axis(scores, eidx, axis=-1)
    w = w / jnp.sum(w, axis=-1, keepdims=True) * ROUTED_SCALE
    return eidx.astype(jnp.int32), w


def routed_experts(xt, eidx, ew, w_e_in, w_e_down):
    N, D = xt.shape
    A = N * TOP_K
    flat_e = eidx.reshape(-1)
    flat_tok = jnp.arange(A, dtype=jnp.int32) // TOP_K
    flat_w = ew.reshape(-1).astype(xt.dtype)
    order = jnp.argsort(flat_e)
    se = flat_e[order]
    counts = jnp.bincount(flat_e, length=N_EXPERTS)
    padded = (counts + DISPATCH_BLOCK - 1) // DISPATCH_BLOCK * DISPATCH_BLOCK
    pad_end = jnp.cumsum(padded)
    pad_start = pad_end - padded
    start = jnp.cumsum(counts) - counts
    dest = pad_start[se] + jnp.arange(A, dtype=jnp.int32) - start[se]
    n_blocks = (A + N_EXPERTS * (DISPATCH_BLOCK - 1) + DISPATCH_BLOCK - 1) // DISPATCH_BLOCK
    P = n_blocks * DISPATCH_BLOCK
    row_tok = jnp.full((P,), N, dtype=jnp.int32).at[dest].set(flat_tok[order])
    row_w = jnp.zeros((P,), xt.dtype).at[dest].set(flat_w[order])
    blk_start = jnp.arange(n_blocks, dtype=jnp.int32) * DISPATCH_BLOCK
    blk_e = jnp.minimum(jnp.searchsorted(pad_end, blk_start, side="right"), N_EXPERTS - 1)
    x_pad = jnp.concatenate([xt, jnp.zeros((1, D), xt.dtype)], axis=0)

    def run_block(args):
        tok, wb, e = args
        return swiglu(x_pad[tok], w_e_in[e], w_e_down[e]) * wb[:, None]

    out = lax.map(run_block, (row_tok.reshape(n_blocks, DISPATCH_BLOCK),
                              row_w.reshape(n_blocks, DISPATCH_BLOCK), blk_e))
    y = jnp.zeros((N + 1, D), xt.dtype).at[row_tok].add(out.reshape(P, D))
    return y[:N]


def moe_ffn(x, w_router, e_bias, w_e_in, w_e_down, w_sh_in, w_sh_down):
    B, S, D = x.shape
    xt = x.reshape(B * S, D)
    eidx, ew = route(xt, w_router, e_bias)
    y = routed_experts(xt, eidx, ew, w_e_in, w_e_down) + swiglu(xt, w_sh_in, w_sh_down)
    return y.reshape(B, S, D)


def setup_inputs(seed: int = 0) -> dict:
    key = jax.random.key(seed)
    ks = jax.random.split(key, 20)
    L, D = DEPTH, D_MODEL
    f32 = jnp.float32

    def nrm(k, shape, scale):
        return jax.random.normal(k, shape, f32) * scale

    return {
        "x": jax.random.normal(ks[0], (BATCH, SEQ, D), f32),
        "w_in": nrm(ks[1], (L, D, IN_WIDTH), D ** -0.5),
        "b_gate": nrm(ks[2], (L, N_BRANCHES * D), 0.02),
        "q_norm_g": 1.0 + nrm(ks[3], (L, HEAD_DIM), 0.02),
        "k_norm_g": 1.0 + nrm(ks[4], (L, HEAD_DIM), 0.02),
        "w_four_proj": nrm(ks[5], (L, FOURIER_WIDTH, D), FOURIER_WIDTH ** -0.5),
        "w_attn_proj": nrm(ks[6], (L, ATTN_WIDTH, D), ATTN_WIDTH ** -0.5),
        "w_o": nrm(ks[7], (L, D, D), DEEPNORM_BETA * D ** -0.5),
        "ln1_g": 1.0 + nrm(ks[8], (L, D), 0.02),
        "ln1_b": nrm(ks[9], (L, D), 0.02),
        "w_router": nrm(ks[10], (L, D, N_EXPERTS), D ** -0.5),
        "e_bias": nrm(ks[11], (L, N_EXPERTS), 0.01),
        "w_e_in": nrm(ks[12], (L, N_EXPERTS, D, 2 * EXPERT_DIM), D ** -0.5),
        "w_e_down": nrm(ks[13], (L, N_EXPERTS, EXPERT_DIM, D), DEEPNORM_BETA * EXPERT_DIM ** -0.5),
        "w_sh_in": nrm(ks[14], (L, D, 2 * SHARED_DIM), D ** -0.5),
        "w_sh_down": nrm(ks[15], (L, SHARED_DIM, D), DEEPNORM_BETA * SHARED_DIM ** -0.5),
        "ln2_g": 1.0 + nrm(ks[16], (L, D), 0.02),
        "ln2_b": nrm(ks[17], (L, D), 0.02),
    }


def reference(x, w_in, b_gate, q_norm_g, k_norm_g, w_four_proj, w_attn_proj, w_o,
              ln1_g, ln1_b, w_router, e_bias, w_e_in, w_e_down, w_sh_in, w_sh_down,
              ln2_g, ln2_b):
    for l in range(DEPTH):
        mix = hybrid_mixer(x, w_in[l], b_gate[l], q_norm_g[l], k_norm_g[l],
                           w_four_proj[l], w_attn_proj[l], w_o[l])
        x = layer_norm(DEEPNORM_ALPHA * x + mix, ln1_g[l], ln1_b[l])
        ffn = moe_ffn(x, w_router[l], e_bias[l], w_e_in[l], w_e_down[l], w_sh_in[l], w_sh_down[l])
        x = layer_norm(DEEPNORM_ALPHA * x + ffn, ln2_g[l], ln2_b[l])
    return x
```

```python
import functools
import math

import numpy as np
import jax
import jax.numpy as jnp
from jax import lax
from jax.experimental import pallas as pl
from jax.experimental.pallas import tpu as pltpu

F32 = jnp.float32
BF16 = jnp.bfloat16

GRID_W = 64
N_FOURIER_GROUPS = 8
FOURIER_GROUP_DIM = 64
FOURIER_WIDTH = N_FOURIER_GROUPS * FOURIER_GROUP_DIM
N_Q_HEADS = 16
N_KV_HEADS = 4
HEAD_DIM = 64
Q_GROUP = N_Q_HEADS // N_KV_HEADS
ATTN_WIDTH = N_Q_HEADS * HEAD_DIM
KV_WIDTH = N_KV_HEADS * HEAD_DIM
ROPE_THETA = 10000.0
QK_EPS = 1e-6
OFF_Q = FOURIER_WIDTH
OFF_K = OFF_Q + ATTN_WIDTH
OFF_V = OFF_K + KV_WIDTH
OFF_G = OFF_V + KV_WIDTH
N_EXPERTS = 256
TOP_K = 8
N_EXPERT_GROUPS = 8
GROUP_SIZE = N_EXPERTS // N_EXPERT_GROUPS
TOPK_GROUPS = 4
ROUTED_SCALE = 2.5
LN_EPS = 1e-5

LANES = 128
SUBLANES = 8
MXU_DIM = 256
VMEM_LIMIT = 56 * 1024 * 1024

EXPERT_BLOCK = 256

NT_DIMS = (((1,), (1,)), ((), ()))


def _dot(a, b):
    return jnp.dot(a, b, preferred_element_type=F32)


def _dot_nt(a, b):
    return lax.dot_general(a, b, NT_DIMS, preferred_element_type=F32)


def _sigmoid(x):
    return 1.0 / (1.0 + jnp.exp(-x))


def _params(*sem):
    return pltpu.CompilerParams(dimension_semantics=sem, vmem_limit_bytes=VMEM_LIMIT)


def _tile(n, pref):
    t = min(n, pref)
    assert n % t == 0, (n, t)
    return t


def _rope_tables(seq):
    lane = np.arange(MXU_DIM)
    d = lane % HEAD_DIM
    sub = d % 32
    j = sub % 16
    t = np.arange(seq)[:, None]
    pos = np.where(d[None, :] < 32, t // GRID_W, t % GRID_W).astype(np.float64)
    freq = ROPE_THETA ** (-(j.astype(np.float64)) / 16.0)
    ang = pos * freq[None, :]
    cos = np.cos(ang)
    sin = np.sin(ang) * np.where(sub < 16, -1.0, 1.0)[None, :]
    return jnp.asarray(cos, F32), jnp.asarray(sin, F32)


def _head_mean_matrix():
    i = np.arange(MXU_DIM)
    m = (i[:, None] // HEAD_DIM == i[None, :] // HEAD_DIM).astype(np.float64) / HEAD_DIM
    return jnp.asarray(m, BF16)


def _dft_tables(seq):
    c = np.arange(FOURIER_GROUP_DIM)
    ang_c = 2.0 * np.pi * ((c[:, None] * c[None, :]) % FOURIER_GROUP_DIM) / FOURIER_GROUP_DIM
    sc = 1.0 / math.sqrt(FOURIER_GROUP_DIM)
    eye = np.eye(N_FOURIER_GROUPS)
    cc = np.kron(eye, np.cos(ang_c) * sc)
    ss = np.kron(eye, np.sin(ang_c) * sc)
    chan = np.concatenate([cc, ss], axis=1)
    s = np.arange(seq)
    ang_s = 2.0 * np.pi * ((s[:, None] * s[None, :]) % seq) / seq
    ssc = 1.0 / math.sqrt(seq)
    seqm = np.concatenate([np.cos(ang_s) * ssc, -np.sin(ang_s) * ssc], axis=1)
    return jnp.asarray(chan, BF16), jnp.asarray(seqm, BF16)


def _norm_rope(z, gain, mean_mat, cos, sin, lo_mask):
    ms = _dot((z * z).astype(BF16), mean_mat)
    y = z * lax.rsqrt(ms + QK_EPS) * gain
    outs = []
    for c in range(MXU_DIM // LANES):
        yc = y[:, c * LANES:(c + 1) * LANES]
        up = pltpu.roll(yc, LANES - 16, 1)
        dn = pltpu.roll(yc, 16, 1)
        partner = jnp.where(lo_mask, up, dn)
        sl = slice(c * LANES, (c + 1) * LANES)
        outs.append(yc * cos[:, sl] + partner * sin[:, sl])
    return jnp.concatenate(outs, axis=1)


def _inproj_kernel(x_ref, w_ref, bg_ref, gq_ref, gk_ref, mm_ref, cos_ref, sin_ref,
                   u_ref, q_ref, k_ref, v_ref, g_ref):
    xb = x_ref[...].astype(BF16)
    u_ref[...] = _dot(xb, w_ref[:, 0:OFF_Q]).astype(BF16)

    lane = lax.broadcasted_iota(jnp.int32, (1, LANES), 1)
    lo_mask = (lane & 16) == 0
    mean_mat = mm_ref[...]
    cos = cos_ref[...]
    sin = sin_ref[...]

    for c in range(ATTN_WIDTH // MXU_DIM):
        z = _dot(xb, w_ref[:, OFF_Q + c * MXU_DIM:OFF_Q + (c + 1) * MXU_DIM])
        q = _norm_rope(z, gq_ref[...], mean_mat, cos, sin, lo_mask).astype(BF16)
        for j in range(MXU_DIM // HEAD_DIM):
            q_ref[0, c * (MXU_DIM // HEAD_DIM) + j] = q[:, j * HEAD_DIM:(j + 1) * HEAD_DIM]

    z = _dot(xb, w_ref[:, OFF_K:OFF_V])
    k = _norm_rope(z, gk_ref[...], mean_mat, cos, sin, lo_mask).astype(BF16)
    v = _dot(xb, w_ref[:, OFF_V:OFF_G]).astype(BF16)
    for j in range(N_KV_HEADS):
        k_ref[0, j] = k[:, j * HEAD_DIM:(j + 1) * HEAD_DIM]
        v_ref[0, j] = v[:, j * HEAD_DIM:(j + 1) * HEAD_DIM]

    gw = 512
    for c in range((w_ref.shape[1] - OFF_G) // gw):
        sl = slice(OFF_G + c * gw, OFF_G + (c + 1) * gw)
        z = _dot(xb, w_ref[:, sl]) + bg_ref[:, c * gw:(c + 1) * gw]
        g_ref[:, c * gw:(c + 1) * gw] = _sigmoid(z).astype(BF16)


def _inproj(x2, w_in, b_gate, q_g, k_g, batch, seq):
    n, d = x2.shape
    tm = _tile(seq, 512)
    spb = seq // tm
    in_width = w_in.shape[1]
    gate_w = in_width - OFF_G
    cos, sin = _rope_tables(seq)
    mean_mat = _head_mean_matrix()
    scale = HEAD_DIM ** -0.5
    gq = jnp.tile(q_g.astype(F32) * scale, MXU_DIM // HEAD_DIM)[None, :]
    gk = jnp.tile(k_g.astype(F32), MXU_DIM // HEAD_DIM)[None, :]
    const = lambda i: (0, 0)
    return pl.pallas_call(
        _inproj_kernel,
        grid=(n // tm,),
        in_specs=[
            pl.BlockSpec((tm, d), lambda i: (i, 0)),
            pl.BlockSpec((d, in_width), const),
            pl.BlockSpec((1, gate_w), const),
            pl.BlockSpec((1, MXU_DIM), const),
            pl.BlockSpec((1, MXU_DIM), const),
            pl.BlockSpec((MXU_DIM, MXU_DIM), const),
            pl.BlockSpec((tm, MXU_DIM), lambda i: (i % spb, 0)),
            pl.BlockSpec((tm, MXU_DIM), lambda i: (i % spb, 0)),
        ],
        out_specs=[
            pl.BlockSpec((tm, FOURIER_WIDTH), lambda i: (i, 0)),
            pl.BlockSpec((1, N_Q_HEADS, tm, HEAD_DIM), lambda i: (i // spb, 0, i % spb, 0)),
            pl.BlockSpec((1, N_KV_HEADS, tm, HEAD_DIM), lambda i: (i // spb, 0, i % spb, 0)),
            pl.BlockSpec((1, N_KV_HEADS, tm, HEAD_DIM), lambda i: (i // spb, 0, i % spb, 0)),
            pl.BlockSpec((tm, gate_w), lambda i: (i, 0)),
        ],
        out_shape=[
            jax.ShapeDtypeStruct((n, FOURIER_WIDTH), BF16),
            jax.ShapeDtypeStruct((batch, N_Q_HEADS, seq, HEAD_DIM), BF16),
            jax.ShapeDtypeStruct((batch, N_KV_HEADS, seq, HEAD_DIM), BF16),
            jax.ShapeDtypeStruct((batch, N_KV_HEADS, seq, HEAD_DIM), BF16),
            jax.ShapeDtypeStruct((n, gate_w), BF16),
        ],
        compiler_params=_params("parallel"),
        name="inproj",
    )(x2, w_in.astype(BF16), b_gate.astype(F32)[None, :], gq, gk, mean_mat, cos, sin)


def _fourier_kernel(u_ref, chan_ref, seqm_ref, wp_ref, g_ref, o_ref, ab_ref):
    seq = u_ref.shape[1]

    @pl.when(pl.program_id(1) == 0)
    def _():
        ab = _dot(u_ref[0], chan_ref[...])
        ab_ref[0:seq, :] = ab[:, 0:FOURIER_WIDTH].astype(BF16)
        ab_ref[seq:2 * seq, :] = ab[:, FOURIER_WIDTH:].astype(BF16)

    f = _dot(seqm_ref[...], ab_ref[...]).astype(BF16)
    y = _dot(f, wp_ref[...])
    o_ref[0] = (g_ref[0].astype(F32) * y).astype(BF16)


def _fourier(u3, g3, w_four_proj):
    batch, seq, _ = u3.shape
    d = w_four_proj.shape[1]
    tr = _tile(seq, 512)
    chan, seqm = _dft_tables(seq)
    return pl.pallas_call(
        _fourier_kernel,
        grid=(batch, seq // tr),
        in_specs=[
            pl.BlockSpec((1, seq, FOURIER_WIDTH), lambda b, r: (b, 0, 0)),
            pl.BlockSpec((FOURIER_WIDTH, 2 * FOURIER_WIDTH), lambda b, r: (0, 0)),
            pl.BlockSpec((tr, 2 * seq), lambda b, r: (r, 0)),
            pl.BlockSpec((FOURIER_WIDTH, d), lambda b, r: (0, 0)),
            pl.BlockSpec((1, tr, d), lambda b, r: (b, r, 0)),
        ],
        out_specs=pl.BlockSpec((1, tr, d), lambda b, r: (b, r, 0)),
        out_shape=jax.ShapeDtypeStruct((batch, seq, d), BF16),
        scratch_shapes=[pltpu.VMEM((2 * seq, FOURIER_WIDTH), BF16)],
        compiler_params=_params("parallel", "arbitrary"),
        name="fourier",
    )(u3, chan, seqm, w_four_proj.astype(BF16), g3)


def _attention_kernel(q_ref, k_ref, v_ref, o_ref):
    k = k_ref[0, 0]
    v = v_ref[0, 0]
    outs = []
    for g in range(Q_GROUP):
        s = _dot_nt(q_ref[0, g], k)
        m = jnp.max(s, axis=-1, keepdims=True)
        p = jnp.exp(s - m)
        l = jnp.sum(p, axis=-1, keepdims=True)
        o = _dot(p.astype(BF16), v)
        outs.append((o / l).astype(BF16))
    o_ref[0] = jnp.concatenate(outs, axis=1)


def _attention(q4, k4, v4):
    batch, _, seq, _ = q4.shape
    tq = _tile(seq, 256)
    return pl.pallas_call(
        _attention_kernel,
        grid=(batch, N_KV_HEADS, seq // tq),
        in_specs=[
            pl.BlockSpec((1, Q_GROUP, tq, HEAD_DIM), lambda b, h, i: (b, h, i, 0)),
            pl.BlockSpec((1, 1, seq, HEAD_DIM), lambda b, h, i: (b, h, 0, 0)),
            pl.BlockSpec((1, 1, seq, HEAD_DIM), lambda b, h, i: (b, h, 0, 0)),
        ],
        out_specs=pl.BlockSpec((1, tq, Q_GROUP * HEAD_DIM), lambda b, h, i: (b, i, h)),
        out_shape=jax.ShapeDtypeStruct((batch, seq, ATTN_WIDTH), BF16),
        compiler_params=_params("parallel", "parallel", "arbitrary"),
        name="attention",
    )(q4, k4, v4)


def _layer_norm(h, g, b):
    mu = jnp.mean(h, axis=-1, keepdims=True)
    c = h - mu
    var = jnp.mean(c * c, axis=-1, keepdims=True)
    return c * lax.rsqrt(var + LN_EPS) * g + b


def _to_token_tiles(dst_ref, val):
    t, d = val.shape
    for s in range(d // LANES):
        dst_ref[pl.ds(s, t, stride=d // LANES), :] = val[:, s * LANES:(s + 1) * LANES]


def _from_token_tiles(src_ref, t, d):
    return jnp.concatenate(
        [src_ref[pl.ds(s, t, stride=d // LANES), :] for s in range(d // LANES)], axis=1)


def _mix_kernel(alpha, o_ref, mf_ref, g_ref, x_ref, wap_ref, wo_ref, lg_ref, lb_ref,
                x1_ref, x1t_ref):
    y = _dot(o_ref[...], wap_ref[...])
    merged = mf_ref[...].astype(F32) + g_ref[...].astype(F32) * y
    mix = _dot(merged.astype(BF16), wo_ref[...])
    x1 = _layer_norm(alpha * x_ref[...] + mix, lg_ref[...], lb_ref[...])
    x1_ref[...] = x1
    _to_token_tiles(x1t_ref, x1)


def _mix(o2, mf2, g2, x2, w_attn_proj, w_o, ln_g, ln_b, alpha):
    n, d = x2.shape
    assert d == SUBLANES * LANES
    tm = _tile(n, 512)
    const = lambda i: (0, 0)
    return pl.pallas_call(
        functools.partial(_mix_kernel, alpha),
        grid=(n // tm,),
        in_specs=[
            pl.BlockSpec((tm, ATTN_WIDTH), lambda i: (i, 0)),
            pl.BlockSpec((tm, d), lambda i: (i, 0)),
            pl.BlockSpec((tm, d), lambda i: (i, 1)),
            pl.BlockSpec((tm, d), lambda i: (i, 0)),
            pl.BlockSpec((ATTN_WIDTH, d), const),
            pl.BlockSpec((d, d), const),
            pl.BlockSpec((1, d), const),
            pl.BlockSpec((1, d), const),
        ],
        out_specs=[
            pl.BlockSpec((tm, d), lambda i: (i, 0)),
            pl.BlockSpec((tm * SUBLANES, LANES), lambda i: (i, 0)),
        ],
        out_shape=[
            jax.ShapeDtypeStruct((n, d), F32),
            jax.ShapeDtypeStruct((n * SUBLANES, LANES), F32),
        ],
        compiler_params=_params("parallel"),
        name="mix",
    )(o2, mf2, g2, x2, w_attn_proj.astype(BF16), w_o.astype(BF16),
      ln_g.astype(F32)[None, :], ln_b.astype(F32)[None, :])


def _route_kernel(x_ref, wh_ref, wl_ref, eb_ref, tri_ref,
                  eidx_ref, rank_ref, w_ref, cnt_ref, carry_ref):
    tm = x_ref.shape[0]

    @pl.when(pl.program_id(0) == 0)
    def _():
        carry_ref[...] = jnp.zeros_like(carry_ref)

    x = x_ref[...]
    xh = x.astype(BF16)
    xl = (x - xh.astype(F32)).astype(BF16)
    wh = wh_ref[...]
    logits = _dot_nt(wh, xh) + _dot_nt(wh, xl) + _dot_nt(wl_ref[...], xh)
    scores = _sigmoid(logits)
    biased = scores + eb_ref[:, 0:1]
    neg = -jnp.inf

    sub_iota = lax.broadcasted_iota(jnp.int32, (GROUP_SIZE, tm), 0).astype(F32)
    gs = []
    for g in range(N_EXPERT_GROUPS):
        blk = biased[g * GROUP_SIZE:(g + 1) * GROUP_SIZE, :]
        m1 = jnp.max(blk, axis=0, keepdims=True)
        a1 = jnp.min(jnp.where(blk == m1, sub_iota, float(GROUP_SIZE)), axis=0, keepdims=True)
        m2 = jnp.max(jnp.where(sub_iota == a1, neg, blk), axis=0, keepdims=True)
        gs.append(m1 + m2)

    masked = []
    for g in range(N_EXPERT_GROUPS):
        beat = jnp.zeros((1, tm), F32)
        for h in range(N_EXPERT_GROUPS):
            if h == g:
                continue
            wins = (gs[h] >= gs[g]) if h < g else (gs[h] > gs[g])
            beat = beat + jnp.where(wins, 1.0, 0.0)
        keep = beat < float(TOPK_GROUPS)
        blk = biased[g * GROUP_SIZE:(g + 1) * GROUP_SIZE, :]
        masked.append(jnp.where(keep, blk, neg))
    masked = jnp.concatenate(masked, axis=0)

    e_iota = lax.broadcasted_iota(jnp.int32, (N_EXPERTS, tm), 0).astype(F32)
    sel = jnp.zeros((N_EXPERTS, tm), F32)
    idxs, ws = [], []
    for _ in range(TOP_K):
        mx = jnp.max(masked, axis=0, keepdims=True)
        idx = jnp.min(jnp.where(masked == mx, e_iota, float(N_EXPERTS)), axis=0, keepdims=True)
        hit = e_iota == idx
        masked = jnp.where(hit, neg, masked)
        sel = jnp.where(hit, 1.0, sel)
        idxs.append(idx)
        ws.append(jnp.sum(jnp.where(hit, scores, 0.0), axis=0, keepdims=True))

    carry = carry_ref[...]
    selb = sel.astype(BF16)
    prefix = _dot(selb, tri_ref[...])
    rank_all = prefix + jnp.concatenate([carry] * (tm // LANES), axis=1)
    total = carry + _dot(selb, jnp.ones((tm, LANES), BF16))
    carry_ref[...] = total
    cnt_ref[...] = total

    wsum = ws[0]
    for j in range(1, TOP_K):
        wsum = wsum + ws[j]
    for j in range(TOP_K):
        eidx_ref[j:j + 1, :] = idxs[j].astype(jnp.int32)
        r = jnp.sum(jnp.where(e_iota == idxs[j], rank_all, 0.0), axis=0, keepdims=True)
        rank_ref[j:j + 1, :] = r.astype(jnp.int32)
        w_ref[j:j + 1, :] = ws[j] / wsum * ROUTED_SCALE


def _route(x1, w_router, e_bias):
    n, d = x1.shape
    tm = _tile(n, 512)
    wt = w_router.astype(F32).T
    wh = wt.astype(BF16)
    wl = (wt - wh.astype(F32)).astype(BF16)
    eb = jnp.broadcast_to(e_bias.astype(F32)[:, None], (N_EXPERTS, LANES))
    tri = jnp.asarray(np.triu(np.ones((tm, tm)), k=1), BF16)
    const = lambda i: (0, 0)
    return pl.pallas_call(
        _route_kernel,
        grid=(n // tm,),
        in_specs=[
            pl.BlockSpec((tm, d), lambda i: (i, 0)),
            pl.BlockSpec((N_EXPERTS, d), const),
            pl.BlockSpec((N_EXPERTS, d), const),
            pl.BlockSpec((N_EXPERTS, LANES), const),
            pl.BlockSpec((tm, tm), const),
        ],
        out_specs=[
            pl.BlockSpec((TOP_K, tm), lambda i: (0, i)),
            pl.BlockSpec((TOP_K, tm), lambda i: (0, i)),
            pl.BlockSpec((TOP_K, tm), lambda i: (0, i)),
            pl.BlockSpec((N_EXPERTS, LANES), const),
        ],
        out_shape=[
            jax.ShapeDtypeStruct((TOP_K, n), jnp.int32),
            jax.ShapeDtypeStruct((TOP_K, n), jnp.int32),
            jax.ShapeDtypeStruct((TOP_K, n), F32),
            jax.ShapeDtypeStruct((N_EXPERTS, LANES), F32),
        ],
        scratch_shapes=[pltpu.VMEM((N_EXPERTS, LANES), F32)],
        compiler_params=_params("arbitrary"),
        name="route",
    )(x1, wh, wl, eb, tri)


def _row_copy(src_ref, src_row, dst_ref, dst_row, sem):
    src = src_ref.at[pl.ds(pl.multiple_of(src_row * SUBLANES, SUBLANES), SUBLANES), :]
    dst = dst_ref.at[pl.ds(pl.multiple_of(dst_row * SUBLANES, SUBLANES), SUBLANES), :]
    return pltpu.make_async_copy(src, dst, sem)


def _dispatch_kernel(dest_ref, x_ref, init_ref, xs_ref, sem):
    del init_ref
    tm = dest_ref.shape[1]

    def issue(t, c):
        for j in range(TOP_K):
            _row_copy(x_ref, t, xs_ref, dest_ref[j, t], sem).start()
        return c

    def drain(t, c):
        for j in range(TOP_K):
            _row_copy(x_ref, t, xs_ref, dest_ref[j, t], sem).wait()
        return c

    lax.fori_loop(0, tm, issue, 0)
    lax.fori_loop(0, tm, drain, 0)


def _dispatch(dest, x1t, n_rows):
    n = dest.shape[1]
    tm = _tile(n, 256)
    init = jnp.zeros((n_rows * SUBLANES, LANES), F32)
    return pl.pallas_call(
        _dispatch_kernel,
        grid=(n // tm,),
        in_specs=[
            pl.BlockSpec((TOP_K, tm), lambda i: (0, i), memory_space=pltpu.SMEM),
            pl.BlockSpec((tm * SUBLANES, LANES), lambda i: (i, 0)),
            pl.BlockSpec(memory_space=pl.ANY),
        ],
        out_specs=pl.BlockSpec(memory_space=pl.ANY),
        out_shape=jax.ShapeDtypeStruct((n_rows * SUBLANES, LANES), F32),
        scratch_shapes=[pltpu.SemaphoreType.DMA],
        input_output_aliases={2: 0},
        compiler_params=_params("arbitrary"),
        name="dispatch",
    )(dest, x1t, init)


def _swiglu(xb, w_in, w_down):
    h = _dot(xb, w_in)
    half = h.shape[1] // 2
    g = h[:, :half]
    act = g * _sigmoid(g) * h[:, half:]
    return _dot(act.astype(BF16), w_down)


def _experts_kernel(be_ref, nu_ref, x_ref, wi_ref, wd_ref, o_ref, wib_ref, wdb_ref):
    i = pl.program_id(0)
    used = i < nu_ref[0]
    prev = be_ref[jnp.maximum(i - 1, 0)]
    fresh = jnp.logical_or(i == 0, be_ref[i] != prev)
    blk = o_ref.shape[0] // SUBLANES
    d = wi_ref.shape[0]

    @pl.when(jnp.logical_and(used, fresh))
    def _():
        wib_ref[...] = wi_ref[...].astype(BF16)
        wdb_ref[...] = wd_ref[...].astype(BF16)

    @pl.when(used)
    def _():
        xb = _from_token_tiles(x_ref, blk, d).astype(BF16)
        _to_token_tiles(o_ref, _swiglu(xb, wib_ref[...], wdb_ref[...]))

    @pl.when(jnp.logical_not(used))
    def _():
        o_ref[...] = jnp.zeros_like(o_ref)


def _experts(blk_e, n_used, xs, w_e_in, w_e_down):
    n_blocks = blk_e.shape[0]
    _, d, h2 = w_e_in.shape
    hdim = w_e_down.shape[1]
    rows = EXPERT_BLOCK * SUBLANES

    def x_map(i, be, nu):
        return (jnp.minimum(i, nu[0] - 1), 0)

    def w_map(i, be, nu):
        return (be[jnp.minimum(i, nu[0] - 1)], 0, 0)

    return pl.pallas_call(
        _experts_kernel,
        grid_spec=pltpu.PrefetchScalarGridSpec(
            num_scalar_prefetch=2,
            grid=(n_blocks,),
            in_specs=[
                pl.BlockSpec((rows, LANES), x_map),
                pl.BlockSpec((None, d, h2), w_map),
                pl.BlockSpec((None, hdim, d), w_map),
            ],
            out_specs=pl.BlockSpec((rows, LANES), lambda i, be, nu: (i, 0)),
            scratch_shapes=[pltpu.VMEM((d, h2), BF16), pltpu.VMEM((hdim, d), BF16)],
        ),
        out_shape=jax.ShapeDtypeStruct(xs.shape, F32),
        compiler_params=_params("arbitrary"),
        name="experts",
    )(blk_e, n_used, xs, w_e_in, w_e_down)


def _combine_kernel(alpha, dest_ref, x1_ref, wrep_ref, wsi_ref, wsd_ref, lg_ref, lb_ref, os_ref,
                    out_ref, buf_ref, y_ref, sem):
    tm, d = x1_ref.shape

    def issue(t, c):
        for j in range(TOP_K):
            _row_copy(os_ref, dest_ref[j, t], buf_ref, j * tm + t, sem).start()
        return c

    def drain(t, c):
        for j in range(TOP_K):
            _row_copy(os_ref, dest_ref[j, t], buf_ref, j * tm + t, sem).wait()
        return c

    lax.fori_loop(0, tm, issue, 0)
    x1 = x1_ref[...]
    shared = _swiglu(x1.astype(BF16), wsi_ref[...], wsd_ref[...])
    lax.fori_loop(0, tm, drain, 0)

    rows = tm * SUBLANES
    acc = wrep_ref[:, 0:1] * buf_ref[0:rows, :]
    for j in range(1, TOP_K):
        acc = acc + wrep_ref[:, j:j + 1] * buf_ref[j * rows:(j + 1) * rows, :]
    y_ref[...] = acc
    routed = _from_token_tiles(y_ref, tm, d)
    out_ref[...] = _layer_norm(alpha * x1 + routed + shared, lg_ref[...], lb_ref[...])


def _combine(dest, x1, wrep, w_sh_in, w_sh_down, ln_g, ln_b, os, alpha):
    n, d = x1.shape
    tm = _tile(n, 128)
    const = lambda i: (0, 0)
    return pl.pallas_call(
        functools.partial(_combine_kernel, alpha),
        grid=(n // tm,),
        in_specs=[
            pl.BlockSpec((TOP_K, tm), lambda i: (0, i), memory_space=pltpu.SMEM),
            pl.BlockSpec((tm, d), lambda i: (i, 0)),
            pl.BlockSpec((tm * SUBLANES, TOP_K), lambda i: (i, 0)),
            pl.BlockSpec(w_sh_in.shape, const),
            pl.BlockSpec(w_sh_down.shape, const),
            pl.BlockSpec((1, d), const),
            pl.BlockSpec((1, d), const),
            pl.BlockSpec(memory_space=pl.ANY),
        ],
        out_specs=pl.BlockSpec((tm, d), lambda i: (i, 0)),
        out_shape=jax.ShapeDtypeStruct((n, d), F32),
        scratch_shapes=[
            pltpu.VMEM((TOP_K * tm * SUBLANES, LANES), F32),
            pltpu.VMEM((tm * SUBLANES, LANES), F32),
            pltpu.SemaphoreType.DMA,
        ],
        compiler_params=_params("arbitrary"),
        name="combine",
    )(dest, x1, wrep, w_sh_in.astype(BF16), w_sh_down.astype(BF16),
      ln_g.astype(F32)[None, :], ln_b.astype(F32)[None, :], os)


def _block_layout(counts, n_assign):
    n_blocks = (n_assign + N_EXPERTS * (EXPERT_BLOCK - 1) + EXPERT_BLOCK - 1) // EXPERT_BLOCK
    nblk = (counts + EXPERT_BLOCK - 1) // EXPERT_BLOCK
    blk_end = jnp.cumsum(nblk)
    pad_start = (blk_end - nblk) * EXPERT_BLOCK
    blk_e = jnp.searchsorted(blk_end, jnp.arange(n_blocks, dtype=jnp.int32), side="right")
    blk_e = jnp.minimum(blk_e, N_EXPERTS - 1).astype(jnp.int32)
    n_used = blk_end[-1:].astype(jnp.int32)
    return n_blocks, pad_start.astype(jnp.int32), blk_e, n_used


def _layer(x, w_in, b_gate, q_g, k_g, w_four_proj, w_attn_proj, w_o, ln1_g, ln1_b,
           w_router, e_bias, w_e_in, w_e_down, w_sh_in, w_sh_down, ln2_g, ln2_b, alpha):
    batch, seq, d = x.shape
    n = batch * seq
    x2 = x.reshape(n, d)

    u, q4, k4, v4, gates = _inproj(x2, w_in, b_gate, q_g, k_g, batch, seq)
    mf = _fourier(u.reshape(batch, seq, FOURIER_WIDTH), gates.reshape(batch, seq, -1), w_four_proj)
    o = _attention(q4, k4, v4)
    x1, x1t = _mix(o.reshape(n, ATTN_WIDTH), mf.reshape(n, d), gates, x2,
                   w_attn_proj, w_o, ln1_g, ln1_b, alpha)

    eidx, rank, wts, cnt = _route(x1, w_router, e_bias)
    counts = cnt[:, 0].astype(jnp.int32)
    n_blocks, pad_start, blk_e, n_used = _block_layout(counts, n * TOP_K)
    dest = pad_start[eidx] + rank
    wrep = jnp.repeat(wts.T, SUBLANES, axis=0)

    xs = _dispatch(dest, x1t, n_blocks * EXPERT_BLOCK)
    os = _experts(blk_e, n_used, xs, w_e_in, w_e_down)
    out = _combine(dest, x1, wrep, w_sh_in, w_sh_down, ln2_g, ln2_b, os, alpha)
    return out.reshape(batch, seq, d)


def kernel(x, w_in, b_gate, q_norm_g, k_norm_g, w_four_proj, w_attn_proj, w_o, ln1_g, ln1_b, w_router, e_bias, w_e_in, w_e_down, w_sh_in, w_sh_down, ln2_g, ln2_b):
    depth = w_in.shape[0]
    alpha = (2 * depth) ** 0.25
    for l in range(depth):
        x = _layer(x, w_in[l], b_gate[l], q_norm_g[l], k_norm_g[l], w_four_proj[l],
                   w_attn_proj[l], w_o[l], ln1_g[l], ln1_b[l], w_router[l], e_bias[l],
                   w_e_in[l], w_e_down[l], w_sh_in[l], w_sh_down[l], ln2_g[l], ln2_b[l], alpha)
    return x
```

```python
import functools
import math

import numpy as np
import jax
import jax.numpy as jnp
from jax import lax
from jax.experimental import pallas as pl
from jax.experimental.pallas import tpu as pltpu

F32 = jnp.float32
BF16 = jnp.bfloat16

GRID_W = 64
N_FOURIER_GROUPS = 8
FOURIER_GROUP_DIM = 64
FOURIER_WIDTH = N_FOURIER_GROUPS * FOURIER_GROUP_DIM
N_Q_HEADS = 16
N_KV_HEADS = 4
HEAD_DIM = 64
Q_GROUP = N_Q_HEADS // N_KV_HEADS
ATTN_WIDTH = N_Q_HEADS * HEAD_DIM
KV_WIDTH = N_KV_HEADS * HEAD_DIM
ROPE_THETA = 10000.0
QK_EPS = 1e-6
OFF_Q = FOURIER_WIDTH
OFF_K = OFF_Q + ATTN_WIDTH
OFF_V = OFF_K + KV_WIDTH
OFF_G = OFF_V + KV_WIDTH
N_EXPERTS = 256
TOP_K = 8
N_EXPERT_GROUPS = 8
GROUP_SIZE = N_EXPERTS // N_EXPERT_GROUPS
TOPK_GROUPS = 4
ROUTED_SCALE = 2.5
LN_EPS = 1e-5

LANES = 128
SUBLANES = 8
MXU_DIM = 256
VMEM_LIMIT = 56 * 1024 * 1024

EXPERT_BLOCK = 256
TOKEN_ROWS = 4

NT_DIMS = (((1,), (1,)), ((), ()))


def _dot(a, b):
    return jnp.dot(a, b, preferred_element_type=F32)


def _dot_nt(a, b):
    return lax.dot_general(a, b, NT_DIMS, preferred_element_type=F32)


def _sigmoid(x):
    return 1.0 / (1.0 + jnp.exp(-x))


def _params(*sem):
    return pltpu.CompilerParams(dimension_semantics=sem, vmem_limit_bytes=VMEM_LIMIT)


def _tile(n, pref):
    t = min(n, pref)
    assert n % t == 0, (n, t)
    return t


def _rope_tables(seq):
    lane = np.arange(MXU_DIM)
    d = lane % HEAD_DIM
    sub = d % 32
    j = sub % 16
    t = np.arange(seq)[:, None]
    pos = np.where(d[None, :] < 32, t // GRID_W, t % GRID_W).astype(np.float64)
    freq = ROPE_THETA ** (-(j.astype(np.float64)) / 16.0)
    ang = pos * freq[None, :]
    cos = np.cos(ang)
    sin = np.sin(ang) * np.where(sub < 16, -1.0, 1.0)[None, :]
    return jnp.asarray(cos, F32), jnp.asarray(sin, F32)


def _head_mean_matrix():
    i = np.arange(MXU_DIM)
    m = (i[:, None] // HEAD_DIM == i[None, :] // HEAD_DIM).astype(np.float64) / HEAD_DIM
    return jnp.asarray(m, BF16)


def _dft_tables(seq):
    c = np.arange(FOURIER_GROUP_DIM)
    ang_c = 2.0 * np.pi * ((c[:, None] * c[None, :]) % FOURIER_GROUP_DIM) / FOURIER_GROUP_DIM
    sc = 1.0 / math.sqrt(FOURIER_GROUP_DIM)
    eye = np.eye(N_FOURIER_GROUPS)
    cc = np.kron(eye, np.cos(ang_c) * sc)
    ss = np.kron(eye, np.sin(ang_c) * sc)
    chan = np.concatenate([cc, ss], axis=1)
    s = np.arange(seq)
    ang_s = 2.0 * np.pi * ((s[:, None] * s[None, :]) % seq) / seq
    ssc = 1.0 / math.sqrt(seq)
    seqm = np.concatenate([np.cos(ang_s) * ssc, -np.sin(ang_s) * ssc], axis=1)
    return jnp.asarray(chan, BF16), jnp.asarray(seqm, BF16)


def _norm_rope(z, gain, mean_mat, cos, sin, lo_mask):
    ms = _dot((z * z).astype(BF16), mean_mat)
    y = z * lax.rsqrt(ms + QK_EPS) * gain
    outs = []
    for c in range(MXU_DIM // LANES):
        yc = y[:, c * LANES:(c + 1) * LANES]
        up = pltpu.roll(yc, LANES - 16, 1)
        dn = pltpu.roll(yc, 16, 1)
        partner = jnp.where(lo_mask, up, dn)
        sl = slice(c * LANES, (c + 1) * LANES)
        outs.append(yc * cos[:, sl] + partner * sin[:, sl])
    return jnp.concatenate(outs, axis=1)


def _inproj_kernel(x_ref, w_ref, bg_ref, gq_ref, gk_ref, mm_ref, cos_ref, sin_ref,
                   u_ref, q_ref, k_ref, v_ref, g_ref):
    xb = x_ref[...].astype(BF16)
    u_ref[...] = _dot(xb, w_ref[:, 0:OFF_Q]).astype(BF16)

    lane = lax.broadcasted_iota(jnp.int32, (1, LANES), 1)
    lo_mask = (lane & 16) == 0
    mean_mat = mm_ref[...]
    cos = cos_ref[...]
    sin = sin_ref[...]

    for c in range(ATTN_WIDTH // MXU_DIM):
        z = _dot(xb, w_ref[:, OFF_Q + c * MXU_DIM:OFF_Q + (c + 1) * MXU_DIM])
        q = _norm_rope(z, gq_ref[...], mean_mat, cos, sin, lo_mask).astype(BF16)
        for j in range(MXU_DIM // HEAD_DIM):
            q_ref[0, c * (MXU_DIM // HEAD_DIM) + j] = q[:, j * HEAD_DIM:(j + 1) * HEAD_DIM]

    z = _dot(xb, w_ref[:, OFF_K:OFF_V])
    k = _norm_rope(z, gk_ref[...], mean_mat, cos, sin, lo_mask).astype(BF16)
    v = _dot(xb, w_ref[:, OFF_V:OFF_G]).astype(BF16)
    for j in range(N_KV_HEADS):
        k_ref[0, j] = k[:, j * HEAD_DIM:(j + 1) * HEAD_DIM]
        v_ref[0, j] = v[:, j * HEAD_DIM:(j + 1) * HEAD_DIM]

    gw = 512
    for c in range((w_ref.shape[1] - OFF_G) // gw):
        sl = slice(OFF_G + c * gw, OFF_G + (c + 1) * gw)
        z = _dot(xb, w_ref[:, sl]) + bg_ref[:, c * gw:(c + 1) * gw]
        g_ref[:, c * gw:(c + 1) * gw] = _sigmoid(z).astype(BF16)


def _inproj(x2, w_in, b_gate, q_g, k_g, batch, seq):
    n, d = x2.shape
    tm = _tile(seq, 512)
    spb = seq // tm
    in_width = w_in.shape[1]
    gate_w = in_width - OFF_G
    cos, sin = _rope_tables(seq)
    mean_mat = _head_mean_matrix()
    scale = HEAD_DIM ** -0.5 * math.log2(math.e)
    gq =jnp.tile(q_g.astype(F32) * scale, MXU_DIM // HEAD_DIM)[None, :]
    gk = jnp.tile(k_g.astype(F32), MXU_DIM // HEAD_DIM)[None, :]
    const = lambda i: (0, 0)
    return pl.pallas_call(
        _inproj_kernel,
        grid=(n // tm,),
        in_specs=[
            pl.BlockSpec((tm, d), lambda i: (i, 0)),
            pl.BlockSpec((d, in_width), const),
            pl.BlockSpec((1, gate_w), const),
            pl.BlockSpec((1, MXU_DIM), const),
            pl.BlockSpec((1, MXU_DIM), const),
            pl.BlockSpec((MXU_DIM, MXU_DIM), const),
            pl.BlockSpec((tm, MXU_DIM), lambda i: (i % spb, 0)),
            pl.BlockSpec((tm, MXU_DIM), lambda i: (i % spb, 0)),
        ],
        out_specs=[
            pl.BlockSpec((tm, FOURIER_WIDTH), lambda i: (i, 0)),
            pl.BlockSpec((1, N_Q_HEADS, tm, HEAD_DIM), lambda i: (i // spb, 0, i % spb, 0)),
            pl.BlockSpec((1, N_KV_HEADS, tm, HEAD_DIM), lambda i: (i // spb, 0, i % spb, 0)),
            pl.BlockSpec((1, N_KV_HEADS, tm, HEAD_DIM), lambda i: (i // spb, 0, i % spb, 0)),
            pl.BlockSpec((tm, gate_w), lambda i: (i, 0)),
        ],
        out_shape=[
            jax.ShapeDtypeStruct((n, FOURIER_WIDTH), BF16),
            jax.ShapeDtypeStruct((batch, N_Q_HEADS, seq, HEAD_DIM), BF16),
            jax.ShapeDtypeStruct((batch, N_KV_HEADS, seq, HEAD_DIM), BF16),
            jax.ShapeDtypeStruct((batch, N_KV_HEADS, seq, HEAD_DIM), BF16),
            jax.ShapeDtypeStruct((n, gate_w), BF16),
        ],
        compiler_params=_params("parallel"),
        name="inproj",
    )(x2, w_in.astype(BF16), b_gate.astype(F32)[None, :], gq, gk, mean_mat, cos, sin)


def _fourier_kernel(u_ref, chan_ref, seqm_ref, wp_ref, g_ref, o_ref, ab_ref):
    seq = u_ref.shape[1]

    @pl.when(pl.program_id(1) == 0)
    def _():
        ab = _dot(u_ref[0], chan_ref[...])
        ab_ref[0:seq, :] = ab[:, 0:FOURIER_WIDTH].astype(BF16)
        ab_ref[seq:2 * seq, :] = ab[:, FOURIER_WIDTH:].astype(BF16)

    f = _dot(seqm_ref[...], ab_ref[...]).astype(BF16)
    y = _dot(f, wp_ref[...])
    o_ref[0] = (g_ref[0].astype(F32) * y).astype(BF16)


def _fourier(u3, g3, w_four_proj):
    batch, seq, _ = u3.shape
    d = w_four_proj.shape[1]
    tr = _tile(seq, 512)
    chan, seqm = _dft_tables(seq)
    return pl.pallas_call(
        _fourier_kernel,
        grid=(batch, seq // tr),
        in_specs=[
            pl.BlockSpec((1, seq, FOURIER_WIDTH), lambda b, r: (b, 0, 0)),
            pl.BlockSpec((FOURIER_WIDTH, 2 * FOURIER_WIDTH), lambda b, r: (0, 0)),
            pl.BlockSpec((tr, 2 * seq), lambda b, r: (r, 0)),
            pl.BlockSpec((FOURIER_WIDTH, d), lambda b, r: (0, 0)),
            pl.BlockSpec((1, tr, d), lambda b, r: (b, r, 0)),
        ],
        out_specs=pl.BlockSpec((1, tr, d), lambda b, r: (b, r, 0)),
        out_shape=jax.ShapeDtypeStruct((batch, seq, d), BF16),
        scratch_shapes=[pltpu.VMEM((2 * seq, FOURIER_WIDTH), BF16)],
        compiler_params=_params("parallel", "arbitrary"),
        name="fourier",
    )(u3, chan, seqm, w_four_proj.astype(BF16), g3)


def _attention_kernel(q_ref, k_ref, v_ref, o_ref, vone_ref):
    seq = k_ref.shape[2]

    @pl.when(pl.program_id(2) == 0)
    def _():
        vone_ref[:, 0:HEAD_DIM] = v_ref[0, 0]
        vone_ref[:, HEAD_DIM:] = jnp.ones((seq, LANES - HEAD_DIM), BF16)

    k = k_ref[0, 0]
    vone = vone_ref[...]
    outs = []
    for g in range(Q_GROUP):
        s = _dot_nt(q_ref[0, g], k)
        m = jnp.max(s, axis=-1, keepdims=True)
        p = jnp.exp2(s - m).astype(BF16)
        ol = _dot(p, vone)
        l = pltpu.roll(ol, HEAD_DIM, 1)[:, 0:HEAD_DIM]
        outs.append((ol[:, 0:HEAD_DIM] / l).astype(BF16))
    o_ref[0] = jnp.concatenate(outs, axis=1)


def _attention(q4, k4, v4):
    batch, _, seq, _ = q4.shape
    tq = _tile(seq, 256)
    return pl.pallas_call(
        _attention_kernel,
        grid=(batch, N_KV_HEADS, seq // tq),
        in_specs=[
            pl.BlockSpec((1, Q_GROUP, tq, HEAD_DIM), lambda b, h, i: (b, h, i, 0)),
            pl.BlockSpec((1, 1, seq, HEAD_DIM), lambda b, h, i: (b, h, 0, 0)),
            pl.BlockSpec((1, 1, seq, HEAD_DIM), lambda b, h, i: (b, h, 0, 0)),
        ],
        out_specs=pl.BlockSpec((1, tq, Q_GROUP * HEAD_DIM), lambda b, h, i: (b, i, h)),
        out_shape=jax.ShapeDtypeStruct((batch, seq, ATTN_WIDTH), BF16),
        scratch_shapes=[pltpu.VMEM((seq, LANES), BF16)],
        compiler_params=_params("parallel", "parallel", "arbitrary"),
        name="attention",
    )(q4, k4, v4)


def _layer_norm(h, g, b):
    mu = jnp.mean(h, axis=-1, keepdims=True)
    c = h - mu
    var = jnp.mean(c * c, axis=-1, keepdims=True)
    return c * lax.rsqrt(var + LN_EPS) * g + b


def _slab(ref, row):
    return ref.at[pl.ds(pl.multiple_of(row * TOKEN_ROWS, TOKEN_ROWS), TOKEN_ROWS), :]


def _pack_tokens(dst_ref, val):
    t, d = val.shape
    half = d // 2
    assert half == TOKEN_ROWS * LANES
    bits = lax.bitcast_convert_type(val.astype(BF16).astype(F32), jnp.uint32)
    words = (bits[:, :half] >> 16) | bits[:, half:]
    for s in range(TOKEN_ROWS):
        dst_ref[pl.ds(s, t, stride=TOKEN_ROWS), :] = words[:, s * LANES:(s + 1) * LANES]


def _unpack_tokens(src_ref, tok0, t):
    lo, hi = [], []
    for s in range(TOKEN_ROWS):
        w = src_ref[pl.ds(tok0 * TOKEN_ROWS + s, t, stride=TOKEN_ROWS), :]
        lo.append(lax.bitcast_convert_type(w << 16, F32))
        hi.append(lax.bitcast_convert_type(w & jnp.uint32(0xFFFF0000), F32))
    return jnp.concatenate(lo + hi, axis=1)


def _mix_kernel(alpha, o_ref, mf_ref, g_ref, x_ref, wap_ref, wo_ref, lg_ref, lb_ref,
                x1_ref, x1p_ref):
    y = _dot(o_ref[...], wap_ref[...])
    merged = mf_ref[...].astype(F32) + g_ref[...].astype(F32) * y
    mix = _dot(merged.astype(BF16), wo_ref[...])
    x1 = _layer_norm(alpha * x_ref[...] + mix, lg_ref[...], lb_ref[...])
    x1_ref[...] = x1
    _pack_tokens(x1p_ref, x1)


def _mix(o2, mf2, g2, x2, w_attn_proj, w_o, ln_g, ln_b, alpha):
    n, d = x2.shape
    assert d == 2 * TOKEN_ROWS * LANES
    tm = _tile(n, 512)
    const = lambda i: (0, 0)
    return pl.pallas_call(
        functools.partial(_mix_kernel, alpha),
        grid=(n // tm,),
        in_specs=[
            pl.BlockSpec((tm, ATTN_WIDTH), lambda i: (i, 0)),
            pl.BlockSpec((tm, d), lambda i: (i, 0)),
            pl.BlockSpec((tm, d), lambda i: (i, 1)),
            pl.BlockSpec((tm, d), lambda i: (i, 0)),
            pl.BlockSpec((ATTN_WIDTH, d), const),
            pl.BlockSpec((d, d), const),
            pl.BlockSpec((1, d), const),
            pl.BlockSpec((1, d), const),
        ],
        out_specs=[
            pl.BlockSpec((tm, d), lambda i: (i, 0)),
            pl.BlockSpec((tm * TOKEN_ROWS, LANES), lambda i: (i, 0)),
        ],
        out_shape=[
            jax.ShapeDtypeStruct((n, d), F32),
            jax.ShapeDtypeStruct((n * TOKEN_ROWS, LANES), jnp.uint32),
        ],
        compiler_params=_params("parallel"),
        name="mix",
    )(o2, mf2, g2, x2, w_attn_proj.astype(BF16), w_o.astype(BF16),
      ln_g.astype(F32)[None, :], ln_b.astype(F32)[None, :])


def _route_kernel(x_ref, wh_ref, wl_ref, eb_ref, tri_ref,
                  eidx_ref, rank_ref, w_ref, cnt_ref, carry_ref):
    tm = x_ref.shape[0]

    @pl.when(pl.program_id(0) == 0)
    def _():
        carry_ref[...] = jnp.zeros_like(carry_ref)

    x = x_ref[...]
    xh = x.astype(BF16)
    xl = (x - xh.astype(F32)).astype(BF16)
    wh = wh_ref[...]
    logits = _dot_nt(wh, xh) + _dot_nt(wh, xl) + _dot_nt(wl_ref[...], xh)
    scores = _sigmoid(logits)
    biased = scores + eb_ref[:, 0:1]
    neg = -jnp.inf

    sub_iota = lax.broadcasted_iota(jnp.int32, (GROUP_SIZE, tm), 0).astype(F32)
    gs = []
    for g in range(N_EXPERT_GROUPS):
        blk = biased[g * GROUP_SIZE:(g + 1) * GROUP_SIZE, :]
        m1 = jnp.max(blk, axis=0, keepdims=True)
        a1 = jnp.min(jnp.where(blk == m1, sub_iota, float(GROUP_SIZE)), axis=0, keepdims=True)
        m2 = jnp.max(jnp.where(sub_iota == a1, neg, blk), axis=0, keepdims=True)
        gs.append(m1 + m2)

    masked = []
    for g in range(N_EXPERT_GROUPS):
        beat = jnp.zeros((1, tm), F32)
        for h in range(N_EXPERT_GROUPS):
            if h == g:
                continue
            wins = (gs[h] >= gs[g]) if h < g else (gs[h] > gs[g])
            beat = beat + jnp.where(wins, 1.0, 0.0)
        keep = beat < float(TOPK_GROUPS)
        blk = biased[g * GROUP_SIZE:(g + 1) * GROUP_SIZE, :]
        masked.append(jnp.where(keep, blk, neg))
    masked = jnp.concatenate(masked, axis=0)

    e_iota = lax.broadcasted_iota(jnp.int32, (N_EXPERTS, tm), 0).astype(F32)
    sel = jnp.zeros((N_EXPERTS, tm), F32)
    idxs, ws = [], []
    for _ in range(TOP_K):
        mx = jnp.max(masked, axis=0, keepdims=True)
        idx = jnp.min(jnp.where(masked == mx, e_iota, float(N_EXPERTS)), axis=0, keepdims=True)
        hit = e_iota == idx
        masked = jnp.where(hit, neg, masked)
        sel = jnp.where(hit, 1.0, sel)
        idxs.append(idx)
        ws.append(jnp.sum(jnp.where(hit, scores, 0.0), axis=0, keepdims=True))

    carry = carry_ref[...]
    selb = sel.astype(BF16)
    prefix = _dot(selb, tri_ref[...])
    rank_all = prefix + jnp.concatenate([carry] * (tm // LANES), axis=1)
    total = carry + _dot(selb, jnp.ones((tm, LANES), BF16))
    carry_ref[...] = total
    cnt_ref[...] = total

    wsum = ws[0]
    for j in range(1, TOP_K):
        wsum = wsum + ws[j]
    for j in range(TOP_K):
        eidx_ref[j:j + 1, :] = idxs[j].astype(jnp.int32)
        r = jnp.sum(jnp.where(e_iota == idxs[j], rank_all, 0.0), axis=0, keepdims=True)
        rank_ref[j:j + 1, :] = r.astype(jnp.int32)
        w_ref[j:j + 1, :] = ws[j] / wsum * ROUTED_SCALE


def _route(x1, w_router, e_bias):
    n, d = x1.shape
    tm = _tile(n, 512)
    wt = w_router.astype(F32).T
    wh = wt.astype(BF16)
    wl = (wt - wh.astype(F32)).astype(BF16)
    eb = jnp.broadcast_to(e_bias.astype(F32)[:, None], (N_EXPERTS, LANES))
    tri = jnp.asarray(np.triu(np.ones((tm, tm)), k=1), BF16)
    const = lambda i: (0, 0)
    return pl.pallas_call(
        _route_kernel,
        grid=(n // tm,),
        in_specs=[
            pl.BlockSpec((tm, d), lambda i: (i, 0)),
            pl.BlockSpec((N_EXPERTS, d), const),
            pl.BlockSpec((N_EXPERTS, d), const),
            pl.BlockSpec((N_EXPERTS, LANES), const),
            pl.BlockSpec((tm, tm), const),
        ],
        out_specs=[
            pl.BlockSpec((TOP_K, tm), lambda i: (0, i)),
            pl.BlockSpec((TOP_K, tm), lambda i: (0, i)),
            pl.BlockSpec((TOP_K, tm), lambda i: (0, i)),
            pl.BlockSpec((N_EXPERTS, LANES), const),
        ],
        out_shape=[
            jax.ShapeDtypeStruct((TOP_K, n), jnp.int32),
            jax.ShapeDtypeStruct((TOP_K, n), jnp.int32),
            jax.ShapeDtypeStruct((TOP_K, n), F32),
            jax.ShapeDtypeStruct((N_EXPERTS, LANES), F32),
        ],
        scratch_shapes=[pltpu.VMEM((N_EXPERTS, LANES), F32)],
        compiler_params=_params("arbitrary"),
        name="route",
    )(x1, wh, wl, eb, tri)


def _dest_kernel(eidx_ref, rank_ref, ps_ref, dest_ref):
    tm = eidx_ref.shape[1]
    e_iota = lax.broadcasted_iota(jnp.int32, (N_EXPERTS, tm), 0)
    ps = jnp.concatenate([ps_ref[...]] * (tm // LANES), axis=1)
    for j in range(TOP_K):
        hit = e_iota == eidx_ref[j:j + 1, :]
        start = jnp.sum(jnp.where(hit, ps, 0.0), axis=0, keepdims=True)
        dest_ref[j:j + 1, :] = start.astype(jnp.int32) + rank_ref[j:j + 1, :]


def _dest(eidx, rank, pad_start):
    n = eidx.shape[1]
    tm = _tile(n, 512)
    ps = jnp.broadcast_to(pad_start.astype(F32)[:, None], (N_EXPERTS, LANES))
    return pl.pallas_call(
        _dest_kernel,
        grid=(n // tm,),
        in_specs=[
            pl.BlockSpec((TOP_K, tm), lambda i: (0, i)),
            pl.BlockSpec((TOP_K, tm), lambda i: (0, i)),
            pl.BlockSpec((N_EXPERTS, LANES), lambda i: (0, 0)),
        ],
        out_specs=pl.BlockSpec((TOP_K, tm), lambda i: (0, i)),
        out_shape=jax.ShapeDtypeStruct((TOP_K, n), jnp.int32),
        compiler_params=_params("parallel"),
        name="dest",
    )(eidx, rank, ps)


def _dispatch_kernel(tail_ref, dest_ref, x_ref, xs_ref, zero_ref, sem):
    tm = dest_ref.shape[1]

    @pl.when(pl.program_id(0) == 0)
    def _():
        zero_ref[...] = jnp.zeros_like(zero_ref)

        def tail_copy(e):
            rows = EXPERT_BLOCK * TOKEN_ROWS
            row0 = pl.multiple_of(tail_ref[e] * rows, rows)
            return pltpu.make_async_copy(zero_ref, xs_ref.at[pl.ds(row0, rows), :], sem)

        def zstart(e, c):
            tail_copy(e).start()
            return c

        def zwait(e, c):
            tail_copy(e).wait()
            return c

        lax.fori_loop(0, N_EXPERTS, zstart, 0)
        lax.fori_loop(0, N_EXPERTS, zwait, 0)

    def row_copy(j, t):
        return pltpu.make_async_copy(_slab(x_ref, t), _slab(xs_ref, dest_ref[j, t]), sem)

    def issue(t, c):
        for j in range(TOP_K):
            row_copy(j, t).start()
        return c

    def drain(t, c):
        for j in range(TOP_K):
            row_copy(j, t).wait()
        return c

    lax.fori_loop(0, tm, issue, 0)
    lax.fori_loop(0, tm, drain, 0)


def _dispatch(tail_blk, dest, x1p, n_rows):
    n = dest.shape[1]
    tm = _tile(n, 256)
    return pl.pallas_call(
        _dispatch_kernel,
        grid_spec=pltpu.PrefetchScalarGridSpec(
            num_scalar_prefetch=1,
            grid=(n // tm,),
            in_specs=[
                pl.BlockSpec((TOP_K, tm), lambda i, tb: (0, i), memory_space=pltpu.SMEM),
                pl.BlockSpec((tm * TOKEN_ROWS, LANES), lambda i, tb: (i, 0)),
            ],
            out_specs=pl.BlockSpec(memory_space=pl.ANY),
            scratch_shapes=[pltpu.VMEM((EXPERT_BLOCK * TOKEN_ROWS, LANES), jnp.uint32),
                            pltpu.SemaphoreType.DMA],
        ),
        out_shape=jax.ShapeDtypeStruct((n_rows * TOKEN_ROWS, LANES), jnp.uint32),
        compiler_params=_params("arbitrary"),
        name="dispatch",
    )(tail_blk, dest, x1p)


def _swiglu(xb, w_in, w_down):
    h = _dot(xb, w_in)
    half = h.shape[1] // 2
    g = h[:, :half]
    act = g * _sigmoid(g) * h[:, half:]
    return _dot(act.astype(BF16), w_down)


def _experts_kernel(be_ref, nu_ref, x_ref, wi_ref, wd_ref, o_ref, wib_ref, wdb_ref):
    i = pl.program_id(0)
    used = i < nu_ref[0]
    prev = be_ref[jnp.maximum(i - 1, 0)]
    fresh = jnp.logical_or(i == 0, be_ref[i] != prev)
    blk = o_ref.shape[0] // TOKEN_ROWS

    @pl.when(jnp.logical_and(used, fresh))
    def _():
        wib_ref[...] = wi_ref[...].astype(BF16)
        wdb_ref[...] = wd_ref[...].astype(BF16)

    @pl.when(used)
    def _():
        xb = _unpack_tokens(x_ref, 0, blk).astype(BF16)
        _pack_tokens(o_ref, _swiglu(xb, wib_ref[...], wdb_ref[...]))

    @pl.when(jnp.logical_not(used))
    def _():
        o_ref[...] = jnp.zeros_like(o_ref)


def _experts(blk_e, n_used, xs, w_e_in, w_e_down):
    n_blocks = blk_e.shape[0]
    _, d, h2 = w_e_in.shape
    hdim = w_e_down.shape[1]
    rows = EXPERT_BLOCK * TOKEN_ROWS

    def x_map(i, be, nu):
        return (jnp.minimum(i, nu[0] - 1), 0)

    def w_map(i, be, nu):
        return (be[jnp.minimum(i, nu[0] - 1)], 0, 0)

    return pl.pallas_call(
        _experts_kernel,
        grid_spec=pltpu.PrefetchScalarGridSpec(
            num_scalar_prefetch=2,
            grid=(n_blocks,),
            in_specs=[
                pl.BlockSpec((rows, LANES), x_map),
                pl.BlockSpec((None, d, h2), w_map),
                pl.BlockSpec((None, hdim, d), w_map),
            ],
            out_specs=pl.BlockSpec((rows, LANES), lambda i, be, nu: (i, 0)),
            scratch_shapes=[pltpu.VMEM((d, h2), BF16), pltpu.VMEM((hdim, d), BF16)],
        ),
        out_shape=jax.ShapeDtypeStruct(xs.shape, jnp.uint32),
        compiler_params=_params("arbitrary"),
        name="experts",
    )(blk_e, n_used, xs, w_e_in, w_e_down)


def _combine_kernel(alpha, dest_ref, x1_ref, wt_ref, wsi_ref, wsd_ref, lg_ref, lb_ref, os_ref,
                    out_ref, buf_ref, sem):
    tm = x1_ref.shape[0]

    def row_copy(j, t):
        return pltpu.make_async_copy(_slab(os_ref, dest_ref[j, t]), _slab(buf_ref, j * tm + t), sem)

    def issue(t, c):
        for j in range(TOP_K):
            row_copy(j, t).start()
        return c

    def drain(t, c):
        for j in range(TOP_K):
            row_copy(j, t).wait()
        return c

    lax.fori_loop(0, tm, issue, 0)
    x1 = x1_ref[...]
    shared = _swiglu(x1.astype(BF16), wsi_ref[...], wsd_ref[...])
    lax.fori_loop(0, tm, drain, 0)

    routed = wt_ref[:, 0:1] * _unpack_tokens(buf_ref, 0, tm)
    for j in range(1, TOP_K):
        routed = routed + wt_ref[:, j:j + 1] * _unpack_tokens(buf_ref, j * tm, tm)
    out_ref[...] = _layer_norm(alpha * x1 + routed + shared, lg_ref[...], lb_ref[...])


def _combine(dest, x1, wt, w_sh_in, w_sh_down, ln_g, ln_b, os, alpha):
    n, d = x1.shape
    tm = _tile(n, 128)
    const = lambda i: (0, 0)
    return pl.pallas_call(
        functools.partial(_combine_kernel, alpha),
        grid=(n // tm,),
        in_specs=[
            pl.BlockSpec((TOP_K, tm), lambda i: (0, i), memory_space=pltpu.SMEM),
            pl.BlockSpec((tm, d), lambda i: (i, 0)),
            pl.BlockSpec((tm, TOP_K), lambda i: (i, 0)),
            pl.BlockSpec(w_sh_in.shape, const),
            pl.BlockSpec(w_sh_down.shape, const),
            pl.BlockSpec((1, d), const),
            pl.BlockSpec((1, d), const),
            pl.BlockSpec(memory_space=pl.ANY),
        ],
        out_specs=pl.BlockSpec((tm, d), lambda i: (i, 0)),
        out_shape=jax.ShapeDtypeStruct((n, d), F32),
        scratch_shapes=[
            pltpu.VMEM((TOP_K * tm * TOKEN_ROWS, LANES), jnp.uint32),
            pltpu.SemaphoreType.DMA,
        ],
        compiler_params=_params("arbitrary"),
        name="combine",
    )(dest, x1, wt, w_sh_in.astype(BF16), w_sh_down.astype(BF16),
      ln_g.astype(F32)[None, :], ln_b.astype(F32)[None, :], os)


def _block_layout(counts, n_assign):
    n_blocks = (n_assign + N_EXPERTS * (EXPERT_BLOCK - 1) + EXPERT_BLOCK - 1) // EXPERT_BLOCK
    nblk = (counts + EXPERT_BLOCK - 1) // EXPERT_BLOCK
    blk_end = jnp.cumsum(nblk)
    pad_start = (blk_end - nblk) * EXPERT_BLOCK
    blk_e = jnp.searchsorted(blk_end, jnp.arange(n_blocks, dtype=jnp.int32), side="right")
    blk_e = jnp.minimum(blk_e, N_EXPERTS - 1).astype(jnp.int32)
    n_used = blk_end[-1:].astype(jnp.int32)
    tail_blk = jnp.minimum(blk_end - jnp.minimum(nblk, 1), n_blocks - 1).astype(jnp.int32)
    return n_blocks, pad_start.astype(jnp.int32), blk_e, n_used, tail_blk


def _layer(x, w_in, b_gate, q_g, k_g, w_four_proj, w_attn_proj, w_o, ln1_g, ln1_b,
           w_router, e_bias, w_e_in, w_e_down, w_sh_in, w_sh_down, ln2_g, ln2_b, alpha):
    batch, seq, d = x.shape
    n = batch * seq
    x2 = x.reshape(n, d)

    u, q4, k4, v4, gates = _inproj(x2, w_in, b_gate, q_g, k_g, batch, seq)
    mf = _fourier(u.reshape(batch, seq, FOURIER_WIDTH), gates.reshape(batch, seq, -1), w_four_proj)
    o = _attention(q4, k4, v4)
    x1, x1p = _mix(o.reshape(n, ATTN_WIDTH), mf.reshape(n, d), gates, x2,
                   w_attn_proj, w_o, ln1_g, ln1_b, alpha)

    eidx, rank, wts, cnt = _route(x1, w_router, e_bias)
    counts = cnt[:, 0].astype(jnp.int32)
    n_blocks, pad_start, blk_e, n_used, tail_blk = _block_layout(counts, n * TOP_K)
    dest = _dest(eidx, rank, pad_start)

    xs = _dispatch(tail_blk, dest, x1p, n_blocks * EXPERT_BLOCK)
    os = _experts(blk_e, n_used, xs, w_e_in, w_e_down)
    out = _combine(dest, x1, wts.T, w_sh_in, w_sh_down, ln2_g, ln2_b, os, alpha)
    return out.reshape(batch, seq, d)


def kernel(x, w_in, b_gate, q_norm_g, k_norm_g, w_four_proj, w_attn_proj, w_o, ln1_g, ln1_b, w_router, e_bias, w_e_in, w_e_down, w_sh_in, w_sh_down, ln2_g, ln2_b):
    depth = w_in.shape[0]
    alpha = (2 * depth) ** 0.25
    for l in range(depth):
        x = _layer(x, w_in[l], b_gate[l], q_norm_g[l], k_norm_g[l], w_four_proj[l],
                   w_attn_proj[l], w_o[l], ln1_g[l], ln1_b[l], w_router[l], e_bias[l],
                   w_e_in[l], w_e_down[l], w_sh_in[l], w_sh_down[l], ln2_g[l], ln2_b[l], alpha)
    return x
```

```python
import functools
import math

import numpy as np
import jax
import jax.numpy as jnp
from jax import lax
from jax.experimental import pallas as pl
from jax.experimental.pallas import tpu as pltpu

F32 = jnp.float32
BF16 = jnp.bfloat16

GRID_W = 64
N_FOURIER_GROUPS = 8
FOURIER_GROUP_DIM = 64
FOURIER_WIDTH = N_FOURIER_GROUPS * FOURIER_GROUP_DIM
N_Q_HEADS = 16
N_KV_HEADS = 4
HEAD_DIM = 64
Q_GROUP = N_Q_HEADS // N_KV_HEADS
ATTN_WIDTH = N_Q_HEADS * HEAD_DIM
KV_WIDTH = N_KV_HEADS * HEAD_DIM
ROPE_THETA = 10000.0
QK_EPS = 1e-6
OFF_Q = FOURIER_WIDTH
OFF_K = OFF_Q + ATTN_WIDTH
OFF_V = OFF_K + KV_WIDTH
OFF_G = OFF_V + KV_WIDTH
N_EXPERTS = 256
TOP_K = 8
N_EXPERT_GROUPS = 8
GROUP_SIZE = N_EXPERTS // N_EXPERT_GROUPS
TOPK_GROUPS = 4
ROUTED_SCALE = 2.5
LN_EPS = 1e-5

LANES = 128
SUBLANES = 8
MXU_DIM = 256
VMEM_LIMIT = 56 * 1024 * 1024

DMA_PRIORITIES = 2

EXPERT_BLOCK = 512
TOKEN_ROWS = 4

NT_DIMS = (((1,), (1,)), ((), ()))


def _dot(a, b):
    return jnp.dot(a, b, preferred_element_type=F32)


def _dot_nt(a, b):
    return lax.dot_general(a, b, NT_DIMS, preferred_element_type=F32)


def _sigmoid(x):
    return 1.0 / (1.0 + jnp.exp(-x))


def _params(*sem):
    return pltpu.CompilerParams(dimension_semantics=sem, vmem_limit_bytes=VMEM_LIMIT)


def _tile(n, pref):
    t = min(n, pref)
    assert n % t == 0, (n, t)
    return t


def _rope_tables(seq):
    lane = np.arange(MXU_DIM)
    d = lane % HEAD_DIM
    sub = d % 32
    j = sub % 16
    t = np.arange(seq)[:, None]
    pos = np.where(d[None, :] < 32, t // GRID_W, t % GRID_W).astype(np.float64)
    freq = ROPE_THETA ** (-(j.astype(np.float64)) / 16.0)
    ang = pos * freq[None, :]
    cos = np.cos(ang)
    sin = np.sin(ang) * np.where(sub < 16, -1.0, 1.0)[None, :]
    return jnp.asarray(cos, F32), jnp.asarray(sin, F32)


def _head_mean_matrix():
    i = np.arange(MXU_DIM)
    m = (i[:, None] // HEAD_DIM == i[None, :] // HEAD_DIM).astype(np.float64) / HEAD_DIM
    return jnp.asarray(m, BF16)


def _dft_tables(seq):
    c = np.arange(FOURIER_GROUP_DIM)
    ang_c = 2.0 * np.pi * ((c[:, None] * c[None, :]) % FOURIER_GROUP_DIM) / FOURIER_GROUP_DIM
    sc = 1.0 / math.sqrt(FOURIER_GROUP_DIM)
    eye = np.eye(N_FOURIER_GROUPS)
    cc = np.kron(eye, np.cos(ang_c) * sc)
    ss = np.kron(eye, np.sin(ang_c) * sc)
    chan = np.concatenate([cc, ss], axis=1)
    s = np.arange(seq)
    ang_s = 2.0 * np.pi * ((s[:, None] * s[None, :]) % seq) / seq
    ssc = 1.0 / math.sqrt(seq)
    seqm = np.concatenate([np.cos(ang_s) * ssc, -np.sin(ang_s) * ssc], axis=1)
    return jnp.asarray(chan, BF16), jnp.asarray(seqm, BF16)


def _norm_rope(z, gain, mean_mat, cos, sin, lo_mask):
    ms = _dot((z * z).astype(BF16), mean_mat)
    y = z * lax.rsqrt(ms + QK_EPS) * gain
    outs = []
    for c in range(MXU_DIM // LANES):
        yc = y[:, c * LANES:(c + 1) * LANES]
        up = pltpu.roll(yc, LANES - 16, 1)
        dn = pltpu.roll(yc, 16, 1)
        partner = jnp.where(lo_mask, up, dn)
        sl = slice(c * LANES, (c + 1) * LANES)
        outs.append(yc * cos[:, sl] + partner * sin[:, sl])
    return jnp.concatenate(outs, axis=1)


def _inproj_kernel(x_ref, w_ref, bg_ref, gq_ref, gk_ref, mm_ref, cos_ref, sin_ref,
                   u_ref, q_ref, k_ref, v_ref, g_ref):
    xb = x_ref[...].astype(BF16)
    u_ref[...] = _dot(xb, w_ref[:, 0:OFF_Q]).astype(BF16)

    lane = lax.broadcasted_iota(jnp.int32, (1, LANES), 1)
    lo_mask = (lane & 16) == 0
    mean_mat = mm_ref[...]
    cos = cos_ref[...]
    sin = sin_ref[...]

    for c in range(ATTN_WIDTH // MXU_DIM):
        z = _dot(xb, w_ref[:, OFF_Q + c * MXU_DIM:OFF_Q + (c + 1) * MXU_DIM])
        q = _norm_rope(z, gq_ref[...], mean_mat, cos, sin, lo_mask).astype(BF16)
        for j in range(MXU_DIM // HEAD_DIM):
            q_ref[0, c * (MXU_DIM // HEAD_DIM) + j] = q[:, j * HEAD_DIM:(j + 1) * HEAD_DIM]

    z = _dot(xb, w_ref[:, OFF_K:OFF_V])
    k = _norm_rope(z, gk_ref[...], mean_mat, cos, sin, lo_mask).astype(BF16)
    v = _dot(xb, w_ref[:, OFF_V:OFF_G]).astype(BF16)
    for j in range(N_KV_HEADS):
        k_ref[0, j] = k[:, j * HEAD_DIM:(j + 1) * HEAD_DIM]
        v_ref[0, j] = v[:, j * HEAD_DIM:(j + 1) * HEAD_DIM]

    gw = 512
    for c in range((w_ref.shape[1] - OFF_G) // gw):
        sl = slice(OFF_G + c * gw, OFF_G + (c + 1) * gw)
        z = _dot(xb, w_ref[:, sl]) + bg_ref[:, c * gw:(c + 1) * gw]
        g_ref[:, c * gw:(c + 1) * gw] = _sigmoid(z).astype(BF16)


def _inproj(x2, w_in, b_gate, q_g, k_g, batch, seq):
    n, d = x2.shape
    tm = _tile(seq, 512)
    spb = seq // tm
    in_width = w_in.shape[1]
    gate_w = in_width - OFF_G
    cos, sin = _rope_tables(seq)
    mean_mat = _head_mean_matrix()
    scale = HEAD_DIM ** -0.5 * math.log2(math.e)
    gq =jnp.tile(q_g.astype(F32) * scale, MXU_DIM // HEAD_DIM)[None, :]
    gk = jnp.tile(k_g.astype(F32), MXU_DIM // HEAD_DIM)[None, :]
    const = lambda i: (0, 0)
    return pl.pallas_call(
        _inproj_kernel,
        grid=(n // tm,),
        in_specs=[
            pl.BlockSpec((tm, d), lambda i: (i, 0)),
            pl.BlockSpec((d, in_width), const),
            pl.BlockSpec((1, gate_w), const),
            pl.BlockSpec((1, MXU_DIM), const),
            pl.BlockSpec((1, MXU_DIM), const),
            pl.BlockSpec((MXU_DIM, MXU_DIM), const),
            pl.BlockSpec((tm, MXU_DIM), lambda i: (i % spb, 0)),
            pl.BlockSpec((tm, MXU_DIM), lambda i: (i % spb, 0)),
        ],
        out_specs=[
            pl.BlockSpec((tm, FOURIER_WIDTH), lambda i: (i, 0)),
            pl.BlockSpec((1, N_Q_HEADS, tm, HEAD_DIM), lambda i: (i // spb, 0, i % spb, 0)),
            pl.BlockSpec((1, N_KV_HEADS, tm, HEAD_DIM), lambda i: (i // spb, 0, i % spb, 0)),
            pl.BlockSpec((1, N_KV_HEADS, tm, HEAD_DIM), lambda i: (i // spb, 0, i % spb, 0)),
            pl.BlockSpec((tm, gate_w), lambda i: (i, 0)),
        ],
        out_shape=[
            jax.ShapeDtypeStruct((n, FOURIER_WIDTH), BF16),
            jax.ShapeDtypeStruct((batch, N_Q_HEADS, seq, HEAD_DIM), BF16),
            jax.ShapeDtypeStruct((batch, N_KV_HEADS, seq, HEAD_DIM), BF16),
            jax.ShapeDtypeStruct((batch, N_KV_HEADS, seq, HEAD_DIM), BF16),
            jax.ShapeDtypeStruct((n, gate_w), BF16),
        ],
        compiler_params=_params("parallel"),
        name="inproj",
    )(x2, w_in.astype(BF16), b_gate.astype(F32)[None, :], gq, gk, mean_mat, cos, sin)


def _fourier_kernel(u_ref, chan_ref, seqm_ref, wp_ref, g_ref, o_ref, ab_ref):
    seq = u_ref.shape[1]

    @pl.when(pl.program_id(1) == 0)
    def _():
        ab = _dot(u_ref[0], chan_ref[...])
        ab_ref[0:seq, :] = ab[:, 0:FOURIER_WIDTH].astype(BF16)
        ab_ref[seq:2 * seq, :] = ab[:, FOURIER_WIDTH:].astype(BF16)

    f = _dot(seqm_ref[...], ab_ref[...]).astype(BF16)
    y = _dot(f, wp_ref[...])
    o_ref[0] = (g_ref[0].astype(F32) * y).astype(BF16)


def _fourier(u3, g3, w_four_proj):
    batch, seq, _ = u3.shape
    d = w_four_proj.shape[1]
    tr = _tile(seq, 512)
    chan, seqm = _dft_tables(seq)
    return pl.pallas_call(
        _fourier_kernel,
        grid=(batch, seq // tr),
        in_specs=[
            pl.BlockSpec((1, seq, FOURIER_WIDTH), lambda b, r: (b, 0, 0)),
            pl.BlockSpec((FOURIER_WIDTH, 2 * FOURIER_WIDTH), lambda b, r: (0, 0)),
            pl.BlockSpec((tr, 2 * seq), lambda b, r: (r, 0)),
            pl.BlockSpec((FOURIER_WIDTH, d), lambda b, r: (0, 0)),
            pl.BlockSpec((1, tr, d), lambda b, r: (b, r, 0)),
        ],
        out_specs=pl.BlockSpec((1, tr, d), lambda b, r: (b, r, 0)),
        out_shape=jax.ShapeDtypeStruct((batch, seq, d), BF16),
        scratch_shapes=[pltpu.VMEM((2 * seq, FOURIER_WIDTH), BF16)],
        compiler_params=_params("parallel", "arbitrary"),
        name="fourier",
    )(u3, chan, seqm, w_four_proj.astype(BF16), g3)


def _attention_kernel(q_ref, k_ref, v_ref, o_ref, vone_ref):
    seq = k_ref.shape[2]

    @pl.when(pl.program_id(2) == 0)
    def _():
        vone_ref[:, 0:HEAD_DIM] = v_ref[0, 0]
        vone_ref[:, HEAD_DIM:] = jnp.ones((seq, LANES - HEAD_DIM), BF16)

    k = k_ref[0, 0]
    vone = vone_ref[...]
    outs = []
    for g in range(Q_GROUP):
        s = _dot_nt(q_ref[0, g], k)
        m = jnp.max(s, axis=-1, keepdims=True)
        p = jnp.exp2(s - m).astype(BF16)
        ol = _dot(p, vone)
        l = pltpu.roll(ol, HEAD_DIM, 1)[:, 0:HEAD_DIM]
        outs.append((ol[:, 0:HEAD_DIM] / l).astype(BF16))
    o_ref[0] = jnp.concatenate(outs, axis=1)


def _attention(q4, k4, v4):
    batch, _, seq, _ = q4.shape
    tq = _tile(seq, 256)
    return pl.pallas_call(
        _attention_kernel,
        grid=(batch, N_KV_HEADS, seq // tq),
        in_specs=[
            pl.BlockSpec((1, Q_GROUP, tq, HEAD_DIM), lambda b, h, i: (b, h, i, 0)),
            pl.BlockSpec((1, 1, seq, HEAD_DIM), lambda b, h, i: (b, h, 0, 0)),
            pl.BlockSpec((1, 1, seq, HEAD_DIM), lambda b, h, i: (b, h, 0, 0)),
        ],
        out_specs=pl.BlockSpec((1, tq, Q_GROUP * HEAD_DIM), lambda b, h, i: (b, i, h)),
        out_shape=jax.ShapeDtypeStruct((batch, seq, ATTN_WIDTH), BF16),
        scratch_shapes=[pltpu.VMEM((seq, LANES), BF16)],
        compiler_params=_params("parallel", "parallel", "arbitrary"),
        name="attention",
    )(q4, k4, v4)


def _layer_norm(h, g, b):
    mu = jnp.mean(h, axis=-1, keepdims=True)
    c = h - mu
    var = jnp.mean(c * c, axis=-1, keepdims=True)
    return c * lax.rsqrt(var + LN_EPS) * g + b


def _slab(ref, row):
    return ref.at[pl.ds(pl.multiple_of(row * TOKEN_ROWS, TOKEN_ROWS), TOKEN_ROWS), :]


def _pack_tokens(dst_ref, val):
    t, d = val.shape
    half = d // 2
    assert half == TOKEN_ROWS * LANES
    bits = lax.bitcast_convert_type(val.astype(BF16).astype(F32), jnp.uint32)
    words = (bits[:, :half] >> 16) | bits[:, half:]
    for s in range(TOKEN_ROWS):
        dst_ref[pl.ds(s, t, stride=TOKEN_ROWS), :] = words[:, s * LANES:(s + 1) * LANES]


def _unpack_tokens(src_ref, tok0, t):
    lo, hi = [], []
    for s in range(TOKEN_ROWS):
        w = src_ref[pl.ds(tok0 * TOKEN_ROWS + s, t, stride=TOKEN_ROWS), :]
        lo.append(lax.bitcast_convert_type(w << 16, F32))
        hi.append(lax.bitcast_convert_type(w & jnp.uint32(0xFFFF0000), F32))
    return jnp.concatenate(lo + hi, axis=1)


def _mix_kernel(alpha, o_ref, mf_ref, g_ref, x_ref, wap_ref, wo_ref, lg_ref, lb_ref,
                x1_ref, x1p_ref):
    y = _dot(o_ref[...], wap_ref[...])
    merged = mf_ref[...].astype(F32) + g_ref[...].astype(F32) * y
    mix = _dot(merged.astype(BF16), wo_ref[...])
    x1 = _layer_norm(alpha * x_ref[...] + mix, lg_ref[...], lb_ref[...])
    x1_ref[...] = x1
    _pack_tokens(x1p_ref, x1)


def _mix(o2, mf2, g2, x2, w_attn_proj, w_o, ln_g, ln_b, alpha):
    n, d = x2.shape
    assert d == 2 * TOKEN_ROWS * LANES
    tm = _tile(n, 512)
    const = lambda i: (0, 0)
    return pl.pallas_call(
        functools.partial(_mix_kernel, alpha),
        grid=(n // tm,),
        in_specs=[
            pl.BlockSpec((tm, ATTN_WIDTH), lambda i: (i, 0)),
            pl.BlockSpec((tm, d), lambda i: (i, 0)),
            pl.BlockSpec((tm, d), lambda i: (i, 1)),
            pl.BlockSpec((tm, d), lambda i: (i, 0)),
            pl.BlockSpec((ATTN_WIDTH, d), const),
            pl.BlockSpec((d, d), const),
            pl.BlockSpec((1, d), const),
            pl.BlockSpec((1, d), const),
        ],
        out_specs=[
            pl.BlockSpec((tm, d), lambda i: (i, 0)),
            pl.BlockSpec((tm * TOKEN_ROWS, LANES), lambda i: (i, 0)),
        ],
        out_shape=[
            jax.ShapeDtypeStruct((n, d), F32),
            jax.ShapeDtypeStruct((n * TOKEN_ROWS, LANES), jnp.uint32),
        ],
        compiler_params=_params("parallel"),
        name="mix",
    )(o2, mf2, g2, x2, w_attn_proj.astype(BF16), w_o.astype(BF16),
      ln_g.astype(F32)[None, :], ln_b.astype(F32)[None, :])


def _route_kernel(x_ref, wh_ref, wl_ref, eb_ref, tri_ref,
                  eidx_ref, rank_ref, w_ref, cnt_ref, carry_ref):
    tm = x_ref.shape[0]

    @pl.when(pl.program_id(0) == 0)
    def _():
        carry_ref[...] = jnp.zeros_like(carry_ref)

    x = x_ref[...]
    xh = x.astype(BF16)
    xl = (x - xh.astype(F32)).astype(BF16)
    wh = wh_ref[...]
    logits = _dot_nt(wh, xh) + _dot_nt(wh, xl) + _dot_nt(wl_ref[...], xh)
    scores = _sigmoid(logits)
    biased = scores + eb_ref[:, 0:1]
    neg = -jnp.inf

    sub_iota = lax.broadcasted_iota(jnp.int32, (GROUP_SIZE, tm), 0).astype(F32)
    gs = []
    for g in range(N_EXPERT_GROUPS):
        blk = biased[g * GROUP_SIZE:(g + 1) * GROUP_SIZE, :]
        m1 = jnp.max(blk, axis=0, keepdims=True)
        a1 = jnp.min(jnp.where(blk == m1, sub_iota, float(GROUP_SIZE)), axis=0, keepdims=True)
        m2 = jnp.max(jnp.where(sub_iota == a1, neg, blk), axis=0, keepdims=True)
        gs.append(m1 + m2)

    masked = []
    for g in range(N_EXPERT_GROUPS):
        beat = jnp.zeros((1, tm), F32)
        for h in range(N_EXPERT_GROUPS):
            if h == g:
                continue
            wins = (gs[h] >= gs[g]) if h < g else (gs[h] > gs[g])
            beat = beat + jnp.where(wins, 1.0, 0.0)
        keep = beat < float(TOPK_GROUPS)
        blk = biased[g * GROUP_SIZE:(g + 1) * GROUP_SIZE, :]
        masked.append(jnp.where(keep, blk, neg))
    masked = jnp.concatenate(masked, axis=0)

    e_iota = lax.broadcasted_iota(jnp.int32, (N_EXPERTS, tm), 0).astype(F32)
    sel = jnp.zeros((N_EXPERTS, tm), F32)
    idxs, ws = [], []
    for _ in range(TOP_K):
        mx = jnp.max(masked, axis=0, keepdims=True)
        idx = jnp.min(jnp.where(masked == mx, e_iota, float(N_EXPERTS)), axis=0, keepdims=True)
        hit = e_iota == idx
        masked = jnp.where(hit, neg, masked)
        sel = jnp.where(hit, 1.0, sel)
        idxs.append(idx)
        ws.append(jnp.sum(jnp.where(hit, scores, 0.0), axis=0, keepdims=True))

    carry = carry_ref[...]
    selb = sel.astype(BF16)
    prefix = _dot(selb, tri_ref[...])
    rank_all = prefix + jnp.concatenate([carry] * (tm // LANES), axis=1)
    total = carry + _dot(selb, jnp.ones((tm, LANES), BF16))
    carry_ref[...] = total
    cnt_ref[...] = total

    wsum = ws[0]
    for j in range(1, TOP_K):
        wsum = wsum + ws[j]
    for j in range(TOP_K):
        eidx_ref[j:j + 1, :] = idxs[j].astype(jnp.int32)
        r = jnp.sum(jnp.where(e_iota == idxs[j], rank_all, 0.0), axis=0, keepdims=True)
        rank_ref[j:j + 1, :] = r.astype(jnp.int32)
        w_ref[j:j + 1, :] = ws[j] / wsum * ROUTED_SCALE


def _route(x1, w_router, e_bias):
    n, d = x1.shape
    tm = _tile(n, 512)
    wt = w_router.astype(F32).T
    wh = wt.astype(BF16)
    wl = (wt - wh.astype(F32)).astype(BF16)
    eb = jnp.broadcast_to(e_bias.astype(F32)[:, None], (N_EXPERTS, LANES))
    tri = jnp.asarray(np.triu(np.ones((tm, tm)), k=1), BF16)
    const = lambda i: (0, 0)
    return pl.pallas_call(
        _route_kernel,
        grid=(n // tm,),
        in_specs=[
            pl.BlockSpec((tm, d), lambda i: (i, 0)),
            pl.BlockSpec((N_EXPERTS, d), const),
            pl.BlockSpec((N_EXPERTS, d), const),
            pl.BlockSpec((N_EXPERTS, LANES), const),
            pl.BlockSpec((tm, tm), const),
        ],
        out_specs=[
            pl.BlockSpec((TOP_K, tm), lambda i: (0, i)),
            pl.BlockSpec((TOP_K, tm), lambda i: (0, i)),
            pl.BlockSpec((TOP_K, tm), lambda i: (0, i)),
            pl.BlockSpec((N_EXPERTS, LANES), const),
        ],
        out_shape=[
            jax.ShapeDtypeStruct((TOP_K, n), jnp.int32),
            jax.ShapeDtypeStruct((TOP_K, n), jnp.int32),
            jax.ShapeDtypeStruct((TOP_K, n), F32),
            jax.ShapeDtypeStruct((N_EXPERTS, LANES), F32),
        ],
        scratch_shapes=[pltpu.VMEM((N_EXPERTS, LANES), F32)],
        compiler_params=_params("arbitrary"),
        name="route",
    )(x1, wh, wl, eb, tri)


def _dest_kernel(eidx_ref, rank_ref, ps_ref, dest_ref):
    tm = eidx_ref.shape[1]
    e_iota = lax.broadcasted_iota(jnp.int32, (N_EXPERTS, tm), 0)
    ps = jnp.concatenate([ps_ref[...]] * (tm // LANES), axis=1)
    for j in range(TOP_K):
        hit = e_iota == eidx_ref[j:j + 1, :]
        start = jnp.sum(jnp.where(hit, ps, 0.0), axis=0, keepdims=True)
        dest_ref[j:j + 1, :] = start.astype(jnp.int32) + rank_ref[j:j + 1, :]


def _dest(eidx, rank, pad_start):
    n = eidx.shape[1]
    tm = _tile(n, 512)
    ps = jnp.broadcast_to(pad_start.astype(F32)[:, None], (N_EXPERTS, LANES))
    return pl.pallas_call(
        _dest_kernel,
        grid=(n // tm,),
        in_specs=[
            pl.BlockSpec((TOP_K, tm), lambda i: (0, i)),
            pl.BlockSpec((TOP_K, tm), lambda i: (0, i)),
            pl.BlockSpec((N_EXPERTS, LANES), lambda i: (0, 0)),
        ],
        out_specs=pl.BlockSpec((TOP_K, tm), lambda i: (0, i)),
        out_shape=jax.ShapeDtypeStruct((TOP_K, n), jnp.int32),
        compiler_params=_params("parallel"),
        name="dest",
    )(eidx, rank, ps)


def _dispatch_kernel(tail_ref, dest_ref, x_ref, xs_ref, zero_ref, sem):
    tm = dest_ref.shape[1]

    @pl.when(pl.program_id(0) == 0)
    def _():
        zero_ref[...] = jnp.zeros_like(zero_ref)

        def tail_copy(e):
            rows = EXPERT_BLOCK * TOKEN_ROWS
            row0 = pl.multiple_of(tail_ref[e] * rows, rows)
            return pltpu.make_async_copy(zero_ref, xs_ref.at[pl.ds(row0, rows), :], sem)

        def zstart(e, c):
            @pl.when(tail_ref[e] >= 0)
            def _():
                tail_copy(e).start()
            return c

        def zwait(e, c):
            @pl.when(tail_ref[e] >= 0)
            def _():
                tail_copy(e).wait()
            return c

        lax.fori_loop(0, N_EXPERTS, zstart, 0)
        lax.fori_loop(0, N_EXPERTS, zwait, 0)

    def row_copy(j, t):
        return pltpu.make_async_copy(_slab(x_ref, t), _slab(xs_ref, dest_ref[j, t]), sem)

    def issue(t, c):
        for j in range(TOP_K):
            row_copy(j, t).start(priority=j % DMA_PRIORITIES)
        return c

    def drain(t, c):
        for j in range(TOP_K):
            row_copy(j, t).wait()
        return c

    lax.fori_loop(0, tm, issue, 0)
    lax.fori_loop(0, tm, drain, 0)


def _dispatch(tail_blk, dest, x1p, n_rows):
    n = dest.shape[1]
    tm = _tile(n, 512)
    return pl.pallas_call(
        _dispatch_kernel,
        grid_spec=pltpu.PrefetchScalarGridSpec(
            num_scalar_prefetch=1,
            grid=(n // tm,),
            in_specs=[
                pl.BlockSpec((TOP_K, tm), lambda i, tb: (0, i), memory_space=pltpu.SMEM),
                pl.BlockSpec((tm * TOKEN_ROWS, LANES), lambda i, tb: (i, 0)),
            ],
            out_specs=pl.BlockSpec(memory_space=pl.ANY),
            scratch_shapes=[pltpu.VMEM((EXPERT_BLOCK * TOKEN_ROWS, LANES), jnp.uint32),
                            pltpu.SemaphoreType.DMA],
        ),
        out_shape=jax.ShapeDtypeStruct((n_rows * TOKEN_ROWS, LANES), jnp.uint32),
        compiler_params=_params("arbitrary"),
        name="dispatch",
    )(tail_blk, dest, x1p)


def _swiglu(xb, w_in, w_down):
    h = _dot(xb, w_in)
    half = h.shape[1] // 2
    g = h[:, :half]
    act = g * _sigmoid(g) * h[:, half:]
    return _dot(act.astype(BF16), w_down)


def _experts_kernel(be_ref, nu_ref, x_ref, wi_ref, wd_ref, o_ref, wib_ref, wdb_ref):
    i = pl.program_id(0)
    used = i < nu_ref[0]
    prev = be_ref[jnp.maximum(i - 1, 0)]
    fresh = jnp.logical_or(i == 0, be_ref[i] != prev)
    blk = o_ref.shape[0] // TOKEN_ROWS

    @pl.when(jnp.logical_and(used, fresh))
    def _():
        wib_ref[...] = wi_ref[...].astype(BF16)
        wdb_ref[...] = wd_ref[...].astype(BF16)

    @pl.when(used)
    def _():
        xb = _unpack_tokens(x_ref, 0, blk).astype(BF16)
        _pack_tokens(o_ref, _swiglu(xb, wib_ref[...], wdb_ref[...]))

    @pl.when(jnp.logical_not(used))
    def _():
        o_ref[...] = jnp.zeros_like(o_ref)


def _experts(blk_e, n_used, xs, w_e_in, w_e_down):
    n_blocks = blk_e.shape[0]
    _, d, h2 = w_e_in.shape
    hdim = w_e_down.shape[1]
    rows = EXPERT_BLOCK * TOKEN_ROWS

    def x_map(i, be, nu):
        return (jnp.minimum(i, nu[0] - 1), 0)

    def w_map(i, be, nu):
        return (be[jnp.minimum(i, nu[0] - 1)], 0, 0)

    return pl.pallas_call(
        _experts_kernel,
        grid_spec=pltpu.PrefetchScalarGridSpec(
            num_scalar_prefetch=2,
            grid=(n_blocks,),
            in_specs=[
                pl.BlockSpec((rows, LANES), x_map),
                pl.BlockSpec((None, d, h2), w_map),
                pl.BlockSpec((None, hdim, d), w_map),
            ],
            out_specs=pl.BlockSpec((rows, LANES), lambda i, be, nu: (i, 0)),
            scratch_shapes=[pltpu.VMEM((d, h2), BF16), pltpu.VMEM((hdim, d), BF16)],
        ),
        out_shape=jax.ShapeDtypeStruct(xs.shape, jnp.uint32),
        compiler_params=_params("arbitrary"),
        name="experts",
    )(blk_e, n_used, xs, w_e_in, w_e_down)


def _combine_kernel(alpha, dest_ref, dnext_ref, x1_ref, wt_ref, wsi_ref, wsd_ref, lg_ref, lb_ref,
                    os_ref, out_ref, buf_ref, sem):
    tm = x1_ref.shape[0]
    i = pl.program_id(0)
    slot = i % 2

    def gather(d_ref, s, start):
        def body(t, c):
            for j in range(TOP_K):
                cp = pltpu.make_async_copy(_slab(os_ref, d_ref[j, t]),
                                           _slab(buf_ref.at[s], j * tm + t), sem.at[s])
                if start:
                    cp.start(priority=j % DMA_PRIORITIES)
                else:
                    cp.wait()
            return c
        lax.fori_loop(0, tm, body, 0)

    @pl.when(i == 0)
    def _():
        gather(dest_ref, slot, True)

    @pl.when(i + 1 < pl.num_programs(0))
    def _():
        gather(dnext_ref, 1 - slot, True)

    x1 = x1_ref[...]
    shared = _swiglu(x1.astype(BF16), wsi_ref[...], wsd_ref[...])
    gather(dest_ref, slot, False)

    rows = buf_ref.at[slot]
    routed = wt_ref[:, 0:1] * _unpack_tokens(rows, 0, tm)
    for j in range(1, TOP_K):
        routed = routed + wt_ref[:, j:j + 1] * _unpack_tokens(rows, j * tm, tm)
    out_ref[...] = _layer_norm(alpha * x1 + routed + shared, lg_ref[...], lb_ref[...])


def _combine(dest, x1, wt, w_sh_in, w_sh_down, ln_g, ln_b, os, alpha):
    n, d = x1.shape
    tm = _tile(n, 256)
    last = n // tm - 1
    const = lambda i: (0, 0)
    return pl.pallas_call(
        functools.partial(_combine_kernel, alpha),
        grid=(n // tm,),
        in_specs=[
            pl.BlockSpec((TOP_K, tm), lambda i: (0, i), memory_space=pltpu.SMEM),
            pl.BlockSpec((TOP_K, tm), lambda i: (0, jnp.minimum(i + 1, last)), memory_space=pltpu.SMEM),
            pl.BlockSpec((tm, d), lambda i: (i, 0)),
            pl.BlockSpec((tm, TOP_K), lambda i: (i, 0)),
            pl.BlockSpec(w_sh_in.shape, const),
            pl.BlockSpec(w_sh_down.shape, const),
            pl.BlockSpec((1, d), const),
            pl.BlockSpec((1, d), const),
            pl.BlockSpec(memory_space=pl.ANY),
        ],
        out_specs=pl.BlockSpec((tm, d), lambda i: (i, 0)),
        out_shape=jax.ShapeDtypeStruct((n, d), F32),
        scratch_shapes=[
            pltpu.VMEM((2, TOP_K * tm * TOKEN_ROWS, LANES), jnp.uint32),
            pltpu.SemaphoreType.DMA((2,)),
        ],
        compiler_params=_params("arbitrary"),
        name="combine",
    )(dest, dest, x1, wt, w_sh_in.astype(BF16), w_sh_down.astype(BF16),
      ln_g.astype(F32)[None, :], ln_b.astype(F32)[None, :], os)


def _block_layout(counts, n_assign):
    n_blocks = (n_assign + N_EXPERTS * (EXPERT_BLOCK - 1) + EXPERT_BLOCK - 1) // EXPERT_BLOCK
    nblk = (counts + EXPERT_BLOCK - 1) // EXPERT_BLOCK
    blk_end = jnp.cumsum(nblk)
    pad_start = (blk_end - nblk) * EXPERT_BLOCK
    blk_e = jnp.sum(blk_end[None, :] <= jnp.arange(n_blocks, dtype=jnp.int32)[:, None], axis=1)
    blk_e = jnp.minimum(blk_e, N_EXPERTS - 1).astype(jnp.int32)
    n_used = blk_end[-1:].astype(jnp.int32)
    tail_blk = jnp.where(nblk > 0, blk_end - 1, -1).astype(jnp.int32)
    return n_blocks, pad_start.astype(jnp.int32), blk_e, n_used, tail_blk


def _layer(x, w_in, b_gate, q_g, k_g, w_four_proj, w_attn_proj, w_o, ln1_g, ln1_b,
           w_router, e_bias, w_e_in, w_e_down, w_sh_in, w_sh_down, ln2_g, ln2_b, alpha):
    batch, seq, d = x.shape
    n = batch * seq
    x2 = x.reshape(n, d)

    u, q4, k4, v4, gates = _inproj(x2, w_in, b_gate, q_g, k_g, batch, seq)
    mf = _fourier(u.reshape(batch, seq, FOURIER_WIDTH), gates.reshape(batch, seq, -1), w_four_proj)
    o = _attention(q4, k4, v4)
    x1, x1p = _mix(o.reshape(n, ATTN_WIDTH), mf.reshape(n, d), gates, x2,
                   w_attn_proj, w_o, ln1_g, ln1_b, alpha)

    eidx, rank, wts, cnt = _route(x1, w_router, e_bias)
    counts = cnt[:, 0].astype(jnp.int32)
    n_blocks, pad_start, blk_e, n_used, tail_blk = _block_layout(counts, n * TOP_K)
    dest = _dest(eidx, rank, pad_start)

    xs = _dispatch(tail_blk, dest, x1p, n_blocks * EXPERT_BLOCK)
    os = _experts(blk_e, n_used, xs, w_e_in, w_e_down)
    out = _combine(dest, x1, wts.T, w_sh_in, w_sh_down, ln2_g, ln2_b, os, alpha)
    return out.reshape(batch, seq, d)


def kernel(x, w_in, b_gate, q_norm_g, k_norm_g, w_four_proj, w_attn_proj, w_o, ln1_g, ln1_b, w_router, e_bias, w_e_in, w_e_down, w_sh_in, w_sh_down, ln2_g, ln2_b):
    depth = w_in.shape[0]
    alpha = (2 * depth) ** 0.25
    for l in range(depth):
        x = _layer(x, w_in[l], b_gate[l], q_norm_g[l], k_norm_g[l], w_four_proj[l],
                   w_attn_proj[l], w_o[l], ln1_g[l], ln1_b[l], w_router[l], e_bias[l],
                   w_e_in[l], w_e_down[l], w_sh_in[l], w_sh_down[l], ln2_g[l], ln2_b[l], alpha)
    return x
```

```python
import functools
import math

import numpy as np
import jax
import jax.numpy as jnp
from jax import lax
from jax.experimental import pallas as pl
from jax.experimental.pallas import tpu as pltpu

F32 = jnp.float32
BF16 = jnp.bfloat16

GRID_W = 64
N_FOURIER_GROUPS = 8
FOURIER_GROUP_DIM = 64
FOURIER_WIDTH = N_FOURIER_GROUPS * FOURIER_GROUP_DIM
N_Q_HEADS = 16
N_KV_HEADS = 4
HEAD_DIM = 64
Q_GROUP = N_Q_HEADS // N_KV_HEADS
ATTN_WIDTH = N_Q_HEADS * HEAD_DIM
KV_WIDTH = N_KV_HEADS * HEAD_DIM
ROPE_THETA = 10000.0
QK_EPS = 1e-6
OFF_Q = FOURIER_WIDTH
OFF_K = OFF_Q + ATTN_WIDTH
OFF_V = OFF_K + KV_WIDTH
OFF_G = OFF_V + KV_WIDTH
N_EXPERTS = 256
TOP_K = 8
N_EXPERT_GROUPS = 8
GROUP_SIZE = N_EXPERTS // N_EXPERT_GROUPS
TOPK_GROUPS = 4
ROUTED_SCALE = 2.5
LN_EPS = 1e-5

LANES = 128
SUBLANES = 8
MXU_DIM = 256
VMEM_LIMIT = 56 * 1024 * 1024

MAX_EXP2_RANGE = 100.0
DMA_PRIORITIES = 2

EXPERT_BLOCK = 512
TOKEN_ROWS = 4

NT_DIMS = (((1,), (1,)), ((), ()))


def _dot(a, b):
    return jnp.dot(a, b, preferred_element_type=F32)


def _dot_nt(a, b):
    return lax.dot_general(a, b, NT_DIMS, preferred_element_type=F32)


def _sigmoid(x):
    return 1.0 / (1.0 + jnp.exp(-x))


def _params(*sem):
    return pltpu.CompilerParams(dimension_semantics=sem, vmem_limit_bytes=VMEM_LIMIT)


def _tile(n, pref):
    t = min(n, pref)
    assert n % t == 0, (n, t)
    return t


def _rope_tables(seq):
    lane = np.arange(MXU_DIM)
    d = lane % HEAD_DIM
    sub = d % 32
    j = sub % 16
    t = np.arange(seq)[:, None]
    pos = np.where(d[None, :] < 32, t // GRID_W, t % GRID_W).astype(np.float64)
    freq = ROPE_THETA ** (-(j.astype(np.float64)) / 16.0)
    ang = pos * freq[None, :]
    cos = np.cos(ang)
    sin = np.sin(ang) * np.where(sub < 16, -1.0, 1.0)[None, :]
    return jnp.asarray(cos, F32), jnp.asarray(sin, F32)


def _head_mean_matrix():
    i = np.arange(MXU_DIM)
    m = (i[:, None] // HEAD_DIM == i[None, :] // HEAD_DIM).astype(np.float64) / HEAD_DIM
    return jnp.asarray(m, BF16)


def _dft_tables(seq):
    c = np.arange(FOURIER_GROUP_DIM)
    ang_c = 2.0 * np.pi * ((c[:, None] * c[None, :]) % FOURIER_GROUP_DIM) / FOURIER_GROUP_DIM
    sc = 1.0 / math.sqrt(FOURIER_GROUP_DIM)
    eye = np.eye(N_FOURIER_GROUPS)
    cc = np.kron(eye, np.cos(ang_c) * sc)
    ss = np.kron(eye, np.sin(ang_c) * sc)
    chan = np.concatenate([cc, ss], axis=1)
    s = np.arange(seq)
    ang_s = 2.0 * np.pi * ((s[:, None] * s[None, :]) % seq) / seq
    ssc = 1.0 / math.sqrt(seq)
    seqm = np.concatenate([np.cos(ang_s) * ssc, -np.sin(ang_s) * ssc], axis=1)
    return jnp.asarray(chan, BF16), jnp.asarray(seqm, BF16)


def _norm_rope(z, gain, mean_mat, cos, sin, lo_mask):
    ms = _dot((z * z).astype(BF16), mean_mat)
    y = z * lax.rsqrt(ms + QK_EPS) * gain
    outs = []
    for c in range(MXU_DIM // LANES):
        yc = y[:, c * LANES:(c + 1) * LANES]
        up = pltpu.roll(yc, LANES - 16, 1)
        dn = pltpu.roll(yc, 16, 1)
        partner = jnp.where(lo_mask, up, dn)
        sl = slice(c * LANES, (c + 1) * LANES)
        outs.append(yc * cos[:, sl] + partner * sin[:, sl])
    return jnp.concatenate(outs, axis=1)


def _inproj_kernel(x_ref, w_ref, bg_ref, gq_ref, gk_ref, mm_ref, cos_ref, sin_ref,
                   u_ref, q_ref, k_ref, v_ref, g_ref):
    xb = x_ref[...].astype(BF16)
    u_ref[...] = _dot(xb, w_ref[:, 0:OFF_Q]).astype(BF16)

    lane = lax.broadcasted_iota(jnp.int32, (1, LANES), 1)
    lo_mask = (lane & 16) == 0
    mean_mat = mm_ref[...]
    cos = cos_ref[...]
    sin = sin_ref[...]

    for c in range(ATTN_WIDTH // MXU_DIM):
        z = _dot(xb, w_ref[:, OFF_Q + c * MXU_DIM:OFF_Q + (c + 1) * MXU_DIM])
        q = _norm_rope(z, gq_ref[...], mean_mat, cos, sin, lo_mask).astype(BF16)
        for j in range(MXU_DIM // HEAD_DIM):
            q_ref[0, c * (MXU_DIM // HEAD_DIM) + j] = q[:, j * HEAD_DIM:(j + 1) * HEAD_DIM]

    z = _dot(xb, w_ref[:, OFF_K:OFF_V])
    k = _norm_rope(z, gk_ref[...], mean_mat, cos, sin, lo_mask).astype(BF16)
    vt = _dot(xb, w_ref[:, OFF_V:OFF_G]).T.astype(BF16)
    for j in range(N_KV_HEADS):
        k_ref[0, j] = k[:, j * HEAD_DIM:(j + 1) * HEAD_DIM]
        v_ref[0, j] = vt[j * HEAD_DIM:(j + 1) * HEAD_DIM, :]

    gw = 512
    for c in range((w_ref.shape[1] - OFF_G) // gw):
        sl = slice(OFF_G + c * gw, OFF_G + (c + 1) * gw)
        z = _dot(xb, w_ref[:, sl]) + bg_ref[:, c * gw:(c + 1) * gw]
        g_ref[:, c * gw:(c + 1) * gw] = _sigmoid(z).astype(BF16)


def _inproj(x2, w_in, b_gate, q_g, k_g, batch, seq):
    n, d = x2.shape
    tm = _tile(seq, 512)
    spb = seq // tm
    in_width = w_in.shape[1]
    gate_w = in_width - OFF_G
    cos, sin = _rope_tables(seq)
    mean_mat = _head_mean_matrix()
    scale = HEAD_DIM ** -0.5 * math.log2(math.e)
    gq =jnp.tile(q_g.astype(F32) * scale, MXU_DIM // HEAD_DIM)[None, :]
    gk = jnp.tile(k_g.astype(F32), MXU_DIM // HEAD_DIM)[None, :]
    score_bound = (HEAD_DIM * jnp.max(jnp.abs(gq)) * jnp.max(jnp.abs(gk))).reshape(1)
    const = lambda i: (0, 0)
    outs = pl.pallas_call(
        _inproj_kernel,
        grid=(n // tm,),
        in_specs=[
            pl.BlockSpec((tm, d), lambda i: (i, 0)),
            pl.BlockSpec((d, in_width), const),
            pl.BlockSpec((1, gate_w), const),
            pl.BlockSpec((1, MXU_DIM), const),
            pl.BlockSpec((1, MXU_DIM), const),
            pl.BlockSpec((MXU_DIM, MXU_DIM), const),
            pl.BlockSpec((tm, MXU_DIM), lambda i: (i % spb, 0)),
            pl.BlockSpec((tm, MXU_DIM), lambda i: (i % spb, 0)),
        ],
        out_specs=[
            pl.BlockSpec((tm, FOURIER_WIDTH), lambda i: (i, 0)),
            pl.BlockSpec((1, N_Q_HEADS, tm, HEAD_DIM), lambda i: (i // spb, 0, i % spb, 0)),
            pl.BlockSpec((1, N_KV_HEADS, tm, HEAD_DIM), lambda i: (i // spb, 0, i % spb, 0)),
            pl.BlockSpec((1, N_KV_HEADS, HEAD_DIM, tm), lambda i: (i // spb, 0, 0, i % spb)),
            pl.BlockSpec((tm, gate_w), lambda i: (i, 0)),
        ],
        out_shape=[
            jax.ShapeDtypeStruct((n, FOURIER_WIDTH), BF16),
            jax.ShapeDtypeStruct((batch, N_Q_HEADS, seq, HEAD_DIM), BF16),
            jax.ShapeDtypeStruct((batch, N_KV_HEADS, seq, HEAD_DIM), BF16),
            jax.ShapeDtypeStruct((batch, N_KV_HEADS, HEAD_DIM, seq), BF16),
            jax.ShapeDtypeStruct((n, gate_w), BF16),
        ],
        compiler_params=_params("parallel"),
        name="inproj",
    )(x2, w_in.astype(BF16), b_gate.astype(F32)[None, :], gq, gk, mean_mat, cos, sin)
    return (*outs, score_bound)


def _fourier_kernel(u_ref, chan_ref, seqm_ref, wp_ref, g_ref, o_ref, ab_ref):
    seq = u_ref.shape[1]

    @pl.when(pl.program_id(1) == 0)
    def _():
        ab = _dot(u_ref[0], chan_ref[...])
        ab_ref[0:seq, :] = ab[:, 0:FOURIER_WIDTH].astype(BF16)
        ab_ref[seq:2 * seq, :] = ab[:, FOURIER_WIDTH:].astype(BF16)

    f = _dot(seqm_ref[...], ab_ref[...]).astype(BF16)
    y = _dot(f, wp_ref[...])
    o_ref[0] = (g_ref[0].astype(F32) * y).astype(BF16)


def _fourier(u3, g3, w_four_proj):
    batch, seq, _ = u3.shape
    d = w_four_proj.shape[1]
    tr = _tile(seq, 512)
    chan, seqm = _dft_tables(seq)
    return pl.pallas_call(
        _fourier_kernel,
        grid=(batch, seq // tr),
        in_specs=[
            pl.BlockSpec((1, seq, FOURIER_WIDTH), lambda b, r: (b, 0, 0)),
            pl.BlockSpec((FOURIER_WIDTH, 2 * FOURIER_WIDTH), lambda b, r: (0, 0)),
            pl.BlockSpec((tr, 2 * seq), lambda b, r: (r, 0)),
            pl.BlockSpec((FOURIER_WIDTH, d), lambda b, r: (0, 0)),
            pl.BlockSpec((1, tr, d), lambda b, r: (b, r, 0)),
        ],
        out_specs=pl.BlockSpec((1, tr, d), lambda b, r: (b, r, 0)),
        out_shape=jax.ShapeDtypeStruct((batch, seq, d), BF16),
        scratch_shapes=[pltpu.VMEM((2 * seq, FOURIER_WIDTH), BF16)],
        compiler_params=_params("parallel", "arbitrary"),
        name="fourier",
    )(u3, chan, seqm, w_four_proj.astype(BF16), g3)


def _attention_kernel(bounded, sb_ref, q_ref, k_ref, vt_ref, o_ref, vone_ref):
    seq = k_ref.shape[2]

    @pl.when(pl.program_id(2) == 0)
    def _():
        vone_ref[0:HEAD_DIM, :] = vt_ref[0, 0]
        vone_ref[HEAD_DIM:, :] = jnp.ones((HEAD_DIM, seq), BF16)

    k = k_ref[0, 0]
    vone = vone_ref[...]
    outs = []
    st_next = _dot_nt(k, q_ref[0, 0])
    for g in range(Q_GROUP):
        st = st_next
        if g + 1 < Q_GROUP:
            st_next = _dot_nt(k, q_ref[0, g + 1])
        if bounded:
            m = sb_ref[0]
        else:
            m = jnp.max(st, axis=0, keepdims=True)
        pt = jnp.exp2(st - m).astype(BF16)
        ol = _dot(vone, pt)
        ot = ol[0:HEAD_DIM, :] / ol[HEAD_DIM:HEAD_DIM + 1, :]
        outs.append(ot.T.astype(BF16))
    o_ref[0] = jnp.concatenate(outs, axis=1)


def _attention(q4, k4, vt4, score_bound):
    batch, _, seq, _ = q4.shape
    tq = _tile(seq, 256)

    def call(bounded):
        return pl.pallas_call(
            functools.partial(_attention_kernel, bounded),
            grid_spec=pltpu.PrefetchScalarGridSpec(
                num_scalar_prefetch=1,
                grid=(batch, N_KV_HEADS, seq // tq),
                in_specs=[
                    pl.BlockSpec((1, Q_GROUP, tq, HEAD_DIM), lambda b, h, i, sb: (b, h, i, 0)),
                    pl.BlockSpec((1, 1, seq, HEAD_DIM), lambda b, h, i, sb: (b, h, 0, 0)),
                    pl.BlockSpec((1, 1, HEAD_DIM, seq), lambda b, h, i, sb: (b, h, 0, 0)),
                ],
                out_specs=pl.BlockSpec((1, tq, Q_GROUP * HEAD_DIM), lambda b, h, i, sb: (b, i, h)),
                scratch_shapes=[pltpu.VMEM((2 * HEAD_DIM, seq), BF16)],
            ),
            out_shape=jax.ShapeDtypeStruct((batch, seq, ATTN_WIDTH), BF16),
            compiler_params=_params("parallel", "parallel", "arbitrary"),
            name="attention_bounded" if bounded else "attention",
        )(score_bound, q4, k4, vt4)

    return lax.cond(2.0 * score_bound[0] <= MAX_EXP2_RANGE,
                    lambda: call(True), lambda: call(False))


def _layer_norm(h, g, b):
    mu = jnp.mean(h, axis=-1, keepdims=True)
    c = h - mu
    var = jnp.mean(c * c, axis=-1, keepdims=True)
    return c * lax.rsqrt(var + LN_EPS) * g + b


def _slab(ref, row):
    return ref.at[pl.ds(pl.multiple_of(row * TOKEN_ROWS, TOKEN_ROWS), TOKEN_ROWS), :]


def _pack_tokens(dst_ref, val):
    t, d = val.shape
    half = d // 2
    assert half == TOKEN_ROWS * LANES
    bits = lax.bitcast_convert_type(val.astype(BF16).astype(F32), jnp.uint32)
    words = (bits[:, :half] >> 16) | bits[:, half:]
    for s in range(TOKEN_ROWS):
        dst_ref[pl.ds(s, t, stride=TOKEN_ROWS), :] = words[:, s * LANES:(s + 1) * LANES]


def _unpack_tokens(src_ref, tok0, t):
    lo, hi = [], []
    for s in range(TOKEN_ROWS):
        w = src_ref[pl.ds(tok0 * TOKEN_ROWS + s, t, stride=TOKEN_ROWS), :]
        lo.append(lax.bitcast_convert_type(w << 16, F32))
        hi.append(lax.bitcast_convert_type(w & jnp.uint32(0xFFFF0000), F32))
    return jnp.concatenate(lo + hi, axis=1)


def _mix_kernel(alpha, o_ref, mf_ref, g_ref, x_ref, wap_ref, wo_ref, lg_ref, lb_ref,
                x1_ref, x1p_ref):
    y = _dot(o_ref[...], wap_ref[...])
    merged = mf_ref[...].astype(F32) + g_ref[...].astype(F32) * y
    mix = _dot(merged.astype(BF16), wo_ref[...])
    x1 = _layer_norm(alpha * x_ref[...] + mix, lg_ref[...], lb_ref[...])
    x1_ref[...] = x1
    _pack_tokens(x1p_ref, x1)


def _mix(o2, mf2, g2, x2, w_attn_proj, w_o, ln_g, ln_b, alpha):
    n, d = x2.shape
    assert d == 2 * TOKEN_ROWS * LANES
    tm = _tile(n, 512)
    const = lambda i: (0, 0)
    return pl.pallas_call(
        functools.partial(_mix_kernel, alpha),
        grid=(n // tm,),
        in_specs=[
            pl.BlockSpec((tm, ATTN_WIDTH), lambda i: (i, 0)),
            pl.BlockSpec((tm, d), lambda i: (i, 0)),
            pl.BlockSpec((tm, d), lambda i: (i, 1)),
            pl.BlockSpec((tm, d), lambda i: (i, 0)),
            pl.BlockSpec((ATTN_WIDTH, d), const),
            pl.BlockSpec((d, d), const),
            pl.BlockSpec((1, d), const),
            pl.BlockSpec((1, d), const),
        ],
        out_specs=[
            pl.BlockSpec((tm, d), lambda i: (i, 0)),
            pl.BlockSpec((tm * TOKEN_ROWS, LANES), lambda i: (i, 0)),
        ],
        out_shape=[
            jax.ShapeDtypeStruct((n, d), F32),
            jax.ShapeDtypeStruct((n * TOKEN_ROWS, LANES), jnp.uint32),
        ],
        compiler_params=_params("parallel"),
        name="mix",
    )(o2, mf2, g2, x2, w_attn_proj.astype(BF16), w_o.astype(BF16),
      ln_g.astype(F32)[None, :], ln_b.astype(F32)[None, :])


def _route_kernel(x_ref, wh_ref, wl_ref, eb_ref, tri_ref,
                  eidx_ref, rank_ref, w_ref, cnt_ref, carry_ref):
    tm = x_ref.shape[0]

    @pl.when(pl.program_id(0) == 0)
    def _():
        carry_ref[...] = jnp.zeros_like(carry_ref)

    x = x_ref[...]
    xh = x.astype(BF16)
    xl = (x - xh.astype(F32)).astype(BF16)
    wh = wh_ref[...]
    logits = _dot_nt(wh, xh) + _dot_nt(wh, xl) + _dot_nt(wl_ref[...], xh)
    scores = _sigmoid(logits)
    biased = scores + eb_ref[:, 0:1]
    neg = -jnp.inf

    sub_iota = lax.broadcasted_iota(jnp.int32, (GROUP_SIZE, tm), 0).astype(F32)
    gs = []
    for g in range(N_EXPERT_GROUPS):
        blk = biased[g * GROUP_SIZE:(g + 1) * GROUP_SIZE, :]
        m1 = jnp.max(blk, axis=0, keepdims=True)
        a1 = jnp.min(jnp.where(blk == m1, sub_iota, float(GROUP_SIZE)), axis=0, keepdims=True)
        m2 = jnp.max(jnp.where(sub_iota == a1, neg, blk), axis=0, keepdims=True)
        gs.append(m1 + m2)

    masked = []
    for g in range(N_EXPERT_GROUPS):
        beat = jnp.zeros((1, tm), F32)
        for h in range(N_EXPERT_GROUPS):
            if h == g:
                continue
            wins = (gs[h] >= gs[g]) if h < g else (gs[h] > gs[g])
            beat = beat + jnp.where(wins, 1.0, 0.0)
        keep = beat < float(TOPK_GROUPS)
        blk = biased[g * GROUP_SIZE:(g + 1) * GROUP_SIZE, :]
        masked.append(jnp.where(keep, blk, neg))
    masked = jnp.concatenate(masked, axis=0)

    e_iota = lax.broadcasted_iota(jnp.int32, (N_EXPERTS, tm), 0).astype(F32)
    sel = jnp.zeros((N_EXPERTS, tm), F32)
    idxs, ws = [], []
    for _ in range(TOP_K):
        mx = jnp.max(masked, axis=0, keepdims=True)
        idx = jnp.min(jnp.where(masked == mx, e_iota, float(N_EXPERTS)), axis=0, keepdims=True)
        hit = e_iota == idx
        masked = jnp.where(hit, neg, masked)
        sel = jnp.where(hit, 1.0, sel)
        idxs.append(idx)
        ws.append(jnp.sum(jnp.where(hit, scores, 0.0), axis=0, keepdims=True))

    carry = carry_ref[...]
    selb = sel.astype(BF16)
    prefix = _dot(selb, tri_ref[...])
    rank_all = prefix + jnp.concatenate([carry] * (tm // LANES), axis=1)
    total = carry + _dot(selb, jnp.ones((tm, LANES), BF16))
    carry_ref[...] = total
    cnt_ref[...] = total

    wsum = ws[0]
    for j in range(1, TOP_K):
        wsum = wsum + ws[j]
    for j in range(TOP_K):
        eidx_ref[j:j + 1, :] = idxs[j].astype(jnp.int32)
        r = jnp.sum(jnp.where(e_iota == idxs[j], rank_all, 0.0), axis=0, keepdims=True)
        rank_ref[j:j + 1, :] = r.astype(jnp.int32)
        w_ref[j:j + 1, :] = ws[j] / wsum * ROUTED_SCALE


def _route(x1, w_router, e_bias):
    n, d = x1.shape
    tm = _tile(n, 512)
    wt = w_router.astype(F32).T
    wh = wt.astype(BF16)
    wl = (wt - wh.astype(F32)).astype(BF16)
    eb = jnp.broadcast_to(e_bias.astype(F32)[:, None], (N_EXPERTS, LANES))
    tri = jnp.asarray(np.triu(np.ones((tm, tm)), k=1), BF16)
    const = lambda i: (0, 0)
    return pl.pallas_call(
        _route_kernel,
        grid=(n // tm,),
        in_specs=[
            pl.BlockSpec((tm, d), lambda i: (i, 0)),
            pl.BlockSpec((N_EXPERTS, d), const),
            pl.BlockSpec((N_EXPERTS, d), const),
            pl.BlockSpec((N_EXPERTS, LANES), const),
            pl.BlockSpec((tm, tm), const),
        ],
        out_specs=[
            pl.BlockSpec((TOP_K, tm), lambda i: (0, i)),
            pl.BlockSpec((TOP_K, tm), lambda i: (0, i)),
            pl.BlockSpec((TOP_K, tm), lambda i: (0, i)),
            pl.BlockSpec((N_EXPERTS, LANES), const),
        ],
        out_shape=[
            jax.ShapeDtypeStruct((TOP_K, n), jnp.int32),
            jax.ShapeDtypeStruct((TOP_K, n), jnp.int32),
            jax.ShapeDtypeStruct((TOP_K, n), F32),
            jax.ShapeDtypeStruct((N_EXPERTS, LANES), F32),
        ],
        scratch_shapes=[pltpu.VMEM((N_EXPERTS, LANES), F32)],
        compiler_params=_params("arbitrary"),
        name="route",
    )(x1, wh, wl, eb, tri)


def _dest_kernel(eidx_ref, rank_ref, ps_ref, dest_ref):
    tm = eidx_ref.shape[1]
    e_iota = lax.broadcasted_iota(jnp.int32, (N_EXPERTS, tm), 0)
    ps = jnp.concatenate([ps_ref[...]] * (tm // LANES), axis=1)
    for j in range(TOP_K):
        hit = e_iota == eidx_ref[j:j + 1, :]
        start = jnp.sum(jnp.where(hit, ps, 0.0), axis=0, keepdims=True)
        dest_ref[j:j + 1, :] = start.astype(jnp.int32) + rank_ref[j:j + 1, :]


def _dest(eidx, rank, pad_start):
    n = eidx.shape[1]
    tm = _tile(n, 512)
    ps = jnp.broadcast_to(pad_start.astype(F32)[:, None], (N_EXPERTS, LANES))
    return pl.pallas_call(
        _dest_kernel,
        grid=(n // tm,),
        in_specs=[
            pl.BlockSpec((TOP_K, tm), lambda i: (0, i)),
            pl.BlockSpec((TOP_K, tm), lambda i: (0, i)),
            pl.BlockSpec((N_EXPERTS, LANES), lambda i: (0, 0)),
        ],
        out_specs=pl.BlockSpec((TOP_K, tm), lambda i: (0, i)),
        out_shape=jax.ShapeDtypeStruct((TOP_K, n), jnp.int32),
        compiler_params=_params("parallel"),
        name="dest",
    )(eidx, rank, ps)


def _dispatch_kernel(tail_ref, dest_ref, x_ref, xs_ref, zero_ref, sem):
    tm = dest_ref.shape[1]

    @pl.when(pl.program_id(0) == 0)
    def _():
        zero_ref[...] = jnp.zeros_like(zero_ref)

        def tail_copy(e):
            rows = EXPERT_BLOCK * TOKEN_ROWS
            row0 = pl.multiple_of(tail_ref[e] * rows, rows)
            return pltpu.make_async_copy(zero_ref, xs_ref.at[pl.ds(row0, rows), :], sem)

        def zstart(e, c):
            @pl.when(tail_ref[e] >= 0)
            def _():
                tail_copy(e).start()
            return c

        def zwait(e, c):
            @pl.when(tail_ref[e] >= 0)
            def _():
                tail_copy(e).wait()
            return c

        lax.fori_loop(0, N_EXPERTS, zstart, 0)
        lax.fori_loop(0, N_EXPERTS, zwait, 0)

    def row_copy(j, t):
        return pltpu.make_async_copy(_slab(x_ref, t), _slab(xs_ref, dest_ref[j, t]), sem)

    def issue(t, c):
        for j in range(TOP_K):
            row_copy(j, t).start(priority=j % DMA_PRIORITIES)
        return c

    def drain(t, c):
        for j in range(TOP_K):
            row_copy(j, t).wait()
        return c

    lax.fori_loop(0, tm, issue, 0)
    lax.fori_loop(0, tm, drain, 0)


def _dispatch(tail_blk, dest, x1p, n_rows):
    n = dest.shape[1]
    tm = _tile(n, 512)
    return pl.pallas_call(
        _dispatch_kernel,
        grid_spec=pltpu.PrefetchScalarGridSpec(
            num_scalar_prefetch=1,
            grid=(n // tm,),
            in_specs=[
                pl.BlockSpec((TOP_K, tm), lambda i, tb: (0, i), memory_space=pltpu.SMEM),
                pl.BlockSpec((tm * TOKEN_ROWS, LANES), lambda i, tb: (i, 0)),
            ],
            out_specs=pl.BlockSpec(memory_space=pl.ANY),
            scratch_shapes=[pltpu.VMEM((EXPERT_BLOCK * TOKEN_ROWS, LANES), jnp.uint32),
                            pltpu.SemaphoreType.DMA],
        ),
        out_shape=jax.ShapeDtypeStruct((n_rows * TOKEN_ROWS, LANES), jnp.uint32),
        compiler_params=_params("arbitrary"),
        name="dispatch",
    )(tail_blk, dest, x1p)


def _swiglu(xb, w_in, w_down):
    h = _dot(xb, w_in)
    half = h.shape[1] // 2
    g = h[:, :half]
    act = g * _sigmoid(g) * h[:, half:]
    return _dot(act.astype(BF16), w_down)


def _experts_kernel(be_ref, nu_ref, x_ref, wi_ref, wd_ref, o_ref, wib_ref, wdb_ref):
    i = pl.program_id(0)
    used = i < nu_ref[0]
    prev = be_ref[jnp.maximum(i - 1, 0)]
    fresh = jnp.logical_or(i == 0, be_ref[i] != prev)
    blk = o_ref.shape[0] // TOKEN_ROWS

    @pl.when(jnp.logical_and(used, fresh))
    def _():
        wib_ref[...] = wi_ref[...].astype(BF16)
        wdb_ref[...] = wd_ref[...].astype(BF16)

    @pl.when(used)
    def _():
        xb = _unpack_tokens(x_ref, 0, blk).astype(BF16)
        _pack_tokens(o_ref, _swiglu(xb, wib_ref[...], wdb_ref[...]))

    @pl.when(jnp.logical_not(used))
    def _():
        o_ref[...] = jnp.zeros_like(o_ref)


def _experts(blk_e, n_used, xs, w_e_in, w_e_down):
    n_blocks = blk_e.shape[0]
    _, d, h2 = w_e_in.shape
    hdim = w_e_down.shape[1]
    rows = EXPERT_BLOCK * TOKEN_ROWS

    def x_map(i, be, nu):
        return (jnp.minimum(i, nu[0] - 1), 0)

    def w_map(i, be, nu):
        return (be[jnp.minimum(i, nu[0] - 1)], 0, 0)

    return pl.pallas_call(
        _experts_kernel,
        grid_spec=pltpu.PrefetchScalarGridSpec(
            num_scalar_prefetch=2,
            grid=(n_blocks,),
            in_specs=[
                pl.BlockSpec((rows, LANES), x_map),
                pl.BlockSpec((None, d, h2), w_map),
                pl.BlockSpec((None, hdim, d), w_map),
            ],
            out_specs=pl.BlockSpec((rows, LANES), lambda i, be, nu: (i, 0)),
            scratch_shapes=[pltpu.VMEM((d, h2), BF16), pltpu.VMEM((hdim, d), BF16)],
        ),
        out_shape=jax.ShapeDtypeStruct(xs.shape, jnp.uint32),
        compiler_params=_params("arbitrary"),
        name="experts",
    )(blk_e, n_used, xs, w_e_in, w_e_down)


def _combine_kernel(alpha, dest_ref, dnext_ref, x1_ref, wt_ref, wsi_ref, wsd_ref, lg_ref, lb_ref,
                    os_ref, out_ref, buf_ref, sem):
    tm = x1_ref.shape[0]
    i = pl.program_id(0)
    slot = i % 2

    def gather(d_ref, s, start):
        def body(t, c):
            for j in range(TOP_K):
                cp = pltpu.make_async_copy(_slab(os_ref, d_ref[j, t]),
                                           _slab(buf_ref.at[s], j * tm + t), sem.at[s])
                if start:
                    cp.start(priority=j % DMA_PRIORITIES)
                else:
                    cp.wait()
            return c
        lax.fori_loop(0, tm, body, 0)

    @pl.when(i == 0)
    def _():
        gather(dest_ref, slot, True)

    @pl.when(i + 1 < pl.num_programs(0))
    def _():
        gather(dnext_ref, 1 - slot, True)

    x1 = x1_ref[...]
    shared = _swiglu(x1.astype(BF16), wsi_ref[...], wsd_ref[...])
    gather(dest_ref, slot, False)

    rows = buf_ref.at[slot]
    routed = wt_ref[:, 0:1] * _unpack_tokens(rows, 0, tm)
    for j in range(1, TOP_K):
        routed = routed + wt_ref[:, j:j + 1] * _unpack_tokens(rows, j * tm, tm)
    out_ref[...] = _layer_norm(alpha * x1 + routed + shared, lg_ref[...], lb_ref[...])


def _combine(dest, x1, wt, w_sh_in, w_sh_down, ln_g, ln_b, os, alpha):
    n, d = x1.shape
    tm = _tile(n, 256)
    last = n // tm - 1
    const = lambda i: (0, 0)
    return pl.pallas_call(
        functools.partial(_combine_kernel, alpha),
        grid=(n // tm,),
        in_specs=[
            pl.BlockSpec((TOP_K, tm), lambda i: (0, i), memory_space=pltpu.SMEM),
            pl.BlockSpec((TOP_K, tm), lambda i: (0, jnp.minimum(i + 1, last)), memory_space=pltpu.SMEM),
            pl.BlockSpec((tm, d), lambda i: (i, 0)),
            pl.BlockSpec((tm, TOP_K), lambda i: (i, 0)),
            pl.BlockSpec(w_sh_in.shape, const),
            pl.BlockSpec(w_sh_down.shape, const),
            pl.BlockSpec((1, d), const),
            pl.BlockSpec((1, d), const),
            pl.BlockSpec(memory_space=pl.ANY),
        ],
        out_specs=pl.BlockSpec((tm, d), lambda i: (i, 0)),
        out_shape=jax.ShapeDtypeStruct((n, d), F32),
        scratch_shapes=[
            pltpu.VMEM((2, TOP_K * tm * TOKEN_ROWS, LANES), jnp.uint32),
            pltpu.SemaphoreType.DMA((2,)),
        ],
        compiler_params=_params("arbitrary"),
        name="combine",
    )(dest, dest, x1, wt, w_sh_in.astype(BF16), w_sh_down.astype(BF16),
      ln_g.astype(F32)[None, :], ln_b.astype(F32)[None, :], os)


def _block_layout(counts, n_assign):
    n_blocks = (n_assign + N_EXPERTS * (EXPERT_BLOCK - 1) + EXPERT_BLOCK - 1) // EXPERT_BLOCK
    nblk = (counts + EXPERT_BLOCK - 1) // EXPERT_BLOCK
    blk_end = jnp.cumsum(nblk)
    pad_start = (blk_end - nblk) * EXPERT_BLOCK
    blk_e = jnp.sum(blk_end[None, :] <= jnp.arange(n_blocks, dtype=jnp.int32)[:, None], axis=1)
    blk_e = jnp.minimum(blk_e, N_EXPERTS - 1).astype(jnp.int32)
    n_used = blk_end[-1:].astype(jnp.int32)
    tail_blk = jnp.where(nblk > 0, blk_end - 1, -1).astype(jnp.int32)
    return n_blocks, pad_start.astype(jnp.int32), blk_e, n_used, tail_blk


def _layer(x, w_in, b_gate, q_g, k_g, w_four_proj, w_attn_proj, w_o, ln1_g, ln1_b,
           w_router, e_bias, w_e_in, w_e_down, w_sh_in, w_sh_down, ln2_g, ln2_b, alpha):
    batch, seq, d = x.shape
    n = batch * seq
    x2 = x.reshape(n, d)

    u, q4, k4, vt4, gates, score_bound = _inproj(x2, w_in, b_gate, q_g, k_g, batch, seq)
    mf = _fourier(u.reshape(batch, seq, FOURIER_WIDTH), gates.reshape(batch, seq, -1), w_four_proj)
    o = _attention(q4, k4, vt4, score_bound)
    x1, x1p = _mix(o.reshape(n, ATTN_WIDTH), mf.reshape(n, d), gates, x2,
                   w_attn_proj, w_o, ln1_g, ln1_b, alpha)

    eidx, rank, wts, cnt = _route(x1, w_router, e_bias)
    counts = cnt[:, 0].astype(jnp.int32)
    n_blocks, pad_start, blk_e, n_used, tail_blk = _block_layout(counts, n * TOP_K)
    dest = _dest(eidx, rank, pad_start)

    xs = _dispatch(tail_blk, dest, x1p, n_blocks * EXPERT_BLOCK)
    os = _experts(blk_e, n_used, xs, w_e_in, w_e_down)
    out = _combine(dest, x1, wts.T, w_sh_in, w_sh_down, ln2_g, ln2_b, os, alpha)
    return out.reshape(batch, seq, d)


def kernel(x, w_in, b_gate, q_norm_g, k_norm_g, w_four_proj, w_attn_proj, w_o, ln1_g, ln1_b, w_router, e_bias, w_e_in, w_e_down, w_sh_in, w_sh_down, ln2_g, ln2_b):
    depth = w_in.shape[0]
    alpha = (2 * depth) ** 0.25
    for l in range(depth):
        x = _layer(x, w_in[l], b_gate[l], q_norm_g[l], k_norm_g[l], w_four_proj[l],
                   w_attn_proj[l], w_o[l], ln1_g[l], ln1_b[l], w_router[l], e_bias[l],
                   w_e_in[l], w_e_down[l], w_sh_in[l], w_sh_down[l], ln2_g[l], ln2_b[l], alpha)
    return x
```

```python
import functools
import math

import numpy as np
import jax
import jax.numpy as jnp
from jax import lax
from jax.experimental import pallas as pl
from jax.experimental.pallas import tpu as pltpu
from jax.experimental.pallas import tpu_sc as plsc

F32 = jnp.float32
BF16 = jnp.bfloat16

GRID_W = 64
N_FOURIER_GROUPS = 8
FOURIER_GROUP_DIM = 64
FOURIER_WIDTH = N_FOURIER_GROUPS * FOURIER_GROUP_DIM
N_Q_HEADS = 16
N_KV_HEADS = 4
HEAD_DIM = 64
Q_GROUP = N_Q_HEADS // N_KV_HEADS
ATTN_WIDTH = N_Q_HEADS * HEAD_DIM
KV_WIDTH = N_KV_HEADS * HEAD_DIM
ROPE_THETA = 10000.0
QK_EPS = 1e-6
OFF_Q = FOURIER_WIDTH
OFF_K = OFF_Q + ATTN_WIDTH
OFF_V = OFF_K + KV_WIDTH
OFF_G = OFF_V + KV_WIDTH
N_EXPERTS = 256
TOP_K = 8
N_EXPERT_GROUPS = 8
GROUP_SIZE = N_EXPERTS // N_EXPERT_GROUPS
TOPK_GROUPS = 4
ROUTED_SCALE = 2.5
LN_EPS = 1e-5

LANES = 128
SUBLANES = 8
MXU_DIM = 256
VMEM_LIMIT = 56 * 1024 * 1024

MAX_EXP2_RANGE = 100.0
DMA_PRIORITIES = 2

EXPERT_BLOCK = 512
SC_SLOTS = 4
SC_WINDOW = 128
TOKEN_ROWS = 4

NT_DIMS = (((1,), (1,)), ((), ()))


def _dot(a, b):
    return jnp.dot(a, b, preferred_element_type=F32)


def _dot_nt(a, b):
    return lax.dot_general(a, b, NT_DIMS, preferred_element_type=F32)


def _sigmoid(x):
    return 1.0 / (1.0 + jnp.exp(-x))


def _params(*sem):
    return pltpu.CompilerParams(dimension_semantics=sem, vmem_limit_bytes=VMEM_LIMIT)


def _tile(n, pref):
    t = min(n, pref)
    assert n % t == 0, (n, t)
    return t


def _rope_tables(seq):
    lane = np.arange(MXU_DIM)
    d = lane % HEAD_DIM
    sub = d % 32
    j = sub % 16
    t = np.arange(seq)[:, None]
    pos = np.where(d[None, :] < 32, t // GRID_W, t % GRID_W).astype(np.float64)
    freq = ROPE_THETA ** (-(j.astype(np.float64)) / 16.0)
    ang = pos * freq[None, :]
    cos = np.cos(ang)
    sin = np.sin(ang) * np.where(sub < 16, -1.0, 1.0)[None, :]
    return jnp.asarray(cos, F32), jnp.asarray(sin, F32)


def _head_mean_matrix():
    i = np.arange(MXU_DIM)
    m = (i[:, None] // HEAD_DIM == i[None, :] // HEAD_DIM).astype(np.float64) / HEAD_DIM
    return jnp.asarray(m, BF16)


def _dft_tables(seq):
    c = np.arange(FOURIER_GROUP_DIM)
    ang_c = 2.0 * np.pi * ((c[:, None] * c[None, :]) % FOURIER_GROUP_DIM) / FOURIER_GROUP_DIM
    sc = 1.0 / math.sqrt(FOURIER_GROUP_DIM)
    eye = np.eye(N_FOURIER_GROUPS)
    cc = np.kron(eye, np.cos(ang_c) * sc)
    ss = np.kron(eye, np.sin(ang_c) * sc)
    chan = np.concatenate([cc, ss], axis=1)
    s = np.arange(seq)
    ang_s = 2.0 * np.pi * ((s[:, None] * s[None, :]) % seq) / seq
    ssc = 1.0 / math.sqrt(seq)
    seqm = np.concatenate([np.cos(ang_s) * ssc, -np.sin(ang_s) * ssc], axis=1)
    return jnp.asarray(chan, BF16), jnp.asarray(seqm, BF16)


def _norm_rope(z, gain, mean_mat, cos, sin, lo_mask):
    ms = _dot((z * z).astype(BF16), mean_mat)
    y = z * lax.rsqrt(ms + QK_EPS) * gain
    outs = []
    for c in range(MXU_DIM // LANES):
        yc = y[:, c * LANES:(c + 1) * LANES]
        up = pltpu.roll(yc, LANES - 16, 1)
        dn = pltpu.roll(yc, 16, 1)
        partner = jnp.where(lo_mask, up, dn)
        sl = slice(c * LANES, (c + 1) * LANES)
        outs.append(yc * cos[:, sl] + partner * sin[:, sl])
    return jnp.concatenate(outs, axis=1)


def _inproj_kernel(x_ref, w_ref, bg_ref, gq_ref, gk_ref, mm_ref, cos_ref, sin_ref,
                   u_ref, q_ref, k_ref, v_ref, g_ref):
    xb = x_ref[...].astype(BF16)
    u_ref[...] = _dot(xb, w_ref[:, 0:OFF_Q]).astype(BF16)

    lane = lax.broadcasted_iota(jnp.int32, (1, LANES), 1)
    lo_mask = (lane & 16) == 0
    mean_mat = mm_ref[...]
    cos = cos_ref[...]
    sin = sin_ref[...]

    for c in range(ATTN_WIDTH // MXU_DIM):
        z = _dot(xb, w_ref[:, OFF_Q + c * MXU_DIM:OFF_Q + (c + 1) * MXU_DIM])
        q = _norm_rope(z, gq_ref[...], mean_mat, cos, sin, lo_mask).astype(BF16)
        for j in range(MXU_DIM // HEAD_DIM):
            q_ref[0, c * (MXU_DIM // HEAD_DIM) + j] = q[:, j * HEAD_DIM:(j + 1) * HEAD_DIM]

    z = _dot(xb, w_ref[:, OFF_K:OFF_V])
    k = _norm_rope(z, gk_ref[...], mean_mat, cos, sin, lo_mask).astype(BF16)
    vt = _dot(xb, w_ref[:, OFF_V:OFF_G]).T.astype(BF16)
    for j in range(N_KV_HEADS):
        k_ref[0, j] = k[:, j * HEAD_DIM:(j + 1) * HEAD_DIM]
        v_ref[0, j] = vt[j * HEAD_DIM:(j + 1) * HEAD_DIM, :]

    gw = 512
    for c in range((w_ref.shape[1] - OFF_G) // gw):
        sl = slice(OFF_G + c * gw, OFF_G + (c + 1) * gw)
        z = _dot(xb, w_ref[:, sl]) + bg_ref[:, c * gw:(c + 1) * gw]
        g_ref[:, c * gw:(c + 1) * gw] = _sigmoid(z).astype(BF16)


def _inproj(x2, w_in, b_gate, q_g, k_g, batch, seq):
    n, d = x2.shape
    tm = _tile(seq, 512)
    spb = seq // tm
    in_width = w_in.shape[1]
    gate_w = in_width - OFF_G
    cos, sin = _rope_tables(seq)
    mean_mat = _head_mean_matrix()
    scale = HEAD_DIM ** -0.5 * math.log2(math.e)
    gq =jnp.tile(q_g.astype(F32) * scale, MXU_DIM // HEAD_DIM)[None, :]
    gk = jnp.tile(k_g.astype(F32), MXU_DIM // HEAD_DIM)[None, :]
    score_bound = (HEAD_DIM * jnp.max(jnp.abs(gq)) * jnp.max(jnp.abs(gk))).reshape(1)
    const = lambda i: (0, 0)
    outs = pl.pallas_call(
        _inproj_kernel,
        grid=(n // tm,),
        in_specs=[
            pl.BlockSpec((tm, d), lambda i: (i, 0)),
            pl.BlockSpec((d, in_width), const),
            pl.BlockSpec((1, gate_w), const),
            pl.BlockSpec((1, MXU_DIM), const),
            pl.BlockSpec((1, MXU_DIM), const),
            pl.BlockSpec((MXU_DIM, MXU_DIM), const),
            pl.BlockSpec((tm, MXU_DIM), lambda i: (i % spb, 0)),
            pl.BlockSpec((tm, MXU_DIM), lambda i: (i % spb, 0)),
        ],
        out_specs=[
            pl.BlockSpec((tm, FOURIER_WIDTH), lambda i: (i, 0)),
            pl.BlockSpec((1, N_Q_HEADS, tm, HEAD_DIM), lambda i: (i // spb, 0, i % spb, 0)),
            pl.BlockSpec((1, N_KV_HEADS, tm, HEAD_DIM), lambda i: (i // spb, 0, i % spb, 0)),
            pl.BlockSpec((1, N_KV_HEADS, HEAD_DIM, tm), lambda i: (i // spb, 0, 0, i % spb)),
            pl.BlockSpec((tm, gate_w), lambda i: (i, 0)),
        ],
        out_shape=[
            jax.ShapeDtypeStruct((n, FOURIER_WIDTH), BF16),
            jax.ShapeDtypeStruct((batch, N_Q_HEADS, seq, HEAD_DIM), BF16),
            jax.ShapeDtypeStruct((batch, N_KV_HEADS, seq, HEAD_DIM), BF16),
            jax.ShapeDtypeStruct((batch, N_KV_HEADS, HEAD_DIM, seq), BF16),
            jax.ShapeDtypeStruct((n, gate_w), BF16),
        ],
        compiler_params=_params("parallel"),
        name="inproj",
    )(x2, w_in.astype(BF16), b_gate.astype(F32)[None, :], gq, gk, mean_mat, cos, sin)
    return (*outs, score_bound)


def _fourier_kernel(u_ref, chan_ref, seqm_ref, wp_ref, g_ref, o_ref, ab_ref):
    seq = u_ref.shape[1]

    @pl.when(pl.program_id(1) == 0)
    def _():
        ab = _dot(u_ref[0], chan_ref[...])
        ab_ref[0:seq, :] = ab[:, 0:FOURIER_WIDTH].astype(BF16)
        ab_ref[seq:2 * seq, :] = ab[:, FOURIER_WIDTH:].astype(BF16)

    f = _dot(seqm_ref[...], ab_ref[...]).astype(BF16)
    y = _dot(f, wp_ref[...])
    o_ref[0] = (g_ref[0].astype(F32) * y).astype(BF16)


def _fourier(u3, g3, w_four_proj):
    batch, seq, _ = u3.shape
    d = w_four_proj.shape[1]
    tr = _tile(seq, 512)
    chan, seqm = _dft_tables(seq)
    return pl.pallas_call(
        _fourier_kernel,
        grid=(batch, seq // tr),
        in_specs=[
            pl.BlockSpec((1, seq, FOURIER_WIDTH), lambda b, r: (b, 0, 0)),
            pl.BlockSpec((FOURIER_WIDTH, 2 * FOURIER_WIDTH), lambda b, r: (0, 0)),
            pl.BlockSpec((tr, 2 * seq), lambda b, r: (r, 0)),
            pl.BlockSpec((FOURIER_WIDTH, d), lambda b, r: (0, 0)),
            pl.BlockSpec((1, tr, d), lambda b, r: (b, r, 0)),
        ],
        out_specs=pl.BlockSpec((1, tr, d), lambda b, r: (b, r, 0)),
        out_shape=jax.ShapeDtypeStruct((batch, seq, d), BF16),
        scratch_shapes=[pltpu.VMEM((2 * seq, FOURIER_WIDTH), BF16)],
        compiler_params=_params("parallel", "arbitrary"),
        name="fourier",
    )(u3, chan, seqm, w_four_proj.astype(BF16), g3)


def _attention_kernel(bounded, sb_ref, q_ref, k_ref, vt_ref, o_ref, vone_ref):
    seq = k_ref.shape[2]

    @pl.when(pl.program_id(2) == 0)
    def _():
        vone_ref[0:HEAD_DIM, :] = vt_ref[0, 0]
        vone_ref[HEAD_DIM:, :] = jnp.ones((HEAD_DIM, seq), BF16)

    k = k_ref[0, 0]
    vone = vone_ref[...]
    outs = []
    st_next = _dot_nt(k, q_ref[0, 0])
    for g in range(Q_GROUP):
        st = st_next
        if g + 1 < Q_GROUP:
            st_next = _dot_nt(k, q_ref[0, g + 1])
        if bounded:
            m = sb_ref[0]
        else:
            m = jnp.max(st, axis=0, keepdims=True)
        pt = jnp.exp2(st - m).astype(BF16)
        ol = _dot(vone, pt)
        ot = ol[0:HEAD_DIM, :] / ol[HEAD_DIM:HEAD_DIM + 1, :]
        outs.append(ot.T.astype(BF16))
    o_ref[0] = jnp.concatenate(outs, axis=1)


def _attention(q4, k4, vt4, score_bound):
    batch, _, seq, _ = q4.shape
    tq = _tile(seq, 256)

    def call(bounded):
        return pl.pallas_call(
            functools.partial(_attention_kernel, bounded),
            grid_spec=pltpu.PrefetchScalarGridSpec(
                num_scalar_prefetch=1,
                grid=(batch, N_KV_HEADS, seq // tq),
                in_specs=[
                    pl.BlockSpec((1, Q_GROUP, tq, HEAD_DIM), lambda b, h, i, sb: (b, h, i, 0)),
                    pl.BlockSpec((1, 1, seq, HEAD_DIM), lambda b, h, i, sb: (b, h, 0, 0)),
                    pl.BlockSpec((1, 1, HEAD_DIM, seq), lambda b, h, i, sb: (b, h, 0, 0)),
                ],
                out_specs=pl.BlockSpec((1, tq, Q_GROUP * HEAD_DIM), lambda b, h, i, sb: (b, i, h)),
                scratch_shapes=[pltpu.VMEM((2 * HEAD_DIM, seq), BF16)],
            ),
            out_shape=jax.ShapeDtypeStruct((batch, seq, ATTN_WIDTH), BF16),
            compiler_params=_params("parallel", "parallel", "arbitrary"),
            name="attention_bounded" if bounded else "attention",
        )(score_bound, q4, k4, vt4)

    return lax.cond(2.0 * score_bound[0] <= MAX_EXP2_RANGE,
                    lambda: call(True), lambda: call(False))


def _layer_norm(h, g, b):
    mu = jnp.mean(h, axis=-1, keepdims=True)
    c = h - mu
    var = jnp.mean(c * c, axis=-1, keepdims=True)
    return c * lax.rsqrt(var + LN_EPS) * g + b


def _slab(ref, row):
    return ref.at[pl.ds(pl.multiple_of(row * TOKEN_ROWS, TOKEN_ROWS), TOKEN_ROWS), :]


def _pack_tokens(dst_ref, val):
    t, d = val.shape
    half = d // 2
    assert half == TOKEN_ROWS * LANES
    bits = lax.bitcast_convert_type(val.astype(BF16).astype(F32), jnp.uint32)
    words = (bits[:, :half] >> 16) | bits[:, half:]
    for s in range(TOKEN_ROWS):
        dst_ref[pl.ds(s, t, stride=TOKEN_ROWS), :] = words[:, s * LANES:(s + 1) * LANES]


def _unpack_tokens(src_ref, tok0, t):
    lo, hi = [], []
    for s in range(TOKEN_ROWS):
        w = src_ref[pl.ds(tok0 * TOKEN_ROWS + s, t, stride=TOKEN_ROWS), :]
        lo.append(lax.bitcast_convert_type(w << 16, F32))
        hi.append(lax.bitcast_convert_type(w & jnp.uint32(0xFFFF0000), F32))
    return jnp.concatenate(lo + hi, axis=1)


def _mix_kernel(alpha, o_ref, mf_ref, g_ref, x_ref, wap_ref, wo_ref, lg_ref, lb_ref,
                x1_ref, x1p_ref):
    y = _dot(o_ref[...], wap_ref[...])
    merged = mf_ref[...].astype(F32) + g_ref[...].astype(F32) * y
    mix = _dot(merged.astype(BF16), wo_ref[...])
    x1 = _layer_norm(alpha * x_ref[...] + mix, lg_ref[...], lb_ref[...])
    x1_ref[...] = x1
    _pack_tokens(x1p_ref, x1)


def _mix(o2, mf2, g2, x2, w_attn_proj, w_o, ln_g, ln_b, alpha):
    n, d = x2.shape
    assert d == 2 * TOKEN_ROWS * LANES
    tm = _tile(n, 512)
    const = lambda i: (0, 0)
    return pl.pallas_call(
        functools.partial(_mix_kernel, alpha),
        grid=(n // tm,),
        in_specs=[
            pl.BlockSpec((tm, ATTN_WIDTH), lambda i: (i, 0)),
            pl.BlockSpec((tm, d), lambda i: (i, 0)),
            pl.BlockSpec((tm, d), lambda i: (i, 1)),
            pl.BlockSpec((tm, d), lambda i: (i, 0)),
            pl.BlockSpec((ATTN_WIDTH, d), const),
            pl.BlockSpec((d, d), const),
            pl.BlockSpec((1, d), const),
            pl.BlockSpec((1, d), const),
        ],
        out_specs=[
            pl.BlockSpec((tm, d), lambda i: (i, 0)),
            pl.BlockSpec((tm * TOKEN_ROWS, LANES), lambda i: (i, 0)),
        ],
        out_shape=[
            jax.ShapeDtypeStruct((n, d), F32),
            jax.ShapeDtypeStruct((n * TOKEN_ROWS, LANES), jnp.uint32),
        ],
        compiler_params=_params("parallel"),
        name="mix",
    )(o2, mf2, g2, x2, w_attn_proj.astype(BF16), w_o.astype(BF16),
      ln_g.astype(F32)[None, :], ln_b.astype(F32)[None, :])


def _route_kernel(x_ref, wh_ref, wl_ref, eb_ref, tri_ref,
                  eidx_ref, rank_ref, w_ref, cnt_ref, carry_ref):
    tm = x_ref.shape[0]

    @pl.when(pl.program_id(0) == 0)
    def _():
        carry_ref[...] = jnp.zeros_like(carry_ref)

    x = x_ref[...]
    xh = x.astype(BF16)
    xl = (x - xh.astype(F32)).astype(BF16)
    wh = wh_ref[...]
    logits = _dot_nt(wh, xh) + _dot_nt(wh, xl) + _dot_nt(wl_ref[...], xh)
    scores = _sigmoid(logits)
    biased = scores + eb_ref[:, 0:1]
    neg = -jnp.inf

    sub_iota = lax.broadcasted_iota(jnp.int32, (GROUP_SIZE, tm), 0).astype(F32)
    gs = []
    for g in range(N_EXPERT_GROUPS):
        blk = biased[g * GROUP_SIZE:(g + 1) * GROUP_SIZE, :]
        m1 = jnp.max(blk, axis=0, keepdims=True)
        a1 = jnp.min(jnp.where(blk == m1, sub_iota, float(GROUP_SIZE)), axis=0, keepdims=True)
        m2 = jnp.max(jnp.where(sub_iota == a1, neg, blk), axis=0, keepdims=True)
        gs.append(m1 + m2)

    masked = []
    for g in range(N_EXPERT_GROUPS):
        beat = jnp.zeros((1, tm), F32)
        for h in range(N_EXPERT_GROUPS):
            if h == g:
                continue
            wins = (gs[h] >= gs[g]) if h < g else (gs[h] > gs[g])
            beat = beat + jnp.where(wins, 1.0, 0.0)
        keep = beat < float(TOPK_GROUPS)
        blk = biased[g * GROUP_SIZE:(g + 1) * GROUP_SIZE, :]
        masked.append(jnp.where(keep, blk, neg))
    masked = jnp.concatenate(masked, axis=0)

    e_iota = lax.broadcasted_iota(jnp.int32, (N_EXPERTS, tm), 0).astype(F32)
    sel = jnp.zeros((N_EXPERTS, tm), F32)
    idxs, ws = [], []
    for _ in range(TOP_K):
        mx = jnp.max(masked, axis=0, keepdims=True)
        idx = jnp.min(jnp.where(masked == mx, e_iota, float(N_EXPERTS)), axis=0, keepdims=True)
        hit = e_iota == idx
        masked = jnp.where(hit, neg, masked)
        sel = jnp.where(hit, 1.0, sel)
        idxs.append(idx)
        ws.append(jnp.sum(jnp.where(hit, scores, 0.0), axis=0, keepdims=True))

    carry = carry_ref[...]
    selb = sel.astype(BF16)
    prefix = _dot(selb, tri_ref[...])
    rank_all = prefix + jnp.concatenate([carry] * (tm // LANES), axis=1)
    total = carry + _dot(selb, jnp.ones((tm, LANES), BF16))
    carry_ref[...] = total
    cnt_ref[...] = total

    wsum = ws[0]
    for j in range(1, TOP_K):
        wsum = wsum + ws[j]
    for j in range(TOP_K):
        eidx_ref[j:j + 1, :] = idxs[j].astype(jnp.int32)
        r = jnp.sum(jnp.where(e_iota == idxs[j], rank_all, 0.0), axis=0, keepdims=True)
        rank_ref[j:j + 1, :] = r.astype(jnp.int32)
        w_ref[j:j + 1, :] = ws[j] / wsum * ROUTED_SCALE


def _route(x1, w_router, e_bias):
    n, d = x1.shape
    tm = _tile(n, 512)
    wt = w_router.astype(F32).T
    wh = wt.astype(BF16)
    wl = (wt - wh.astype(F32)).astype(BF16)
    eb = jnp.broadcast_to(e_bias.astype(F32)[:, None], (N_EXPERTS, LANES))
    tri = jnp.asarray(np.triu(np.ones((tm, tm)), k=1), BF16)
    const = lambda i: (0, 0)
    return pl.pallas_call(
        _route_kernel,
        grid=(n // tm,),
        in_specs=[
            pl.BlockSpec((tm, d), lambda i: (i, 0)),
            pl.BlockSpec((N_EXPERTS, d), const),
            pl.BlockSpec((N_EXPERTS, d), const),
            pl.BlockSpec((N_EXPERTS, LANES), const),
            pl.BlockSpec((tm, tm), const),
        ],
        out_specs=[
            pl.BlockSpec((TOP_K, tm), lambda i: (0, i)),
            pl.BlockSpec((TOP_K, tm), lambda i: (0, i)),
            pl.BlockSpec((TOP_K, tm), lambda i: (0, i)),
            pl.BlockSpec((N_EXPERTS, LANES), const),
        ],
        out_shape=[
            jax.ShapeDtypeStruct((TOP_K, n), jnp.int32),
            jax.ShapeDtypeStruct((TOP_K, n), jnp.int32),
            jax.ShapeDtypeStruct((TOP_K, n), F32),
            jax.ShapeDtypeStruct((N_EXPERTS, LANES), F32),
        ],
        scratch_shapes=[pltpu.VMEM((N_EXPERTS, LANES), F32)],
        compiler_params=_params("arbitrary"),
        name="route",
    )(x1, wh, wl, eb, tri)


def _dest_kernel(eidx_ref, rank_ref, ps_ref, dest_ref):
    tm = eidx_ref.shape[1]
    e_iota = lax.broadcasted_iota(jnp.int32, (N_EXPERTS, tm), 0)
    ps = jnp.concatenate([ps_ref[...]] * (tm // LANES), axis=1)
    for j in range(TOP_K):
        hit = e_iota == eidx_ref[j:j + 1, :]
        start = jnp.sum(jnp.where(hit, ps, 0.0), axis=0, keepdims=True)
        dest_ref[j:j + 1, :] = start.astype(jnp.int32) + rank_ref[j:j + 1, :]


def _dest(eidx, rank, pad_start):
    n = eidx.shape[1]
    tm = _tile(n, 512)
    ps = jnp.broadcast_to(pad_start.astype(F32)[:, None], (N_EXPERTS, LANES))
    return pl.pallas_call(
        _dest_kernel,
        grid=(n // tm,),
        in_specs=[
            pl.BlockSpec((TOP_K, tm), lambda i: (0, i)),
            pl.BlockSpec((TOP_K, tm), lambda i: (0, i)),
            pl.BlockSpec((N_EXPERTS, LANES), lambda i: (0, 0)),
        ],
        out_specs=pl.BlockSpec((TOP_K, tm), lambda i: (0, i)),
        out_shape=jax.ShapeDtypeStruct((TOP_K, n), jnp.int32),
        compiler_params=_params("parallel"),
        name="dest",
    )(eidx, rank, ps)


def _dispatch_kernel(tail_ref, dest_ref, x_ref, xs_ref, zero_ref, sem):
    tm = dest_ref.shape[1]

    @pl.when(pl.program_id(0) == 0)
    def _():
        zero_ref[...] = jnp.zeros_like(zero_ref)

        def tail_copy(e):
            rows = EXPERT_BLOCK * TOKEN_ROWS
            row0 = pl.multiple_of(tail_ref[e] * rows, rows)
            return pltpu.make_async_copy(zero_ref, xs_ref.at[pl.ds(row0, rows), :], sem)

        def zstart(e, c):
            @pl.when(tail_ref[e] >= 0)
            def _():
                tail_copy(e).start()
            return c

        def zwait(e, c):
            @pl.when(tail_ref[e] >= 0)
            def _():
                tail_copy(e).wait()
            return c

        lax.fori_loop(0, N_EXPERTS, zstart, 0)
        lax.fori_loop(0, N_EXPERTS, zwait, 0)

    def row_copy(j, t):
        return pltpu.make_async_copy(_slab(x_ref, t), _slab(xs_ref, dest_ref[j, t]), sem)

    def issue(t, c):
        for j in range(TOP_K):
            row_copy(j, t).start(priority=j % DMA_PRIORITIES)
        return c

    def drain(t, c):
        for j in range(TOP_K):
            row_copy(j, t).wait()
        return c

    lax.fori_loop(0, tm, issue, 0)
    lax.fori_loop(0, tm, drain, 0)


def _dispatch(tail_blk, dest, x1p, n_rows):
    n = dest.shape[1]
    tm = _tile(n, 512)
    return pl.pallas_call(
        _dispatch_kernel,
        grid_spec=pltpu.PrefetchScalarGridSpec(
            num_scalar_prefetch=1,
            grid=(n // tm,),
            in_specs=[
                pl.BlockSpec((TOP_K, tm), lambda i, tb: (0, i), memory_space=pltpu.SMEM),
                pl.BlockSpec((tm * TOKEN_ROWS, LANES), lambda i, tb: (i, 0)),
            ],
            out_specs=pl.BlockSpec(memory_space=pl.ANY),
            scratch_shapes=[pltpu.VMEM((EXPERT_BLOCK * TOKEN_ROWS, LANES), jnp.uint32),
                            pltpu.SemaphoreType.DMA],
        ),
        out_shape=jax.ShapeDtypeStruct((n_rows * TOKEN_ROWS, LANES), jnp.uint32),
        compiler_params=_params("arbitrary"),
        name="dispatch",
    )(tail_blk, dest, x1p)


def _swiglu(xb, w_in, w_down):
    h = _dot(xb, w_in)
    half = h.shape[1] // 2
    g = h[:, :half]
    act = g * _sigmoid(g) * h[:, half:]
    return _dot(act.astype(BF16), w_down)


def _experts_kernel(be_ref, nu_ref, x_ref, wi_ref, wd_ref, o_ref, wib_ref, wdb_ref):
    i = pl.program_id(0)
    used = i < nu_ref[0]
    prev = be_ref[jnp.maximum(i - 1, 0)]
    fresh = jnp.logical_or(i == 0, be_ref[i] != prev)
    blk = o_ref.shape[0] // TOKEN_ROWS

    @pl.when(jnp.logical_and(used, fresh))
    def _():
        wib_ref[...] = wi_ref[...].astype(BF16)
        wdb_ref[...] = wd_ref[...].astype(BF16)

    @pl.when(used)
    def _():
        xb = _unpack_tokens(x_ref, 0, blk).astype(BF16)
        _pack_tokens(o_ref, _swiglu(xb, wib_ref[...], wdb_ref[...]))

    @pl.when(jnp.logical_not(used))
    def _():
        o_ref[...] = jnp.zeros_like(o_ref)


def _experts(blk_e, n_used, xs, w_e_in, w_e_down):
    n_blocks = blk_e.shape[0]
    _, d, h2 = w_e_in.shape
    hdim = w_e_down.shape[1]
    rows = EXPERT_BLOCK * TOKEN_ROWS

    def x_map(i, be, nu):
        return (jnp.minimum(i, nu[0] - 1), 0)

    def w_map(i, be, nu):
        return (be[jnp.minimum(i, nu[0] - 1)], 0, 0)

    return pl.pallas_call(
        _experts_kernel,
        grid_spec=pltpu.PrefetchScalarGridSpec(
            num_scalar_prefetch=2,
            grid=(n_blocks,),
            in_specs=[
                pl.BlockSpec((rows, LANES), x_map),
                pl.BlockSpec((None, d, h2), w_map),
                pl.BlockSpec((None, hdim, d), w_map),
            ],
            out_specs=pl.BlockSpec((rows, LANES), lambda i, be, nu: (i, 0)),
            scratch_shapes=[pltpu.VMEM((d, h2), BF16), pltpu.VMEM((hdim, d), BF16)],
        ),
        out_shape=jax.ShapeDtypeStruct(xs.shape, jnp.uint32),
        compiler_params=_params("arbitrary"),
        name="experts",
    )(blk_e, n_used, xs, w_e_in, w_e_down)


def _sc_gather_rows(table, idx):
    ni = idx.shape[0]
    mesh = plsc.VectorSubcoreMesh(core_axis_name="core", subcore_axis_name="subcore")

    @functools.partial(pl.kernel, mesh=mesh, scratch_types=[],
                       out_type=jax.ShapeDtypeStruct((ni, LANES), table.dtype))
    def gather(table_hbm, idx_hbm, out_hbm):
        def window(idx_vmem, out_vmem):
            pltpu.sync_copy(table_hbm.at[idx_vmem.at[0]], out_vmem)

        pltpu.emit_pipeline(
            window,
            grid=(ni // SC_WINDOW,),
            in_specs=[pl.BlockSpec((1, SC_WINDOW), index_map=lambda i: (0, i))],
            out_specs=[pl.BlockSpec((SC_WINDOW, LANES), index_map=lambda i: (i, 0))],
            core_axis_name=("core", "subcore"),
            dimension_semantics=(pltpu.PARALLEL,),
        )(idx_hbm, out_hbm)

    return gather(table, idx.reshape(1, ni))


def _combine_kernel(alpha, dest_ref, dnext_ref, x1_ref, wt_ref, wsi_ref, wsd_ref, os_ref,
                    part_ref, buf_ref, sem):
    tm = x1_ref.shape[0]
    i = pl.program_id(0)
    slot = i % 2

    def gather(d_ref, s, start):
        def body(t, c):
            for j in range(SC_SLOTS, TOP_K):
                cp = pltpu.make_async_copy(_slab(os_ref, d_ref[j, t]),
                                           _slab(buf_ref.at[s], (j - SC_SLOTS) * tm + t), sem.at[s])
                if start:
                    cp.start(priority=j % DMA_PRIORITIES)
                else:
                    cp.wait()
            return c
        lax.fori_loop(0, tm, body, 0)

    @pl.when(i == 0)
    def _():
        gather(dest_ref, slot, True)

    @pl.when(i + 1 < pl.num_programs(0))
    def _():
        gather(dnext_ref, 1 - slot, True)

    x1 = x1_ref[...]
    acc = alpha * x1 + _swiglu(x1.astype(BF16), wsi_ref[...], wsd_ref[...])
    gather(dest_ref, slot, False)

    rows = buf_ref.at[slot]
    for j in range(SC_SLOTS, TOP_K):
        acc = acc + wt_ref[:, j:j + 1] * _unpack_tokens(rows, (j - SC_SLOTS) * tm, tm)
    part_ref[...] = acc


def _finish_kernel(part_ref, wt_ref, st_ref, lg_ref, lb_ref, out_ref):
    tm = part_ref.shape[0]
    acc = part_ref[...]
    for j in range(SC_SLOTS):
        acc = acc + wt_ref[:, j:j + 1] * _unpack_tokens(st_ref.at[j], 0, tm)
    out_ref[...] = _layer_norm(acc, lg_ref[...], lb_ref[...])


def _combine(dest, x1, wt, w_sh_in, w_sh_down, ln_g, ln_b, os, alpha):
    n, d = x1.shape
    tm = _tile(n, 256)
    last = n // tm - 1
    const = lambda i: (0, 0)

    word_rows = (dest[:SC_SLOTS, :, None] * TOKEN_ROWS
                 + jnp.arange(TOKEN_ROWS, dtype=jnp.int32)).reshape(-1)
    staged = _sc_gather_rows(os, word_rows).reshape(SC_SLOTS, n * TOKEN_ROWS, LANES)

    part = pl.pallas_call(
        functools.partial(_combine_kernel, alpha),
        grid=(n // tm,),
        in_specs=[
            pl.BlockSpec((TOP_K, tm), lambda i: (0, i), memory_space=pltpu.SMEM),
            pl.BlockSpec((TOP_K, tm), lambda i: (0, jnp.minimum(i + 1, last)), memory_space=pltpu.SMEM),
            pl.BlockSpec((tm, d), lambda i: (i, 0)),
            pl.BlockSpec((tm, TOP_K), lambda i: (i, 0)),
            pl.BlockSpec(w_sh_in.shape, const),
            pl.BlockSpec(w_sh_down.shape, const),
            pl.BlockSpec(memory_space=pl.ANY),
        ],
        out_specs=pl.BlockSpec((tm, d), lambda i: (i, 0)),
        out_shape=jax.ShapeDtypeStruct((n, d), F32),
        scratch_shapes=[
            pltpu.VMEM((2, (TOP_K - SC_SLOTS) * tm * TOKEN_ROWS, LANES), jnp.uint32),
            pltpu.SemaphoreType.DMA((2,)),
        ],
        compiler_params=_params("arbitrary"),
        name="combine",
    )(dest, dest, x1, wt, w_sh_in.astype(BF16), w_sh_down.astype(BF16), os)

    tf = _tile(n, 512)
    return pl.pallas_call(
        _finish_kernel,
        grid=(n // tf,),
        in_specs=[
            pl.BlockSpec((tf, d), lambda i: (i, 0)),
            pl.BlockSpec((tf, TOP_K), lambda i: (i, 0)),
            pl.BlockSpec((SC_SLOTS, tf * TOKEN_ROWS, LANES), lambda i: (0, i, 0)),
            pl.BlockSpec((1, d), const),
            pl.BlockSpec((1, d), const),
        ],
        out_specs=pl.BlockSpec((tf, d), lambda i: (i, 0)),
        out_shape=jax.ShapeDtypeStruct((n, d), F32),
        compiler_params=_params("parallel"),
        name="finish",
    )(part, wt, staged, ln_g.astype(F32)[None, :], ln_b.astype(F32)[None, :])


def _block_layout(counts, n_assign):
    n_blocks = (n_assign + N_EXPERTS * (EXPERT_BLOCK - 1) + EXPERT_BLOCK - 1) // EXPERT_BLOCK
    nblk = (counts + EXPERT_BLOCK - 1) // EXPERT_BLOCK
    blk_end = jnp.cumsum(nblk)
    pad_start = (blk_end - nblk) * EXPERT_BLOCK
    blk_e = jnp.sum(blk_end[None, :] <= jnp.arange(n_blocks, dtype=jnp.int32)[:, None], axis=1)
    blk_e = jnp.minimum(blk_e, N_EXPERTS - 1).astype(jnp.int32)
    n_used = blk_end[-1:].astype(jnp.int32)
    tail_blk = jnp.where(nblk > 0, blk_end - 1, -1).astype(jnp.int32)
    return n_blocks, pad_start.astype(jnp.int32), blk_e, n_used, tail_blk


def _layer(x, w_in, b_gate, q_g, k_g, w_four_proj, w_attn_proj, w_o, ln1_g, ln1_b,
           w_router, e_bias, w_e_in, w_e_down, w_sh_in, w_sh_down, ln2_g, ln2_b, alpha):
    batch, seq, d = x.shape
    n = batch * seq
    x2 = x.reshape(n, d)

    u, q4, k4, vt4, gates, score_bound = _inproj(x2, w_in, b_gate, q_g, k_g, batch, seq)
    mf = _fourier(u.reshape(batch, seq, FOURIER_WIDTH), gates.reshape(batch, seq, -1), w_four_proj)
    o = _attention(q4, k4, vt4, score_bound)
    x1, x1p = _mix(o.reshape(n, ATTN_WIDTH), mf.reshape(n, d), gates, x2,
                   w_attn_proj, w_o, ln1_g, ln1_b, alpha)

    eidx, rank, wts, cnt = _route(x1, w_router, e_bias)
    counts = cnt[:, 0].astype(jnp.int32)
    n_blocks, pad_start, blk_e, n_used, tail_blk = _block_layout(counts, n * TOP_K)
    dest = _dest(eidx, rank, pad_start)

    xs = _dispatch(tail_blk, dest, x1p, n_blocks * EXPERT_BLOCK)
    os = _experts(blk_e, n_used, xs, w_e_in, w_e_down)
    out = _combine(dest, x1, wts.T, w_sh_in, w_sh_down, ln2_g, ln2_b, os, alpha)
    return out.reshape(batch, seq, d)


def kernel(x, w_in, b_gate, q_norm_g, k_norm_g, w_four_proj, w_attn_proj, w_o, ln1_g, ln1_b, w_router, e_bias, w_e_in, w_e_down, w_sh_in, w_sh_down, ln2_g, ln2_b):
    depth = w_in.shape[0]
    alpha = (2 * depth) ** 0.25
    for l in range(depth):
        x = _layer(x, w_in[l], b_gate[l], q_norm_g[l], k_norm_g[l], w_four_proj[l],
                   w_attn_proj[l], w_o[l], ln1_g[l], ln1_b[l], w_router[l], e_bias[l],
                   w_e_in[l], w_e_down[l], w_sh_in[l], w_sh_down[l], ln2_g[l], ln2_b[l], alpha)
    return x
```

```python
import functools
import math

import numpy as np
import jax
import jax.numpy as jnp
from jax import lax
from jax.experimental import pallas as pl
from jax.experimental.pallas import tpu as pltpu
from jax.experimental.pallas import tpu_sc as plsc

F32 = jnp.float32
BF16 = jnp.bfloat16

GRID_W = 64
N_FOURIER_GROUPS = 8
FOURIER_GROUP_DIM = 64
FOURIER_WIDTH = N_FOURIER_GROUPS * FOURIER_GROUP_DIM
N_Q_HEADS = 16
N_KV_HEADS = 4
HEAD_DIM = 64
Q_GROUP = N_Q_HEADS // N_KV_HEADS
ATTN_WIDTH = N_Q_HEADS * HEAD_DIM
KV_WIDTH = N_KV_HEADS * HEAD_DIM
ROPE_THETA = 10000.0
QK_EPS = 1e-6
OFF_Q = FOURIER_WIDTH
OFF_K = OFF_Q + ATTN_WIDTH
OFF_V = OFF_K + KV_WIDTH
OFF_G = OFF_V + KV_WIDTH
N_EXPERTS = 256
TOP_K = 8
N_EXPERT_GROUPS = 8
GROUP_SIZE = N_EXPERTS // N_EXPERT_GROUPS
TOPK_GROUPS = 4
ROUTED_SCALE = 2.5
LN_EPS = 1e-5

LANES = 128
SUBLANES = 8
MXU_DIM = 256
VMEM_LIMIT = 56 * 1024 * 1024

MAX_EXP2_RANGE = 100.0
DMA_PRIORITIES = 2

EXPERT_BLOCK = 512
SC_SLOTS = 6
SC_WINDOW = 128
TOKEN_ROWS = 4

NT_DIMS = (((1,), (1,)), ((), ()))


def _dot(a, b):
    return jnp.dot(a, b, preferred_element_type=F32)


def _dot_nt(a, b):
    return lax.dot_general(a, b, NT_DIMS, preferred_element_type=F32)


def _sigmoid(x):
    return 1.0 / (1.0 + jnp.exp(-x))


def _params(*sem):
    return pltpu.CompilerParams(dimension_semantics=sem, vmem_limit_bytes=VMEM_LIMIT)


def _tile(n, pref):
    t = min(n, pref)
    assert n % t == 0, (n, t)
    return t


def _rope_tables(seq):
    lane = np.arange(MXU_DIM)
    d = lane % HEAD_DIM
    sub = d % 32
    j = sub % 16
    t = np.arange(seq)[:, None]
    pos = np.where(d[None, :] < 32, t // GRID_W, t % GRID_W).astype(np.float64)
    freq = ROPE_THETA ** (-(j.astype(np.float64)) / 16.0)
    ang = pos * freq[None, :]
    cos = np.cos(ang)
    sin = np.sin(ang) * np.where(sub < 16, -1.0, 1.0)[None, :]
    return jnp.asarray(cos, F32), jnp.asarray(sin, F32)


def _head_mean_matrix():
    i = np.arange(MXU_DIM)
    m = (i[:, None] // HEAD_DIM == i[None, :] // HEAD_DIM).astype(np.float64) / HEAD_DIM
    return jnp.asarray(m, BF16)


def _dft_tables(seq):
    c = np.arange(FOURIER_GROUP_DIM)
    ang_c = 2.0 * np.pi * ((c[:, None] * c[None, :]) % FOURIER_GROUP_DIM) / FOURIER_GROUP_DIM
    sc = 1.0 / math.sqrt(FOURIER_GROUP_DIM)
    eye = np.eye(N_FOURIER_GROUPS)
    cc = np.kron(eye, np.cos(ang_c) * sc)
    ss = np.kron(eye, np.sin(ang_c) * sc)
    chan = np.concatenate([cc, ss], axis=1)
    s = np.arange(seq)
    ang_s = 2.0 * np.pi * ((s[:, None] * s[None, :]) % seq) / seq
    ssc = 1.0 / math.sqrt(seq)
    seqm = np.concatenate([np.cos(ang_s) * ssc, -np.sin(ang_s) * ssc], axis=1)
    return jnp.asarray(chan, BF16), jnp.asarray(seqm, BF16)


def _norm_rope(z, gain, mean_mat, cos, sin, lo_mask):
    ms = _dot((z * z).astype(BF16), mean_mat)
    y = z * lax.rsqrt(ms + QK_EPS) * gain
    outs = []
    for c in range(MXU_DIM // LANES):
        yc = y[:, c * LANES:(c + 1) * LANES]
        up = pltpu.roll(yc, LANES - 16, 1)
        dn = pltpu.roll(yc, 16, 1)
        partner = jnp.where(lo_mask, up, dn)
        sl = slice(c * LANES, (c + 1) * LANES)
        outs.append(yc * cos[:, sl] + partner * sin[:, sl])
    return jnp.concatenate(outs, axis=1)


def _inproj_kernel(x_ref, w_ref, bg_ref, gq_ref, gk_ref, mm_ref, cos_ref, sin_ref,
                   u_ref, q_ref, k_ref, v_ref, g_ref):
    xb = x_ref[...].astype(BF16)
    u_ref[...] = _dot(xb, w_ref[:, 0:OFF_Q]).astype(BF16)

    lane = lax.broadcasted_iota(jnp.int32, (1, LANES), 1)
    lo_mask = (lane & 16) == 0
    mean_mat = mm_ref[...]
    cos = cos_ref[...]
    sin = sin_ref[...]

    for c in range(ATTN_WIDTH // MXU_DIM):
        z = _dot(xb, w_ref[:, OFF_Q + c * MXU_DIM:OFF_Q + (c + 1) * MXU_DIM])
        q = _norm_rope(z, gq_ref[...], mean_mat, cos, sin, lo_mask).astype(BF16)
        for j in range(MXU_DIM // HEAD_DIM):
            q_ref[0, c * (MXU_DIM // HEAD_DIM) + j] = q[:, j * HEAD_DIM:(j + 1) * HEAD_DIM]

    z = _dot(xb, w_ref[:, OFF_K:OFF_V])
    k = _norm_rope(z, gk_ref[...], mean_mat, cos, sin, lo_mask).astype(BF16)
    vt = _dot(xb, w_ref[:, OFF_V:OFF_G]).T.astype(BF16)
    for j in range(N_KV_HEADS):
        k_ref[0, j] = k[:, j * HEAD_DIM:(j + 1) * HEAD_DIM]
        v_ref[0, j] = vt[j * HEAD_DIM:(j + 1) * HEAD_DIM, :]

    gw = 512
    for c in range((w_ref.shape[1] - OFF_G) // gw):
        sl = slice(OFF_G + c * gw, OFF_G + (c + 1) * gw)
        z = _dot(xb, w_ref[:, sl]) + bg_ref[:, c * gw:(c + 1) * gw]
        g_ref[:, c * gw:(c + 1) * gw] = _sigmoid(z).astype(BF16)


def _inproj(x2, w_in, b_gate, q_g, k_g, batch, seq):
    n, d = x2.shape
    tm = _tile(seq, 512)
    spb = seq // tm
    in_width = w_in.shape[1]
    gate_w = in_width - OFF_G
    cos, sin = _rope_tables(seq)
    mean_mat = _head_mean_matrix()
    scale = HEAD_DIM ** -0.5 * math.log2(math.e)
    gq =jnp.tile(q_g.astype(F32) * scale, MXU_DIM // HEAD_DIM)[None, :]
    gk = jnp.tile(k_g.astype(F32), MXU_DIM // HEAD_DIM)[None, :]
    score_bound = (HEAD_DIM * jnp.max(jnp.abs(gq)) * jnp.max(jnp.abs(gk))).reshape(1)
    const = lambda i: (0, 0)
    outs = pl.pallas_call(
        _inproj_kernel,
        grid=(n // tm,),
        in_specs=[
            pl.BlockSpec((tm, d), lambda i: (i, 0)),
            pl.BlockSpec((d, in_width), const),
            pl.BlockSpec((1, gate_w), const),
            pl.BlockSpec((1, MXU_DIM), const),
            pl.BlockSpec((1, MXU_DIM), const),
            pl.BlockSpec((MXU_DIM, MXU_DIM), const),
            pl.BlockSpec((tm, MXU_DIM), lambda i: (i % spb, 0)),
            pl.BlockSpec((tm, MXU_DIM), lambda i: (i % spb, 0)),
        ],
        out_specs=[
            pl.BlockSpec((tm, FOURIER_WIDTH), lambda i: (i, 0)),
            pl.BlockSpec((1, N_Q_HEADS, tm, HEAD_DIM), lambda i: (i // spb, 0, i % spb, 0)),
            pl.BlockSpec((1, N_KV_HEADS, tm, HEAD_DIM), lambda i: (i // spb, 0, i % spb, 0)),
            pl.BlockSpec((1, N_KV_HEADS, HEAD_DIM, tm), lambda i: (i // spb, 0, 0, i % spb)),
            pl.BlockSpec((tm, gate_w), lambda i: (i, 0)),
        ],
        out_shape=[
            jax.ShapeDtypeStruct((n, FOURIER_WIDTH), BF16),
            jax.ShapeDtypeStruct((batch, N_Q_HEADS, seq, HEAD_DIM), BF16),
            jax.ShapeDtypeStruct((batch, N_KV_HEADS, seq, HEAD_DIM), BF16),
            jax.ShapeDtypeStruct((batch, N_KV_HEADS, HEAD_DIM, seq), BF16),
            jax.ShapeDtypeStruct((n, gate_w), BF16),
        ],
        compiler_params=_params("parallel"),
        name="inproj",
    )(x2, w_in.astype(BF16), b_gate.astype(F32)[None, :], gq, gk, mean_mat, cos, sin)
    return (*outs, score_bound)


def _fourier_kernel(u_ref, chan_ref, seqm_ref, wp_ref, g_ref, o_ref, ab_ref):
    seq = u_ref.shape[1]

    @pl.when(pl.program_id(1) == 0)
    def _():
        ab = _dot(u_ref[0], chan_ref[...])
        ab_ref[0:seq, :] = ab[:, 0:FOURIER_WIDTH].astype(BF16)
        ab_ref[seq:2 * seq, :] = ab[:, FOURIER_WIDTH:].astype(BF16)

    f = _dot(seqm_ref[...], ab_ref[...]).astype(BF16)
    y = _dot(f, wp_ref[...])
    o_ref[0] = (g_ref[0].astype(F32) * y).astype(BF16)


def _fourier(u3, g3, w_four_proj):
    batch, seq, _ = u3.shape
    d = w_four_proj.shape[1]
    tr = _tile(seq, 512)
    chan, seqm = _dft_tables(seq)
    return pl.pallas_call(
        _fourier_kernel,
        grid=(batch, seq // tr),
        in_specs=[
            pl.BlockSpec((1, seq, FOURIER_WIDTH), lambda b, r: (b, 0, 0)),
            pl.BlockSpec((FOURIER_WIDTH, 2 * FOURIER_WIDTH), lambda b, r: (0, 0)),
            pl.BlockSpec((tr, 2 * seq), lambda b, r: (r, 0)),
            pl.BlockSpec((FOURIER_WIDTH, d), lambda b, r: (0, 0)),
            pl.BlockSpec((1, tr, d), lambda b, r: (b, r, 0)),
        ],
        out_specs=pl.BlockSpec((1, tr, d), lambda b, r: (b, r, 0)),
        out_shape=jax.ShapeDtypeStruct((batch, seq, d), BF16),
        scratch_shapes=[pltpu.VMEM((2 * seq, FOURIER_WIDTH), BF16)],
        compiler_params=_params("parallel", "arbitrary"),
        name="fourier",
    )(u3, chan, seqm, w_four_proj.astype(BF16), g3)


def _attention_kernel(bounded, sb_ref, q_ref, k_ref, vt_ref, o_ref, vone_ref):
    seq = k_ref.shape[2]

    @pl.when(pl.program_id(2) == 0)
    def _():
        vone_ref[0:HEAD_DIM, :] = vt_ref[0, 0]
        vone_ref[HEAD_DIM:, :] = jnp.ones((HEAD_DIM, seq), BF16)

    k = k_ref[0, 0]
    vone = vone_ref[...]
    outs = []
    st_next = _dot_nt(k, q_ref[0, 0])
    for g in range(Q_GROUP):
        st = st_next
        if g + 1 < Q_GROUP:
            st_next = _dot_nt(k, q_ref[0, g + 1])
        if bounded:
            m = sb_ref[0]
        else:
            m = jnp.max(st, axis=0, keepdims=True)
        pt = jnp.exp2(st - m).astype(BF16)
        ol = _dot(vone, pt)
        ot = ol[0:HEAD_DIM, :] / ol[HEAD_DIM:HEAD_DIM + 1, :]
        outs.append(ot.T.astype(BF16))
    o_ref[0] = jnp.concatenate(outs, axis=1)


def _attention(q4, k4, vt4, score_bound):
    batch, _, seq, _ = q4.shape
    tq = _tile(seq, 256)

    def call(bounded):
        return pl.pallas_call(
            functools.partial(_attention_kernel, bounded),
            grid_spec=pltpu.PrefetchScalarGridSpec(
                num_scalar_prefetch=1,
                grid=(batch, N_KV_HEADS, seq // tq),
                in_specs=[
                    pl.BlockSpec((1, Q_GROUP, tq, HEAD_DIM), lambda b, h, i, sb: (b, h, i, 0)),
                    pl.BlockSpec((1, 1, seq, HEAD_DIM), lambda b, h, i, sb: (b, h, 0, 0)),
                    pl.BlockSpec((1, 1, HEAD_DIM, seq), lambda b, h, i, sb: (b, h, 0, 0)),
                ],
                out_specs=pl.BlockSpec((1, tq, Q_GROUP * HEAD_DIM), lambda b, h, i, sb: (b, i, h)),
                scratch_shapes=[pltpu.VMEM((2 * HEAD_DIM, seq), BF16)],
            ),
            out_shape=jax.ShapeDtypeStruct((batch, seq, ATTN_WIDTH), BF16),
            compiler_params=_params("parallel", "parallel", "arbitrary"),
            name="attention_bounded" if bounded else "attention",
        )(score_bound, q4, k4, vt4)

    return lax.cond(2.0 * score_bound[0] <= MAX_EXP2_RANGE,
                    lambda: call(True), lambda: call(False))


def _layer_norm(h, g, b):
    mu = jnp.mean(h, axis=-1, keepdims=True)
    c = h - mu
    var = jnp.mean(c * c, axis=-1, keepdims=True)
    return c * lax.rsqrt(var + LN_EPS) * g + b


def _slab(ref, row):
    return ref.at[pl.ds(pl.multiple_of(row * TOKEN_ROWS, TOKEN_ROWS), TOKEN_ROWS), :]


def _pack_tokens(dst_ref, val):
    t, d = val.shape
    half = d // 2
    assert half == TOKEN_ROWS * LANES
    bits = lax.bitcast_convert_type(val.astype(BF16).astype(F32), jnp.uint32)
    words = (bits[:, :half] >> 16) | bits[:, half:]
    for s in range(TOKEN_ROWS):
        dst_ref[pl.ds(s, t, stride=TOKEN_ROWS), :] = words[:, s * LANES:(s + 1) * LANES]


def _unpack_words(words):
    lo = [lax.bitcast_convert_type(w << 16, F32) for w in words]
    hi = [lax.bitcast_convert_type(w & jnp.uint32(0xFFFF0000), F32) for w in words]
    return jnp.concatenate(lo + hi, axis=1)


def _unpack_tokens(src_ref, tok0, t):
    return _unpack_words([src_ref[pl.ds(tok0 * TOKEN_ROWS + s, t, stride=TOKEN_ROWS), :]
                          for s in range(TOKEN_ROWS)])


def _mix_kernel(alpha, o_ref, mf_ref, g_ref, x_ref, wap_ref, wo_ref, lg_ref, lb_ref,
                x1_ref, x1p_ref):
    y = _dot(o_ref[...], wap_ref[...])
    merged = mf_ref[...].astype(F32) + g_ref[...].astype(F32) * y
    mix = _dot(merged.astype(BF16), wo_ref[...])
    x1 = _layer_norm(alpha * x_ref[...] + mix, lg_ref[...], lb_ref[...])
    x1_ref[...] = x1
    _pack_tokens(x1p_ref, x1)


def _mix(o2, mf2, g2, x2, w_attn_proj, w_o, ln_g, ln_b, alpha):
    n, d = x2.shape
    assert d == 2 * TOKEN_ROWS * LANES
    tm = _tile(n, 512)
    const = lambda i: (0, 0)
    return pl.pallas_call(
        functools.partial(_mix_kernel, alpha),
        grid=(n // tm,),
        in_specs=[
            pl.BlockSpec((tm, ATTN_WIDTH), lambda i: (i, 0)),
            pl.BlockSpec((tm, d), lambda i: (i, 0)),
            pl.BlockSpec((tm, d), lambda i: (i, 1)),
            pl.BlockSpec((tm, d), lambda i: (i, 0)),
            pl.BlockSpec((ATTN_WIDTH, d), const),
            pl.BlockSpec((d, d), const),
            pl.BlockSpec((1, d), const),
            pl.BlockSpec((1, d), const),
        ],
        out_specs=[
            pl.BlockSpec((tm, d), lambda i: (i, 0)),
            pl.BlockSpec((tm * TOKEN_ROWS, LANES), lambda i: (i, 0)),
        ],
        out_shape=[
            jax.ShapeDtypeStruct((n, d), F32),
            jax.ShapeDtypeStruct((n * TOKEN_ROWS, LANES), jnp.uint32),
        ],
        compiler_params=_params("parallel"),
        name="mix",
    )(o2, mf2, g2, x2, w_attn_proj.astype(BF16), w_o.astype(BF16),
      ln_g.astype(F32)[None, :], ln_b.astype(F32)[None, :])


def _route_kernel(x_ref, wh_ref, wl_ref, eb_ref, tri_ref,
                  eidx_ref, rank_ref, w_ref, cnt_ref, carry_ref):
    tm = x_ref.shape[0]

    @pl.when(pl.program_id(0) == 0)
    def _():
        carry_ref[...] = jnp.zeros_like(carry_ref)

    x = x_ref[...]
    xh = x.astype(BF16)
    xl = (x - xh.astype(F32)).astype(BF16)
    wh = wh_ref[...]
    logits = _dot_nt(wh, xh) + _dot_nt(wh, xl) + _dot_nt(wl_ref[...], xh)
    scores = _sigmoid(logits)
    biased = scores + eb_ref[:, 0:1]
    neg = -jnp.inf

    sub_iota = lax.broadcasted_iota(jnp.int32, (GROUP_SIZE, tm), 0).astype(F32)
    gs = []
    for g in range(N_EXPERT_GROUPS):
        blk = biased[g * GROUP_SIZE:(g + 1) * GROUP_SIZE, :]
        m1 = jnp.max(blk, axis=0, keepdims=True)
        a1 = jnp.min(jnp.where(blk == m1, sub_iota, float(GROUP_SIZE)), axis=0, keepdims=True)
        m2 = jnp.max(jnp.where(sub_iota == a1, neg, blk), axis=0, keepdims=True)
        gs.append(m1 + m2)

    masked = []
    for g in range(N_EXPERT_GROUPS):
        beat = jnp.zeros((1, tm), F32)
        for h in range(N_EXPERT_GROUPS):
            if h == g:
                continue
            wins = (gs[h] >= gs[g]) if h < g else (gs[h] > gs[g])
            beat = beat + jnp.where(wins, 1.0, 0.0)
        keep = beat < float(TOPK_GROUPS)
        blk = biased[g * GROUP_SIZE:(g + 1) * GROUP_SIZE, :]
        masked.append(jnp.where(keep, blk, neg))
    masked = jnp.concatenate(masked, axis=0)

    e_iota = lax.broadcasted_iota(jnp.int32, (N_EXPERTS, tm), 0).astype(F32)
    sel = jnp.zeros((N_EXPERTS, tm), F32)
    idxs, ws = [], []
    for _ in range(TOP_K):
        mx = jnp.max(masked, axis=0, keepdims=True)
        idx = jnp.min(jnp.where(masked == mx, e_iota, float(N_EXPERTS)), axis=0, keepdims=True)
        hit = e_iota == idx
        masked = jnp.where(hit, neg, masked)
        sel = jnp.where(hit, 1.0, sel)
        idxs.append(idx)
        ws.append(jnp.sum(jnp.where(hit, scores, 0.0), axis=0, keepdims=True))

    carry = carry_ref[...]
    selb = sel.astype(BF16)
    prefix = _dot(selb, tri_ref[...])
    rank_all = prefix + jnp.concatenate([carry] * (tm // LANES), axis=1)
    total = carry + _dot(selb, jnp.ones((tm, LANES), BF16))
    carry_ref[...] = total
    cnt_ref[...] = total

    wsum = ws[0]
    for j in range(1, TOP_K):
        wsum = wsum + ws[j]
    for j in range(TOP_K):
        eidx_ref[j:j + 1, :] = idxs[j].astype(jnp.int32)
        r = jnp.sum(jnp.where(e_iota == idxs[j], rank_all, 0.0), axis=0, keepdims=True)
        rank_ref[j:j + 1, :] = r.astype(jnp.int32)
        w_ref[j:j + 1, :] = ws[j] / wsum * ROUTED_SCALE


def _route(x1, w_router, e_bias):
    n, d = x1.shape
    tm = _tile(n, 512)
    wt = w_router.astype(F32).T
    wh = wt.astype(BF16)
    wl = (wt - wh.astype(F32)).astype(BF16)
    eb = jnp.broadcast_to(e_bias.astype(F32)[:, None], (N_EXPERTS, LANES))
    tri = jnp.asarray(np.triu(np.ones((tm, tm)), k=1), BF16)
    const = lambda i: (0, 0)
    return pl.pallas_call(
        _route_kernel,
        grid=(n // tm,),
        in_specs=[
            pl.BlockSpec((tm, d), lambda i: (i, 0)),
            pl.BlockSpec((N_EXPERTS, d), const),
            pl.BlockSpec((N_EXPERTS, d), const),
            pl.BlockSpec((N_EXPERTS, LANES), const),
            pl.BlockSpec((tm, tm), const),
        ],
        out_specs=[
            pl.BlockSpec((TOP_K, tm), lambda i: (0, i)),
            pl.BlockSpec((TOP_K, tm), lambda i: (0, i)),
            pl.BlockSpec((TOP_K, tm), lambda i: (0, i)),
            pl.BlockSpec((N_EXPERTS, LANES), const),
        ],
        out_shape=[
            jax.ShapeDtypeStruct((TOP_K, n), jnp.int32),
            jax.ShapeDtypeStruct((TOP_K, n), jnp.int32),
            jax.ShapeDtypeStruct((TOP_K, n), F32),
            jax.ShapeDtypeStruct((N_EXPERTS, LANES), F32),
        ],
        scratch_shapes=[pltpu.VMEM((N_EXPERTS, LANES), F32)],
        compiler_params=_params("arbitrary"),
        name="route",
    )(x1, wh, wl, eb, tri)


def _dest_kernel(eidx_ref, rank_ref, ps_ref, dest_ref):
    tm = eidx_ref.shape[1]
    e_iota = lax.broadcasted_iota(jnp.int32, (N_EXPERTS, tm), 0)
    ps = jnp.concatenate([ps_ref[...]] * (tm // LANES), axis=1)
    for j in range(TOP_K):
        hit = e_iota == eidx_ref[j:j + 1, :]
        start = jnp.sum(jnp.where(hit, ps, 0.0), axis=0, keepdims=True)
        dest_ref[j:j + 1, :] = start.astype(jnp.int32) + rank_ref[j:j + 1, :]


def _dest(eidx, rank, pad_start):
    n = eidx.shape[1]
    tm = _tile(n, 512)
    ps = jnp.broadcast_to(pad_start.astype(F32)[:, None], (N_EXPERTS, LANES))
    return pl.pallas_call(
        _dest_kernel,
        grid=(n // tm,),
        in_specs=[
            pl.BlockSpec((TOP_K, tm), lambda i: (0, i)),
            pl.BlockSpec((TOP_K, tm), lambda i: (0, i)),
            pl.BlockSpec((N_EXPERTS, LANES), lambda i: (0, 0)),
        ],
        out_specs=pl.BlockSpec((TOP_K, tm), lambda i: (0, i)),
        out_shape=jax.ShapeDtypeStruct((TOP_K, n), jnp.int32),
        compiler_params=_params("parallel"),
        name="dest",
    )(eidx, rank, ps)


def _dispatch_kernel(tail_ref, dest_ref, x_ref, xs_ref, zero_ref, sem):
    tm = dest_ref.shape[1]

    @pl.when(pl.program_id(0) == 0)
    def _():
        zero_ref[...] = jnp.zeros_like(zero_ref)

        def tail_copy(e):
            rows = EXPERT_BLOCK * TOKEN_ROWS
            row0 = pl.multiple_of(tail_ref[e] * rows, rows)
            return pltpu.make_async_copy(zero_ref, xs_ref.at[pl.ds(row0, rows), :], sem)

        def zstart(e, c):
            @pl.when(tail_ref[e] >= 0)
            def _():
                tail_copy(e).start()
            return c

        def zwait(e, c):
            @pl.when(tail_ref[e] >= 0)
            def _():
                tail_copy(e).wait()
            return c

        lax.fori_loop(0, N_EXPERTS, zstart, 0)
        lax.fori_loop(0, N_EXPERTS, zwait, 0)

    def row_copy(j, t):
        return pltpu.make_async_copy(_slab(x_ref, t), _slab(xs_ref, dest_ref[j, t]), sem)

    def issue(t, c):
        for j in range(TOP_K):
            row_copy(j, t).start(priority=j % DMA_PRIORITIES)
        return c

    def drain(t, c):
        for j in range(TOP_K):
            row_copy(j, t).wait()
        return c

    lax.fori_loop(0, tm, issue, 0)
    lax.fori_loop(0, tm, drain, 0)


def _dispatch(tail_blk, dest, x1p, n_rows):
    n = dest.shape[1]
    tm = _tile(n, 512)
    return pl.pallas_call(
        _dispatch_kernel,
        grid_spec=pltpu.PrefetchScalarGridSpec(
            num_scalar_prefetch=1,
            grid=(n // tm,),
            in_specs=[
                pl.BlockSpec((TOP_K, tm), lambda i, tb: (0, i), memory_space=pltpu.SMEM),
                pl.BlockSpec((tm * TOKEN_ROWS, LANES), lambda i, tb: (i, 0)),
            ],
            out_specs=pl.BlockSpec(memory_space=pl.ANY),
            scratch_shapes=[pltpu.VMEM((EXPERT_BLOCK * TOKEN_ROWS, LANES), jnp.uint32),
                            pltpu.SemaphoreType.DMA],
        ),
        out_shape=jax.ShapeDtypeStruct((n_rows * TOKEN_ROWS, LANES), jnp.uint32),
        compiler_params=_params("arbitrary"),
        name="dispatch",
    )(tail_blk, dest, x1p)


def _swiglu(xb, w_in, w_down):
    h = _dot(xb, w_in)
    half = h.shape[1] // 2
    g = h[:, :half]
    act = g * _sigmoid(g) * h[:, half:]
    return _dot(act.astype(BF16), w_down)


def _experts_kernel(be_ref, nu_ref, x_ref, wi_ref, wd_ref, o_ref, wib_ref, wdb_ref):
    i = pl.program_id(0)
    used = i < nu_ref[0]
    prev = be_ref[jnp.maximum(i - 1, 0)]
    fresh = jnp.logical_or(i == 0, be_ref[i] != prev)
    blk = o_ref.shape[0] // TOKEN_ROWS

    @pl.when(jnp.logical_and(used, fresh))
    def _():
        wib_ref[...] = wi_ref[...].astype(BF16)
        wdb_ref[...] = wd_ref[...].astype(BF16)

    @pl.when(used)
    def _():
        xb = _unpack_tokens(x_ref, 0, blk).astype(BF16)
        _pack_tokens(o_ref, _swiglu(xb, wib_ref[...], wdb_ref[...]))

    @pl.when(jnp.logical_not(used))
    def _():
        o_ref[...] = jnp.zeros_like(o_ref)


def _experts(blk_e, n_used, xs, w_e_in, w_e_down):
    n_blocks = blk_e.shape[0]
    _, d, h2 = w_e_in.shape
    hdim = w_e_down.shape[1]
    rows = EXPERT_BLOCK * TOKEN_ROWS

    def x_map(i, be, nu):
        return (jnp.minimum(i, nu[0] - 1), 0)

    def w_map(i, be, nu):
        return (be[jnp.minimum(i, nu[0] - 1)], 0, 0)

    return pl.pallas_call(
        _experts_kernel,
        grid_spec=pltpu.PrefetchScalarGridSpec(
            num_scalar_prefetch=2,
            grid=(n_blocks,),
            in_specs=[
                pl.BlockSpec((rows, LANES), x_map),
                pl.BlockSpec((None, d, h2), w_map),
                pl.BlockSpec((None, hdim, d), w_map),
            ],
            out_specs=pl.BlockSpec((rows, LANES), lambda i, be, nu: (i, 0)),
            scratch_shapes=[pltpu.VMEM((d, h2), BF16), pltpu.VMEM((hdim, d), BF16)],
        ),
        out_shape=jax.ShapeDtypeStruct(xs.shape, jnp.uint32),
        compiler_params=_params("arbitrary"),
        name="experts",
    )(blk_e, n_used, xs, w_e_in, w_e_down)


def _sc_gather_rows(table, idx):
    ni = idx.shape[0]
    mesh = plsc.VectorSubcoreMesh(core_axis_name="core", subcore_axis_name="subcore")

    @functools.partial(pl.kernel, mesh=mesh, scratch_types=[],
                       out_type=jax.ShapeDtypeStruct((ni, LANES), table.dtype))
    def gather(table_hbm, idx_hbm, out_hbm):
        def window(idx_vmem, out_vmem):
            pltpu.sync_copy(table_hbm.at[idx_vmem.at[0]], out_vmem)

        pltpu.emit_pipeline(
            window,
            grid=(ni // SC_WINDOW,),
            in_specs=[pl.BlockSpec((1, SC_WINDOW), index_map=lambda i: (0, i))],
            out_specs=[pl.BlockSpec((SC_WINDOW, LANES), index_map=lambda i: (i, 0))],
            core_axis_name=("core", "subcore"),
            dimension_semantics=(pltpu.PARALLEL,),
        )(idx_hbm, out_hbm)

    return gather(table, idx.reshape(1, ni))


def _combine_kernel(alpha, dest_ref, dnext_ref, x1_ref, wt_ref, wsi_ref, wsd_ref, os_ref,
                    part_ref, buf_ref, sem):
    tm = x1_ref.shape[0]
    i = pl.program_id(0)
    slot = i % 2

    def gather(d_ref, s, start):
        def body(t, c):
            for j in range(SC_SLOTS, TOP_K):
                cp = pltpu.make_async_copy(_slab(os_ref, d_ref[j, t]),
                                           _slab(buf_ref.at[s], (j - SC_SLOTS) * tm + t), sem.at[s])
                if start:
                    cp.start(priority=j % DMA_PRIORITIES)
                else:
                    cp.wait()
            return c
        lax.fori_loop(0, tm, body, 0)

    @pl.when(i == 0)
    def _():
        gather(dest_ref, slot, True)

    @pl.when(i + 1 < pl.num_programs(0))
    def _():
        gather(dnext_ref, 1 - slot, True)

    x1 = x1_ref[...]
    acc = alpha * x1 + _swiglu(x1.astype(BF16), wsi_ref[...], wsd_ref[...])
    gather(dest_ref, slot, False)

    rows = buf_ref.at[slot]
    for j in range(SC_SLOTS, TOP_K):
        acc = acc + wt_ref[:, j:j + 1] * _unpack_tokens(rows, (j - SC_SLOTS) * tm, tm)
    part_ref[...] = acc


def _finish_kernel(part_ref, wt_ref, st_ref, lg_ref, lb_ref, out_ref):
    acc = part_ref[...]
    for j in range(SC_SLOTS):
        words = [st_ref[s, j] for s in range(TOKEN_ROWS)]
        acc = acc + wt_ref[:, j:j + 1] * _unpack_words(words)
    out_ref[...] = _layer_norm(acc, lg_ref[...], lb_ref[...])


def _combine(dest, x1, wt, w_sh_in, w_sh_down, ln_g, ln_b, os, alpha):
    n, d = x1.shape
    tm = _tile(n, 256)
    last = n // tm - 1
    const = lambda i: (0, 0)

    word_rows = (dest[None, :SC_SLOTS, :] * TOKEN_ROWS
                 + jnp.arange(TOKEN_ROWS, dtype=jnp.int32)[:, None, None]).reshape(-1)
    staged = _sc_gather_rows(os, word_rows).reshape(TOKEN_ROWS, SC_SLOTS, n, LANES)

    part = pl.pallas_call(
        functools.partial(_combine_kernel, alpha),
        grid=(n // tm,),
        in_specs=[
            pl.BlockSpec((TOP_K, tm), lambda i: (0, i), memory_space=pltpu.SMEM),
            pl.BlockSpec((TOP_K, tm), lambda i: (0, jnp.minimum(i + 1, last)), memory_space=pltpu.SMEM),
            pl.BlockSpec((tm, d), lambda i: (i, 0)),
            pl.BlockSpec((tm, TOP_K), lambda i: (i, 0)),
            pl.BlockSpec(w_sh_in.shape, const),
            pl.BlockSpec(w_sh_down.shape, const),
            pl.BlockSpec(memory_space=pl.ANY),
        ],
        out_specs=pl.BlockSpec((tm, d), lambda i: (i, 0)),
        out_shape=jax.ShapeDtypeStruct((n, d), F32),
        scratch_shapes=[
            pltpu.VMEM((2, (TOP_K - SC_SLOTS) * tm * TOKEN_ROWS, LANES), jnp.uint32),
            pltpu.SemaphoreType.DMA((2,)),
        ],
        compiler_params=_params("arbitrary"),
        name="combine",
    )(dest, dest, x1, wt, w_sh_in.astype(BF16), w_sh_down.astype(BF16), os)

    tf = _tile(n, 512)
    return pl.pallas_call(
        _finish_kernel,
        grid=(n // tf,),
        in_specs=[
            pl.BlockSpec((tf, d), lambda i: (i, 0)),
            pl.BlockSpec((tf, TOP_K), lambda i: (i, 0)),
            pl.BlockSpec((TOKEN_ROWS, SC_SLOTS, tf, LANES), lambda i: (0, 0, i, 0)),
            pl.BlockSpec((1, d), const),
            pl.BlockSpec((1, d), const),
        ],
        out_specs=pl.BlockSpec((tf, d), lambda i: (i, 0)),
        out_shape=jax.ShapeDtypeStruct((n, d), F32),
        compiler_params=_params("parallel"),
        name="finish",
    )(part, wt, staged, ln_g.astype(F32)[None, :], ln_b.astype(F32)[None, :])


def _block_layout(counts, n_assign):
    n_blocks = (n_assign + N_EXPERTS * (EXPERT_BLOCK - 1) + EXPERT_BLOCK - 1) // EXPERT_BLOCK
    nblk = (counts + EXPERT_BLOCK - 1) // EXPERT_BLOCK
    blk_end = jnp.cumsum(nblk)
    pad_start = (blk_end - nblk) * EXPERT_BLOCK
    blk_e = jnp.sum(blk_end[None, :] <= jnp.arange(n_blocks, dtype=jnp.int32)[:, None], axis=1)
    blk_e = jnp.minimum(blk_e, N_EXPERTS - 1).astype(jnp.int32)
    n_used = blk_end[-1:].astype(jnp.int32)
    tail_blk = jnp.where(nblk > 0, blk_end - 1, -1).astype(jnp.int32)
    return n_blocks, pad_start.astype(jnp.int32), blk_e, n_used, tail_blk


def _layer(x, w_in, b_gate, q_g, k_g, w_four_proj, w_attn_proj, w_o, ln1_g, ln1_b,
           w_router, e_bias, w_e_in, w_e_down, w_sh_in, w_sh_down, ln2_g, ln2_b, alpha):
    batch, seq, d = x.shape
    n = batch * seq
    x2 = x.reshape(n, d)

    u, q4, k4, vt4, gates, score_bound = _inproj(x2, w_in, b_gate, q_g, k_g, batch, seq)
    mf = _fourier(u.reshape(batch, seq, FOURIER_WIDTH), gates.reshape(batch, seq, -1), w_four_proj)
    o = _attention(q4, k4, vt4, score_bound)
    x1, x1p = _mix(o.reshape(n, ATTN_WIDTH), mf.reshape(n, d), gates, x2,
                   w_attn_proj, w_o, ln1_g, ln1_b, alpha)

    eidx, rank, wts, cnt = _route(x1, w_router, e_bias)
    counts = cnt[:, 0].astype(jnp.int32)
    n_blocks, pad_start, blk_e, n_used, tail_blk = _block_layout(counts, n * TOP_K)
    dest = _dest(eidx, rank, pad_start)

    xs = _dispatch(tail_blk, dest, x1p, n_blocks * EXPERT_BLOCK)
    os = _experts(blk_e, n_used, xs, w_e_in, w_e_down)
    out = _combine(dest, x1, wts.T, w_sh_in, w_sh_down, ln2_g, ln2_b, os, alpha)
    return out.reshape(batch, seq, d)


def kernel(x, w_in, b_gate, q_norm_g, k_norm_g, w_four_proj, w_attn_proj, w_o, ln1_g, ln1_b, w_router, e_bias, w_e_in, w_e_down, w_sh_in, w_sh_down, ln2_g, ln2_b):
    depth = w_in.shape[0]
    alpha = (2 * depth) ** 0.25
    for l in range(depth):
        x = _layer(x, w_in[l], b_gate[l], q_norm_g[l], k_norm_g[l], w_four_proj[l],
                   w_attn_proj[l], w_o[l], ln1_g[l], ln1_b[l], w_router[l], e_bias[l],
                   w_e_in[l], w_e_down[l], w_sh_in[l], w_sh_down[l], ln2_g[l], ln2_b[l], alpha)
    return x
```

```python
import functools
import math

import numpy as np
import jax
import jax.numpy as jnp
from jax import lax
from jax.experimental import pallas as pl
from jax.experimental.pallas import tpu as pltpu
from jax.experimental.pallas import tpu_sc as plsc

F32 = jnp.float32
BF16 = jnp.bfloat16

GRID_W = 64
N_FOURIER_GROUPS = 8
FOURIER_GROUP_DIM = 64
FOURIER_WIDTH = N_FOURIER_GROUPS * FOURIER_GROUP_DIM
N_Q_HEADS = 16
N_KV_HEADS = 4
HEAD_DIM = 64
Q_GROUP = N_Q_HEADS // N_KV_HEADS
ATTN_WIDTH = N_Q_HEADS * HEAD_DIM
KV_WIDTH = N_KV_HEADS * HEAD_DIM
ROPE_THETA = 10000.0
QK_EPS = 1e-6
OFF_Q = FOURIER_WIDTH
OFF_K = OFF_Q + ATTN_WIDTH
OFF_V = OFF_K + KV_WIDTH
OFF_G = OFF_V + KV_WIDTH
N_EXPERTS = 256
TOP_K = 8
N_EXPERT_GROUPS = 8
GROUP_SIZE = N_EXPERTS // N_EXPERT_GROUPS
TOPK_GROUPS = 4
ROUTED_SCALE = 2.5
LN_EPS = 1e-5

LANES = 128
SUBLANES = 8
MXU_DIM = 256
VMEM_LIMIT = 56 * 1024 * 1024

MAX_EXP2_RANGE = 100.0
DMA_PRIORITIES = 2

EXPERT_BLOCK = 512
SC_SLOTS = 7
SC_WINDOW = 128
TOKEN_ROWS = 4

NT_DIMS = (((1,), (1,)), ((), ()))


def _dot(a, b):
    return jnp.dot(a, b, preferred_element_type=F32)


def _dot_nt(a, b):
    return lax.dot_general(a, b, NT_DIMS, preferred_element_type=F32)


def _sigmoid(x):
    return 1.0 / (1.0 + jnp.exp(-x))


def _params(*sem):
    return pltpu.CompilerParams(dimension_semantics=sem, vmem_limit_bytes=VMEM_LIMIT)


def _tile(n, pref):
    t = min(n, pref)
    assert n % t == 0, (n, t)
    return t


def _rope_tables(seq):
    lane = np.arange(MXU_DIM)
    d = lane % HEAD_DIM
    sub = d % 32
    j = sub % 16
    t = np.arange(seq)[:, None]
    pos = np.where(d[None, :] < 32, t // GRID_W, t % GRID_W).astype(np.float64)
    freq = ROPE_THETA ** (-(j.astype(np.float64)) / 16.0)
    ang = pos * freq[None, :]
    cos = np.cos(ang)
    sin = np.sin(ang) * np.where(sub < 16, -1.0, 1.0)[None, :]
    return jnp.asarray(cos, F32), jnp.asarray(sin, F32)


def _head_mean_matrix():
    i = np.arange(MXU_DIM)
    m = (i[:, None] // HEAD_DIM == i[None, :] // HEAD_DIM).astype(np.float64) / HEAD_DIM
    return jnp.asarray(m, BF16)


def _dft_tables(seq):
    c = np.arange(FOURIER_GROUP_DIM)
    ang_c = 2.0 * np.pi * ((c[:, None] * c[None, :]) % FOURIER_GROUP_DIM) / FOURIER_GROUP_DIM
    sc = 1.0 / math.sqrt(FOURIER_GROUP_DIM)
    eye = np.eye(N_FOURIER_GROUPS)
    cc = np.kron(eye, np.cos(ang_c) * sc)
    ss = np.kron(eye, np.sin(ang_c) * sc)
    chan = np.concatenate([cc, ss], axis=1)
    s = np.arange(seq)
    ang_s = 2.0 * np.pi * ((s[:, None] * s[None, :]) % seq) / seq
    ssc = 1.0 / math.sqrt(seq)
    seqm = np.concatenate([np.cos(ang_s) * ssc, -np.sin(ang_s) * ssc], axis=1)
    return jnp.asarray(chan, BF16), jnp.asarray(seqm, BF16)


def _norm_rope(z, gain, mean_mat, cos, sin, lo_mask):
    ms = _dot((z * z).astype(BF16), mean_mat)
    y = z * lax.rsqrt(ms + QK_EPS) * gain
    outs = []
    for c in range(MXU_DIM // LANES):
        yc = y[:, c * LANES:(c + 1) * LANES]
        up = pltpu.roll(yc, LANES - 16, 1)
        dn = pltpu.roll(yc, 16, 1)
        partner = jnp.where(lo_mask, up, dn)
        sl = slice(c * LANES, (c + 1) * LANES)
        outs.append(yc * cos[:, sl] + partner * sin[:, sl])
    return jnp.concatenate(outs, axis=1)


def _inproj_kernel(x_ref, w_ref, bg_ref, gq_ref, gk_ref, mm_ref, cos_ref, sin_ref,
                   u_ref, q_ref, k_ref, v_ref, g_ref):
    xb = x_ref[...].astype(BF16)
    u_ref[...] = _dot(xb, w_ref[:, 0:OFF_Q]).astype(BF16)

    lane = lax.broadcasted_iota(jnp.int32, (1, LANES), 1)
    lo_mask = (lane & 16) == 0
    mean_mat = mm_ref[...]
    cos = cos_ref[...]
    sin = sin_ref[...]

    for c in range(ATTN_WIDTH // MXU_DIM):
        z = _dot(xb, w_ref[:, OFF_Q + c * MXU_DIM:OFF_Q + (c + 1) * MXU_DIM])
        q = _norm_rope(z, gq_ref[...], mean_mat, cos, sin, lo_mask).astype(BF16)
        for j in range(MXU_DIM // HEAD_DIM):
            q_ref[0, c * (MXU_DIM // HEAD_DIM) + j] = q[:, j * HEAD_DIM:(j + 1) * HEAD_DIM]

    z = _dot(xb, w_ref[:, OFF_K:OFF_V])
    k = _norm_rope(z, gk_ref[...], mean_mat, cos, sin, lo_mask).astype(BF16)
    vt = _dot(xb, w_ref[:, OFF_V:OFF_G]).T.astype(BF16)
    for j in range(N_KV_HEADS):
        k_ref[0, j] = k[:, j * HEAD_DIM:(j + 1) * HEAD_DIM]
        v_ref[0, j] = vt[j * HEAD_DIM:(j + 1) * HEAD_DIM, :]

    gw = 512
    for c in range((w_ref.shape[1] - OFF_G) // gw):
        sl = slice(OFF_G + c * gw, OFF_G + (c + 1) * gw)
        z = _dot(xb, w_ref[:, sl]) + bg_ref[:, c * gw:(c + 1) * gw]
        g_ref[:, c * gw:(c + 1) * gw] = _sigmoid(z).astype(BF16)


def _inproj(x2, w_in, b_gate, q_g, k_g, batch, seq):
    n, d = x2.shape
    tm = _tile(seq, 512)
    spb = seq // tm
    in_width = w_in.shape[1]
    gate_w = in_width - OFF_G
    cos, sin = _rope_tables(seq)
    mean_mat = _head_mean_matrix()
    scale = HEAD_DIM ** -0.5 * math.log2(math.e)
    gq =jnp.tile(q_g.astype(F32) * scale, MXU_DIM // HEAD_DIM)[None, :]
    gk = jnp.tile(k_g.astype(F32), MXU_DIM // HEAD_DIM)[None, :]
    score_bound = (HEAD_DIM * jnp.max(jnp.abs(gq)) * jnp.max(jnp.abs(gk))).reshape(1)
    const = lambda i: (0, 0)
    outs = pl.pallas_call(
        _inproj_kernel,
        grid=(n // tm,),
        in_specs=[
            pl.BlockSpec((tm, d), lambda i: (i, 0)),
            pl.BlockSpec((d, in_width), const),
            pl.BlockSpec((1, gate_w), const),
            pl.BlockSpec((1, MXU_DIM), const),
            pl.BlockSpec((1, MXU_DIM), const),
            pl.BlockSpec((MXU_DIM, MXU_DIM), const),
            pl.BlockSpec((tm, MXU_DIM), lambda i: (i % spb, 0)),
            pl.BlockSpec((tm, MXU_DIM), lambda i: (i % spb, 0)),
        ],
        out_specs=[
            pl.BlockSpec((tm, FOURIER_WIDTH), lambda i: (i, 0)),
            pl.BlockSpec((1, N_Q_HEADS, tm, HEAD_DIM), lambda i: (i // spb, 0, i % spb, 0)),
            pl.BlockSpec((1, N_KV_HEADS, tm, HEAD_DIM), lambda i: (i // spb, 0, i % spb, 0)),
            pl.BlockSpec((1, N_KV_HEADS, HEAD_DIM, tm), lambda i: (i // spb, 0, 0, i % spb)),
            pl.BlockSpec((tm, gate_w), lambda i: (i, 0)),
        ],
        out_shape=[
            jax.ShapeDtypeStruct((n, FOURIER_WIDTH), BF16),
            jax.ShapeDtypeStruct((batch, N_Q_HEADS, seq, HEAD_DIM), BF16),
            jax.ShapeDtypeStruct((batch, N_KV_HEADS, seq, HEAD_DIM), BF16),
            jax.ShapeDtypeStruct((batch, N_KV_HEADS, HEAD_DIM, seq), BF16),
            jax.ShapeDtypeStruct((n, gate_w), BF16),
        ],
        compiler_params=_params("parallel"),
        name="inproj",
    )(x2, w_in.astype(BF16), b_gate.astype(F32)[None, :], gq, gk, mean_mat, cos, sin)
    return (*outs, score_bound)


def _fourier_kernel(u_ref, chan_ref, seqm_ref, wp_ref, g_ref, o_ref, ab_ref):
    seq = u_ref.shape[1]

    @pl.when(pl.program_id(1) == 0)
    def _():
        ab = _dot(u_ref[0], chan_ref[...])
        ab_ref[0:seq, :] = ab[:, 0:FOURIER_WIDTH].astype(BF16)
        ab_ref[seq:2 * seq, :] = ab[:, FOURIER_WIDTH:].astype(BF16)

    f = _dot(seqm_ref[...], ab_ref[...]).astype(BF16)
    y = _dot(f, wp_ref[...])
    o_ref[0] = (g_ref[0].astype(F32) * y).astype(BF16)


def _fourier(u3, g3, w_four_proj):
    batch, seq, _ = u3.shape
    d = w_four_proj.shape[1]
    tr = _tile(seq, 512)
    chan, seqm = _dft_tables(seq)
    return pl.pallas_call(
        _fourier_kernel,
        grid=(batch, seq // tr),
        in_specs=[
            pl.BlockSpec((1, seq, FOURIER_WIDTH), lambda b, r: (b, 0, 0)),
            pl.BlockSpec((FOURIER_WIDTH, 2 * FOURIER_WIDTH), lambda b, r: (0, 0)),
            pl.BlockSpec((tr, 2 * seq), lambda b, r: (r, 0)),
            pl.BlockSpec((FOURIER_WIDTH, d), lambda b, r: (0, 0)),
            pl.BlockSpec((1, tr, d), lambda b, r: (b, r, 0)),
        ],
        out_specs=pl.BlockSpec((1, tr, d), lambda b, r: (b, r, 0)),
        out_shape=jax.ShapeDtypeStruct((batch, seq, d), BF16),
        scratch_shapes=[pltpu.VMEM((2 * seq, FOURIER_WIDTH), BF16)],
        compiler_params=_params("parallel", "arbitrary"),
        name="fourier",
    )(u3, chan, seqm, w_four_proj.astype(BF16), g3)


def _attention_kernel(bounded, sb_ref, q_ref, k_ref, vt_ref, o_ref, vone_ref):
    seq = k_ref.shape[2]

    @pl.when(pl.program_id(2) == 0)
    def _():
        vone_ref[0:HEAD_DIM, :] = vt_ref[0, 0]
        vone_ref[HEAD_DIM:, :] = jnp.ones((HEAD_DIM, seq), BF16)

    k = k_ref[0, 0]
    vone = vone_ref[...]
    outs = []
    st_next = _dot_nt(k, q_ref[0, 0])
    for g in range(Q_GROUP):
        st = st_next
        if g + 1 < Q_GROUP:
            st_next = _dot_nt(k, q_ref[0, g + 1])
        if bounded:
            m = sb_ref[0]
        else:
            m = jnp.max(st, axis=0, keepdims=True)
        pt = jnp.exp2(st - m).astype(BF16)
        ol = _dot(vone, pt)
        ot = ol[0:HEAD_DIM, :] / ol[HEAD_DIM:HEAD_DIM + 1, :]
        outs.append(ot.T.astype(BF16))
    o_ref[0] = jnp.concatenate(outs, axis=1)


def _attention(q4, k4, vt4, score_bound):
    batch, _, seq, _ = q4.shape
    tq = _tile(seq, 256)

    def call(bounded):
        return pl.pallas_call(
            functools.partial(_attention_kernel, bounded),
            grid_spec=pltpu.PrefetchScalarGridSpec(
                num_scalar_prefetch=1,
                grid=(batch, N_KV_HEADS, seq // tq),
                in_specs=[
                    pl.BlockSpec((1, Q_GROUP, tq, HEAD_DIM), lambda b, h, i, sb: (b, h, i, 0)),
                    pl.BlockSpec((1, 1, seq, HEAD_DIM), lambda b, h, i, sb: (b, h, 0, 0)),
                    pl.BlockSpec((1, 1, HEAD_DIM, seq), lambda b, h, i, sb: (b, h, 0, 0)),
                ],
                out_specs=pl.BlockSpec((1, tq, Q_GROUP * HEAD_DIM), lambda b, h, i, sb: (b, i, h)),
                scratch_shapes=[pltpu.VMEM((2 * HEAD_DIM, seq), BF16)],
            ),
            out_shape=jax.ShapeDtypeStruct((batch, seq, ATTN_WIDTH), BF16),
            compiler_params=_params("parallel", "parallel", "arbitrary"),
            name="attention_bounded" if bounded else "attention",
        )(score_bound, q4, k4, vt4)

    return lax.cond(2.0 * score_bound[0] <= MAX_EXP2_RANGE,
                    lambda: call(True), lambda: call(False))


def _layer_norm(h, g, b):
    mu = jnp.mean(h, axis=-1, keepdims=True)
    c = h - mu
    var = jnp.mean(c * c, axis=-1, keepdims=True)
    return c * lax.rsqrt(var + LN_EPS) * g + b


def _slab(ref, row):
    return ref.at[pl.ds(pl.multiple_of(row * TOKEN_ROWS, TOKEN_ROWS), TOKEN_ROWS), :]


def _pack_tokens(dst_ref, val):
    t = val.shape[0]
    words = _pack_words(val)
    for s in range(TOKEN_ROWS):
        dst_ref[pl.ds(s, t, stride=TOKEN_ROWS), :] = words[s]


def _pack_words(val):
    half = val.shape[1] // 2
    assert half == TOKEN_ROWS * LANES
    bits = lax.bitcast_convert_type(val.astype(BF16).astype(F32), jnp.uint32)
    words = (bits[:, :half] >> 16) | bits[:, half:]
    return [words[:, s * LANES:(s + 1) * LANES] for s in range(TOKEN_ROWS)]


def _unpack_words(words):
    lo = [lax.bitcast_convert_type(w << 16, F32) for w in words]
    hi = [lax.bitcast_convert_type(w & jnp.uint32(0xFFFF0000), F32) for w in words]
    return jnp.concatenate(lo + hi, axis=1)


def _unpack_tokens(src_ref, tok0, t):
    return _unpack_words([src_ref[pl.ds(tok0 * TOKEN_ROWS + s, t, stride=TOKEN_ROWS), :]
                          for s in range(TOKEN_ROWS)])


def _mix_kernel(alpha, o_ref, mf_ref, g_ref, x_ref, wap_ref, wo_ref, lg_ref, lb_ref,
                x1_ref, x1w_ref):
    y = _dot(o_ref[...], wap_ref[...])
    merged = mf_ref[...].astype(F32) + g_ref[...].astype(F32) * y
    mix = _dot(merged.astype(BF16), wo_ref[...])
    x1 = _layer_norm(alpha * x_ref[...] + mix, lg_ref[...], lb_ref[...])
    x1_ref[...] = x1
    for s, w in enumerate(_pack_words(x1)):
        x1w_ref[s] = w


def _mix(o2, mf2, g2, x2, w_attn_proj, w_o, ln_g, ln_b, alpha):
    n, d = x2.shape
    assert d == 2 * TOKEN_ROWS * LANES
    tm = _tile(n, 512)
    const = lambda i: (0, 0)
    return pl.pallas_call(
        functools.partial(_mix_kernel, alpha),
        grid=(n // tm,),
        in_specs=[
            pl.BlockSpec((tm, ATTN_WIDTH), lambda i: (i, 0)),
            pl.BlockSpec((tm, d), lambda i: (i, 0)),
            pl.BlockSpec((tm, d), lambda i: (i, 1)),
            pl.BlockSpec((tm, d), lambda i: (i, 0)),
            pl.BlockSpec((ATTN_WIDTH, d), const),
            pl.BlockSpec((d, d), const),
            pl.BlockSpec((1, d), const),
            pl.BlockSpec((1, d), const),
        ],
        out_specs=[
            pl.BlockSpec((tm, d), lambda i: (i, 0)),
            pl.BlockSpec((TOKEN_ROWS, tm, LANES), lambda i: (0, i, 0)),
        ],
        out_shape=[
            jax.ShapeDtypeStruct((n, d), F32),
            jax.ShapeDtypeStruct((TOKEN_ROWS, n, LANES), jnp.uint32),
        ],
        compiler_params=_params("parallel"),
        name="mix",
    )(o2, mf2, g2, x2, w_attn_proj.astype(BF16), w_o.astype(BF16),
      ln_g.astype(F32)[None, :], ln_b.astype(F32)[None, :])


def _route_kernel(x_ref, wh_ref, wl_ref, eb_ref, tri_ref,
                  eidx_ref, rank_ref, w_ref, cnt_ref, carry_ref):
    tm = x_ref.shape[0]

    @pl.when(pl.program_id(0) == 0)
    def _():
        carry_ref[...] = jnp.zeros_like(carry_ref)

    x = x_ref[...]
    xh = x.astype(BF16)
    xl = (x - xh.astype(F32)).astype(BF16)
    wh = wh_ref[...]
    logits = _dot_nt(wh, xh) + _dot_nt(wh, xl) + _dot_nt(wl_ref[...], xh)
    scores = _sigmoid(logits)
    biased = scores + eb_ref[:, 0:1]
    neg = -jnp.inf

    sub_iota = lax.broadcasted_iota(jnp.int32, (GROUP_SIZE, tm), 0).astype(F32)
    gs = []
    for g in range(N_EXPERT_GROUPS):
        blk = biased[g * GROUP_SIZE:(g + 1) * GROUP_SIZE, :]
        m1 = jnp.max(blk, axis=0, keepdims=True)
        a1 = jnp.min(jnp.where(blk == m1, sub_iota, float(GROUP_SIZE)), axis=0, keepdims=True)
        m2 = jnp.max(jnp.where(sub_iota == a1, neg, blk), axis=0, keepdims=True)
        gs.append(m1 + m2)

    masked = []
    for g in range(N_EXPERT_GROUPS):
        beat = jnp.zeros((1, tm), F32)
        for h in range(N_EXPERT_GROUPS):
            if h == g:
                continue
            wins = (gs[h] >= gs[g]) if h < g else (gs[h] > gs[g])
            beat = beat + jnp.where(wins, 1.0, 0.0)
        keep = beat < float(TOPK_GROUPS)
        blk = biased[g * GROUP_SIZE:(g + 1) * GROUP_SIZE, :]
        masked.append(jnp.where(keep, blk, neg))
    masked = jnp.concatenate(masked, axis=0)

    e_iota = lax.broadcasted_iota(jnp.int32, (N_EXPERTS, tm), 0).astype(F32)
    sel = jnp.zeros((N_EXPERTS, tm), F32)
    idxs, ws = [], []
    for _ in range(TOP_K):
        mx = jnp.max(masked, axis=0, keepdims=True)
        idx = jnp.min(jnp.where(masked == mx, e_iota, float(N_EXPERTS)), axis=0, keepdims=True)
        hit = e_iota == idx
        masked = jnp.where(hit, neg, masked)
        sel = jnp.where(hit, 1.0, sel)
        idxs.append(idx)
        ws.append(jnp.sum(jnp.where(hit, scores, 0.0), axis=0, keepdims=True))

    carry = carry_ref[...]
    selb = sel.astype(BF16)
    prefix = _dot(selb, tri_ref[...])
    rank_all = prefix + jnp.concatenate([carry] * (tm // LANES), axis=1)
    total = carry + _dot(selb, jnp.ones((tm, LANES), BF16))
    carry_ref[...] = total
    cnt_ref[...] = total

    wsum = ws[0]
    for j in range(1, TOP_K):
        wsum = wsum + ws[j]
    for j in range(TOP_K):
        eidx_ref[j:j + 1, :] = idxs[j].astype(jnp.int32)
        r = jnp.sum(jnp.where(e_iota == idxs[j], rank_all, 0.0), axis=0, keepdims=True)
        rank_ref[j:j + 1, :] = r.astype(jnp.int32)
        w_ref[j:j + 1, :] = ws[j] / wsum * ROUTED_SCALE


def _route(x1, w_router, e_bias):
    n, d = x1.shape
    tm = _tile(n, 512)
    wt = w_router.astype(F32).T
    wh = wt.astype(BF16)
    wl = (wt - wh.astype(F32)).astype(BF16)
    eb = jnp.broadcast_to(e_bias.astype(F32)[:, None], (N_EXPERTS, LANES))
    tri = jnp.asarray(np.triu(np.ones((tm, tm)), k=1), BF16)
    const = lambda i: (0, 0)
    return pl.pallas_call(
        _route_kernel,
        grid=(n // tm,),
        in_specs=[
            pl.BlockSpec((tm, d), lambda i: (i, 0)),
            pl.BlockSpec((N_EXPERTS, d), const),
            pl.BlockSpec((N_EXPERTS, d), const),
            pl.BlockSpec((N_EXPERTS, LANES), const),
            pl.BlockSpec((tm, tm), const),
        ],
        out_specs=[
            pl.BlockSpec((TOP_K, tm), lambda i: (0, i)),
            pl.BlockSpec((TOP_K, tm), lambda i: (0, i)),
            pl.BlockSpec((TOP_K, tm), lambda i: (0, i)),
            pl.BlockSpec((N_EXPERTS, LANES), const),
        ],
        out_shape=[
            jax.ShapeDtypeStruct((TOP_K, n), jnp.int32),
            jax.ShapeDtypeStruct((TOP_K, n), jnp.int32),
            jax.ShapeDtypeStruct((TOP_K, n), F32),
            jax.ShapeDtypeStruct((N_EXPERTS, LANES), F32),
        ],
        scratch_shapes=[pltpu.VMEM((N_EXPERTS, LANES), F32)],
        compiler_params=_params("arbitrary"),
        name="route",
    )(x1, wh, wl, eb, tri)


def _dest_kernel(eidx_ref, rank_ref, ps_ref, dest_ref):
    tm = eidx_ref.shape[1]
    e_iota = lax.broadcasted_iota(jnp.int32, (N_EXPERTS, tm), 0)
    ps = jnp.concatenate([ps_ref[...]] * (tm // LANES), axis=1)
    for j in range(TOP_K):
        hit = e_iota == eidx_ref[j:j + 1, :]
        start = jnp.sum(jnp.where(hit, ps, 0.0), axis=0, keepdims=True)
        dest_ref[j:j + 1, :] = start.astype(jnp.int32) + rank_ref[j:j + 1, :]


def _dest(eidx, rank, pad_start):
    n = eidx.shape[1]
    tm = _tile(n, 512)
    ps = jnp.broadcast_to(pad_start.astype(F32)[:, None], (N_EXPERTS, LANES))
    return pl.pallas_call(
        _dest_kernel,
        grid=(n // tm,),
        in_specs=[
            pl.BlockSpec((TOP_K, tm), lambda i: (0, i)),
            pl.BlockSpec((TOP_K, tm), lambda i: (0, i)),
            pl.BlockSpec((N_EXPERTS, LANES), lambda i: (0, 0)),
        ],
        out_specs=pl.BlockSpec((TOP_K, tm), lambda i: (0, i)),
        out_shape=jax.ShapeDtypeStruct((TOP_K, n), jnp.int32),
        compiler_params=_params("parallel"),
        name="dest",
    )(eidx, rank, ps)


def _tails_kernel(tail_ref, xs_ref, zero_ref, sem):
    zero_ref[...] = jnp.zeros_like(zero_ref)

    def tail_copy(e):
        rows = EXPERT_BLOCK * TOKEN_ROWS
        row0 = pl.multiple_of(tail_ref[e] * rows, rows)
        return pltpu.make_async_copy(zero_ref, xs_ref.at[pl.ds(row0, rows), :], sem)

    def zstart(e, c):
        @pl.when(tail_ref[e] >= 0)
        def _():
            tail_copy(e).start()
        return c

    def zwait(e, c):
        @pl.when(tail_ref[e] >= 0)
        def _():
            tail_copy(e).wait()
        return c

    lax.fori_loop(0, N_EXPERTS, zstart, 0)
    lax.fori_loop(0, N_EXPERTS, zwait, 0)


def _sc_scatter_rows(xs_init, src, idx):
    nr = src.shape[0]
    slots = idx.shape[0]
    mesh = plsc.VectorSubcoreMesh(core_axis_name="core", subcore_axis_name="subcore")

    @functools.partial(pl.kernel, mesh=mesh, scratch_types=[], out_type=())
    def scatter(src_hbm, idx_hbm, xs_hbm):
        def window(src_vmem, *idx_vmems):
            for idx_vmem in idx_vmems:
                pltpu.sync_copy(src_vmem, xs_hbm.at[idx_vmem.at[0]])

        pltpu.emit_pipeline(
            window,
            grid=(nr // SC_WINDOW,),
            in_specs=[pl.BlockSpec((SC_WINDOW, LANES), index_map=lambda i: (i, 0))]
                     + [pl.BlockSpec((1, SC_WINDOW), index_map=lambda i, j=j: (j, i))
                        for j in range(slots)],
            out_specs=[],
            core_axis_name=("core", "subcore"),
            dimension_semantics=(pltpu.PARALLEL,),
        )(src_hbm, *([idx_hbm] * slots))

    xs_ref = jax.new_ref(xs_init)
    scatter(src, idx, xs_ref)
    return xs_ref[...]


def _dispatch(tail_blk, dest, x1w, n_rows):
    n = dest.shape[1]
    xs = pl.pallas_call(
        _tails_kernel,
        grid_spec=pltpu.PrefetchScalarGridSpec(
            num_scalar_prefetch=1,
            grid=(1,),
            in_specs=[],
            out_specs=pl.BlockSpec(memory_space=pl.ANY),
            scratch_shapes=[pltpu.VMEM((EXPERT_BLOCK * TOKEN_ROWS, LANES), jnp.uint32),
                            pltpu.SemaphoreType.DMA],
        ),
        out_shape=jax.ShapeDtypeStruct((n_rows * TOKEN_ROWS, LANES), jnp.uint32),
        compiler_params=_params("arbitrary"),
        name="tails",
    )(tail_blk)
    word_rows = (dest[:, None, :] * TOKEN_ROWS
                 + jnp.arange(TOKEN_ROWS, dtype=jnp.int32)[None, :, None]).reshape(TOP_K, TOKEN_ROWS * n)
    return _sc_scatter_rows(xs, x1w.reshape(TOKEN_ROWS * n, LANES), word_rows)


def _swiglu(xb, w_in, w_down):
    h = _dot(xb, w_in)
    half = h.shape[1] // 2
    g = h[:, :half]
    act = g * _sigmoid(g) * h[:, half:]
    return _dot(act.astype(BF16), w_down)


def _experts_kernel(be_ref, nu_ref, x_ref, wi_ref, wd_ref, o_ref, wib_ref, wdb_ref):
    i = pl.program_id(0)
    used = i < nu_ref[0]
    prev = be_ref[jnp.maximum(i - 1, 0)]
    fresh = jnp.logical_or(i == 0, be_ref[i] != prev)
    blk = o_ref.shape[0] // TOKEN_ROWS

    @pl.when(jnp.logical_and(used, fresh))
    def _():
        wib_ref[...] = wi_ref[...].astype(BF16)
        wdb_ref[...] = wd_ref[...].astype(BF16)

    @pl.when(used)
    def _():
        xb = _unpack_tokens(x_ref, 0, blk).astype(BF16)
        _pack_tokens(o_ref, _swiglu(xb, wib_ref[...], wdb_ref[...]))

    @pl.when(jnp.logical_not(used))
    def _():
        o_ref[...] = jnp.zeros_like(o_ref)


def _experts(blk_e, n_used, xs, w_e_in, w_e_down):
    n_blocks = blk_e.shape[0]
    _, d, h2 = w_e_in.shape
    hdim = w_e_down.shape[1]
    rows = EXPERT_BLOCK * TOKEN_ROWS

    def x_map(i, be, nu):
        return (jnp.minimum(i, nu[0] - 1), 0)

    def w_map(i, be, nu):
        return (be[jnp.minimum(i, nu[0] - 1)], 0, 0)

    return pl.pallas_call(
        _experts_kernel,
        grid_spec=pltpu.PrefetchScalarGridSpec(
            num_scalar_prefetch=2,
            grid=(n_blocks,),
            in_specs=[
                pl.BlockSpec((rows, LANES), x_map),
                pl.BlockSpec((None, d, h2), w_map),
                pl.BlockSpec((None, hdim, d), w_map),
            ],
            out_specs=pl.BlockSpec((rows, LANES), lambda i, be, nu: (i, 0)),
            scratch_shapes=[pltpu.VMEM((d, h2), BF16), pltpu.VMEM((hdim, d), BF16)],
        ),
        out_shape=jax.ShapeDtypeStruct(xs.shape, jnp.uint32),
        compiler_params=_params("arbitrary"),
        name="experts",
    )(blk_e, n_used, xs, w_e_in, w_e_down)


def _sc_gather_rows(table, idx):
    ni = idx.shape[0]
    mesh = plsc.VectorSubcoreMesh(core_axis_name="core", subcore_axis_name="subcore")

    @functools.partial(pl.kernel, mesh=mesh, scratch_types=[],
                       out_type=jax.ShapeDtypeStruct((ni, LANES), table.dtype))
    def gather(table_hbm, idx_hbm, out_hbm):
        def window(idx_vmem, out_vmem):
            pltpu.sync_copy(table_hbm.at[idx_vmem.at[0]], out_vmem)

        pltpu.emit_pipeline(
            window,
            grid=(ni // SC_WINDOW,),
            in_specs=[pl.BlockSpec((1, SC_WINDOW), index_map=lambda i: (0, i))],
            out_specs=[pl.BlockSpec((SC_WINDOW, LANES), index_map=lambda i: (i, 0))],
            core_axis_name=("core", "subcore"),
            dimension_semantics=(pltpu.PARALLEL,),
        )(idx_hbm, out_hbm)

    return gather(table, idx.reshape(1, ni))


def _combine_kernel(alpha, dest_ref, dnext_ref, x1_ref, wt_ref, wsi_ref, wsd_ref, os_ref,
                    part_ref, buf_ref, sem):
    tm = x1_ref.shape[0]
    i = pl.program_id(0)
    slot = i % 2

    def gather(d_ref, s, start):
        def body(t, c):
            for j in range(SC_SLOTS, TOP_K):
                cp = pltpu.make_async_copy(_slab(os_ref, d_ref[j, t]),
                                           _slab(buf_ref.at[s], (j - SC_SLOTS) * tm + t), sem.at[s])
                if start:
                    cp.start(priority=j % DMA_PRIORITIES)
                else:
                    cp.wait()
            return c
        lax.fori_loop(0, tm, body, 0)

    @pl.when(i == 0)
    def _():
        gather(dest_ref, slot, True)

    @pl.when(i + 1 < pl.num_programs(0))
    def _():
        gather(dnext_ref, 1 - slot, True)

    x1 = x1_ref[...]
    acc = alpha * x1 + _swiglu(x1.astype(BF16), wsi_ref[...], wsd_ref[...])
    gather(dest_ref, slot, False)

    rows = buf_ref.at[slot]
    for j in range(SC_SLOTS, TOP_K):
        acc = acc + wt_ref[:, j:j + 1] * _unpack_tokens(rows, (j - SC_SLOTS) * tm, tm)
    part_ref[...] = acc


def _finish_kernel(part_ref, wt_ref, st_ref, lg_ref, lb_ref, out_ref):
    acc = part_ref[...]
    for j in range(SC_SLOTS):
        words = [st_ref[s, j] for s in range(TOKEN_ROWS)]
        acc = acc + wt_ref[:, j:j + 1] * _unpack_words(words)
    out_ref[...] = _layer_norm(acc, lg_ref[...], lb_ref[...])


def _combine(dest, x1, wt, w_sh_in, w_sh_down, ln_g, ln_b, os, alpha):
    n, d = x1.shape
    tm = _tile(n, 256)
    last = n // tm - 1
    const = lambda i: (0, 0)

    word_rows = (dest[None, :SC_SLOTS, :] * TOKEN_ROWS
                 + jnp.arange(TOKEN_ROWS, dtype=jnp.int32)[:, None, None]).reshape(-1)
    staged = _sc_gather_rows(os, word_rows).reshape(TOKEN_ROWS, SC_SLOTS, n, LANES)

    part = pl.pallas_call(
        functools.partial(_combine_kernel, alpha),
        grid=(n // tm,),
        in_specs=[
            pl.BlockSpec((TOP_K, tm), lambda i: (0, i), memory_space=pltpu.SMEM),
            pl.BlockSpec((TOP_K, tm), lambda i: (0, jnp.minimum(i + 1, last)), memory_space=pltpu.SMEM),
            pl.BlockSpec((tm, d), lambda i: (i, 0)),
            pl.BlockSpec((tm, TOP_K), lambda i: (i, 0)),
            pl.BlockSpec(w_sh_in.shape, const),
            pl.BlockSpec(w_sh_down.shape, const),
            pl.BlockSpec(memory_space=pl.ANY),
        ],
        out_specs=pl.BlockSpec((tm, d), lambda i: (i, 0)),
        out_shape=jax.ShapeDtypeStruct((n, d), F32),
        scratch_shapes=[
            pltpu.VMEM((2, (TOP_K - SC_SLOTS) * tm * TOKEN_ROWS, LANES), jnp.uint32),
            pltpu.SemaphoreType.DMA((2,)),
        ],
        compiler_params=_params("arbitrary"),
        name="combine",
    )(dest, dest, x1, wt, w_sh_in.astype(BF16), w_sh_down.astype(BF16), os)

    tf = _tile(n, 512)
    return pl.pallas_call(
        _finish_kernel,
        grid=(n // tf,),
        in_specs=[
            pl.BlockSpec((tf, d), lambda i: (i, 0)),
            pl.BlockSpec((tf, TOP_K), lambda i: (i, 0)),
            pl.BlockSpec((TOKEN_ROWS, SC_SLOTS, tf, LANES), lambda i: (0, 0, i, 0)),
            pl.BlockSpec((1, d), const),
            pl.BlockSpec((1, d), const),
        ],
        out_specs=pl.BlockSpec((tf, d), lambda i: (i, 0)),
        out_shape=jax.ShapeDtypeStruct((n, d), F32),
        compiler_params=_params("parallel"),
        name="finish",
    )(part, wt, staged, ln_g.astype(F32)[None, :], ln_b.astype(F32)[None, :])


def _block_layout(counts, n_assign):
    n_blocks = (n_assign + N_EXPERTS * (EXPERT_BLOCK - 1) + EXPERT_BLOCK - 1) // EXPERT_BLOCK
    nblk = (counts + EXPERT_BLOCK - 1) // EXPERT_BLOCK
    blk_end = jnp.cumsum(nblk)
    pad_start = (blk_end - nblk) * EXPERT_BLOCK
    blk_e = jnp.sum(blk_end[None, :] <= jnp.arange(n_blocks, dtype=jnp.int32)[:, None], axis=1)
    blk_e = jnp.minimum(blk_e, N_EXPERTS - 1).astype(jnp.int32)
    n_used = blk_end[-1:].astype(jnp.int32)
    tail_blk = jnp.where(nblk > 0, blk_end - 1, -1).astype(jnp.int32)
    return n_blocks, pad_start.astype(jnp.int32), blk_e, n_used, tail_blk


def _layer(x, w_in, b_gate, q_g, k_g, w_four_proj, w_attn_proj, w_o, ln1_g, ln1_b,
           w_router, e_bias, w_e_in, w_e_down, w_sh_in, w_sh_down, ln2_g, ln2_b, alpha):
    batch, seq, d = x.shape
    n = batch * seq
    x2 = x.reshape(n, d)

    u, q4, k4, vt4, gates, score_bound = _inproj(x2, w_in, b_gate, q_g, k_g, batch, seq)
    mf = _fourier(u.reshape(batch, seq, FOURIER_WIDTH), gates.reshape(batch, seq, -1), w_four_proj)
    o = _attention(q4, k4, vt4, score_bound)
    x1, x1w = _mix(o.reshape(n, ATTN_WIDTH), mf.reshape(n, d), gates, x2,
                   w_attn_proj, w_o, ln1_g, ln1_b, alpha)

    eidx, rank, wts, cnt = _route(x1, w_router, e_bias)
    counts = cnt[:, 0].astype(jnp.int32)
    n_blocks, pad_start, blk_e, n_used, tail_blk = _block_layout(counts, n * TOP_K)
    dest = _dest(eidx, rank, pad_start)

    xs = _dispatch(tail_blk, dest, x1w, n_blocks * EXPERT_BLOCK)
    os = _experts(blk_e, n_used, xs, w_e_in, w_e_down)
    out = _combine(dest, x1, wts.T, w_sh_in, w_sh_down, ln2_g, ln2_b, os, alpha)
    return out.reshape(batch, seq, d)


def kernel(x, w_in, b_gate, q_norm_g, k_norm_g, w_four_proj, w_attn_proj, w_o, ln1_g, ln1_b, w_router, e_bias, w_e_in, w_e_down, w_sh_in, w_sh_down, ln2_g, ln2_b):
    depth = w_in.shape[0]
    alpha = (2 * depth) ** 0.25
    for l in range(depth):
        x = _layer(x, w_in[l], b_gate[l], q_norm_g[l], k_norm_g[l], w_four_proj[l],
                   w_attn_proj[l], w_o[l], ln1_g[l], ln1_b[l], w_router[l], e_bias[l],
                   w_e_in[l], w_e_down[l], w_sh_in[l], w_sh_down[l], ln2_g[l], ln2_b[l], alpha)
    return x
```

```python
import functools
import math

import numpy as np
import jax
import jax.numpy as jnp
from jax import lax
from jax.experimental import pallas as pl
from jax.experimental.pallas import tpu as pltpu
from jax.experimental.pallas import tpu_sc as plsc

F32 = jnp.float32
BF16 = jnp.bfloat16

GRID_W = 64
N_FOURIER_GROUPS = 8
FOURIER_GROUP_DIM = 64
FOURIER_WIDTH = N_FOURIER_GROUPS * FOURIER_GROUP_DIM
N_Q_HEADS = 16
N_KV_HEADS = 4
HEAD_DIM = 64
Q_GROUP = N_Q_HEADS // N_KV_HEADS
ATTN_WIDTH = N_Q_HEADS * HEAD_DIM
KV_WIDTH = N_KV_HEADS * HEAD_DIM
ROPE_THETA = 10000.0
QK_EPS = 1e-6
OFF_Q = FOURIER_WIDTH
OFF_K = OFF_Q + ATTN_WIDTH
OFF_V = OFF_K + KV_WIDTH
OFF_G = OFF_V + KV_WIDTH
N_EXPERTS = 256
TOP_K = 8
N_EXPERT_GROUPS = 8
GROUP_SIZE = N_EXPERTS // N_EXPERT_GROUPS
TOPK_GROUPS = 4
ROUTED_SCALE = 2.5
LN_EPS = 1e-5

LANES = 128
SUBLANES = 8
MXU_DIM = 256
VMEM_LIMIT = 56 * 1024 * 1024

MAX_EXP2_RANGE = 100.0

EXPERT_BLOCK = 512
SC_WINDOW = 128
COMBINE_CHUNKS = 4
TOKEN_ROWS = 4

NT_DIMS = (((1,), (1,)), ((), ()))


def _dot(a, b):
    return jnp.dot(a, b, preferred_element_type=F32)


def _dot_nt(a, b):
    return lax.dot_general(a, b, NT_DIMS, preferred_element_type=F32)


def _sigmoid(x):
    return 1.0 / (1.0 + jnp.exp(-x))


def _params(*sem):
    return pltpu.CompilerParams(dimension_semantics=sem, vmem_limit_bytes=VMEM_LIMIT)


def _tile(n, pref):
    t = min(n, pref)
    assert n % t == 0, (n, t)
    return t


def _rope_tables(seq):
    lane = np.arange(MXU_DIM)
    d = lane % HEAD_DIM
    sub = d % 32
    j = sub % 16
    t = np.arange(seq)[:, None]
    pos = np.where(d[None, :] < 32, t // GRID_W, t % GRID_W).astype(np.float64)
    freq = ROPE_THETA ** (-(j.astype(np.float64)) / 16.0)
    ang = pos * freq[None, :]
    cos = np.cos(ang)
    sin = np.sin(ang) * np.where(sub < 16, -1.0, 1.0)[None, :]
    return jnp.asarray(cos, F32), jnp.asarray(sin, F32)


def _head_mean_matrix():
    i = np.arange(MXU_DIM)
    m = (i[:, None] // HEAD_DIM == i[None, :] // HEAD_DIM).astype(np.float64) / HEAD_DIM
    return jnp.asarray(m, BF16)


def _dft_tables(seq):
    c = np.arange(FOURIER_GROUP_DIM)
    ang_c = 2.0 * np.pi * ((c[:, None] * c[None, :]) % FOURIER_GROUP_DIM) / FOURIER_GROUP_DIM
    sc = 1.0 / math.sqrt(FOURIER_GROUP_DIM)
    eye = np.eye(N_FOURIER_GROUPS)
    cc = np.kron(eye, np.cos(ang_c) * sc)
    ss = np.kron(eye, np.sin(ang_c) * sc)
    chan = np.concatenate([cc, ss], axis=1)
    s = np.arange(seq)
    ang_s = 2.0 * np.pi * ((s[:, None] * s[None, :]) % seq) / seq
    ssc = 1.0 / math.sqrt(seq)
    seqm = np.concatenate([np.cos(ang_s) * ssc, -np.sin(ang_s) * ssc], axis=1)
    return jnp.asarray(chan, BF16), jnp.asarray(seqm, BF16)


def _norm_rope(z, gain, mean_mat, cos, sin, lo_mask):
    ms = _dot((z * z).astype(BF16), mean_mat)
    y = z * lax.rsqrt(ms + QK_EPS) * gain
    outs = []
    for c in range(MXU_DIM // LANES):
        yc = y[:, c * LANES:(c + 1) * LANES]
        up = pltpu.roll(yc, LANES - 16, 1)
        dn = pltpu.roll(yc, 16, 1)
        partner = jnp.where(lo_mask, up, dn)
        sl = slice(c * LANES, (c + 1) * LANES)
        outs.append(yc * cos[:, sl] + partner * sin[:, sl])
    return jnp.concatenate(outs, axis=1)


def _inproj_kernel(x_ref, w_ref, bg_ref, gq_ref, gk_ref, mm_ref, cos_ref, sin_ref,
                   u_ref, q_ref, k_ref, v_ref, g_ref):
    xb = x_ref[...].astype(BF16)
    u_ref[...] = _dot(xb, w_ref[:, 0:OFF_Q]).astype(BF16)

    lane = lax.broadcasted_iota(jnp.int32, (1, LANES), 1)
    lo_mask = (lane & 16) == 0
    mean_mat = mm_ref[...]
    cos = cos_ref[...]
    sin = sin_ref[...]

    for c in range(ATTN_WIDTH // MXU_DIM):
        z = _dot(xb, w_ref[:, OFF_Q + c * MXU_DIM:OFF_Q + (c + 1) * MXU_DIM])
        q = _norm_rope(z, gq_ref[...], mean_mat, cos, sin, lo_mask).astype(BF16)
        for j in range(MXU_DIM // HEAD_DIM):
            q_ref[0, c * (MXU_DIM // HEAD_DIM) + j] = q[:, j * HEAD_DIM:(j + 1) * HEAD_DIM]

    z = _dot(xb, w_ref[:, OFF_K:OFF_V])
    k = _norm_rope(z, gk_ref[...], mean_mat, cos, sin, lo_mask).astype(BF16)
    vt = _dot(xb, w_ref[:, OFF_V:OFF_G]).T.astype(BF16)
    for j in range(N_KV_HEADS):
        k_ref[0, j] = k[:, j * HEAD_DIM:(j + 1) * HEAD_DIM]
        v_ref[0, j] = vt[j * HEAD_DIM:(j + 1) * HEAD_DIM, :]

    gw = 512
    for c in range((w_ref.shape[1] - OFF_G) // gw):
        sl = slice(OFF_G + c * gw, OFF_G + (c + 1) * gw)
        z = _dot(xb, w_ref[:, sl]) + bg_ref[:, c * gw:(c + 1) * gw]
        g_ref[:, c * gw:(c + 1) * gw] = _sigmoid(z).astype(BF16)


def _inproj(x2, w_in, b_gate, q_g, k_g, batch, seq):
    n, d = x2.shape
    tm = _tile(seq, 512)
    spb = seq // tm
    in_width = w_in.shape[1]
    gate_w = in_width - OFF_G
    cos, sin = _rope_tables(seq)
    mean_mat = _head_mean_matrix()
    scale = HEAD_DIM ** -0.5 * math.log2(math.e)
    gq =jnp.tile(q_g.astype(F32) * scale, MXU_DIM // HEAD_DIM)[None, :]
    gk = jnp.tile(k_g.astype(F32), MXU_DIM // HEAD_DIM)[None, :]
    score_bound = (HEAD_DIM * jnp.max(jnp.abs(gq)) * jnp.max(jnp.abs(gk))).reshape(1)
    const = lambda i: (0, 0)
    outs = pl.pallas_call(
        _inproj_kernel,
        grid=(n // tm,),
        in_specs=[
            pl.BlockSpec((tm, d), lambda i: (i, 0)),
            pl.BlockSpec((d, in_width), const),
            pl.BlockSpec((1, gate_w), const),
            pl.BlockSpec((1, MXU_DIM), const),
            pl.BlockSpec((1, MXU_DIM), const),
            pl.BlockSpec((MXU_DIM, MXU_DIM), const),
            pl.BlockSpec((tm, MXU_DIM), lambda i: (i % spb, 0)),
            pl.BlockSpec((tm, MXU_DIM), lambda i: (i % spb, 0)),
        ],
        out_specs=[
            pl.BlockSpec((tm, FOURIER_WIDTH), lambda i: (i, 0)),
            pl.BlockSpec((1, N_Q_HEADS, tm, HEAD_DIM), lambda i: (i // spb, 0, i % spb, 0)),
            pl.BlockSpec((1, N_KV_HEADS, tm, HEAD_DIM), lambda i: (i // spb, 0, i % spb, 0)),
            pl.BlockSpec((1, N_KV_HEADS, HEAD_DIM, tm), lambda i: (i // spb, 0, 0, i % spb)),
            pl.BlockSpec((tm, gate_w), lambda i: (i, 0)),
        ],
        out_shape=[
            jax.ShapeDtypeStruct((n, FOURIER_WIDTH), BF16),
            jax.ShapeDtypeStruct((batch, N_Q_HEADS, seq, HEAD_DIM), BF16),
            jax.ShapeDtypeStruct((batch, N_KV_HEADS, seq, HEAD_DIM), BF16),
            jax.ShapeDtypeStruct((batch, N_KV_HEADS, HEAD_DIM, seq), BF16),
            jax.ShapeDtypeStruct((n, gate_w), BF16),
        ],
        compiler_params=_params("parallel"),
        name="inproj",
    )(x2, w_in.astype(BF16), b_gate.astype(F32)[None, :], gq, gk, mean_mat, cos, sin)
    return (*outs, score_bound)


def _fourier_kernel(u_ref, chan_ref, seqm_ref, wp_ref, g_ref, o_ref, ab_ref):
    seq = u_ref.shape[1]

    @pl.when(pl.program_id(1) == 0)
    def _():
        ab = _dot(u_ref[0], chan_ref[...])
        ab_ref[0:seq, :] = ab[:, 0:FOURIER_WIDTH].astype(BF16)
        ab_ref[seq:2 * seq, :] = ab[:, FOURIER_WIDTH:].astype(BF16)

    f = _dot(seqm_ref[...], ab_ref[...]).astype(BF16)
    y = _dot(f, wp_ref[...])
    o_ref[0] = (g_ref[0].astype(F32) * y).astype(BF16)


def _fourier(u3, g3, w_four_proj):
    batch, seq, _ = u3.shape
    d = w_four_proj.shape[1]
    tr = _tile(seq, 512)
    chan, seqm = _dft_tables(seq)
    return pl.pallas_call(
        _fourier_kernel,
        grid=(batch, seq // tr),
        in_specs=[
            pl.BlockSpec((1, seq, FOURIER_WIDTH), lambda b, r: (b, 0, 0)),
            pl.BlockSpec((FOURIER_WIDTH, 2 * FOURIER_WIDTH), lambda b, r: (0, 0)),
            pl.BlockSpec((tr, 2 * seq), lambda b, r: (r, 0)),
            pl.BlockSpec((FOURIER_WIDTH, d), lambda b, r: (0, 0)),
            pl.BlockSpec((1, tr, d), lambda b, r: (b, r, 0)),
        ],
        out_specs=pl.BlockSpec((1, tr, d), lambda b, r: (b, r, 0)),
        out_shape=jax.ShapeDtypeStruct((batch, seq, d), BF16),
        scratch_shapes=[pltpu.VMEM((2 * seq, FOURIER_WIDTH), BF16)],
        compiler_params=_params("parallel", "arbitrary"),
        name="fourier",
    )(u3, chan, seqm, w_four_proj.astype(BF16), g3)


def _attention_kernel(bounded, sb_ref, q_ref, k_ref, vt_ref, o_ref, vone_ref):
    seq = k_ref.shape[2]

    @pl.when(pl.program_id(2) == 0)
    def _():
        vone_ref[0:HEAD_DIM, :] = vt_ref[0, 0]
        vone_ref[HEAD_DIM:, :] = jnp.ones((HEAD_DIM, seq), BF16)

    k = k_ref[0, 0]
    vone = vone_ref[...]
    outs = []
    st_next = _dot_nt(k, q_ref[0, 0])
    for g in range(Q_GROUP):
        st = st_next
        if g + 1 < Q_GROUP:
            st_next = _dot_nt(k, q_ref[0, g + 1])
        if bounded:
            m = sb_ref[0]
        else:
            m = jnp.max(st, axis=0, keepdims=True)
        pt = jnp.exp2(st - m).astype(BF16)
        ol = _dot(vone, pt)
        ot = ol[0:HEAD_DIM, :] / ol[HEAD_DIM:HEAD_DIM + 1, :]
        outs.append(ot.T.astype(BF16))
    o_ref[0] = jnp.concatenate(outs, axis=1)


def _attention(q4, k4, vt4, score_bound):
    batch, _, seq, _ = q4.shape
    tq = _tile(seq, 256)

    def call(bounded):
        return pl.pallas_call(
            functools.partial(_attention_kernel, bounded),
            grid_spec=pltpu.PrefetchScalarGridSpec(
                num_scalar_prefetch=1,
                grid=(batch, N_KV_HEADS, seq // tq),
                in_specs=[
                    pl.BlockSpec((1, Q_GROUP, tq, HEAD_DIM), lambda b, h, i, sb: (b, h, i, 0)),
                    pl.BlockSpec((1, 1, seq, HEAD_DIM), lambda b, h, i, sb: (b, h, 0, 0)),
                    pl.BlockSpec((1, 1, HEAD_DIM, seq), lambda b, h, i, sb: (b, h, 0, 0)),
                ],
                out_specs=pl.BlockSpec((1, tq, Q_GROUP * HEAD_DIM), lambda b, h, i, sb: (b, i, h)),
                scratch_shapes=[pltpu.VMEM((2 * HEAD_DIM, seq), BF16)],
            ),
            out_shape=jax.ShapeDtypeStruct((batch, seq, ATTN_WIDTH), BF16),
            compiler_params=_params("parallel", "parallel", "arbitrary"),
            name="attention_bounded" if bounded else "attention",
        )(score_bound, q4, k4, vt4)

    return lax.cond(2.0 * score_bound[0] <= MAX_EXP2_RANGE,
                    lambda: call(True), lambda: call(False))


def _layer_norm(h, g, b):
    mu = jnp.mean(h, axis=-1, keepdims=True)
    c = h - mu
    var = jnp.mean(c * c, axis=-1, keepdims=True)
    return c * lax.rsqrt(var + LN_EPS) * g + b


def _pack_tokens(dst_ref, val):
    t = val.shape[0]
    words = _pack_words(val)
    for s in range(TOKEN_ROWS):
        dst_ref[pl.ds(s, t, stride=TOKEN_ROWS), :] = words[s]


def _pack_words(val):
    half = val.shape[1] // 2
    assert half == TOKEN_ROWS * LANES
    bits = lax.bitcast_convert_type(val.astype(BF16).astype(F32), jnp.uint32)
    words = (bits[:, :half] >> 16) | bits[:, half:]
    return [words[:, s * LANES:(s + 1) * LANES] for s in range(TOKEN_ROWS)]


def _unpack_words(words):
    lo = [lax.bitcast_convert_type(w << 16, F32) for w in words]
    hi = [lax.bitcast_convert_type(w & jnp.uint32(0xFFFF0000), F32) for w in words]
    return jnp.concatenate(lo + hi, axis=1)


def _unpack_tokens(src_ref, tok0, t):
    return _unpack_words([src_ref[pl.ds(tok0 * TOKEN_ROWS + s, t, stride=TOKEN_ROWS), :]
                          for s in range(TOKEN_ROWS)])


def _mix_kernel(alpha, o_ref, mf_ref, g_ref, x_ref, wap_ref, wo_ref, lg_ref, lb_ref,
                x1_ref, x1w_ref):
    y = _dot(o_ref[...], wap_ref[...])
    merged = mf_ref[...].astype(F32) + g_ref[...].astype(F32) * y
    mix = _dot(merged.astype(BF16), wo_ref[...])
    x1 = _layer_norm(alpha * x_ref[...] + mix, lg_ref[...], lb_ref[...])
    x1_ref[...] = x1
    for s, w in enumerate(_pack_words(x1)):
        x1w_ref[s] = w


def _mix(o2, mf2, g2, x2, w_attn_proj, w_o, ln_g, ln_b, alpha):
    n, d = x2.shape
    assert d == 2 * TOKEN_ROWS * LANES
    tm = _tile(n, 512)
    const = lambda i: (0, 0)
    return pl.pallas_call(
        functools.partial(_mix_kernel, alpha),
        grid=(n // tm,),
        in_specs=[
            pl.BlockSpec((tm, ATTN_WIDTH), lambda i: (i, 0)),
            pl.BlockSpec((tm, d), lambda i: (i, 0)),
            pl.BlockSpec((tm, d), lambda i: (i, 1)),
            pl.BlockSpec((tm, d), lambda i: (i, 0)),
            pl.BlockSpec((ATTN_WIDTH, d), const),
            pl.BlockSpec((d, d), const),
            pl.BlockSpec((1, d), const),
            pl.BlockSpec((1, d), const),
        ],
        out_specs=[
            pl.BlockSpec((tm, d), lambda i: (i, 0)),
            pl.BlockSpec((TOKEN_ROWS, tm, LANES), lambda i: (0, i, 0)),
        ],
        out_shape=[
            jax.ShapeDtypeStruct((n, d), F32),
            jax.ShapeDtypeStruct((TOKEN_ROWS, n, LANES), jnp.uint32),
        ],
        compiler_params=_params("parallel"),
        name="mix",
    )(o2, mf2, g2, x2, w_attn_proj.astype(BF16), w_o.astype(BF16),
      ln_g.astype(F32)[None, :], ln_b.astype(F32)[None, :])


def _route_kernel(x_ref, wh_ref, wl_ref, eb_ref, tri_ref,
                  eidx_ref, rank_ref, w_ref, cnt_ref, carry_ref):
    tm = x_ref.shape[0]

    @pl.when(pl.program_id(0) == 0)
    def _():
        carry_ref[...] = jnp.zeros_like(carry_ref)

    x = x_ref[...]
    xh = x.astype(BF16)
    xl = (x - xh.astype(F32)).astype(BF16)
    wh = wh_ref[...]
    logits = _dot_nt(wh, xh) + _dot_nt(wh, xl) + _dot_nt(wl_ref[...], xh)
    scores = _sigmoid(logits)
    biased = scores + eb_ref[:, 0:1]
    neg = -jnp.inf

    sub_iota = lax.broadcasted_iota(jnp.int32, (GROUP_SIZE, tm), 0).astype(F32)
    gs = []
    for g in range(N_EXPERT_GROUPS):
        blk = biased[g * GROUP_SIZE:(g + 1) * GROUP_SIZE, :]
        m1 = jnp.max(blk, axis=0, keepdims=True)
        a1 = jnp.min(jnp.where(blk == m1, sub_iota, float(GROUP_SIZE)), axis=0, keepdims=True)
        m2 = jnp.max(jnp.where(sub_iota == a1, neg, blk), axis=0, keepdims=True)
        gs.append(m1 + m2)

    masked = []
    for g in range(N_EXPERT_GROUPS):
        beat = jnp.zeros((1, tm), F32)
        for h in range(N_EXPERT_GROUPS):
            if h == g:
                continue
            wins = (gs[h] >= gs[g]) if h < g else (gs[h] > gs[g])
            beat = beat + jnp.where(wins, 1.0, 0.0)
        keep = beat < float(TOPK_GROUPS)
        blk = biased[g * GROUP_SIZE:(g + 1) * GROUP_SIZE, :]
        masked.append(jnp.where(keep, blk, neg))
    masked = jnp.concatenate(masked, axis=0)

    e_iota = lax.broadcasted_iota(jnp.int32, (N_EXPERTS, tm), 0).astype(F32)
    sel = jnp.zeros((N_EXPERTS, tm), F32)
    idxs, ws = [], []
    for _ in range(TOP_K):
        mx = jnp.max(masked, axis=0, keepdims=True)
        idx = jnp.min(jnp.where(masked == mx, e_iota, float(N_EXPERTS)), axis=0, keepdims=True)
        hit = e_iota == idx
        masked = jnp.where(hit, neg, masked)
        sel = jnp.where(hit, 1.0, sel)
        idxs.append(idx)
        ws.append(jnp.sum(jnp.where(hit, scores, 0.0), axis=0, keepdims=True))

    carry = carry_ref[...]
    selb = sel.astype(BF16)
    prefix = _dot(selb, tri_ref[...])
    rank_all = prefix + jnp.concatenate([carry] * (tm // LANES), axis=1)
    total = carry + _dot(selb, jnp.ones((tm, LANES), BF16))
    carry_ref[...] = total
    cnt_ref[...] = total

    wsum = ws[0]
    for j in range(1, TOP_K):
        wsum = wsum + ws[j]
    for j in range(TOP_K):
        eidx_ref[j:j + 1, :] = idxs[j].astype(jnp.int32)
        r = jnp.sum(jnp.where(e_iota == idxs[j], rank_all, 0.0), axis=0, keepdims=True)
        rank_ref[j:j + 1, :] = r.astype(jnp.int32)
        w_ref[j:j + 1, :] = ws[j] / wsum * ROUTED_SCALE


def _route(x1, w_router, e_bias):
    n, d = x1.shape
    tm = _tile(n, 512)
    wt = w_router.astype(F32).T
    wh = wt.astype(BF16)
    wl = (wt - wh.astype(F32)).astype(BF16)
    eb = jnp.broadcast_to(e_bias.astype(F32)[:, None], (N_EXPERTS, LANES))
    tri = jnp.asarray(np.triu(np.ones((tm, tm)), k=1), BF16)
    const = lambda i: (0, 0)
    return pl.pallas_call(
        _route_kernel,
        grid=(n // tm,),
        in_specs=[
            pl.BlockSpec((tm, d), lambda i: (i, 0)),
            pl.BlockSpec((N_EXPERTS, d), const),
            pl.BlockSpec((N_EXPERTS, d), const),
            pl.BlockSpec((N_EXPERTS, LANES), const),
            pl.BlockSpec((tm, tm), const),
        ],
        out_specs=[
            pl.BlockSpec((TOP_K, tm), lambda i: (0, i)),
            pl.BlockSpec((TOP_K, tm), lambda i: (0, i)),
            pl.BlockSpec((TOP_K, tm), lambda i: (0, i)),
            pl.BlockSpec((N_EXPERTS, LANES), const),
        ],
        out_shape=[
            jax.ShapeDtypeStruct((TOP_K, n), jnp.int32),
            jax.ShapeDtypeStruct((TOP_K, n), jnp.int32),
            jax.ShapeDtypeStruct((TOP_K, n), F32),
            jax.ShapeDtypeStruct((N_EXPERTS, LANES), F32),
        ],
        scratch_shapes=[pltpu.VMEM((N_EXPERTS, LANES), F32)],
        compiler_params=_params("arbitrary"),
        name="route",
    )(x1, wh, wl, eb, tri)


def _dest_kernel(eidx_ref, rank_ref, ps_ref, dest_ref):
    tm = eidx_ref.shape[1]
    e_iota = lax.broadcasted_iota(jnp.int32, (N_EXPERTS, tm), 0)
    ps = jnp.concatenate([ps_ref[...]] * (tm // LANES), axis=1)
    for j in range(TOP_K):
        hit = e_iota == eidx_ref[j:j + 1, :]
        start = jnp.sum(jnp.where(hit, ps, 0.0), axis=0, keepdims=True)
        dest_ref[j:j + 1, :] = start.astype(jnp.int32) + rank_ref[j:j + 1, :]


def _dest(eidx, rank, pad_start):
    n = eidx.shape[1]
    tm = _tile(n, 512)
    ps = jnp.broadcast_to(pad_start.astype(F32)[:, None], (N_EXPERTS, LANES))
    return pl.pallas_call(
        _dest_kernel,
        grid=(n // tm,),
        in_specs=[
            pl.BlockSpec((TOP_K, tm), lambda i: (0, i)),
            pl.BlockSpec((TOP_K, tm), lambda i: (0, i)),
            pl.BlockSpec((N_EXPERTS, LANES), lambda i: (0, 0)),
        ],
        out_specs=pl.BlockSpec((TOP_K, tm), lambda i: (0, i)),
        out_shape=jax.ShapeDtypeStruct((TOP_K, n), jnp.int32),
        compiler_params=_params("parallel"),
        name="dest",
    )(eidx, rank, ps)


def _tails_kernel(tail_ref, xs_ref, zero_ref, sem):
    zero_ref[...] = jnp.zeros_like(zero_ref)

    def tail_copy(e):
        rows = EXPERT_BLOCK * TOKEN_ROWS
        row0 = pl.multiple_of(tail_ref[e] * rows, rows)
        return pltpu.make_async_copy(zero_ref, xs_ref.at[pl.ds(row0, rows), :], sem)

    def zstart(e, c):
        @pl.when(tail_ref[e] >= 0)
        def _():
            tail_copy(e).start()
        return c

    def zwait(e, c):
        @pl.when(tail_ref[e] >= 0)
        def _():
            tail_copy(e).wait()
        return c

    lax.fori_loop(0, N_EXPERTS, zstart, 0)
    lax.fori_loop(0, N_EXPERTS, zwait, 0)


def _sc_scatter_rows(xs_init, src, idx):
    nr = src.shape[0]
    slots = idx.shape[0]
    mesh = plsc.VectorSubcoreMesh(core_axis_name="core", subcore_axis_name="subcore")

    @functools.partial(pl.kernel, mesh=mesh, scratch_types=[], out_type=())
    def scatter(src_hbm, idx_hbm, xs_hbm):
        def window(src_vmem, *idx_vmems):
            for idx_vmem in idx_vmems:
                pltpu.sync_copy(src_vmem, xs_hbm.at[idx_vmem.at[0]])

        pltpu.emit_pipeline(
            window,
            grid=(nr // SC_WINDOW,),
            in_specs=[pl.BlockSpec((SC_WINDOW, LANES), index_map=lambda i: (i, 0))]
                     + [pl.BlockSpec((1, SC_WINDOW), index_map=lambda i, j=j: (j, i))
                        for j in range(slots)],
            out_specs=[],
            core_axis_name=("core", "subcore"),
            dimension_semantics=(pltpu.PARALLEL,),
        )(src_hbm, *([idx_hbm] * slots))

    xs_ref = jax.new_ref(xs_init)
    scatter(src, idx, xs_ref)
    return xs_ref[...]


def _dispatch(tail_blk, dest, x1w, n_rows):
    n = dest.shape[1]
    xs = pl.pallas_call(
        _tails_kernel,
        grid_spec=pltpu.PrefetchScalarGridSpec(
            num_scalar_prefetch=1,
            grid=(1,),
            in_specs=[],
            out_specs=pl.BlockSpec(memory_space=pl.ANY),
            scratch_shapes=[pltpu.VMEM((EXPERT_BLOCK * TOKEN_ROWS, LANES), jnp.uint32),
                            pltpu.SemaphoreType.DMA],
        ),
        out_shape=jax.ShapeDtypeStruct((n_rows * TOKEN_ROWS, LANES), jnp.uint32),
        compiler_params=_params("arbitrary"),
        name="tails",
    )(tail_blk)
    word_rows = (dest[:, None, :] * TOKEN_ROWS
                 + jnp.arange(TOKEN_ROWS, dtype=jnp.int32)[None, :, None]).reshape(TOP_K, TOKEN_ROWS * n)
    return _sc_scatter_rows(xs, x1w.reshape(TOKEN_ROWS * n, LANES), word_rows)


def _swiglu(xb, w_in, w_down):
    h = _dot(xb, w_in)
    half = h.shape[1] // 2
    g = h[:, :half]
    act = g * _sigmoid(g) * h[:, half:]
    return _dot(act.astype(BF16), w_down)


def _experts_kernel(be_ref, nu_ref, x_ref, wi_ref, wd_ref, o_ref, wib_ref, wdb_ref):
    i = pl.program_id(0)
    used = i < nu_ref[0]
    prev = be_ref[jnp.maximum(i - 1, 0)]
    fresh = jnp.logical_or(i == 0, be_ref[i] != prev)
    blk = o_ref.shape[0] // TOKEN_ROWS

    @pl.when(jnp.logical_and(used, fresh))
    def _():
        wib_ref[...] = wi_ref[...].astype(BF16)
        wdb_ref[...] = wd_ref[...].astype(BF16)

    @pl.when(used)
    def _():
        xb = _unpack_tokens(x_ref, 0, blk).astype(BF16)
        _pack_tokens(o_ref, _swiglu(xb, wib_ref[...], wdb_ref[...]))

    @pl.when(jnp.logical_not(used))
    def _():
        o_ref[...] = jnp.zeros_like(o_ref)


def _experts(blk_e, n_used, xs, w_e_in, w_e_down):
    n_blocks = blk_e.shape[0]
    _, d, h2 = w_e_in.shape
    hdim = w_e_down.shape[1]
    rows = EXPERT_BLOCK * TOKEN_ROWS

    def x_map(i, be, nu):
        return (jnp.minimum(i, nu[0] - 1), 0)

    def w_map(i, be, nu):
        return (be[jnp.minimum(i, nu[0] - 1)], 0, 0)

    return pl.pallas_call(
        _experts_kernel,
        grid_spec=pltpu.PrefetchScalarGridSpec(
            num_scalar_prefetch=2,
            grid=(n_blocks,),
            in_specs=[
                pl.BlockSpec((rows, LANES), x_map),
                pl.BlockSpec((None, d, h2), w_map),
                pl.BlockSpec((None, hdim, d), w_map),
            ],
            out_specs=pl.BlockSpec((rows, LANES), lambda i, be, nu: (i, 0)),
            scratch_shapes=[pltpu.VMEM((d, h2), BF16), pltpu.VMEM((hdim, d), BF16)],
        ),
        out_shape=jax.ShapeDtypeStruct(xs.shape, jnp.uint32),
        compiler_params=_params("arbitrary"),
        name="experts",
    )(blk_e, n_used, xs, w_e_in, w_e_down)


def _sc_gather_rows(table, idx):
    ni = idx.shape[0]
    mesh = plsc.VectorSubcoreMesh(core_axis_name="core", subcore_axis_name="subcore")

    @functools.partial(pl.kernel, mesh=mesh, scratch_types=[],
                       out_type=jax.ShapeDtypeStruct((ni, LANES), table.dtype))
    def gather(table_hbm, idx_hbm, out_hbm):
        def window(idx_vmem, out_vmem):
            pltpu.sync_copy(table_hbm.at[idx_vmem.at[0]], out_vmem)

        pltpu.emit_pipeline(
            window,
            grid=(ni // SC_WINDOW,),
            in_specs=[pl.BlockSpec((1, SC_WINDOW), index_map=lambda i: (0, i))],
            out_specs=[pl.BlockSpec((SC_WINDOW, LANES), index_map=lambda i: (i, 0))],
            core_axis_name=("core", "subcore"),
            dimension_semantics=(pltpu.PARALLEL,),
        )(idx_hbm, out_hbm)

    return gather(table, idx.reshape(1, ni))


def _shared_kernel(alpha, x1_ref, wsi_ref, wsd_ref, part_ref):
    x1 = x1_ref[...]
    part_ref[...] = alpha * x1 + _swiglu(x1.astype(BF16), wsi_ref[...], wsd_ref[...])


def _shared(x1, w_sh_in, w_sh_down, alpha):
    n, d = x1.shape
    tm = _tile(n, 512)
    const = lambda i: (0, 0)
    return pl.pallas_call(
        functools.partial(_shared_kernel, alpha),
        grid=(n // tm,),
        in_specs=[
            pl.BlockSpec((tm, d), lambda i: (i, 0)),
            pl.BlockSpec(w_sh_in.shape, const),
            pl.BlockSpec(w_sh_down.shape, const),
        ],
        out_specs=pl.BlockSpec((tm, d), lambda i: (i, 0)),
        out_shape=jax.ShapeDtypeStruct((n, d), F32),
        compiler_params=_params("parallel"),
        name="shared",
    )(x1, w_sh_in.astype(BF16), w_sh_down.astype(BF16))


def _finish_kernel(acc_ref, wt_ref, st_ref, lg_ref, lb_ref, out_ref):
    acc = acc_ref[...]
    for j in range(TOP_K):
        words = [st_ref[s, j] for s in range(TOKEN_ROWS)]
        acc = acc + wt_ref[:, j:j + 1] * _unpack_words(words)
    out_ref[...] = _layer_norm(acc, lg_ref[...], lb_ref[...])


def _combine(dest, part, wt, ln_g, ln_b, os):
    n, d = part.shape
    nc = n // COMBINE_CHUNKS
    tf = _tile(nc, 512)
    steps = nc // tf
    const = lambda i: (0, 0)
    out = part
    for c in range(COMBINE_CHUNKS):
        word_rows = (dest[None, :, c * nc:(c + 1) * nc] * TOKEN_ROWS
                     + jnp.arange(TOKEN_ROWS, dtype=jnp.int32)[:, None, None]).reshape(-1)
        staged = _sc_gather_rows(os, word_rows).reshape(TOKEN_ROWS, TOP_K, nc, LANES)
        tile = lambda i, c=c: (c * steps + i, 0)
        out = pl.pallas_call(
            _finish_kernel,
            grid=(steps,),
            in_specs=[
                pl.BlockSpec((tf, d), tile),
                pl.BlockSpec((tf, TOP_K), tile),
                pl.BlockSpec((TOKEN_ROWS, TOP_K, tf, LANES), lambda i: (0, 0, i, 0)),
                pl.BlockSpec((1, d), const),
                pl.BlockSpec((1, d), const),
            ],
            out_specs=pl.BlockSpec((tf, d), tile),
            out_shape=jax.ShapeDtypeStruct((n, d), F32),
            input_output_aliases={0: 0},
            compiler_params=_params("arbitrary"),
            name="finish",
        )(out, wt, staged, ln_g.astype(F32)[None, :], ln_b.astype(F32)[None, :])
    return out


def _block_layout(counts, n_assign):
    n_blocks = (n_assign + N_EXPERTS * (EXPERT_BLOCK - 1) + EXPERT_BLOCK - 1) // EXPERT_BLOCK
    nblk = (counts + EXPERT_BLOCK - 1) // EXPERT_BLOCK
    blk_end = jnp.cumsum(nblk)
    pad_start = (blk_end - nblk) * EXPERT_BLOCK
    blk_e = jnp.sum(blk_end[None, :] <= jnp.arange(n_blocks, dtype=jnp.int32)[:, None], axis=1)
    blk_e = jnp.minimum(blk_e, N_EXPERTS - 1).astype(jnp.int32)
    n_used = blk_end[-1:].astype(jnp.int32)
    tail_blk = jnp.where(nblk > 0, blk_end - 1, -1).astype(jnp.int32)
    return n_blocks, pad_start.astype(jnp.int32), blk_e, n_used, tail_blk


def _layer(x, w_in, b_gate, q_g, k_g, w_four_proj, w_attn_proj, w_o, ln1_g, ln1_b,
           w_router, e_bias, w_e_in, w_e_down, w_sh_in, w_sh_down, ln2_g, ln2_b, alpha):
    batch, seq, d = x.shape
    n = batch * seq
    x2 = x.reshape(n, d)

    u, q4, k4, vt4, gates, score_bound = _inproj(x2, w_in, b_gate, q_g, k_g, batch, seq)
    mf = _fourier(u.reshape(batch, seq, FOURIER_WIDTH), gates.reshape(batch, seq, -1), w_four_proj)
    o = _attention(q4, k4, vt4, score_bound)
    x1, x1w = _mix(o.reshape(n, ATTN_WIDTH), mf.reshape(n, d), gates, x2,
                   w_attn_proj, w_o, ln1_g, ln1_b, alpha)

    eidx, rank, wts, cnt = _route(x1, w_router, e_bias)
    counts = cnt[:, 0].astype(jnp.int32)
    n_blocks, pad_start, blk_e, n_used, tail_blk = _block_layout(counts, n * TOP_K)
    dest = _dest(eidx, rank, pad_start)

    xs = _dispatch(tail_blk, dest, x1w, n_blocks * EXPERT_BLOCK)
    part = _shared(x1, w_sh_in, w_sh_down, alpha)
    os = _experts(blk_e, n_used, xs, w_e_in, w_e_down)
    out = _combine(dest, part, wts.T, ln2_g, ln2_b, os)
    return out.reshape(batch, seq, d)


def kernel(x, w_in, b_gate, q_norm_g, k_norm_g, w_four_proj, w_attn_proj, w_o, ln1_g, ln1_b, w_router, e_bias, w_e_in, w_e_down, w_sh_in, w_sh_down, ln2_g, ln2_b):
    depth = w_in.shape[0]
    alpha = (2 * depth) ** 0.25
    for l in range(depth):
        x = _layer(x, w_in[l], b_gate[l], q_norm_g[l], k_norm_g[l], w_four_proj[l],
                   w_attn_proj[l], w_o[l], ln1_g[l], ln1_b[l], w_router[l], e_bias[l],
                   w_e_in[l], w_e_down[l], w_sh_in[l], w_sh_down[l], ln2_g[l], ln2_b[l], alpha)
    return x
```

```python
import functools
import math

import numpy as np
import jax
import jax.numpy as jnp
from jax import lax
from jax.experimental import pallas as pl
from jax.experimental.pallas import tpu as pltpu
from jax.experimental.pallas import tpu_sc as plsc

F32 = jnp.float32
BF16 = jnp.bfloat16

GRID_W = 64
N_FOURIER_GROUPS = 8
FOURIER_GROUP_DIM = 64
FOURIER_WIDTH = N_FOURIER_GROUPS * FOURIER_GROUP_DIM
N_Q_HEADS = 16
N_KV_HEADS = 4
HEAD_DIM = 64
Q_GROUP = N_Q_HEADS // N_KV_HEADS
ATTN_WIDTH = N_Q_HEADS * HEAD_DIM
KV_WIDTH = N_KV_HEADS * HEAD_DIM
ROPE_THETA = 10000.0
QK_EPS = 1e-6
OFF_Q = FOURIER_WIDTH
OFF_K = OFF_Q + ATTN_WIDTH
OFF_V = OFF_K + KV_WIDTH
OFF_G = OFF_V + KV_WIDTH
N_EXPERTS = 256
TOP_K = 8
N_EXPERT_GROUPS = 8
GROUP_SIZE = N_EXPERTS // N_EXPERT_GROUPS
TOPK_GROUPS = 4
ROUTED_SCALE = 2.5
LN_EPS = 1e-5

LANES = 128
SUBLANES = 8
MXU_DIM = 256
VMEM_LIMIT = 56 * 1024 * 1024

MAX_EXP2_RANGE = 100.0

EXPERT_BLOCK = 512
SC_WINDOW = 128
COMBINE_CHUNKS = 4
TOKEN_ROWS = 4

NT_DIMS = (((1,), (1,)), ((), ()))


def _dot(a, b):
    return jnp.dot(a, b, preferred_element_type=F32)


def _dot_nt(a, b):
    return lax.dot_general(a, b, NT_DIMS, preferred_element_type=F32)


def _sigmoid(x):
    return 1.0 / (1.0 + jnp.exp(-x))


def _params(*sem):
    return pltpu.CompilerParams(dimension_semantics=sem, vmem_limit_bytes=VMEM_LIMIT)


def _tile(n, pref):
    t = min(n, pref)
    assert n % t == 0, (n, t)
    return t


def _rope_tables(seq):
    lane = np.arange(MXU_DIM)
    d = lane % HEAD_DIM
    sub = d % 32
    j = sub % 16
    t = np.arange(seq)[:, None]
    pos = np.where(d[None, :] < 32, t // GRID_W, t % GRID_W).astype(np.float64)
    freq = ROPE_THETA ** (-(j.astype(np.float64)) / 16.0)
    ang = pos * freq[None, :]
    cos = np.cos(ang)
    sin = np.sin(ang) * np.where(sub < 16, -1.0, 1.0)[None, :]
    return jnp.asarray(cos, F32), jnp.asarray(sin, F32)


def _head_mean_matrix():
    i = np.arange(MXU_DIM)
    m = (i[:, None] // HEAD_DIM == i[None, :] // HEAD_DIM).astype(np.float64) / HEAD_DIM
    return jnp.asarray(m, BF16)


def _dft_tables(seq):
    c = np.arange(FOURIER_GROUP_DIM)
    ang_c = 2.0 * np.pi * ((c[:, None] * c[None, :]) % FOURIER_GROUP_DIM) / FOURIER_GROUP_DIM
    sc = 1.0 / math.sqrt(FOURIER_GROUP_DIM)
    eye = np.eye(N_FOURIER_GROUPS)
    cc = np.kron(eye, np.cos(ang_c) * sc)
    ss = np.kron(eye, np.sin(ang_c) * sc)
    chan = np.concatenate([cc, ss], axis=1)
    s = np.arange(seq)
    ang_s = 2.0 * np.pi * ((s[:, None] * s[None, :]) % seq) / seq
    ssc = 1.0 / math.sqrt(seq)
    seqm = np.concatenate([np.cos(ang_s) * ssc, -np.sin(ang_s) * ssc], axis=1)
    return jnp.asarray(chan, BF16), jnp.asarray(seqm, BF16)


def _norm_rope(z, gain, mean_mat, cos, sin, lo_mask):
    ms = _dot((z * z).astype(BF16), mean_mat)
    y = z * lax.rsqrt(ms + QK_EPS) * gain
    outs = []
    for c in range(MXU_DIM // LANES):
        yc = y[:, c * LANES:(c + 1) * LANES]
        up = pltpu.roll(yc, LANES - 16, 1)
        dn = pltpu.roll(yc, 16, 1)
        partner = jnp.where(lo_mask, up, dn)
        sl = slice(c * LANES, (c + 1) * LANES)
        outs.append(yc * cos[:, sl] + partner * sin[:, sl])
    return jnp.concatenate(outs, axis=1)


def _inproj_kernel(x_ref, w_ref, bg_ref, gq_ref, gk_ref, mm_ref, cos_ref, sin_ref,
                   u_ref, q_ref, k_ref, v_ref, g_ref):
    xb = x_ref[...].astype(BF16)

    lane = lax.broadcasted_iota(jnp.int32, (1, LANES), 1)
    lo_mask = (lane & 16) == 0
    mean_mat = mm_ref[...]
    cos = cos_ref[...]
    sin = sin_ref[...]
    heads = MXU_DIM // HEAD_DIM

    def put_u(z):
        u_ref[...] = z.astype(BF16)

    def put_q(c, z):
        q = _norm_rope(z, gq_ref[...], mean_mat, cos, sin, lo_mask).astype(BF16)
        for j in range(heads):
            q_ref[0, c * heads + j] = q[:, j * HEAD_DIM:(j + 1) * HEAD_DIM]

    def put_k(z):
        k = _norm_rope(z, gk_ref[...], mean_mat, cos, sin, lo_mask).astype(BF16)
        for j in range(N_KV_HEADS):
            k_ref[0, j] = k[:, j * HEAD_DIM:(j + 1) * HEAD_DIM]

    def put_v(z):
        vt = z.T.astype(BF16)
        for j in range(N_KV_HEADS):
            v_ref[0, j] = vt[j * HEAD_DIM:(j + 1) * HEAD_DIM, :]

    def put_g(lo, hi, z):
        g_ref[:, lo:hi] = _sigmoid(z + bg_ref[:, lo:hi]).astype(BF16)

    stages = [((0, OFF_Q), put_u)]
    for c in range(ATTN_WIDTH // MXU_DIM):
        stages.append(((OFF_Q + c * MXU_DIM, OFF_Q + (c + 1) * MXU_DIM), functools.partial(put_q, c)))
    stages.append(((OFF_K, OFF_V), put_k))
    stages.append(((OFF_V, OFF_G), put_v))
    gw = 512
    for lo in range(0, w_ref.shape[1] - OFF_G, gw):
        stages.append(((OFF_G + lo, OFF_G + lo + gw), functools.partial(put_g, lo, lo + gw)))

    z_next = _dot(xb, w_ref[:, stages[0][0][0]:stages[0][0][1]])
    for s, (_, put) in enumerate(stages):
        z = z_next
        if s + 1 < len(stages):
            lo, hi = stages[s + 1][0]
            z_next = _dot(xb, w_ref[:, lo:hi])
        put(z)


def _inproj(x2, w_in, b_gate, q_g, k_g, batch, seq):
    n, d = x2.shape
    tm = _tile(seq, 512)
    spb = seq // tm
    in_width = w_in.shape[1]
    gate_w = in_width - OFF_G
    cos, sin = _rope_tables(seq)
    mean_mat = _head_mean_matrix()
    scale = HEAD_DIM ** -0.5 * math.log2(math.e)
    gq =jnp.tile(q_g.astype(F32) * scale, MXU_DIM // HEAD_DIM)[None, :]
    gk = jnp.tile(k_g.astype(F32), MXU_DIM // HEAD_DIM)[None, :]
    score_bound = (HEAD_DIM * jnp.max(jnp.abs(gq)) * jnp.max(jnp.abs(gk))).reshape(1)
    const = lambda i: (0, 0)
    outs = pl.pallas_call(
        _inproj_kernel,
        grid=(n // tm,),
        in_specs=[
            pl.BlockSpec((tm, d), lambda i: (i, 0)),
            pl.BlockSpec((d, in_width), const),
            pl.BlockSpec((1, gate_w), const),
            pl.BlockSpec((1, MXU_DIM), const),
            pl.BlockSpec((1, MXU_DIM), const),
            pl.BlockSpec((MXU_DIM, MXU_DIM), const),
            pl.BlockSpec((tm, MXU_DIM), lambda i: (i % spb, 0)),
            pl.BlockSpec((tm, MXU_DIM), lambda i: (i % spb, 0)),
        ],
        out_specs=[
            pl.BlockSpec((tm, FOURIER_WIDTH), lambda i: (i, 0)),
            pl.BlockSpec((1, N_Q_HEADS, tm, HEAD_DIM), lambda i: (i // spb, 0, i % spb, 0)),
            pl.BlockSpec((1, N_KV_HEADS, tm, HEAD_DIM), lambda i: (i // spb, 0, i % spb, 0)),
            pl.BlockSpec((1, N_KV_HEADS, HEAD_DIM, tm), lambda i: (i // spb, 0, 0, i % spb)),
            pl.BlockSpec((tm, gate_w), lambda i: (i, 0)),
        ],
        out_shape=[
            jax.ShapeDtypeStruct((n, FOURIER_WIDTH), BF16),
            jax.ShapeDtypeStruct((batch, N_Q_HEADS, seq, HEAD_DIM), BF16),
            jax.ShapeDtypeStruct((batch, N_KV_HEADS, seq, HEAD_DIM), BF16),
            jax.ShapeDtypeStruct((batch, N_KV_HEADS, HEAD_DIM, seq), BF16),
            jax.ShapeDtypeStruct((n, gate_w), BF16),
        ],
        compiler_params=_params("parallel"),
        name="inproj",
    )(x2, w_in.astype(BF16), b_gate.astype(F32)[None, :], gq, gk, mean_mat, cos, sin)
    return (*outs, score_bound)


def _fourier_kernel(u_ref, chan_ref, seqm_ref, wp_ref, g_ref, o_ref, ab_ref):
    seq = u_ref.shape[1]

    @pl.when(pl.program_id(1) == 0)
    def _():
        ab = _dot(u_ref[0], chan_ref[...])
        ab_ref[0:seq, :] = ab[:, 0:FOURIER_WIDTH].astype(BF16)
        ab_ref[seq:2 * seq, :] = ab[:, FOURIER_WIDTH:].astype(BF16)

    f = _dot(seqm_ref[...], ab_ref[...]).astype(BF16)
    y = _dot(f, wp_ref[...])
    o_ref[0] = (g_ref[0].astype(F32) * y).astype(BF16)


def _fourier(u3, g3, w_four_proj):
    batch, seq, _ = u3.shape
    d = w_four_proj.shape[1]
    tr = _tile(seq, 512)
    chan, seqm = _dft_tables(seq)
    return pl.pallas_call(
        _fourier_kernel,
        grid=(batch, seq // tr),
        in_specs=[
            pl.BlockSpec((1, seq, FOURIER_WIDTH), lambda b, r: (b, 0, 0)),
            pl.BlockSpec((FOURIER_WIDTH, 2 * FOURIER_WIDTH), lambda b, r: (0, 0)),
            pl.BlockSpec((tr, 2 * seq), lambda b, r: (r, 0)),
            pl.BlockSpec((FOURIER_WIDTH, d), lambda b, r: (0, 0)),
            pl.BlockSpec((1, tr, d), lambda b, r: (b, r, 0)),
        ],
        out_specs=pl.BlockSpec((1, tr, d), lambda b, r: (b, r, 0)),
        out_shape=jax.ShapeDtypeStruct((batch, seq, d), BF16),
        scratch_shapes=[pltpu.VMEM((2 * seq, FOURIER_WIDTH), BF16)],
        compiler_params=_params("parallel", "arbitrary"),
        name="fourier",
    )(u3, chan, seqm, w_four_proj.astype(BF16), g3)


def _attention_kernel(bounded, sb_ref, q_ref, k_ref, vt_ref, o_ref, vone_ref):
    seq = k_ref.shape[2]

    @pl.when(pl.program_id(2) == 0)
    def _():
        vone_ref[0:HEAD_DIM, :] = vt_ref[0, 0]
        vone_ref[HEAD_DIM:, :] = jnp.ones((HEAD_DIM, seq), BF16)

    k = k_ref[0, 0]
    vone = vone_ref[...]
    outs = []
    st_next = _dot_nt(k, q_ref[0, 0])
    for g in range(Q_GROUP):
        st = st_next
        if g + 1 < Q_GROUP:
            st_next = _dot_nt(k, q_ref[0, g + 1])
        if bounded:
            m = sb_ref[0]
        else:
            m = jnp.max(st, axis=0, keepdims=True)
        pt = jnp.exp2(st - m).astype(BF16)
        ol = _dot(vone, pt)
        ot = ol[0:HEAD_DIM, :] / ol[HEAD_DIM:HEAD_DIM + 1, :]
        outs.append(ot.T.astype(BF16))
    o_ref[0] = jnp.concatenate(outs, axis=1)


def _attention(q4, k4, vt4, score_bound):
    batch, _, seq, _ = q4.shape
    tq = _tile(seq, 256)

    def call(bounded):
        return pl.pallas_call(
            functools.partial(_attention_kernel, bounded),
            grid_spec=pltpu.PrefetchScalarGridSpec(
                num_scalar_prefetch=1,
                grid=(batch, N_KV_HEADS, seq // tq),
                in_specs=[
                    pl.BlockSpec((1, Q_GROUP, tq, HEAD_DIM), lambda b, h, i, sb: (b, h, i, 0)),
                    pl.BlockSpec((1, 1, seq, HEAD_DIM), lambda b, h, i, sb: (b, h, 0, 0)),
                    pl.BlockSpec((1, 1, HEAD_DIM, seq), lambda b, h, i, sb: (b, h, 0, 0)),
                ],
                out_specs=pl.BlockSpec((1, tq, Q_GROUP * HEAD_DIM), lambda b, h, i, sb: (b, i, h)),
                scratch_shapes=[pltpu.VMEM((2 * HEAD_DIM, seq), BF16)],
            ),
            out_shape=jax.ShapeDtypeStruct((batch, seq, ATTN_WIDTH), BF16),
            compiler_params=_params("parallel", "parallel", "arbitrary"),
            name="attention_bounded" if bounded else "attention",
        )(score_bound, q4, k4, vt4)

    return lax.cond(2.0 * score_bound[0] <= MAX_EXP2_RANGE,
                    lambda: call(True), lambda: call(False))


def _layer_norm(h, g, b):
    mu = jnp.mean(h, axis=-1, keepdims=True)
    c = h - mu
    var = jnp.mean(c * c, axis=-1, keepdims=True)
    return c * lax.rsqrt(var + LN_EPS) * g + b


def _pack_tokens(dst_ref, val):
    t = val.shape[0]
    words = _pack_words(val)
    for s in range(TOKEN_ROWS):
        dst_ref[pl.ds(s, t, stride=TOKEN_ROWS), :] = words[s]


def _pack_words(val):
    half = val.shape[1] // 2
    assert half == TOKEN_ROWS * LANES
    bits = lax.bitcast_convert_type(val.astype(BF16).astype(F32), jnp.uint32)
    words = (bits[:, :half] >> 16) | bits[:, half:]
    return [words[:, s * LANES:(s + 1) * LANES] for s in range(TOKEN_ROWS)]


def _unpack_words(words):
    lo = [lax.bitcast_convert_type(w << 16, F32) for w in words]
    hi = [lax.bitcast_convert_type(w & jnp.uint32(0xFFFF0000), F32) for w in words]
    return jnp.concatenate(lo + hi, axis=1)


def _unpack_tokens(src_ref, tok0, t):
    return _unpack_words([src_ref[pl.ds(tok0 * TOKEN_ROWS + s, t, stride=TOKEN_ROWS), :]
                          for s in range(TOKEN_ROWS)])


def _mix_kernel(alpha, o_ref, mf_ref, g_ref, x_ref, wap_ref, wo_ref, lg_ref, lb_ref,
                x1_ref, x1w_ref):
    y = _dot(o_ref[...], wap_ref[...])
    merged = mf_ref[...].astype(F32) + g_ref[...].astype(F32) * y
    mix = _dot(merged.astype(BF16), wo_ref[...])
    x1 = _layer_norm(alpha * x_ref[...] + mix, lg_ref[...], lb_ref[...])
    x1_ref[...] = x1
    for s, w in enumerate(_pack_words(x1)):
        x1w_ref[s] = w


def _mix(o2, mf2, g2, x2, w_attn_proj, w_o, ln_g, ln_b, alpha):
    n, d = x2.shape
    assert d == 2 * TOKEN_ROWS * LANES
    tm = _tile(n, 512)
    const = lambda i: (0, 0)
    return pl.pallas_call(
        functools.partial(_mix_kernel, alpha),
        grid=(n // tm,),
        in_specs=[
            pl.BlockSpec((tm, ATTN_WIDTH), lambda i: (i, 0)),
            pl.BlockSpec((tm, d), lambda i: (i, 0)),
            pl.BlockSpec((tm, d), lambda i: (i, 1)),
            pl.BlockSpec((tm, d), lambda i: (i, 0)),
            pl.BlockSpec((ATTN_WIDTH, d), const),
            pl.BlockSpec((d, d), const),
            pl.BlockSpec((1, d), const),
            pl.BlockSpec((1, d), const),
        ],
        out_specs=[
            pl.BlockSpec((tm, d), lambda i: (i, 0)),
            pl.BlockSpec((TOKEN_ROWS, tm, LANES), lambda i: (0, i, 0)),
        ],
        out_shape=[
            jax.ShapeDtypeStruct((n, d), F32),
            jax.ShapeDtypeStruct((TOKEN_ROWS, n, LANES), jnp.uint32),
        ],
        compiler_params=_params("parallel"),
        name="mix",
    )(o2, mf2, g2, x2, w_attn_proj.astype(BF16), w_o.astype(BF16),
      ln_g.astype(F32)[None, :], ln_b.astype(F32)[None, :])


def _route_kernel(x_ref, wh_ref, wl_ref, eb_ref, tri_ref,
                  eidx_ref, rank_ref, w_ref, cnt_ref, carry_ref):
    tm = x_ref.shape[0]

    @pl.when(pl.program_id(0) == 0)
    def _():
        carry_ref[...] = jnp.zeros_like(carry_ref)

    x = x_ref[...]
    xh = x.astype(BF16)
    xl = (x - xh.astype(F32)).astype(BF16)
    wh = wh_ref[...]
    logits = _dot_nt(wh, xh) + _dot_nt(wh, xl) + _dot_nt(wl_ref[...], xh)
    scores = _sigmoid(logits)
    biased = scores + eb_ref[:, 0:1]
    neg = -jnp.inf

    sub_iota = lax.broadcasted_iota(jnp.int32, (GROUP_SIZE, tm), 0).astype(F32)
    gs = []
    for g in range(N_EXPERT_GROUPS):
        blk = biased[g * GROUP_SIZE:(g + 1) * GROUP_SIZE, :]
        m1 = jnp.max(blk, axis=0, keepdims=True)
        a1 = jnp.min(jnp.where(blk == m1, sub_iota, float(GROUP_SIZE)), axis=0, keepdims=True)
        m2 = jnp.max(jnp.where(sub_iota == a1, neg, blk), axis=0, keepdims=True)
        gs.append(m1 + m2)

    masked = []
    for g in range(N_EXPERT_GROUPS):
        beat = jnp.zeros((1, tm), F32)
        for h in range(N_EXPERT_GROUPS):
            if h == g:
                continue
            wins = (gs[h] >= gs[g]) if h < g else (gs[h] > gs[g])
            beat = beat + jnp.where(wins, 1.0, 0.0)
        keep = beat < float(TOPK_GROUPS)
        blk = biased[g * GROUP_SIZE:(g + 1) * GROUP_SIZE, :]
        masked.append(jnp.where(keep, blk, neg))
    masked = jnp.concatenate(masked, axis=0)

    e_iota = lax.broadcasted_iota(jnp.int32, (N_EXPERTS, tm), 0).astype(F32)
    sel = jnp.zeros((N_EXPERTS, tm), F32)
    idxs, ws = [], []
    for _ in range(TOP_K):
        mx = jnp.max(masked, axis=0, keepdims=True)
        idx = jnp.min(jnp.where(masked == mx, e_iota, float(N_EXPERTS)), axis=0, keepdims=True)
        hit = e_iota == idx
        masked = jnp.where(hit, neg, masked)
        sel = jnp.where(hit, 1.0, sel)
        idxs.append(idx)
        ws.append(jnp.sum(jnp.where(hit, scores, 0.0), axis=0, keepdims=True))

    carry = carry_ref[...]
    selb = sel.astype(BF16)
    prefix = _dot(selb, tri_ref[...])
    rank_all = prefix + jnp.concatenate([carry] * (tm // LANES), axis=1)
    total = carry + _dot(selb, jnp.ones((tm, LANES), BF16))
    carry_ref[...] = total
    cnt_ref[...] = total

    wsum = ws[0]
    for j in range(1, TOP_K):
        wsum = wsum + ws[j]
    for j in range(TOP_K):
        eidx_ref[j:j + 1, :] = idxs[j].astype(jnp.int32)
        r = jnp.sum(jnp.where(e_iota == idxs[j], rank_all, 0.0), axis=0, keepdims=True)
        rank_ref[j:j + 1, :] = r.astype(jnp.int32)
        w_ref[j:j + 1, :] = ws[j] / wsum * ROUTED_SCALE


def _route(x1, w_router, e_bias):
    n, d = x1.shape
    tm = _tile(n, 512)
    wt = w_router.astype(F32).T
    wh = wt.astype(BF16)
    wl = (wt - wh.astype(F32)).astype(BF16)
    eb = jnp.broadcast_to(e_bias.astype(F32)[:, None], (N_EXPERTS, LANES))
    tri = jnp.asarray(np.triu(np.ones((tm, tm)), k=1), BF16)
    const = lambda i: (0, 0)
    return pl.pallas_call(
        _route_kernel,
        grid=(n // tm,),
        in_specs=[
            pl.BlockSpec((tm, d), lambda i: (i, 0)),
            pl.BlockSpec((N_EXPERTS, d), const),
            pl.BlockSpec((N_EXPERTS, d), const),
            pl.BlockSpec((N_EXPERTS, LANES), const),
            pl.BlockSpec((tm, tm), const),
        ],
        out_specs=[
            pl.BlockSpec((TOP_K, tm), lambda i: (0, i)),
            pl.BlockSpec((TOP_K, tm), lambda i: (0, i)),
            pl.BlockSpec((TOP_K, tm), lambda i: (0, i)),
            pl.BlockSpec((N_EXPERTS, LANES), const),
        ],
        out_shape=[
            jax.ShapeDtypeStruct((TOP_K, n), jnp.int32),
            jax.ShapeDtypeStruct((TOP_K, n), jnp.int32),
            jax.ShapeDtypeStruct((TOP_K, n), F32),
            jax.ShapeDtypeStruct((N_EXPERTS, LANES), F32),
        ],
        scratch_shapes=[pltpu.VMEM((N_EXPERTS, LANES), F32)],
        compiler_params=_params("arbitrary"),
        name="route",
    )(x1, wh, wl, eb, tri)


def _dest_kernel(eidx_ref, rank_ref, ps_ref, dest_ref):
    tm = eidx_ref.shape[1]
    e_iota = lax.broadcasted_iota(jnp.int32, (N_EXPERTS, tm), 0)
    ps = jnp.concatenate([ps_ref[...]] * (tm // LANES), axis=1)
    for j in range(TOP_K):
        hit = e_iota == eidx_ref[j:j + 1, :]
        start = jnp.sum(jnp.where(hit, ps, 0.0), axis=0, keepdims=True)
        dest_ref[j:j + 1, :] = start.astype(jnp.int32) + rank_ref[j:j + 1, :]


def _dest(eidx, rank, pad_start):
    n = eidx.shape[1]
    tm = _tile(n, 512)
    ps = jnp.broadcast_to(pad_start.astype(F32)[:, None], (N_EXPERTS, LANES))
    return pl.pallas_call(
        _dest_kernel,
        grid=(n // tm,),
        in_specs=[
            pl.BlockSpec((TOP_K, tm), lambda i: (0, i)),
            pl.BlockSpec((TOP_K, tm), lambda i: (0, i)),
            pl.BlockSpec((N_EXPERTS, LANES), lambda i: (0, 0)),
        ],
        out_specs=pl.BlockSpec((TOP_K, tm), lambda i: (0, i)),
        out_shape=jax.ShapeDtypeStruct((TOP_K, n), jnp.int32),
        compiler_params=_params("parallel"),
        name="dest",
    )(eidx, rank, ps)


def _tails_kernel(tail_ref, xs_ref, zero_ref, sem):
    zero_ref[...] = jnp.zeros_like(zero_ref)

    def tail_copy(e):
        rows = EXPERT_BLOCK * TOKEN_ROWS
        row0 = pl.multiple_of(tail_ref[e] * rows, rows)
        return pltpu.make_async_copy(zero_ref, xs_ref.at[pl.ds(row0, rows), :], sem)

    def zstart(e, c):
        @pl.when(tail_ref[e] >= 0)
        def _():
            tail_copy(e).start()
        return c

    def zwait(e, c):
        @pl.when(tail_ref[e] >= 0)
        def _():
            tail_copy(e).wait()
        return c

    lax.fori_loop(0, N_EXPERTS, zstart, 0)
    lax.fori_loop(0, N_EXPERTS, zwait, 0)


def _sc_scatter_rows(xs_init, src, idx):
    nr = src.shape[0]
    slots = idx.shape[0]
    mesh = plsc.VectorSubcoreMesh(core_axis_name="core", subcore_axis_name="subcore")

    @functools.partial(pl.kernel, mesh=mesh, scratch_types=[], out_type=())
    def scatter(src_hbm, idx_hbm, xs_hbm):
        def window(src_vmem, *idx_vmems):
            for idx_vmem in idx_vmems:
                pltpu.sync_copy(src_vmem, xs_hbm.at[idx_vmem.at[0]])

        pltpu.emit_pipeline(
            window,
            grid=(nr // SC_WINDOW,),
            in_specs=[pl.BlockSpec((SC_WINDOW, LANES), index_map=lambda i: (i, 0))]
                     + [pl.BlockSpec((1, SC_WINDOW), index_map=lambda i, j=j: (j, i))
                        for j in range(slots)],
            out_specs=[],
            core_axis_name=("core", "subcore"),
            dimension_semantics=(pltpu.PARALLEL,),
        )(src_hbm, *([idx_hbm] * slots))

    xs_ref = jax.new_ref(xs_init)
    scatter(src, idx, xs_ref)
    return xs_ref[...]


def _dispatch(tail_blk, dest, x1w, n_rows):
    n = dest.shape[1]
    xs = pl.pallas_call(
        _tails_kernel,
        grid_spec=pltpu.PrefetchScalarGridSpec(
            num_scalar_prefetch=1,
            grid=(1,),
            in_specs=[],
            out_specs=pl.BlockSpec(memory_space=pl.ANY),
            scratch_shapes=[pltpu.VMEM((EXPERT_BLOCK * TOKEN_ROWS, LANES), jnp.uint32),
                            pltpu.SemaphoreType.DMA],
        ),
        out_shape=jax.ShapeDtypeStruct((n_rows * TOKEN_ROWS, LANES), jnp.uint32),
        compiler_params=_params("arbitrary"),
        name="tails",
    )(tail_blk)
    word_rows = (dest[:, None, :] * TOKEN_ROWS
                 + jnp.arange(TOKEN_ROWS, dtype=jnp.int32)[None, :, None]).reshape(TOP_K, TOKEN_ROWS * n)
    return _sc_scatter_rows(xs, x1w.reshape(TOKEN_ROWS * n, LANES), word_rows)


def _swiglu(xb, w_in, w_down):
    h = _dot(xb, w_in)
    half = h.shape[1] // 2
    g = h[:, :half]
    act = g * _sigmoid(g) * h[:, half:]
    return _dot(act.astype(BF16), w_down)


def _experts_kernel(be_ref, nu_ref, fresh_ref, slot_ref, nxt_ref, x_ref, wi_hbm, wd_hbm, o_ref,
                    wi_buf, wd_buf, wib_ref, wdb_ref, sem):
    i = pl.program_id(0)
    used = i < nu_ref[0]
    blk = o_ref.shape[0] // TOKEN_ROWS

    def fetch(e, s):
        return (pltpu.make_async_copy(wi_hbm.at[e], wi_buf.at[s], sem.at[0, s]),
                pltpu.make_async_copy(wd_hbm.at[e], wd_buf.at[s], sem.at[1, s]))

    @pl.when(jnp.logical_and(used, fresh_ref[i] == 1))
    def _():
        s = slot_ref[i]

        @pl.when(i == 0)
        def _():
            for cp in fetch(be_ref[i], s):
                cp.start()

        for cp in fetch(be_ref[i], s):
            cp.wait()

        @pl.when(nxt_ref[i] >= 0)
        def _():
            for cp in fetch(nxt_ref[i], 1 - s):
                cp.start()

        wib_ref[...] = wi_buf[s].astype(BF16)
        wdb_ref[...] = wd_buf[s].astype(BF16)

    @pl.when(used)
    def _():
        xb = _unpack_tokens(x_ref, 0, blk).astype(BF16)
        _pack_tokens(o_ref, _swiglu(xb, wib_ref[...], wdb_ref[...]))

    @pl.when(jnp.logical_not(used))
    def _():
        o_ref[...] = jnp.zeros_like(o_ref)


def _experts(sched, xs, w_e_in, w_e_down):
    n_blocks = sched[0].shape[0]
    _, d, h2 = w_e_in.shape
    hdim = w_e_down.shape[1]
    rows = EXPERT_BLOCK * TOKEN_ROWS

    def x_map(i, be, nu, *_):
        return (jnp.minimum(i, nu[0] - 1), 0)

    return pl.pallas_call(
        _experts_kernel,
        grid_spec=pltpu.PrefetchScalarGridSpec(
            num_scalar_prefetch=len(sched),
            grid=(n_blocks,),
            in_specs=[
                pl.BlockSpec((rows, LANES), x_map),
                pl.BlockSpec(memory_space=pl.ANY),
                pl.BlockSpec(memory_space=pl.ANY),
            ],
            out_specs=pl.BlockSpec((rows, LANES), lambda i, *_: (i, 0)),
            scratch_shapes=[
                pltpu.VMEM((2, d, h2), F32), pltpu.VMEM((2, hdim, d), F32),
                pltpu.VMEM((d, h2), BF16), pltpu.VMEM((hdim, d), BF16),
                pltpu.SemaphoreType.DMA((2, 2)),
            ],
        ),
        out_shape=jax.ShapeDtypeStruct(xs.shape, jnp.uint32),
        compiler_params=_params("arbitrary"),
        name="experts",
    )(*sched, xs, w_e_in, w_e_down)


def _sc_gather_rows(table, idx):
    ni = idx.shape[0]
    mesh = plsc.VectorSubcoreMesh(core_axis_name="core", subcore_axis_name="subcore")

    @functools.partial(pl.kernel, mesh=mesh, scratch_types=[],
                       out_type=jax.ShapeDtypeStruct((ni, LANES), table.dtype))
    def gather(table_hbm, idx_hbm, out_hbm):
        def window(idx_vmem, out_vmem):
            pltpu.sync_copy(table_hbm.at[idx_vmem.at[0]], out_vmem)

        pltpu.emit_pipeline(
            window,
            grid=(ni // SC_WINDOW,),
            in_specs=[pl.BlockSpec((1, SC_WINDOW), index_map=lambda i: (0, i))],
            out_specs=[pl.BlockSpec((SC_WINDOW, LANES), index_map=lambda i: (i, 0))],
            core_axis_name=("core", "subcore"),
            dimension_semantics=(pltpu.PARALLEL,),
        )(idx_hbm, out_hbm)

    return gather(table, idx.reshape(1, ni))


def _shared_kernel(alpha, x1_ref, wsi_ref, wsd_ref, part_ref):
    x1 = x1_ref[...]
    part_ref[...] = alpha * x1 + _swiglu(x1.astype(BF16), wsi_ref[...], wsd_ref[...])


def _shared(x1, w_sh_in, w_sh_down, alpha):
    n, d = x1.shape
    tm = _tile(n, 512)
    const = lambda i: (0, 0)
    return pl.pallas_call(
        functools.partial(_shared_kernel, alpha),
        grid=(n // tm,),
        in_specs=[
            pl.BlockSpec((tm, d), lambda i: (i, 0)),
            pl.BlockSpec(w_sh_in.shape, const),
            pl.BlockSpec(w_sh_down.shape, const),
        ],
        out_specs=pl.BlockSpec((tm, d), lambda i: (i, 0)),
        out_shape=jax.ShapeDtypeStruct((n, d), F32),
        compiler_params=_params("parallel"),
        name="shared",
    )(x1, w_sh_in.astype(BF16), w_sh_down.astype(BF16))


def _finish_kernel(acc_ref, wt_ref, st_ref, lg_ref, lb_ref, out_ref):
    acc = acc_ref[...]
    for j in range(TOP_K):
        words = [st_ref[s, j] for s in range(TOKEN_ROWS)]
        acc = acc + wt_ref[:, j:j + 1] * _unpack_words(words)
    out_ref[...] = _layer_norm(acc, lg_ref[...], lb_ref[...])


def _combine(dest, part, wt, ln_g, ln_b, os):
    n, d = part.shape
    nc = n // COMBINE_CHUNKS
    tf = _tile(nc, 512)
    steps = nc // tf
    const = lambda i: (0, 0)
    out = part
    for c in range(COMBINE_CHUNKS):
        word_rows = (dest[None, :, c * nc:(c + 1) * nc] * TOKEN_ROWS
                     + jnp.arange(TOKEN_ROWS, dtype=jnp.int32)[:, None, None]).reshape(-1)
        staged = _sc_gather_rows(os, word_rows).reshape(TOKEN_ROWS, TOP_K, nc, LANES)
        tile = lambda i, c=c: (c * steps + i, 0)
        out = pl.pallas_call(
            _finish_kernel,
            grid=(steps,),
            in_specs=[
                pl.BlockSpec((tf, d), tile),
                pl.BlockSpec((tf, TOP_K), tile),
                pl.BlockSpec((TOKEN_ROWS, TOP_K, tf, LANES), lambda i: (0, 0, i, 0)),
                pl.BlockSpec((1, d), const),
                pl.BlockSpec((1, d), const),
            ],
            out_specs=pl.BlockSpec((tf, d), tile),
            out_shape=jax.ShapeDtypeStruct((n, d), F32),
            input_output_aliases={0: 0},
            compiler_params=_params("arbitrary"),
            name="finish",
        )(out, wt, staged, ln_g.astype(F32)[None, :], ln_b.astype(F32)[None, :])
    return out


def _block_layout(counts, n_assign):
    n_blocks = (n_assign + N_EXPERTS * (EXPERT_BLOCK - 1) + EXPERT_BLOCK - 1) // EXPERT_BLOCK
    nblk = (counts + EXPERT_BLOCK - 1) // EXPERT_BLOCK
    blk_end = jnp.cumsum(nblk)
    pad_start = (blk_end - nblk) * EXPERT_BLOCK
    blk_e = jnp.sum(blk_end[None, :] <= jnp.arange(n_blocks, dtype=jnp.int32)[:, None], axis=1)
    blk_e = jnp.minimum(blk_e, N_EXPERTS - 1).astype(jnp.int32)
    n_used = blk_end[-1:].astype(jnp.int32)
    tail_blk = jnp.where(nblk > 0, blk_end - 1, -1).astype(jnp.int32)
    has = nblk > 0
    e_ids = jnp.arange(N_EXPERTS, dtype=jnp.int32)
    nxt_ge = lax.cummin(jnp.where(has, e_ids, N_EXPERTS)[::-1])[::-1]
    nxt_e = jnp.concatenate([nxt_ge[1:], jnp.full((1,), N_EXPERTS, jnp.int32)])
    nxt_e = jnp.where(nxt_e < N_EXPERTS, nxt_e, -1)
    slot_e = (jnp.cumsum(has.astype(jnp.int32)) - 1) % 2
    blk_ids = jnp.arange(n_blocks, dtype=jnp.int32)
    fresh = (blk_ids == (blk_end - nblk)[blk_e]).astype(jnp.int32)
    sched = (blk_e, n_used, fresh, slot_e[blk_e].astype(jnp.int32), nxt_e[blk_e].astype(jnp.int32))
    return n_blocks, pad_start.astype(jnp.int32), sched, tail_blk


def _layer(x, w_in, b_gate, q_g, k_g, w_four_proj, w_attn_proj, w_o, ln1_g, ln1_b,
           w_router, e_bias, w_e_in, w_e_down, w_sh_in, w_sh_down, ln2_g, ln2_b, alpha):
    batch, seq, d = x.shape
    n = batch * seq
    x2 = x.reshape(n, d)

    u, q4, k4, vt4, gates, score_bound = _inproj(x2, w_in, b_gate, q_g, k_g, batch, seq)
    mf = _fourier(u.reshape(batch, seq, FOURIER_WIDTH), gates.reshape(batch, seq, -1), w_four_proj)
    o = _attention(q4, k4, vt4, score_bound)
    x1, x1w = _mix(o.reshape(n, ATTN_WIDTH), mf.reshape(n, d), gates, x2,
                   w_attn_proj, w_o, ln1_g, ln1_b, alpha)

    eidx, rank, wts, cnt = _route(x1, w_router, e_bias)
    counts = cnt[:, 0].astype(jnp.int32)
    n_blocks, pad_start, sched, tail_blk = _block_layout(counts, n * TOP_K)
    dest = _dest(eidx, rank, pad_start)

    xs = _dispatch(tail_blk, dest, x1w, n_blocks * EXPERT_BLOCK)
    part = _shared(x1, w_sh_in, w_sh_down, alpha)
    os = _experts(sched, xs, w_e_in, w_e_down)
    out = _combine(dest, part, wts.T, ln2_g, ln2_b, os)
    return out.reshape(batch, seq, d)


def kernel(x, w_in, b_gate, q_norm_g, k_norm_g, w_four_proj, w_attn_proj, w_o, ln1_g, ln1_b, w_router, e_bias, w_e_in, w_e_down, w_sh_in, w_sh_down, ln2_g, ln2_b):
    depth = w_in.shape[0]
    alpha = (2 * depth) ** 0.25
    for l in range(depth):
        x = _layer(x, w_in[l], b_gate[l], q_norm_g[l], k_norm_g[l], w_four_proj[l],
                   w_attn_proj[l], w_o[l], ln1_g[l], ln1_b[l], w_router[l], e_bias[l],
                   w_e_in[l], w_e_down[l], w_sh_in[l], w_sh_down[l], ln2_g[l], ln2_b[l], alpha)
    return x
```

```python
import functools
import math

import numpy as np
import jax
import jax.numpy as jnp
from jax import lax
from jax.experimental import pallas as pl
from jax.experimental.pallas import tpu as pltpu
from jax.experimental.pallas import tpu_sc as plsc

F32 = jnp.float32
BF16 = jnp.bfloat16

GRID_W = 64
N_FOURIER_GROUPS = 8
FOURIER_GROUP_DIM = 64
FOURIER_WIDTH = N_FOURIER_GROUPS * FOURIER_GROUP_DIM
N_Q_HEADS = 16
N_KV_HEADS = 4
HEAD_DIM = 64
Q_GROUP = N_Q_HEADS // N_KV_HEADS
ATTN_WIDTH = N_Q_HEADS * HEAD_DIM
KV_WIDTH = N_KV_HEADS * HEAD_DIM
ROPE_THETA = 10000.0
QK_EPS = 1e-6
OFF_Q = FOURIER_WIDTH
OFF_K = OFF_Q + ATTN_WIDTH
OFF_V = OFF_K + KV_WIDTH
OFF_G = OFF_V + KV_WIDTH
N_EXPERTS = 256
TOP_K = 8
N_EXPERT_GROUPS = 8
GROUP_SIZE = N_EXPERTS // N_EXPERT_GROUPS
TOPK_GROUPS = 4
ROUTED_SCALE = 2.5
LN_EPS = 1e-5

LANES = 128
SUBLANES = 8
MXU_DIM = 256
VMEM_LIMIT = 56 * 1024 * 1024

MAX_EXP2_RANGE = 100.0

EXPERT_BLOCK = 512
SC_WINDOW = 128
COMBINE_CHUNKS = 8
TOKEN_ROWS = 4

NT_DIMS = (((1,), (1,)), ((), ()))


def _dot(a, b):
    return jnp.dot(a, b, preferred_element_type=F32)


def _dot_nt(a, b):
    return lax.dot_general(a, b, NT_DIMS, preferred_element_type=F32)


def _sigmoid(x):
    return 1.0 / (1.0 + jnp.exp(-x))


def _params(*sem):
    return pltpu.CompilerParams(dimension_semantics=sem, vmem_limit_bytes=VMEM_LIMIT)


def _tile(n, pref):
    t = min(n, pref)
    assert n % t == 0, (n, t)
    return t


def _rope_tables(seq):
    lane = np.arange(MXU_DIM)
    d = lane % HEAD_DIM
    sub = d % 32
    j = sub % 16
    t = np.arange(seq)[:, None]
    pos = np.where(d[None, :] < 32, t // GRID_W, t % GRID_W).astype(np.float64)
    freq = ROPE_THETA ** (-(j.astype(np.float64)) / 16.0)
    ang = pos * freq[None, :]
    cos = np.cos(ang)
    sin = np.sin(ang) * np.where(sub < 16, -1.0, 1.0)[None, :]
    return jnp.asarray(cos, F32), jnp.asarray(sin, F32)


def _head_mean_matrix():
    i = np.arange(MXU_DIM)
    m = (i[:, None] // HEAD_DIM == i[None, :] // HEAD_DIM).astype(np.float64) / HEAD_DIM
    return jnp.asarray(m, BF16)


def _dft_tables(seq):
    c = np.arange(FOURIER_GROUP_DIM)
    ang_c = 2.0 * np.pi * ((c[:, None] * c[None, :]) % FOURIER_GROUP_DIM) / FOURIER_GROUP_DIM
    sc = 1.0 / math.sqrt(FOURIER_GROUP_DIM)
    eye = np.eye(N_FOURIER_GROUPS)
    cc = np.kron(eye, np.cos(ang_c) * sc)
    ss = np.kron(eye, np.sin(ang_c) * sc)
    chan = np.concatenate([cc, ss], axis=1)
    s = np.arange(seq)
    ang_s = 2.0 * np.pi * ((s[:, None] * s[None, :]) % seq) / seq
    ssc = 1.0 / math.sqrt(seq)
    seqm = np.concatenate([np.cos(ang_s) * ssc, -np.sin(ang_s) * ssc], axis=1)
    return jnp.asarray(chan, BF16), jnp.asarray(seqm, BF16)


def _norm_rope(z, gain, mean_mat, cos, sin, lo_mask):
    ms = _dot((z * z).astype(BF16), mean_mat)
    y = z * lax.rsqrt(ms + QK_EPS) * gain
    outs = []
    for c in range(MXU_DIM // LANES):
        yc = y[:, c * LANES:(c + 1) * LANES]
        up = pltpu.roll(yc, LANES - 16, 1)
        dn = pltpu.roll(yc, 16, 1)
        partner = jnp.where(lo_mask, up, dn)
        sl = slice(c * LANES, (c + 1) * LANES)
        outs.append(yc * cos[:, sl] + partner * sin[:, sl])
    return jnp.concatenate(outs, axis=1)


def _inproj_kernel(x_ref, w_ref, bg_ref, gq_ref, gk_ref, mm_ref, cos_ref, sin_ref,
                   u_ref, q_ref, k_ref, v_ref, g_ref):
    xb = x_ref[...].astype(BF16)

    lane = lax.broadcasted_iota(jnp.int32, (1, LANES), 1)
    lo_mask = (lane & 16) == 0
    mean_mat = mm_ref[...]
    cos = cos_ref[...]
    sin = sin_ref[...]
    heads = MXU_DIM // HEAD_DIM

    def put_u(z):
        u_ref[...] = z.astype(BF16)

    def put_q(c, z):
        q = _norm_rope(z, gq_ref[...], mean_mat, cos, sin, lo_mask).astype(BF16)
        for j in range(heads):
            q_ref[0, c * heads + j] = q[:, j * HEAD_DIM:(j + 1) * HEAD_DIM]

    def put_k(z):
        k = _norm_rope(z, gk_ref[...], mean_mat, cos, sin, lo_mask).astype(BF16)
        for j in range(N_KV_HEADS):
            k_ref[0, j] = k[:, j * HEAD_DIM:(j + 1) * HEAD_DIM]

    def put_v(z):
        vt = z.T.astype(BF16)
        for j in range(N_KV_HEADS):
            v_ref[0, j] = vt[j * HEAD_DIM:(j + 1) * HEAD_DIM, :]

    def put_g(lo, hi, z):
        g_ref[:, lo:hi] = _sigmoid(z + bg_ref[:, lo:hi]).astype(BF16)

    stages = [((0, OFF_Q), put_u)]
    for c in range(ATTN_WIDTH // MXU_DIM):
        stages.append(((OFF_Q + c * MXU_DIM, OFF_Q + (c + 1) * MXU_DIM), functools.partial(put_q, c)))
    stages.append(((OFF_K, OFF_V), put_k))
    stages.append(((OFF_V, OFF_G), put_v))
    gw = 512
    for lo in range(0, w_ref.shape[1] - OFF_G, gw):
        stages.append(((OFF_G + lo, OFF_G + lo + gw), functools.partial(put_g, lo, lo + gw)))

    z_next = _dot(xb, w_ref[:, stages[0][0][0]:stages[0][0][1]])
    for s, (_, put) in enumerate(stages):
        z = z_next
        if s + 1 < len(stages):
            lo, hi = stages[s + 1][0]
            z_next = _dot(xb, w_ref[:, lo:hi])
        put(z)


def _inproj(x2, w_in, b_gate, q_g, k_g, batch, seq):
    n, d = x2.shape
    tm = _tile(seq, 512)
    spb = seq // tm
    in_width = w_in.shape[1]
    gate_w = in_width - OFF_G
    cos, sin = _rope_tables(seq)
    mean_mat = _head_mean_matrix()
    scale = HEAD_DIM ** -0.5 * math.log2(math.e)
    gq =jnp.tile(q_g.astype(F32) * scale, MXU_DIM // HEAD_DIM)[None, :]
    gk = jnp.tile(k_g.astype(F32), MXU_DIM // HEAD_DIM)[None, :]
    score_bound = (HEAD_DIM * jnp.max(jnp.abs(gq)) * jnp.max(jnp.abs(gk))).reshape(1)
    const = lambda i: (0, 0)
    outs = pl.pallas_call(
        _inproj_kernel,
        grid=(n // tm,),
        in_specs=[
            pl.BlockSpec((tm, d), lambda i: (i, 0)),
            pl.BlockSpec((d, in_width), const),
            pl.BlockSpec((1, gate_w), const),
            pl.BlockSpec((1, MXU_DIM), const),
            pl.BlockSpec((1, MXU_DIM), const),
            pl.BlockSpec((MXU_DIM, MXU_DIM), const),
            pl.BlockSpec((tm, MXU_DIM), lambda i: (i % spb, 0)),
            pl.BlockSpec((tm, MXU_DIM), lambda i: (i % spb, 0)),
        ],
        out_specs=[
            pl.BlockSpec((tm, FOURIER_WIDTH), lambda i: (i, 0)),
            pl.BlockSpec((1, N_Q_HEADS, tm, HEAD_DIM), lambda i: (i // spb, 0, i % spb, 0)),
            pl.BlockSpec((1, N_KV_HEADS, tm, HEAD_DIM), lambda i: (i // spb, 0, i % spb, 0)),
            pl.BlockSpec((1, N_KV_HEADS, HEAD_DIM, tm), lambda i: (i // spb, 0, 0, i % spb)),
            pl.BlockSpec((tm, gate_w), lambda i: (i, 0)),
        ],
        out_shape=[
            jax.ShapeDtypeStruct((n, FOURIER_WIDTH), BF16),
            jax.ShapeDtypeStruct((batch, N_Q_HEADS, seq, HEAD_DIM), BF16),
            jax.ShapeDtypeStruct((batch, N_KV_HEADS, seq, HEAD_DIM), BF16),
            jax.ShapeDtypeStruct((batch, N_KV_HEADS, HEAD_DIM, seq), BF16),
            jax.ShapeDtypeStruct((n, gate_w), BF16),
        ],
        compiler_params=_params("parallel"),
        name="inproj",
    )(x2, w_in.astype(BF16), b_gate.astype(F32)[None, :], gq, gk, mean_mat, cos, sin)
    return (*outs, score_bound)


def _fourier_kernel(u_ref, chan_ref, seqm_ref, wp_ref, g_ref, o_ref, ab_ref):
    seq = u_ref.shape[1]

    @pl.when(pl.program_id(1) == 0)
    def _():
        ab = _dot(u_ref[0], chan_ref[...])
        ab_ref[0:seq, :] = ab[:, 0:FOURIER_WIDTH].astype(BF16)
        ab_ref[seq:2 * seq, :] = ab[:, FOURIER_WIDTH:].astype(BF16)

    f = _dot(seqm_ref[...], ab_ref[...]).astype(BF16)
    y = _dot(f, wp_ref[...])
    o_ref[0] = (g_ref[0].astype(F32) * y).astype(BF16)


def _fourier(u3, g3, w_four_proj):
    batch, seq, _ = u3.shape
    d = w_four_proj.shape[1]
    tr = _tile(seq, 512)
    chan, seqm = _dft_tables(seq)
    return pl.pallas_call(
        _fourier_kernel,
        grid=(batch, seq // tr),
        in_specs=[
            pl.BlockSpec((1, seq, FOURIER_WIDTH), lambda b, r: (b, 0, 0)),
            pl.BlockSpec((FOURIER_WIDTH, 2 * FOURIER_WIDTH), lambda b, r: (0, 0)),
            pl.BlockSpec((tr, 2 * seq), lambda b, r: (r, 0)),
            pl.BlockSpec((FOURIER_WIDTH, d), lambda b, r: (0, 0)),
            pl.BlockSpec((1, tr, d), lambda b, r: (b, r, 0)),
        ],
        out_specs=pl.BlockSpec((1, tr, d), lambda b, r: (b, r, 0)),
        out_shape=jax.ShapeDtypeStruct((batch, seq, d), BF16),
        scratch_shapes=[pltpu.VMEM((2 * seq, FOURIER_WIDTH), BF16)],
        compiler_params=_params("parallel", "arbitrary"),
        name="fourier",
    )(u3, chan, seqm, w_four_proj.astype(BF16), g3)


def _attention_kernel(bounded, sb_ref, q_ref, k_ref, vt_ref, o_ref, vone_ref):
    seq = k_ref.shape[2]

    @pl.when(pl.program_id(2) == 0)
    def _():
        vone_ref[0:HEAD_DIM, :] = vt_ref[0, 0]
        vone_ref[HEAD_DIM:, :] = jnp.ones((HEAD_DIM, seq), BF16)

    k = k_ref[0, 0]
    vone = vone_ref[...]
    outs = []
    st_next = _dot_nt(k, q_ref[0, 0])
    for g in range(Q_GROUP):
        st = st_next
        if g + 1 < Q_GROUP:
            st_next = _dot_nt(k, q_ref[0, g + 1])
        if bounded:
            m = sb_ref[0]
        else:
            m = jnp.max(st, axis=0, keepdims=True)
        pt = jnp.exp2(st - m).astype(BF16)
        ol = _dot(vone, pt)
        ot = ol[0:HEAD_DIM, :] / ol[HEAD_DIM:HEAD_DIM + 1, :]
        outs.append(ot.T.astype(BF16))
    o_ref[0] = jnp.concatenate(outs, axis=1)


def _attention(q4, k4, vt4, score_bound):
    batch, _, seq, _ = q4.shape
    tq = _tile(seq, 1024)

    def call(bounded):
        return pl.pallas_call(
            functools.partial(_attention_kernel, bounded),
            grid_spec=pltpu.PrefetchScalarGridSpec(
                num_scalar_prefetch=1,
                grid=(batch, N_KV_HEADS, seq // tq),
                in_specs=[
                    pl.BlockSpec((1, Q_GROUP, tq, HEAD_DIM), lambda b, h, i, sb: (b, h, i, 0)),
                    pl.BlockSpec((1, 1, seq, HEAD_DIM), lambda b, h, i, sb: (b, h, 0, 0)),
                    pl.BlockSpec((1, 1, HEAD_DIM, seq), lambda b, h, i, sb: (b, h, 0, 0)),
                ],
                out_specs=pl.BlockSpec((1, tq, Q_GROUP * HEAD_DIM), lambda b, h, i, sb: (b, i, h)),
                scratch_shapes=[pltpu.VMEM((2 * HEAD_DIM, seq), BF16)],
            ),
            out_shape=jax.ShapeDtypeStruct((batch, seq, ATTN_WIDTH), BF16),
            compiler_params=_params("parallel", "parallel", "arbitrary"),
            name="attention_bounded" if bounded else "attention",
        )(score_bound, q4, k4, vt4)

    return lax.cond(2.0 * score_bound[0] <= MAX_EXP2_RANGE,
                    lambda: call(True), lambda: call(False))


def _layer_norm(h, g, b):
    mu = jnp.mean(h, axis=-1, keepdims=True)
    c = h - mu
    var = jnp.mean(c * c, axis=-1, keepdims=True)
    return c * lax.rsqrt(var + LN_EPS) * g + b


def _pack_tokens(dst_ref, val):
    t = val.shape[0]
    words = _pack_words(val)
    for s in range(TOKEN_ROWS):
        dst_ref[pl.ds(s, t, stride=TOKEN_ROWS), :] = words[s]


def _pack_words(val):
    half = val.shape[1] // 2
    assert half == TOKEN_ROWS * LANES
    bits = lax.bitcast_convert_type(val.astype(BF16).astype(F32), jnp.uint32)
    words = (bits[:, :half] >> 16) | bits[:, half:]
    return [words[:, s * LANES:(s + 1) * LANES] for s in range(TOKEN_ROWS)]


def _unpack_words(words):
    lo = [lax.bitcast_convert_type(w << 16, F32) for w in words]
    hi = [lax.bitcast_convert_type(w & jnp.uint32(0xFFFF0000), F32) for w in words]
    return jnp.concatenate(lo + hi, axis=1)


def _unpack_tokens(src_ref, tok0, t):
    return _unpack_words([src_ref[pl.ds(tok0 * TOKEN_ROWS + s, t, stride=TOKEN_ROWS), :]
                          for s in range(TOKEN_ROWS)])


def _mix_kernel(alpha, o_ref, mf_ref, g_ref, x_ref, wap_ref, wo_ref, lg_ref, lb_ref,
                x1_ref, x1w_ref):
    y = _dot(o_ref[...], wap_ref[...])
    merged = mf_ref[...].astype(F32) + g_ref[...].astype(F32) * y
    mix = _dot(merged.astype(BF16), wo_ref[...])
    x1 = _layer_norm(alpha * x_ref[...] + mix, lg_ref[...], lb_ref[...])
    x1_ref[...] = x1
    for s, w in enumerate(_pack_words(x1)):
        x1w_ref[s] = w


def _mix(o2, mf2, g2, x2, w_attn_proj, w_o, ln_g, ln_b, alpha):
    n, d = x2.shape
    assert d == 2 * TOKEN_ROWS * LANES
    tm = _tile(n, 512)
    const = lambda i: (0, 0)
    return pl.pallas_call(
        functools.partial(_mix_kernel, alpha),
        grid=(n // tm,),
        in_specs=[
            pl.BlockSpec((tm, ATTN_WIDTH), lambda i: (i, 0)),
            pl.BlockSpec((tm, d), lambda i: (i, 0)),
            pl.BlockSpec((tm, d), lambda i: (i, 1)),
            pl.BlockSpec((tm, d), lambda i: (i, 0)),
            pl.BlockSpec((ATTN_WIDTH, d), const),
            pl.BlockSpec((d, d), const),
            pl.BlockSpec((1, d), const),
            pl.BlockSpec((1, d), const),
        ],
        out_specs=[
            pl.BlockSpec((tm, d), lambda i: (i, 0)),
            pl.BlockSpec((TOKEN_ROWS, tm, LANES), lambda i: (0, i, 0)),
        ],
        out_shape=[
            jax.ShapeDtypeStruct((n, d), F32),
            jax.ShapeDtypeStruct((TOKEN_ROWS, n, LANES), jnp.uint32),
        ],
        compiler_params=_params("parallel"),
        name="mix",
    )(o2, mf2, g2, x2, w_attn_proj.astype(BF16), w_o.astype(BF16),
      ln_g.astype(F32)[None, :], ln_b.astype(F32)[None, :])


def _route_kernel(x_ref, wh_ref, wl_ref, eb_ref, tri_ref,
                  eidx_ref, rank_ref, w_ref, cnt_ref, carry_ref):
    tm = x_ref.shape[0]

    @pl.when(pl.program_id(0) == 0)
    def _():
        carry_ref[...] = jnp.zeros_like(carry_ref)

    x = x_ref[...]
    xh = x.astype(BF16)
    xl = (x - xh.astype(F32)).astype(BF16)
    wh = wh_ref[...]
    logits = _dot_nt(wh, xh) + _dot_nt(wh, xl) + _dot_nt(wl_ref[...], xh)
    scores = _sigmoid(logits)
    biased = scores + eb_ref[:, 0:1]
    neg = -jnp.inf

    sub_iota = lax.broadcasted_iota(jnp.int32, (GROUP_SIZE, tm), 0).astype(F32)
    gs = []
    for g in range(N_EXPERT_GROUPS):
        blk = biased[g * GROUP_SIZE:(g + 1) * GROUP_SIZE, :]
        m1 = jnp.max(blk, axis=0, keepdims=True)
        a1 = jnp.min(jnp.where(blk == m1, sub_iota, float(GROUP_SIZE)), axis=0, keepdims=True)
        m2 = jnp.max(jnp.where(sub_iota == a1, neg, blk), axis=0, keepdims=True)
        gs.append(m1 + m2)

    masked = []
    for g in range(N_EXPERT_GROUPS):
        beat = jnp.zeros((1, tm), F32)
        for h in range(N_EXPERT_GROUPS):
            if h == g:
                continue
            wins = (gs[h] >= gs[g]) if h < g else (gs[h] > gs[g])
            beat = beat + jnp.where(wins, 1.0, 0.0)
        keep = beat < float(TOPK_GROUPS)
        blk = biased[g * GROUP_SIZE:(g + 1) * GROUP_SIZE, :]
        masked.append(jnp.where(keep, blk, neg))
    masked = jnp.concatenate(masked, axis=0)

    e_iota = lax.broadcasted_iota(jnp.int32, (N_EXPERTS, tm), 0).astype(F32)
    sel = jnp.zeros((N_EXPERTS, tm), F32)
    idxs, ws = [], []
    for _ in range(TOP_K):
        mx = jnp.max(masked, axis=0, keepdims=True)
        idx = jnp.min(jnp.where(masked == mx, e_iota, float(N_EXPERTS)), axis=0, keepdims=True)
        hit = e_iota == idx
        masked = jnp.where(hit, neg, masked)
        sel = jnp.where(hit, 1.0, sel)
        idxs.append(idx)
        ws.append(jnp.sum(jnp.where(hit, scores, 0.0), axis=0, keepdims=True))

    carry = carry_ref[...]
    selb = sel.astype(BF16)
    prefix = _dot(selb, tri_ref[...])
    rank_all = prefix + jnp.concatenate([carry] * (tm // LANES), axis=1)
    total = carry + _dot(selb, jnp.ones((tm, LANES), BF16))
    carry_ref[...] = total
    cnt_ref[...] = total

    wsum = ws[0]
    for j in range(1, TOP_K):
        wsum = wsum + ws[j]
    for j in range(TOP_K):
        eidx_ref[j:j + 1, :] = idxs[j].astype(jnp.int32)
        r = jnp.sum(jnp.where(e_iota == idxs[j], rank_all, 0.0), axis=0, keepdims=True)
        rank_ref[j:j + 1, :] = r.astype(jnp.int32)
        w_ref[j:j + 1, :] = ws[j] / wsum * ROUTED_SCALE


def _route(x1, w_router, e_bias):
    n, d = x1.shape
    tm = _tile(n, 512)
    wt = w_router.astype(F32).T
    wh = wt.astype(BF16)
    wl = (wt - wh.astype(F32)).astype(BF16)
    eb = jnp.broadcast_to(e_bias.astype(F32)[:, None], (N_EXPERTS, LANES))
    tri = jnp.asarray(np.triu(np.ones((tm, tm)), k=1), BF16)
    const = lambda i: (0, 0)
    return pl.pallas_call(
        _route_kernel,
        grid=(n // tm,),
        in_specs=[
            pl.BlockSpec((tm, d), lambda i: (i, 0)),
            pl.BlockSpec((N_EXPERTS, d), const),
            pl.BlockSpec((N_EXPERTS, d), const),
            pl.BlockSpec((N_EXPERTS, LANES), const),
            pl.BlockSpec((tm, tm), const),
        ],
        out_specs=[
            pl.BlockSpec((TOP_K, tm), lambda i: (0, i)),
            pl.BlockSpec((TOP_K, tm), lambda i: (0, i)),
            pl.BlockSpec((TOP_K, tm), lambda i: (0, i)),
            pl.BlockSpec((N_EXPERTS, LANES), const),
        ],
        out_shape=[
            jax.ShapeDtypeStruct((TOP_K, n), jnp.int32),
            jax.ShapeDtypeStruct((TOP_K, n), jnp.int32),
            jax.ShapeDtypeStruct((TOP_K, n), F32),
            jax.ShapeDtypeStruct((N_EXPERTS, LANES), F32),
        ],
        scratch_shapes=[pltpu.VMEM((N_EXPERTS, LANES), F32)],
        compiler_params=_params("arbitrary"),
        name="route",
    )(x1, wh, wl, eb, tri)


def _dest_kernel(eidx_ref, rank_ref, ps_ref, dest_ref):
    tm = eidx_ref.shape[1]
    e_iota = lax.broadcasted_iota(jnp.int32, (N_EXPERTS, tm), 0)
    ps = jnp.concatenate([ps_ref[...]] * (tm // LANES), axis=1)
    for j in range(TOP_K):
        hit = e_iota == eidx_ref[j:j + 1, :]
        start = jnp.sum(jnp.where(hit, ps, 0.0), axis=0, keepdims=True)
        dest_ref[j:j + 1, :] = start.astype(jnp.int32) + rank_ref[j:j + 1, :]


def _dest(eidx, rank, pad_start):
    n = eidx.shape[1]
    tm = _tile(n, 512)
    ps = jnp.broadcast_to(pad_start.astype(F32)[:, None], (N_EXPERTS, LANES))
    return pl.pallas_call(
        _dest_kernel,
        grid=(n // tm,),
        in_specs=[
            pl.BlockSpec((TOP_K, tm), lambda i: (0, i)),
            pl.BlockSpec((TOP_K, tm), lambda i: (0, i)),
            pl.BlockSpec((N_EXPERTS, LANES), lambda i: (0, 0)),
        ],
        out_specs=pl.BlockSpec((TOP_K, tm), lambda i: (0, i)),
        out_shape=jax.ShapeDtypeStruct((TOP_K, n), jnp.int32),
        compiler_params=_params("parallel"),
        name="dest",
    )(eidx, rank, ps)


def _tails_kernel(tail_ref, xs_ref, zero_ref, sem):
    zero_ref[...] = jnp.zeros_like(zero_ref)

    def tail_copy(e):
        rows = EXPERT_BLOCK * TOKEN_ROWS
        row0 = pl.multiple_of(tail_ref[e] * rows, rows)
        return pltpu.make_async_copy(zero_ref, xs_ref.at[pl.ds(row0, rows), :], sem)

    def zstart(e, c):
        @pl.when(tail_ref[e] >= 0)
        def _():
            tail_copy(e).start()
        return c

    def zwait(e, c):
        @pl.when(tail_ref[e] >= 0)
        def _():
            tail_copy(e).wait()
        return c

    lax.fori_loop(0, N_EXPERTS, zstart, 0)
    lax.fori_loop(0, N_EXPERTS, zwait, 0)


def _sc_scatter_rows(xs_init, src, idx):
    nr = src.shape[0]
    slots = idx.shape[0]
    mesh = plsc.VectorSubcoreMesh(core_axis_name="core", subcore_axis_name="subcore")

    @functools.partial(pl.kernel, mesh=mesh, scratch_types=[], out_type=())
    def scatter(src_hbm, idx_hbm, xs_hbm):
        def window(src_vmem, *idx_vmems):
            for idx_vmem in idx_vmems:
                pltpu.sync_copy(src_vmem, xs_hbm.at[idx_vmem.at[0]])

        pltpu.emit_pipeline(
            window,
            grid=(nr // SC_WINDOW,),
            in_specs=[pl.BlockSpec((SC_WINDOW, LANES), index_map=lambda i: (i, 0))]
                     + [pl.BlockSpec((1, SC_WINDOW), index_map=lambda i, j=j: (j, i))
                        for j in range(slots)],
            out_specs=[],
            core_axis_name=("core", "subcore"),
            dimension_semantics=(pltpu.PARALLEL,),
        )(src_hbm, *([idx_hbm] * slots))

    xs_ref = jax.new_ref(xs_init)
    scatter(src, idx, xs_ref)
    return xs_ref[...]


def _dispatch(tail_blk, dest, x1w, n_rows):
    n = dest.shape[1]
    xs = pl.pallas_call(
        _tails_kernel,
        grid_spec=pltpu.PrefetchScalarGridSpec(
            num_scalar_prefetch=1,
            grid=(1,),
            in_specs=[],
            out_specs=pl.BlockSpec(memory_space=pl.ANY),
            scratch_shapes=[pltpu.VMEM((EXPERT_BLOCK * TOKEN_ROWS, LANES), jnp.uint32),
                            pltpu.SemaphoreType.DMA],
        ),
        out_shape=jax.ShapeDtypeStruct((n_rows * TOKEN_ROWS, LANES), jnp.uint32),
        compiler_params=_params("arbitrary"),
        name="tails",
    )(tail_blk)
    word_rows = (dest[:, None, :] * TOKEN_ROWS
                 + jnp.arange(TOKEN_ROWS, dtype=jnp.int32)[None, :, None]).reshape(TOP_K, TOKEN_ROWS * n)
    return _sc_scatter_rows(xs, x1w.reshape(TOKEN_ROWS * n, LANES), word_rows)


def _swiglu(xb, w_in, w_down):
    h = _dot(xb, w_in)
    half = h.shape[1] // 2
    g = h[:, :half]
    act = g * _sigmoid(g) * h[:, half:]
    return _dot(act.astype(BF16), w_down)


def _experts_kernel(be_ref, nu_ref, fresh_ref, slot_ref, nxt_ref, x_ref, wi_hbm, wd_hbm, o_ref,
                    wi_buf, wd_buf, wib_ref, wdb_ref, sem):
    i = pl.program_id(0)
    used = i < nu_ref[0]
    blk = o_ref.shape[0] // TOKEN_ROWS

    def fetch(e, s):
        return (pltpu.make_async_copy(wi_hbm.at[e], wi_buf.at[s], sem.at[0, s]),
                pltpu.make_async_copy(wd_hbm.at[e], wd_buf.at[s], sem.at[1, s]))

    @pl.when(jnp.logical_and(used, fresh_ref[i] == 1))
    def _():
        s = slot_ref[i]

        @pl.when(i == 0)
        def _():
            for cp in fetch(be_ref[i], s):
                cp.start()

        for cp in fetch(be_ref[i], s):
            cp.wait()

        @pl.when(nxt_ref[i] >= 0)
        def _():
            for cp in fetch(nxt_ref[i], 1 - s):
                cp.start()

        wib_ref[...] = wi_buf[s].astype(BF16)
        wdb_ref[...] = wd_buf[s].astype(BF16)

    @pl.when(used)
    def _():
        xb = _unpack_tokens(x_ref, 0, blk).astype(BF16)
        _pack_tokens(o_ref, _swiglu(xb, wib_ref[...], wdb_ref[...]))

    @pl.when(jnp.logical_not(used))
    def _():
        o_ref[...] = jnp.zeros_like(o_ref)


def _experts(sched, xs, w_e_in, w_e_down):
    n_blocks = sched[0].shape[0]
    _, d, h2 = w_e_in.shape
    hdim = w_e_down.shape[1]
    rows = EXPERT_BLOCK * TOKEN_ROWS

    def x_map(i, be, nu, *_):
        return (jnp.minimum(i, nu[0] - 1), 0)

    return pl.pallas_call(
        _experts_kernel,
        grid_spec=pltpu.PrefetchScalarGridSpec(
            num_scalar_prefetch=len(sched),
            grid=(n_blocks,),
            in_specs=[
                pl.BlockSpec((rows, LANES), x_map),
                pl.BlockSpec(memory_space=pl.ANY),
                pl.BlockSpec(memory_space=pl.ANY),
            ],
            out_specs=pl.BlockSpec((rows, LANES), lambda i, *_: (i, 0)),
            scratch_shapes=[
                pltpu.VMEM((2, d, h2), F32), pltpu.VMEM((2, hdim, d), F32),
                pltpu.VMEM((d, h2), BF16), pltpu.VMEM((hdim, d), BF16),
                pltpu.SemaphoreType.DMA((2, 2)),
            ],
        ),
        out_shape=jax.ShapeDtypeStruct(xs.shape, jnp.uint32),
        compiler_params=_params("arbitrary"),
        name="experts",
    )(*sched, xs, w_e_in, w_e_down)


def _sc_gather_rows(table, idx):
    ni = idx.shape[0]
    mesh = plsc.VectorSubcoreMesh(core_axis_name="core", subcore_axis_name="subcore")

    @functools.partial(pl.kernel, mesh=mesh, scratch_types=[],
                       out_type=jax.ShapeDtypeStruct((ni, LANES), table.dtype))
    def gather(table_hbm, idx_hbm, out_hbm):
        def window(idx_vmem, out_vmem):
            pltpu.sync_copy(table_hbm.at[idx_vmem.at[0]], out_vmem)

        pltpu.emit_pipeline(
            window,
            grid=(ni // SC_WINDOW,),
            in_specs=[pl.BlockSpec((1, SC_WINDOW), index_map=lambda i: (0, i))],
            out_specs=[pl.BlockSpec((SC_WINDOW, LANES), index_map=lambda i: (i, 0))],
            core_axis_name=("core", "subcore"),
            dimension_semantics=(pltpu.PARALLEL,),
        )(idx_hbm, out_hbm)

    return gather(table, idx.reshape(1, ni))


def _shared_kernel(alpha, x1_ref, wsi_ref, wsd_ref, part_ref):
    x1 = x1_ref[...]
    part_ref[...] = alpha * x1 + _swiglu(x1.astype(BF16), wsi_ref[...], wsd_ref[...])


def _shared(x1, w_sh_in, w_sh_down, alpha):
    n, d = x1.shape
    tm = _tile(n, 512)
    const = lambda i: (0, 0)
    return pl.pallas_call(
        functools.partial(_shared_kernel, alpha),
        grid=(n // tm,),
        in_specs=[
            pl.BlockSpec((tm, d), lambda i: (i, 0)),
            pl.BlockSpec(w_sh_in.shape, const),
            pl.BlockSpec(w_sh_down.shape, const),
        ],
        out_specs=pl.BlockSpec((tm, d), lambda i: (i, 0)),
        out_shape=jax.ShapeDtypeStruct((n, d), F32),
        compiler_params=_params("parallel"),
        name="shared",
    )(x1, w_sh_in.astype(BF16), w_sh_down.astype(BF16))


def _finish_kernel(acc_ref, wt_ref, st_ref, lg_ref, lb_ref, out_ref):
    acc = acc_ref[...]
    for j in range(TOP_K):
        words = [st_ref[s, j] for s in range(TOKEN_ROWS)]
        acc = acc + wt_ref[:, j:j + 1] * _unpack_words(words)
    out_ref[...] = _layer_norm(acc, lg_ref[...], lb_ref[...])


def _combine(dest, part, wt, ln_g, ln_b, os):
    n, d = part.shape
    nc = n // COMBINE_CHUNKS
    tf = _tile(nc, 512)
    steps = nc // tf
    const = lambda i: (0, 0)
    out = part
    for c in range(COMBINE_CHUNKS):
        word_rows = (dest[None, :, c * nc:(c + 1) * nc] * TOKEN_ROWS
                     + jnp.arange(TOKEN_ROWS, dtype=jnp.int32)[:, None, None]).reshape(-1)
        staged = _sc_gather_rows(os, word_rows).reshape(TOKEN_ROWS, TOP_K, nc, LANES)
        tile = lambda i, c=c: (c * steps + i, 0)
        out = pl.pallas_call(
            _finish_kernel,
            grid=(steps,),
            in_specs=[
                pl.BlockSpec((tf, d), tile),
                pl.BlockSpec((tf, TOP_K), tile),
                pl.BlockSpec((TOKEN_ROWS, TOP_K, tf, LANES), lambda i: (0, 0, i, 0)),
                pl.BlockSpec((1, d), const),
                pl.BlockSpec((1, d), const),
            ],
            out_specs=pl.BlockSpec((tf, d), tile),
            out_shape=jax.ShapeDtypeStruct((n, d), F32),
            input_output_aliases={0: 0},
            compiler_params=_params("arbitrary"),
            name="finish",
        )(out, wt, staged, ln_g.astype(F32)[None, :], ln_b.astype(F32)[None, :])
    return out


def _block_layout(counts, n_assign):
    n_blocks = (n_assign + N_EXPERTS * (EXPERT_BLOCK - 1) + EXPERT_BLOCK - 1) // EXPERT_BLOCK
    nblk = (counts + EXPERT_BLOCK - 1) // EXPERT_BLOCK
    blk_end = jnp.cumsum(nblk)
    pad_start = (blk_end - nblk) * EXPERT_BLOCK
    blk_e = jnp.sum(blk_end[None, :] <= jnp.arange(n_blocks, dtype=jnp.int32)[:, None], axis=1)
    blk_e = jnp.minimum(blk_e, N_EXPERTS - 1).astype(jnp.int32)
    n_used = blk_end[-1:].astype(jnp.int32)
    tail_blk = jnp.where(nblk > 0, blk_end - 1, -1).astype(jnp.int32)
    has = nblk > 0
    e_ids = jnp.arange(N_EXPERTS, dtype=jnp.int32)
    nxt_ge = lax.cummin(jnp.where(has, e_ids, N_EXPERTS)[::-1])[::-1]
    nxt_e = jnp.concatenate([nxt_ge[1:], jnp.full((1,), N_EXPERTS, jnp.int32)])
    nxt_e = jnp.where(nxt_e < N_EXPERTS, nxt_e, -1)
    slot_e = (jnp.cumsum(has.astype(jnp.int32)) - 1) % 2
    blk_ids = jnp.arange(n_blocks, dtype=jnp.int32)
    fresh = (blk_ids == (blk_end - nblk)[blk_e]).astype(jnp.int32)
    sched = (blk_e, n_used, fresh, slot_e[blk_e].astype(jnp.int32), nxt_e[blk_e].astype(jnp.int32))
    return n_blocks, pad_start.astype(jnp.int32), sched, tail_blk


def _layer(x, w_in, b_gate, q_g, k_g, w_four_proj, w_attn_proj, w_o, ln1_g, ln1_b,
           w_router, e_bias, w_e_in, w_e_down, w_sh_in, w_sh_down, ln2_g, ln2_b, alpha):
    batch, seq, d = x.shape
    n = batch * seq
    x2 = x.reshape(n, d)

    u, q4, k4, vt4, gates, score_bound = _inproj(x2, w_in, b_gate, q_g, k_g, batch, seq)
    mf = _fourier(u.reshape(batch, seq, FOURIER_WIDTH), gates.reshape(batch, seq, -1), w_four_proj)
    o = _attention(q4, k4, vt4, score_bound)
    x1, x1w = _mix(o.reshape(n, ATTN_WIDTH), mf.reshape(n, d), gates, x2,
                   w_attn_proj, w_o, ln1_g, ln1_b, alpha)

    eidx, rank, wts, cnt = _route(x1, w_router, e_bias)
    counts = cnt[:, 0].astype(jnp.int32)
    n_blocks, pad_start, sched, tail_blk = _block_layout(counts, n * TOP_K)
    dest = _dest(eidx, rank, pad_start)

    xs = _dispatch(tail_blk, dest, x1w, n_blocks * EXPERT_BLOCK)
    part = _shared(x1, w_sh_in, w_sh_down, alpha)
    xs, part = lax.optimization_barrier((xs, part))
    os = _experts(sched, xs, w_e_in, w_e_down)
    out = _combine(dest, part, wts.T, ln2_g, ln2_b, os)
    return out.reshape(batch, seq, d)


def kernel(x, w_in, b_gate, q_norm_g, k_norm_g, w_four_proj, w_attn_proj, w_o, ln1_g, ln1_b, w_router, e_bias, w_e_in, w_e_down, w_sh_in, w_sh_down, ln2_g, ln2_b):
    depth = w_in.shape[0]
    alpha = (2 * depth) ** 0.25
    for l in range(depth):
        x = _layer(x, w_in[l], b_gate[l], q_norm_g[l], k_norm_g[l], w_four_proj[l],
                   w_attn_proj[l], w_o[l], ln1_g[l], ln1_b[l], w_router[l], e_bias[l],
                   w_e_in[l], w_e_down[l], w_sh_in[l], w_sh_down[l], ln2_g[l], ln2_b[l], alpha)
    return x
```

```python
import functools
import math

import numpy as np
import jax
import jax.numpy as jnp
from jax import lax
from jax.experimental import pallas as pl
from jax.experimental.pallas import tpu as pltpu
from jax.experimental.pallas import tpu_sc as plsc

F32 = jnp.float32
BF16 = jnp.bfloat16

GRID_W = 64
N_FOURIER_GROUPS = 8
FOURIER_GROUP_DIM = 64
FOURIER_WIDTH = N_FOURIER_GROUPS * FOURIER_GROUP_DIM
N_Q_HEADS = 16
N_KV_HEADS = 4
HEAD_DIM = 64
Q_GROUP = N_Q_HEADS // N_KV_HEADS
ATTN_WIDTH = N_Q_HEADS * HEAD_DIM
KV_WIDTH = N_KV_HEADS * HEAD_DIM
ROPE_THETA = 10000.0
QK_EPS = 1e-6
OFF_Q = FOURIER_WIDTH
OFF_K = OFF_Q + ATTN_WIDTH
OFF_V = OFF_K + KV_WIDTH
OFF_G = OFF_V + KV_WIDTH
N_EXPERTS = 256
TOP_K = 8
N_EXPERT_GROUPS = 8
GROUP_SIZE = N_EXPERTS // N_EXPERT_GROUPS
TOPK_GROUPS = 4
ROUTED_SCALE = 2.5
LN_EPS = 1e-5

LANES = 128
SUBLANES = 8
MXU_DIM = 256
VMEM_LIMIT = 56 * 1024 * 1024

MAX_EXP2_RANGE = 100.0

EXPERT_BLOCK = 512
SC_WINDOW = 128
COMBINE_CHUNKS = 8
TOKEN_ROWS = 4

NT_DIMS = (((1,), (1,)), ((), ()))


def _dot(a, b):
    return jnp.dot(a, b, preferred_element_type=F32)


def _dot_nt(a, b):
    return lax.dot_general(a, b, NT_DIMS, preferred_element_type=F32)


def _sigmoid(x):
    return 1.0 / (1.0 + jnp.exp(-x))


def _params(*sem):
    return pltpu.CompilerParams(dimension_semantics=sem, vmem_limit_bytes=VMEM_LIMIT)


def _tile(n, pref):
    t = min(n, pref)
    assert n % t == 0, (n, t)
    return t


def _rope_tables(seq):
    lane = np.arange(MXU_DIM)
    d = lane % HEAD_DIM
    sub = d % 32
    j = sub % 16
    t = np.arange(seq)[:, None]
    pos = np.where(d[None, :] < 32, t // GRID_W, t % GRID_W).astype(np.float64)
    freq = ROPE_THETA ** (-(j.astype(np.float64)) / 16.0)
    ang = pos * freq[None, :]
    cos = np.cos(ang)
    sin = np.sin(ang) * np.where(sub < 16, -1.0, 1.0)[None, :]
    return jnp.asarray(cos, F32), jnp.asarray(sin, F32)


def _head_mean_matrix():
    i = np.arange(MXU_DIM)
    m = (i[:, None] // HEAD_DIM == i[None, :] // HEAD_DIM).astype(np.float64) / HEAD_DIM
    return jnp.asarray(m, BF16)


def _dft_tables(seq):
    c = np.arange(FOURIER_GROUP_DIM)
    ang_c = 2.0 * np.pi * ((c[:, None] * c[None, :]) % FOURIER_GROUP_DIM) / FOURIER_GROUP_DIM
    sc = 1.0 / math.sqrt(FOURIER_GROUP_DIM)
    eye = np.eye(N_FOURIER_GROUPS)
    cc = np.kron(eye, np.cos(ang_c) * sc)
    ss = np.kron(eye, np.sin(ang_c) * sc)
    chan = np.concatenate([cc, ss], axis=1)
    s = np.arange(seq)
    ang_s = 2.0 * np.pi * ((s[:, None] * s[None, :]) % seq) / seq
    ssc = 1.0 / math.sqrt(seq)
    seqm = np.concatenate([np.cos(ang_s) * ssc, -np.sin(ang_s) * ssc], axis=1)
    return jnp.asarray(chan, BF16), jnp.asarray(seqm, BF16)


def _norm_rope(z, gain, mean_mat, cos, sin, lo_mask):
    ms = _dot((z * z).astype(BF16), mean_mat)
    y = z * lax.rsqrt(ms + QK_EPS) * gain
    outs = []
    for c in range(MXU_DIM // LANES):
        yc = y[:, c * LANES:(c + 1) * LANES]
        up = pltpu.roll(yc, LANES - 16, 1)
        dn = pltpu.roll(yc, 16, 1)
        partner = jnp.where(lo_mask, up, dn)
        sl = slice(c * LANES, (c + 1) * LANES)
        outs.append(yc * cos[:, sl] + partner * sin[:, sl])
    return jnp.concatenate(outs, axis=1)


def _inproj_kernel(x_ref, w_ref, bg_ref, gq_ref, gk_ref, mm_ref, cos_ref, sin_ref,
                   u_ref, q_ref, k_ref, v_ref, g_ref):
    xb = x_ref[...].astype(BF16)

    lane = lax.broadcasted_iota(jnp.int32, (1, LANES), 1)
    lo_mask = (lane & 16) == 0
    mean_mat = mm_ref[...]
    cos = cos_ref[...]
    sin = sin_ref[...]
    heads = MXU_DIM // HEAD_DIM

    def put_u(z):
        u_ref[...] = z.astype(BF16)

    def put_q(c, z):
        q = _norm_rope(z, gq_ref[...], mean_mat, cos, sin, lo_mask).astype(BF16)
        for j in range(heads):
            q_ref[0, c * heads + j] = q[:, j * HEAD_DIM:(j + 1) * HEAD_DIM]

    def put_k(z):
        k = _norm_rope(z, gk_ref[...], mean_mat, cos, sin, lo_mask).astype(BF16)
        for j in range(N_KV_HEADS):
            k_ref[0, j] = k[:, j * HEAD_DIM:(j + 1) * HEAD_DIM]

    def put_v(z):
        vt = z.T.astype(BF16)
        for j in range(N_KV_HEADS):
            v_ref[0, j] = vt[j * HEAD_DIM:(j + 1) * HEAD_DIM, :]

    def put_g(lo, hi, z):
        g_ref[:, lo:hi] = _sigmoid(z + bg_ref[:, lo:hi]).astype(BF16)

    stages = [((0, OFF_Q), put_u)]
    for c in range(ATTN_WIDTH // MXU_DIM):
        stages.append(((OFF_Q + c * MXU_DIM, OFF_Q + (c + 1) * MXU_DIM), functools.partial(put_q, c)))
    stages.append(((OFF_K, OFF_V), put_k))
    stages.append(((OFF_V, OFF_G), put_v))
    gw = 512
    for lo in range(0, w_ref.shape[1] - OFF_G, gw):
        stages.append(((OFF_G + lo, OFF_G + lo + gw), functools.partial(put_g, lo, lo + gw)))

    z_next = _dot(xb, w_ref[:, stages[0][0][0]:stages[0][0][1]])
    for s, (_, put) in enumerate(stages):
        z = z_next
        if s + 1 < len(stages):
            lo, hi = stages[s + 1][0]
            z_next = _dot(xb, w_ref[:, lo:hi])
        put(z)


def _inproj(x2, w_in, b_gate, q_g, k_g, batch, seq):
    n, d = x2.shape
    tm = _tile(seq, 512)
    spb = seq // tm
    in_width = w_in.shape[1]
    gate_w = in_width - OFF_G
    cos, sin = _rope_tables(seq)
    mean_mat = _head_mean_matrix()
    scale = HEAD_DIM ** -0.5 * math.log2(math.e)
    gq =jnp.tile(q_g.astype(F32) * scale, MXU_DIM // HEAD_DIM)[None, :]
    gk = jnp.tile(k_g.astype(F32), MXU_DIM // HEAD_DIM)[None, :]
    score_bound = (HEAD_DIM * jnp.max(jnp.abs(gq)) * jnp.max(jnp.abs(gk))).reshape(1)
    const = lambda i: (0, 0)
    outs = pl.pallas_call(
        _inproj_kernel,
        grid=(n // tm,),
        in_specs=[
            pl.BlockSpec((tm, d), lambda i: (i, 0)),
            pl.BlockSpec((d, in_width), const),
            pl.BlockSpec((1, gate_w), const),
            pl.BlockSpec((1, MXU_DIM), const),
            pl.BlockSpec((1, MXU_DIM), const),
            pl.BlockSpec((MXU_DIM, MXU_DIM), const),
            pl.BlockSpec((tm, MXU_DIM), lambda i: (i % spb, 0)),
            pl.BlockSpec((tm, MXU_DIM), lambda i: (i % spb, 0)),
        ],
        out_specs=[
            pl.BlockSpec((tm, FOURIER_WIDTH), lambda i: (i, 0)),
            pl.BlockSpec((1, N_Q_HEADS, tm, HEAD_DIM), lambda i: (i // spb, 0, i % spb, 0)),
            pl.BlockSpec((1, N_KV_HEADS, tm, HEAD_DIM), lambda i: (i // spb, 0, i % spb, 0)),
            pl.BlockSpec((1, N_KV_HEADS, HEAD_DIM, tm), lambda i: (i // spb, 0, 0, i % spb)),
            pl.BlockSpec((tm, gate_w), lambda i: (i, 0)),
        ],
        out_shape=[
            jax.ShapeDtypeStruct((n, FOURIER_WIDTH), BF16),
            jax.ShapeDtypeStruct((batch, N_Q_HEADS, seq, HEAD_DIM), BF16),
            jax.ShapeDtypeStruct((batch, N_KV_HEADS, seq, HEAD_DIM), BF16),
            jax.ShapeDtypeStruct((batch, N_KV_HEADS, HEAD_DIM, seq), BF16),
            jax.ShapeDtypeStruct((n, gate_w), BF16),
        ],
        compiler_params=_params("parallel"),
        name="inproj",
    )(x2, w_in.astype(BF16), b_gate.astype(F32)[None, :], gq, gk, mean_mat, cos, sin)
    return (*outs, score_bound)


def _fourier_kernel(u_ref, chan_ref, seqm_ref, wp_ref, g_ref, o_ref, ab_ref):
    seq = u_ref.shape[1]

    @pl.when(pl.program_id(1) == 0)
    def _():
        ab = _dot(u_ref[0], chan_ref[...])
        ab_ref[0:seq, :] = ab[:, 0:FOURIER_WIDTH].astype(BF16)
        ab_ref[seq:2 * seq, :] = ab[:, FOURIER_WIDTH:].astype(BF16)

    f = _dot(seqm_ref[...], ab_ref[...]).astype(BF16)
    y = _dot(f, wp_ref[...])
    o_ref[0] = (g_ref[0].astype(F32) * y).astype(BF16)


def _fourier(u3, g3, w_four_proj):
    batch, seq, _ = u3.shape
    d = w_four_proj.shape[1]
    tr = _tile(seq, 512)
    chan, seqm = _dft_tables(seq)
    return pl.pallas_call(
        _fourier_kernel,
        grid=(batch, seq // tr),
        in_specs=[
            pl.BlockSpec((1, seq, FOURIER_WIDTH), lambda b, r: (b, 0, 0)),
            pl.BlockSpec((FOURIER_WIDTH, 2 * FOURIER_WIDTH), lambda b, r: (0, 0)),
            pl.BlockSpec((tr, 2 * seq), lambda b, r: (r, 0)),
            pl.BlockSpec((FOURIER_WIDTH, d), lambda b, r: (0, 0)),
            pl.BlockSpec((1, tr, d), lambda b, r: (b, r, 0)),
        ],
        out_specs=pl.BlockSpec((1, tr, d), lambda b, r: (b, r, 0)),
        out_shape=jax.ShapeDtypeStruct((batch, seq, d), BF16),
        scratch_shapes=[pltpu.VMEM((2 * seq, FOURIER_WIDTH), BF16)],
        compiler_params=_params("parallel", "arbitrary"),
        name="fourier",
    )(u3, chan, seqm, w_four_proj.astype(BF16), g3)


def _attention_kernel(bounded, sb_ref, q_ref, k_ref, vt_ref, o_ref, vone_ref):
    seq = k_ref.shape[2]

    @pl.when(pl.program_id(2) == 0)
    def _():
        vone_ref[0:HEAD_DIM, :] = vt_ref[0, 0]
        vone_ref[HEAD_DIM:, :] = jnp.ones((HEAD_DIM, seq), BF16)

    k = k_ref[0, 0]
    vone = vone_ref[...]
    outs = []
    st_next = _dot_nt(k, q_ref[0, 0])
    for g in range(Q_GROUP):
        st = st_next
        if g + 1 < Q_GROUP:
            st_next = _dot_nt(k, q_ref[0, g + 1])
        if bounded:
            m = sb_ref[0]
        else:
            m = jnp.max(st, axis=0, keepdims=True)
        pt = jnp.exp2(st - m).astype(BF16)
        ol = _dot(vone, pt)
        ot = ol[0:HEAD_DIM, :] / ol[HEAD_DIM:HEAD_DIM + 1, :]
        outs.append(ot.T.astype(BF16))
    o_ref[0] = jnp.concatenate(outs, axis=1)


def _attention(q4, k4, vt4, score_bound):
    batch, _, seq, _ = q4.shape
    tq = _tile(seq, 1024)

    def call(bounded):
        return pl.pallas_call(
            functools.partial(_attention_kernel, bounded),
            grid_spec=pltpu.PrefetchScalarGridSpec(
                num_scalar_prefetch=1,
                grid=(batch, N_KV_HEADS, seq // tq),
                in_specs=[
                    pl.BlockSpec((1, Q_GROUP, tq, HEAD_DIM), lambda b, h, i, sb: (b, h, i, 0)),
                    pl.BlockSpec((1, 1, seq, HEAD_DIM), lambda b, h, i, sb: (b, h, 0, 0)),
                    pl.BlockSpec((1, 1, HEAD_DIM, seq), lambda b, h, i, sb: (b, h, 0, 0)),
                ],
                out_specs=pl.BlockSpec((1, tq, Q_GROUP * HEAD_DIM), lambda b, h, i, sb: (b, i, h)),
                scratch_shapes=[pltpu.VMEM((2 * HEAD_DIM, seq), BF16)],
            ),
            out_shape=jax.ShapeDtypeStruct((batch, seq, ATTN_WIDTH), BF16),
            compiler_params=_params("parallel", "parallel", "arbitrary"),
            name="attention_bounded" if bounded else "attention",
        )(score_bound, q4, k4, vt4)

    return lax.cond(2.0 * score_bound[0] <= MAX_EXP2_RANGE,
                    lambda: call(True), lambda: call(False))


def _layer_norm(h, g, b):
    mu = jnp.mean(h, axis=-1, keepdims=True)
    c = h - mu
    var = jnp.mean(c * c, axis=-1, keepdims=True)
    return c * lax.rsqrt(var + LN_EPS) * g + b


def _pack_tokens(dst_ref, val):
    t = val.shape[0]
    words = _pack_words(val)
    for s in range(TOKEN_ROWS):
        dst_ref[pl.ds(s, t, stride=TOKEN_ROWS), :] = words[s]


def _pack_words(val):
    half = val.shape[1] // 2
    assert half == TOKEN_ROWS * LANES
    bits = lax.bitcast_convert_type(val.astype(BF16).astype(F32), jnp.uint32)
    words = (bits[:, :half] >> 16) | bits[:, half:]
    return [words[:, s * LANES:(s + 1) * LANES] for s in range(TOKEN_ROWS)]


def _unpack_words(words):
    lo = [lax.bitcast_convert_type(w << 16, F32) for w in words]
    hi = [lax.bitcast_convert_type(w & jnp.uint32(0xFFFF0000), F32) for w in words]
    return jnp.concatenate(lo + hi, axis=1)


def _unpack_tokens(src_ref, tok0, t):
    return _unpack_words([src_ref[pl.ds(tok0 * TOKEN_ROWS + s, t, stride=TOKEN_ROWS), :]
                          for s in range(TOKEN_ROWS)])


def _mix_kernel(alpha, o_ref, mf_ref, g_ref, x_ref, wap_ref, wo_ref, lg_ref, lb_ref,
                x1_ref, x1w_ref):
    hm = o_ref.shape[0] // 2
    rows = [pl.ds(h * hm, hm) for h in range(2)]
    ys = [_dot(o_ref[r, :], wap_ref[...]) for r in rows]
    merged = [(mf_ref[r, :].astype(F32) + g_ref[r, :].astype(F32) * y).astype(BF16)
              for r, y in zip(rows, ys)]
    mixes = [_dot(m, wo_ref[...]) for m in merged]
    for r, mix in zip(rows, mixes):
        x1 = _layer_norm(alpha * x_ref[r, :] + mix, lg_ref[...], lb_ref[...])
        x1_ref[r, :] = x1
        for s, w in enumerate(_pack_words(x1)):
            x1w_ref[s, r, :] = w


def _mix(o2, mf2, g2, x2, w_attn_proj, w_o, ln_g, ln_b, alpha):
    n, d = x2.shape
    assert d == 2 * TOKEN_ROWS * LANES
    tm = _tile(n, 512)
    const = lambda i: (0, 0)
    return pl.pallas_call(
        functools.partial(_mix_kernel, alpha),
        grid=(n // tm,),
        in_specs=[
            pl.BlockSpec((tm, ATTN_WIDTH), lambda i: (i, 0)),
            pl.BlockSpec((tm, d), lambda i: (i, 0)),
            pl.BlockSpec((tm, d), lambda i: (i, 1)),
            pl.BlockSpec((tm, d), lambda i: (i, 0)),
            pl.BlockSpec((ATTN_WIDTH, d), const),
            pl.BlockSpec((d, d), const),
            pl.BlockSpec((1, d), const),
            pl.BlockSpec((1, d), const),
        ],
        out_specs=[
            pl.BlockSpec((tm, d), lambda i: (i, 0)),
            pl.BlockSpec((TOKEN_ROWS, tm, LANES), lambda i: (0, i, 0)),
        ],
        out_shape=[
            jax.ShapeDtypeStruct((n, d), F32),
            jax.ShapeDtypeStruct((TOKEN_ROWS, n, LANES), jnp.uint32),
        ],
        compiler_params=_params("parallel"),
        name="mix",
    )(o2, mf2, g2, x2, w_attn_proj.astype(BF16), w_o.astype(BF16),
      ln_g.astype(F32)[None, :], ln_b.astype(F32)[None, :])


def _route_kernel(x_ref, wh_ref, wl_ref, eb_ref, tri_ref,
                  eidx_ref, rank_ref, w_ref, cnt_ref, carry_ref):
    tm = x_ref.shape[0]

    @pl.when(pl.program_id(0) == 0)
    def _():
        carry_ref[...] = jnp.zeros_like(carry_ref)

    x = x_ref[...]
    xh = x.astype(BF16)
    xl = (x - xh.astype(F32)).astype(BF16)
    wh = wh_ref[...]
    logits = _dot_nt(wh, xh) + _dot_nt(wh, xl) + _dot_nt(wl_ref[...], xh)
    scores = _sigmoid(logits)
    biased = scores + eb_ref[:, 0:1]
    neg = -jnp.inf

    sub_iota = lax.broadcasted_iota(jnp.int32, (GROUP_SIZE, tm), 0).astype(F32)
    gs = []
    for g in range(N_EXPERT_GROUPS):
        blk = biased[g * GROUP_SIZE:(g + 1) * GROUP_SIZE, :]
        m1 = jnp.max(blk, axis=0, keepdims=True)
        a1 = jnp.min(jnp.where(blk == m1, sub_iota, float(GROUP_SIZE)), axis=0, keepdims=True)
        m2 = jnp.max(jnp.where(sub_iota == a1, neg, blk), axis=0, keepdims=True)
        gs.append(m1 + m2)

    masked = []
    for g in range(N_EXPERT_GROUPS):
        beat = jnp.zeros((1, tm), F32)
        for h in range(N_EXPERT_GROUPS):
            if h == g:
                continue
            wins = (gs[h] >= gs[g]) if h < g else (gs[h] > gs[g])
            beat = beat + jnp.where(wins, 1.0, 0.0)
        keep = beat < float(TOPK_GROUPS)
        blk = biased[g * GROUP_SIZE:(g + 1) * GROUP_SIZE, :]
        masked.append(jnp.where(keep, blk, neg))
    masked = jnp.concatenate(masked, axis=0)

    e_iota = lax.broadcasted_iota(jnp.int32, (N_EXPERTS, tm), 0).astype(F32)
    sel = jnp.zeros((N_EXPERTS, tm), F32)
    idxs, ws = [], []
    for _ in range(TOP_K):
        mx = jnp.max(masked, axis=0, keepdims=True)
        idx = jnp.min(jnp.where(masked == mx, e_iota, float(N_EXPERTS)), axis=0, keepdims=True)
        hit = e_iota == idx
        masked = jnp.where(hit, neg, masked)
        sel = jnp.where(hit, 1.0, sel)
        idxs.append(idx)
        ws.append(jnp.sum(jnp.where(hit, scores, 0.0), axis=0, keepdims=True))

    carry = carry_ref[...]
    selb = sel.astype(BF16)
    prefix = _dot(selb, tri_ref[...])
    rank_all = prefix + jnp.concatenate([carry] * (tm // LANES), axis=1)
    total = carry + _dot(selb, jnp.ones((tm, LANES), BF16))
    carry_ref[...] = total
    cnt_ref[...] = total

    wsum = ws[0]
    for j in range(1, TOP_K):
        wsum = wsum + ws[j]
    for j in range(TOP_K):
        eidx_ref[j:j + 1, :] = idxs[j].astype(jnp.int32)
        r = jnp.sum(jnp.where(e_iota == idxs[j], rank_all, 0.0), axis=0, keepdims=True)
        rank_ref[j:j + 1, :] = r.astype(jnp.int32)
        w_ref[j:j + 1, :] = ws[j] / wsum * ROUTED_SCALE


def _route(x1, w_router, e_bias):
    n, d = x1.shape
    tm = _tile(n, 512)
    wt = w_router.astype(F32).T
    wh = wt.astype(BF16)
    wl = (wt - wh.astype(F32)).astype(BF16)
    eb = jnp.broadcast_to(e_bias.astype(F32)[:, None], (N_EXPERTS, LANES))
    tri = jnp.asarray(np.triu(np.ones((tm, tm)), k=1), BF16)
    const = lambda i: (0, 0)
    return pl.pallas_call(
        _route_kernel,
        grid=(n // tm,),
        in_specs=[
            pl.BlockSpec((tm, d), lambda i: (i, 0)),
            pl.BlockSpec((N_EXPERTS, d), const),
            pl.BlockSpec((N_EXPERTS, d), const),
            pl.BlockSpec((N_EXPERTS, LANES), const),
            pl.BlockSpec((tm, tm), const),
        ],
        out_specs=[
            pl.BlockSpec((TOP_K, tm), lambda i: (0, i)),
            pl.BlockSpec((TOP_K, tm), lambda i: (0, i)),
            pl.BlockSpec((TOP_K, tm), lambda i: (0, i)),
            pl.BlockSpec((N_EXPERTS, LANES), const),
        ],
        out_shape=[
            jax.ShapeDtypeStruct((TOP_K, n), jnp.int32),
            jax.ShapeDtypeStruct((TOP_K, n), jnp.int32),
            jax.ShapeDtypeStruct((TOP_K, n), F32),
            jax.ShapeDtypeStruct((N_EXPERTS, LANES), F32),
        ],
        scratch_shapes=[pltpu.VMEM((N_EXPERTS, LANES), F32)],
        compiler_params=_params("arbitrary"),
        name="route",
    )(x1, wh, wl, eb, tri)


def _dest_kernel(eidx_ref, rank_ref, ps_ref, dest_ref):
    tm = eidx_ref.shape[1]
    e_iota = lax.broadcasted_iota(jnp.int32, (N_EXPERTS, tm), 0)
    ps = jnp.concatenate([ps_ref[...]] * (tm // LANES), axis=1)
    for j in range(TOP_K):
        hit = e_iota == eidx_ref[j:j + 1, :]
        start = jnp.sum(jnp.where(hit, ps, 0.0), axis=0, keepdims=True)
        dest_ref[j:j + 1, :] = start.astype(jnp.int32) + rank_ref[j:j + 1, :]


def _dest(eidx, rank, pad_start):
    n = eidx.shape[1]
    tm = _tile(n, 512)
    ps = jnp.broadcast_to(pad_start.astype(F32)[:, None], (N_EXPERTS, LANES))
    return pl.pallas_call(
        _dest_kernel,
        grid=(n // tm,),
        in_specs=[
            pl.BlockSpec((TOP_K, tm), lambda i: (0, i)),
            pl.BlockSpec((TOP_K, tm), lambda i: (0, i)),
            pl.BlockSpec((N_EXPERTS, LANES), lambda i: (0, 0)),
        ],
        out_specs=pl.BlockSpec((TOP_K, tm), lambda i: (0, i)),
        out_shape=jax.ShapeDtypeStruct((TOP_K, n), jnp.int32),
        compiler_params=_params("parallel"),
        name="dest",
    )(eidx, rank, ps)


def _tails_kernel(tail_ref, xs_ref, zero_ref, sem):
    zero_ref[...] = jnp.zeros_like(zero_ref)

    def tail_copy(e):
        rows = EXPERT_BLOCK * TOKEN_ROWS
        row0 = pl.multiple_of(tail_ref[e] * rows, rows)
        return pltpu.make_async_copy(zero_ref, xs_ref.at[pl.ds(row0, rows), :], sem)

    def zstart(e, c):
        @pl.when(tail_ref[e] >= 0)
        def _():
            tail_copy(e).start()
        return c

    def zwait(e, c):
        @pl.when(tail_ref[e] >= 0)
        def _():
            tail_copy(e).wait()
        return c

    lax.fori_loop(0, N_EXPERTS, zstart, 0)
    lax.fori_loop(0, N_EXPERTS, zwait, 0)


def _sc_scatter_rows(xs_init, src, idx):
    nr = src.shape[0]
    slots = idx.shape[0]
    mesh = plsc.VectorSubcoreMesh(core_axis_name="core", subcore_axis_name="subcore")

    @functools.partial(pl.kernel, mesh=mesh, scratch_types=[], out_type=())
    def scatter(src_hbm, idx_hbm, xs_hbm):
        def window(src_vmem, *idx_vmems):
            for idx_vmem in idx_vmems:
                pltpu.sync_copy(src_vmem, xs_hbm.at[idx_vmem.at[0]])

        pltpu.emit_pipeline(
            window,
            grid=(nr // SC_WINDOW,),
            in_specs=[pl.BlockSpec((SC_WINDOW, LANES), index_map=lambda i: (i, 0))]
                     + [pl.BlockSpec((1, SC_WINDOW), index_map=lambda i, j=j: (j, i))
                        for j in range(slots)],
            out_specs=[],
            core_axis_name=("core", "subcore"),
            dimension_semantics=(pltpu.PARALLEL,),
        )(src_hbm, *([idx_hbm] * slots))

    xs_ref = jax.new_ref(xs_init)
    scatter(src, idx, xs_ref)
    return xs_ref[...]


def _dispatch(tail_blk, dest, x1w, n_rows):
    n = dest.shape[1]
    xs = pl.pallas_call(
        _tails_kernel,
        grid_spec=pltpu.PrefetchScalarGridSpec(
            num_scalar_prefetch=1,
            grid=(1,),
            in_specs=[],
            out_specs=pl.BlockSpec(memory_space=pl.ANY),
            scratch_shapes=[pltpu.VMEM((EXPERT_BLOCK * TOKEN_ROWS, LANES), jnp.uint32),
                            pltpu.SemaphoreType.DMA],
        ),
        out_shape=jax.ShapeDtypeStruct((n_rows * TOKEN_ROWS, LANES), jnp.uint32),
        compiler_params=_params("arbitrary"),
        name="tails",
    )(tail_blk)
    word_rows = (dest[:, None, :] * TOKEN_ROWS
                 + jnp.arange(TOKEN_ROWS, dtype=jnp.int32)[None, :, None]).reshape(TOP_K, TOKEN_ROWS * n)
    return _sc_scatter_rows(xs, x1w.reshape(TOKEN_ROWS * n, LANES), word_rows)


def _swiglu(xb, w_in, w_down):
    h = _dot(xb, w_in)
    half = h.shape[1] // 2
    g = h[:, :half]
    act = g * _sigmoid(g) * h[:, half:]
    return _dot(act.astype(BF16), w_down)


def _experts_kernel(be_ref, nu_ref, fresh_ref, slot_ref, nxt_ref, x_ref, wi_hbm, wd_hbm, o_ref,
                    wi_buf, wd_buf, wib_ref, wdb_ref, sem):
    i = pl.program_id(0)
    used = i < nu_ref[0]
    blk = o_ref.shape[0] // TOKEN_ROWS

    def fetch(e, s):
        return (pltpu.make_async_copy(wi_hbm.at[e], wi_buf.at[s], sem.at[0, s]),
                pltpu.make_async_copy(wd_hbm.at[e], wd_buf.at[s], sem.at[1, s]))

    @pl.when(jnp.logical_and(used, fresh_ref[i] == 1))
    def _():
        s = slot_ref[i]

        @pl.when(i == 0)
        def _():
            for cp in fetch(be_ref[i], s):
                cp.start()

        for cp in fetch(be_ref[i], s):
            cp.wait()

        @pl.when(nxt_ref[i] >= 0)
        def _():
            for cp in fetch(nxt_ref[i], 1 - s):
                cp.start(priority=1)

        wib_ref[...] = wi_buf[s].astype(BF16)
        wdb_ref[...] = wd_buf[s].astype(BF16)

    @pl.when(used)
    def _():
        hb = blk // 2
        wi, wd = wib_ref[...], wdb_ref[...]
        xbs = [_unpack_tokens(x_ref, h * hb, hb).astype(BF16) for h in range(2)]
        hs = [_dot(xb, wi) for xb in xbs]
        half = wi.shape[1] // 2
        acts = [(h[:, :half] * _sigmoid(h[:, :half]) * h[:, half:]).astype(BF16) for h in hs]
        for h, act in enumerate(acts):
            rows = hb * TOKEN_ROWS
            _pack_tokens(o_ref.at[pl.ds(h * rows, rows), :], _dot(act, wd))

    @pl.when(jnp.logical_not(used))
    def _():
        o_ref[...] = jnp.zeros_like(o_ref)


def _experts(sched, xs, w_e_in, w_e_down):
    n_blocks = sched[0].shape[0]
    _, d, h2 = w_e_in.shape
    hdim = w_e_down.shape[1]
    rows = EXPERT_BLOCK * TOKEN_ROWS

    def x_map(i, be, nu, *_):
        return (jnp.minimum(i, nu[0] - 1), 0)

    return pl.pallas_call(
        _experts_kernel,
        grid_spec=pltpu.PrefetchScalarGridSpec(
            num_scalar_prefetch=len(sched),
            grid=(n_blocks,),
            in_specs=[
                pl.BlockSpec((rows, LANES), x_map),
                pl.BlockSpec(memory_space=pl.ANY),
                pl.BlockSpec(memory_space=pl.ANY),
            ],
            out_specs=pl.BlockSpec((rows, LANES), lambda i, *_: (i, 0)),
            scratch_shapes=[
                pltpu.VMEM((2, d, h2), F32), pltpu.VMEM((2, hdim, d), F32),
                pltpu.VMEM((d, h2), BF16), pltpu.VMEM((hdim, d), BF16),
                pltpu.SemaphoreType.DMA((2, 2)),
            ],
        ),
        out_shape=jax.ShapeDtypeStruct(xs.shape, jnp.uint32),
        compiler_params=_params("arbitrary"),
        name="experts",
    )(*sched, xs, w_e_in, w_e_down)


def _sc_gather_rows(table, idx):
    ni = idx.shape[0]
    mesh = plsc.VectorSubcoreMesh(core_axis_name="core", subcore_axis_name="subcore")

    @functools.partial(pl.kernel, mesh=mesh, scratch_types=[],
                       out_type=jax.ShapeDtypeStruct((ni, LANES), table.dtype))
    def gather(table_hbm, idx_hbm, out_hbm):
        def window(idx_vmem, out_vmem):
            pltpu.sync_copy(table_hbm.at[idx_vmem.at[0]], out_vmem)

        pltpu.emit_pipeline(
            window,
            grid=(ni // SC_WINDOW,),
            in_specs=[pl.BlockSpec((1, SC_WINDOW), index_map=lambda i: (0, i))],
            out_specs=[pl.BlockSpec((SC_WINDOW, LANES), index_map=lambda i: (i, 0))],
            core_axis_name=("core", "subcore"),
            dimension_semantics=(pltpu.PARALLEL,),
        )(idx_hbm, out_hbm)

    return gather(table, idx.reshape(1, ni))


def _shared_kernel(alpha, x1_ref, wsi_ref, wsd_ref, part_ref):
    x1 = x1_ref[...]
    part_ref[...] = alpha * x1 + _swiglu(x1.astype(BF16), wsi_ref[...], wsd_ref[...])


def _shared(x1, w_sh_in, w_sh_down, alpha):
    n, d = x1.shape
    tm = _tile(n, 512)
    const = lambda i: (0, 0)
    return pl.pallas_call(
        functools.partial(_shared_kernel, alpha),
        grid=(n // tm,),
        in_specs=[
            pl.BlockSpec((tm, d), lambda i: (i, 0)),
            pl.BlockSpec(w_sh_in.shape, const),
            pl.BlockSpec(w_sh_down.shape, const),
        ],
        out_specs=pl.BlockSpec((tm, d), lambda i: (i, 0)),
        out_shape=jax.ShapeDtypeStruct((n, d), F32),
        compiler_params=_params("parallel"),
        name="shared",
    )(x1, w_sh_in.astype(BF16), w_sh_down.astype(BF16))


def _finish_kernel(acc_ref, wt_ref, st_ref, lg_ref, lb_ref, out_ref):
    acc = acc_ref[...]
    for j in range(TOP_K):
        words = [st_ref[s, j] for s in range(TOKEN_ROWS)]
        acc = acc + wt_ref[:, j:j + 1] * _unpack_words(words)
    out_ref[...] = _layer_norm(acc, lg_ref[...], lb_ref[...])


def _combine(dest, part, wt, ln_g, ln_b, os):
    n, d = part.shape
    nc = n // COMBINE_CHUNKS
    tf = _tile(nc, 512)
    steps = nc // tf
    const = lambda i: (0, 0)
    out = part
    for c in range(COMBINE_CHUNKS):
        word_rows = (dest[None, :, c * nc:(c + 1) * nc] * TOKEN_ROWS
                     + jnp.arange(TOKEN_ROWS, dtype=jnp.int32)[:, None, None]).reshape(-1)
        staged = _sc_gather_rows(os, word_rows).reshape(TOKEN_ROWS, TOP_K, nc, LANES)
        tile = lambda i, c=c: (c * steps + i, 0)
        out = pl.pallas_call(
            _finish_kernel,
            grid=(steps,),
            in_specs=[
                pl.BlockSpec((tf, d), tile),
                pl.BlockSpec((tf, TOP_K), tile),
                pl.BlockSpec((TOKEN_ROWS, TOP_K, tf, LANES), lambda i: (0, 0, i, 0)),
                pl.BlockSpec((1, d), const),
                pl.BlockSpec((1, d), const),
            ],
            out_specs=pl.BlockSpec((tf, d), tile),
            out_shape=jax.ShapeDtypeStruct((n, d), F32),
            input_output_aliases={0: 0},
            compiler_params=_params("arbitrary"),
            name="finish",
        )(out, wt, staged, ln_g.astype(F32)[None, :], ln_b.astype(F32)[None, :])
    return out


def _block_layout(counts, n_assign):
    n_blocks = (n_assign + N_EXPERTS * (EXPERT_BLOCK - 1) + EXPERT_BLOCK - 1) // EXPERT_BLOCK
    nblk = (counts + EXPERT_BLOCK - 1) // EXPERT_BLOCK
    blk_end = jnp.cumsum(nblk)
    pad_start = (blk_end - nblk) * EXPERT_BLOCK
    blk_e = jnp.sum(blk_end[None, :] <= jnp.arange(n_blocks, dtype=jnp.int32)[:, None], axis=1)
    blk_e = jnp.minimum(blk_e, N_EXPERTS - 1).astype(jnp.int32)
    n_used = blk_end[-1:].astype(jnp.int32)
    tail_blk = jnp.where(nblk > 0, blk_end - 1, -1).astype(jnp.int32)
    has = nblk > 0
    e_ids = jnp.arange(N_EXPERTS, dtype=jnp.int32)
    nxt_ge = lax.cummin(jnp.where(has, e_ids, N_EXPERTS)[::-1])[::-1]
    nxt_e = jnp.concatenate([nxt_ge[1:], jnp.full((1,), N_EXPERTS, jnp.int32)])
    nxt_e = jnp.where(nxt_e < N_EXPERTS, nxt_e, -1)
    slot_e = (jnp.cumsum(has.astype(jnp.int32)) - 1) % 2
    blk_ids = jnp.arange(n_blocks, dtype=jnp.int32)
    fresh = (blk_ids == (blk_end - nblk)[blk_e]).astype(jnp.int32)
    sched = (blk_e, n_used, fresh, slot_e[blk_e].astype(jnp.int32), nxt_e[blk_e].astype(jnp.int32))
    return n_blocks, pad_start.astype(jnp.int32), sched, tail_blk


def _layer(x, w_in, b_gate, q_g, k_g, w_four_proj, w_attn_proj, w_o, ln1_g, ln1_b,
           w_router, e_bias, w_e_in, w_e_down, w_sh_in, w_sh_down, ln2_g, ln2_b, alpha):
    batch, seq, d = x.shape
    n = batch * seq
    x2 = x.reshape(n, d)

    u, q4, k4, vt4, gates, score_bound = _inproj(x2, w_in, b_gate, q_g, k_g, batch, seq)
    mf = _fourier(u.reshape(batch, seq, FOURIER_WIDTH), gates.reshape(batch, seq, -1), w_four_proj)
    o = _attention(q4, k4, vt4, score_bound)
    x1, x1w = _mix(o.reshape(n, ATTN_WIDTH), mf.reshape(n, d), gates, x2,
                   w_attn_proj, w_o, ln1_g, ln1_b, alpha)

    eidx, rank, wts, cnt = _route(x1, w_router, e_bias)
    counts = cnt[:, 0].astype(jnp.int32)
    n_blocks, pad_start, sched, tail_blk = _block_layout(counts, n * TOP_K)
    dest = _dest(eidx, rank, pad_start)

    xs = _dispatch(tail_blk, dest, x1w, n_blocks * EXPERT_BLOCK)
    part = _shared(x1, w_sh_in, w_sh_down, alpha)
    xs, part = lax.optimization_barrier((xs, part))
    os = _experts(sched, xs, w_e_in, w_e_down)
    out = _combine(dest, part, wts.T, ln2_g, ln2_b, os)
    return out.reshape(batch, seq, d)


def kernel(x, w_in, b_gate, q_norm_g, k_norm_g, w_four_proj, w_attn_proj, w_o, ln1_g, ln1_b, w_router, e_bias, w_e_in, w_e_down, w_sh_in, w_sh_down, ln2_g, ln2_b):
    depth = w_in.shape[0]
    alpha = (2 * depth) ** 0.25
    for l in range(depth):
        x = _layer(x, w_in[l], b_gate[l], q_norm_g[l], k_norm_g[l], w_four_proj[l],
                   w_attn_proj[l], w_o[l], ln1_g[l], ln1_b[l], w_router[l], e_bias[l],
                   w_e_in[l], w_e_down[l], w_sh_in[l], w_sh_down[l], ln2_g[l], ln2_b[l], alpha)
    return x
```

```python
import functools
import math

import numpy as np
import jax
import jax.numpy as jnp
from jax import lax
from jax.experimental import pallas as pl
from jax.experimental.pallas import tpu as pltpu
from jax.experimental.pallas import tpu_sc as plsc

F32 = jnp.float32
BF16 = jnp.bfloat16

GRID_W = 64
N_FOURIER_GROUPS = 8
FOURIER_GROUP_DIM = 64
FOURIER_WIDTH = N_FOURIER_GROUPS * FOURIER_GROUP_DIM
N_Q_HEADS = 16
N_KV_HEADS = 4
HEAD_DIM = 64
Q_GROUP = N_Q_HEADS // N_KV_HEADS
ATTN_WIDTH = N_Q_HEADS * HEAD_DIM
KV_WIDTH = N_KV_HEADS * HEAD_DIM
ROPE_THETA = 10000.0
QK_EPS = 1e-6
OFF_Q = FOURIER_WIDTH
OFF_K = OFF_Q + ATTN_WIDTH
OFF_V = OFF_K + KV_WIDTH
OFF_G = OFF_V + KV_WIDTH
N_EXPERTS = 256
TOP_K = 8
N_EXPERT_GROUPS = 8
GROUP_SIZE = N_EXPERTS // N_EXPERT_GROUPS
TOPK_GROUPS = 4
ROUTED_SCALE = 2.5
LN_EPS = 1e-5

LANES = 128
SUBLANES = 8
MXU_DIM = 256
VMEM_LIMIT = 56 * 1024 * 1024

MAX_EXP2_RANGE = 100.0

EXPERT_BLOCK = 512
SC_WINDOW = 128
COMBINE_CHUNKS = 8
TOKEN_ROWS = 4

NT_DIMS = (((1,), (1,)), ((), ()))


def _dot(a, b):
    return jnp.dot(a, b, preferred_element_type=F32)


def _dot_nt(a, b):
    return lax.dot_general(a, b, NT_DIMS, preferred_element_type=F32)


def _sigmoid(x):
    return 1.0 / (1.0 + jnp.exp(-x))


def _params(*sem):
    return pltpu.CompilerParams(dimension_semantics=sem, vmem_limit_bytes=VMEM_LIMIT)


def _tile(n, pref):
    t = min(n, pref)
    assert n % t == 0, (n, t)
    return t


def _rope_tables(seq):
    lane = np.arange(MXU_DIM)
    d = lane % HEAD_DIM
    sub = d % 32
    j = sub % 16
    t = np.arange(seq)[:, None]
    pos = np.where(d[None, :] < 32, t // GRID_W, t % GRID_W).astype(np.float64)
    freq = ROPE_THETA ** (-(j.astype(np.float64)) / 16.0)
    ang = pos * freq[None, :]
    cos = np.cos(ang)
    sin = np.sin(ang) * np.where(sub < 16, -1.0, 1.0)[None, :]
    return jnp.asarray(cos, F32), jnp.asarray(sin, F32)


def _head_mean_matrix():
    i = np.arange(MXU_DIM)
    m = (i[:, None] // HEAD_DIM == i[None, :] // HEAD_DIM).astype(np.float64) / HEAD_DIM
    return jnp.asarray(m, BF16)


def _dft_tables(seq):
    c = np.arange(FOURIER_GROUP_DIM)
    ang_c = 2.0 * np.pi * ((c[:, None] * c[None, :]) % FOURIER_GROUP_DIM) / FOURIER_GROUP_DIM
    sc = 1.0 / math.sqrt(FOURIER_GROUP_DIM)
    eye = np.eye(N_FOURIER_GROUPS)
    cc = np.kron(eye, np.cos(ang_c) * sc)
    ss = np.kron(eye, np.sin(ang_c) * sc)
    chan = np.concatenate([cc, ss], axis=1)
    s = np.arange(seq)
    ang_s = 2.0 * np.pi * ((s[:, None] * s[None, :]) % seq) / seq
    ssc = 1.0 / math.sqrt(seq)
    seqm = np.concatenate([np.cos(ang_s) * ssc, -np.sin(ang_s) * ssc], axis=1)
    return jnp.asarray(chan, BF16), jnp.asarray(seqm, BF16)


def _norm_rope(z, gain, mean_mat, cos, sin, lo_mask):
    ms = _dot((z * z).astype(BF16), mean_mat)
    y = z * lax.rsqrt(ms + QK_EPS) * gain
    outs = []
    for c in range(MXU_DIM // LANES):
        yc = y[:, c * LANES:(c + 1) * LANES]
        up = pltpu.roll(yc, LANES - 16, 1)
        dn = pltpu.roll(yc, 16, 1)
        partner = jnp.where(lo_mask, up, dn)
        sl = slice(c * LANES, (c + 1) * LANES)
        outs.append(yc * cos[:, sl] + partner * sin[:, sl])
    return jnp.concatenate(outs, axis=1)


def _inproj_kernel(x_ref, w_ref, bg_ref, gq_ref, gk_ref, mm_ref, cos_ref, sin_ref,
                   u_ref, q_ref, k_ref, v_ref, g_ref):
    xb = x_ref[...].astype(BF16)

    lane = lax.broadcasted_iota(jnp.int32, (1, LANES), 1)
    lo_mask = (lane & 16) == 0
    mean_mat = mm_ref[...]
    cos = cos_ref[...]
    sin = sin_ref[...]
    heads = MXU_DIM // HEAD_DIM

    def put_u(z):
        u_ref[...] = z.astype(BF16)

    def put_q(c, z):
        q = _norm_rope(z, gq_ref[...], mean_mat, cos, sin, lo_mask).astype(BF16)
        for j in range(heads):
            q_ref[0, c * heads + j] = q[:, j * HEAD_DIM:(j + 1) * HEAD_DIM]

    def put_k(z):
        k = _norm_rope(z, gk_ref[...], mean_mat, cos, sin, lo_mask).astype(BF16)
        for j in range(N_KV_HEADS):
            k_ref[0, j] = k[:, j * HEAD_DIM:(j + 1) * HEAD_DIM]

    def put_v(z):
        vt = z.T.astype(BF16)
        for j in range(N_KV_HEADS):
            v_ref[0, j] = vt[j * HEAD_DIM:(j + 1) * HEAD_DIM, :]

    def put_g(lo, hi, z):
        g_ref[:, lo:hi] = _sigmoid(z + bg_ref[:, lo:hi]).astype(BF16)

    stages = [((0, OFF_Q), put_u)]
    for c in range(ATTN_WIDTH // MXU_DIM):
        stages.append(((OFF_Q + c * MXU_DIM, OFF_Q + (c + 1) * MXU_DIM), functools.partial(put_q, c)))
    stages.append(((OFF_K, OFF_V), put_k))
    stages.append(((OFF_V, OFF_G), put_v))
    gw = 512
    for lo in range(0, w_ref.shape[1] - OFF_G, gw):
        stages.append(((OFF_G + lo, OFF_G + lo + gw), functools.partial(put_g, lo, lo + gw)))

    z_next = _dot(xb, w_ref[:, stages[0][0][0]:stages[0][0][1]])
    for s, (_, put) in enumerate(stages):
        z = z_next
        if s + 1 < len(stages):
            lo, hi = stages[s + 1][0]
            z_next = _dot(xb, w_ref[:, lo:hi])
        put(z)


def _inproj(x2, w_in, b_gate, q_g, k_g, batch, seq):
    n, d = x2.shape
    tm = _tile(seq, 512)
    spb = seq // tm
    in_width = w_in.shape[1]
    gate_w = in_width - OFF_G
    cos, sin = _rope_tables(seq)
    mean_mat = _head_mean_matrix()
    scale = HEAD_DIM ** -0.5 * math.log2(math.e)
    gq =jnp.tile(q_g.astype(F32) * scale, MXU_DIM // HEAD_DIM)[None, :]
    gk = jnp.tile(k_g.astype(F32), MXU_DIM // HEAD_DIM)[None, :]
    score_bound = (HEAD_DIM * jnp.max(jnp.abs(gq)) * jnp.max(jnp.abs(gk))).reshape(1)
    const = lambda i: (0, 0)
    outs = pl.pallas_call(
        _inproj_kernel,
        grid=(n // tm,),
        in_specs=[
            pl.BlockSpec((tm, d), lambda i: (i, 0)),
            pl.BlockSpec((d, in_width), const),
            pl.BlockSpec((1, gate_w), const),
            pl.BlockSpec((1, MXU_DIM), const),
            pl.BlockSpec((1, MXU_DIM), const),
            pl.BlockSpec((MXU_DIM, MXU_DIM), const),
            pl.BlockSpec((tm, MXU_DIM), lambda i: (i % spb, 0)),
            pl.BlockSpec((tm, MXU_DIM), lambda i: (i % spb, 0)),
        ],
        out_specs=[
            pl.BlockSpec((tm, FOURIER_WIDTH), lambda i: (i, 0)),
            pl.BlockSpec((1, N_Q_HEADS, tm, HEAD_DIM), lambda i: (i // spb, 0, i % spb, 0)),
            pl.BlockSpec((1, N_KV_HEADS, tm, HEAD_DIM), lambda i: (i // spb, 0, i % spb, 0)),
            pl.BlockSpec((1, N_KV_HEADS, HEAD_DIM, tm), lambda i: (i // spb, 0, 0, i % spb)),
            pl.BlockSpec((tm, gate_w), lambda i: (i, 0)),
        ],
        out_shape=[
            jax.ShapeDtypeStruct((n, FOURIER_WIDTH), BF16),
            jax.ShapeDtypeStruct((batch, N_Q_HEADS, seq, HEAD_DIM), BF16),
            jax.ShapeDtypeStruct((batch, N_KV_HEADS, seq, HEAD_DIM), BF16),
            jax.ShapeDtypeStruct((batch, N_KV_HEADS, HEAD_DIM, seq), BF16),
            jax.ShapeDtypeStruct((n, gate_w), BF16),
        ],
        compiler_params=_params("parallel"),
        name="inproj",
    )(x2, w_in.astype(BF16), b_gate.astype(F32)[None, :], gq, gk, mean_mat, cos, sin)
    return (*outs, score_bound)


def _fourier_kernel(u_ref, chan_ref, seqm_ref, wp_ref, g_ref, o_ref, ab_ref):
    seq = u_ref.shape[1]

    @pl.when(pl.program_id(1) == 0)
    def _():
        ab = _dot(u_ref[0], chan_ref[...])
        ab_ref[0:seq, :] = ab[:, 0:FOURIER_WIDTH].astype(BF16)
        ab_ref[seq:2 * seq, :] = ab[:, FOURIER_WIDTH:].astype(BF16)

    f = _dot(seqm_ref[...], ab_ref[...]).astype(BF16)
    y = _dot(f, wp_ref[...])
    o_ref[0] = (g_ref[0].astype(F32) * y).astype(BF16)


def _fourier(u3, g3, w_four_proj):
    batch, seq, _ = u3.shape
    d = w_four_proj.shape[1]
    tr = _tile(seq, 512)
    chan, seqm = _dft_tables(seq)
    return pl.pallas_call(
        _fourier_kernel,
        grid=(batch, seq // tr),
        in_specs=[
            pl.BlockSpec((1, seq, FOURIER_WIDTH), lambda b, r: (b, 0, 0)),
            pl.BlockSpec((FOURIER_WIDTH, 2 * FOURIER_WIDTH), lambda b, r: (0, 0)),
            pl.BlockSpec((tr, 2 * seq), lambda b, r: (r, 0)),
            pl.BlockSpec((FOURIER_WIDTH, d), lambda b, r: (0, 0)),
            pl.BlockSpec((1, tr, d), lambda b, r: (b, r, 0)),
        ],
        out_specs=pl.BlockSpec((1, tr, d), lambda b, r: (b, r, 0)),
        out_shape=jax.ShapeDtypeStruct((batch, seq, d), BF16),
        scratch_shapes=[pltpu.VMEM((2 * seq, FOURIER_WIDTH), BF16)],
        compiler_params=_params("parallel", "arbitrary"),
        name="fourier",
    )(u3, chan, seqm, w_four_proj.astype(BF16), g3)


def _attention_kernel(bounded, sb_ref, q_ref, k_ref, vt_ref, o_ref, vone_ref):
    seq = k_ref.shape[2]

    @pl.when(pl.program_id(2) == 0)
    def _():
        vone_ref[0:HEAD_DIM, :] = vt_ref[0, 0]
        vone_ref[HEAD_DIM:, :] = jnp.ones((HEAD_DIM, seq), BF16)

    k = k_ref[0, 0]
    vone = vone_ref[...]
    outs = []
    st_next = _dot_nt(k, q_ref[0, 0])
    for g in range(Q_GROUP):
        st = st_next
        if g + 1 < Q_GROUP:
            st_next = _dot_nt(k, q_ref[0, g + 1])
        if bounded:
            m = sb_ref[0]
        else:
            m = jnp.max(st, axis=0, keepdims=True)
        pt = jnp.exp2(st - m).astype(BF16)
        ol = _dot(vone, pt)
        ot = ol[0:HEAD_DIM, :] / ol[HEAD_DIM:HEAD_DIM + 1, :]
        outs.append(ot.T.astype(BF16))
    o_ref[0] = jnp.concatenate(outs, axis=1)


def _attention(q4, k4, vt4, score_bound):
    batch, _, seq, _ = q4.shape
    tq = _tile(seq, 1024)

    def call(bounded):
        return pl.pallas_call(
            functools.partial(_attention_kernel, bounded),
            grid_spec=pltpu.PrefetchScalarGridSpec(
                num_scalar_prefetch=1,
                grid=(batch, N_KV_HEADS, seq // tq),
                in_specs=[
                    pl.BlockSpec((1, Q_GROUP, tq, HEAD_DIM), lambda b, h, i, sb: (b, h, i, 0)),
                    pl.BlockSpec((1, 1, seq, HEAD_DIM), lambda b, h, i, sb: (b, h, 0, 0)),
                    pl.BlockSpec((1, 1, HEAD_DIM, seq), lambda b, h, i, sb: (b, h, 0, 0)),
                ],
                out_specs=pl.BlockSpec((1, tq, Q_GROUP * HEAD_DIM), lambda b, h, i, sb: (b, i, h)),
                scratch_shapes=[pltpu.VMEM((2 * HEAD_DIM, seq), BF16)],
            ),
            out_shape=jax.ShapeDtypeStruct((batch, seq, ATTN_WIDTH), BF16),
            compiler_params=_params("parallel", "parallel", "arbitrary"),
            name="attention_bounded" if bounded else "attention",
        )(score_bound, q4, k4, vt4)

    return lax.cond(2.0 * score_bound[0] <= MAX_EXP2_RANGE,
                    lambda: call(True), lambda: call(False))


def _layer_norm(h, g, b):
    mu = jnp.mean(h, axis=-1, keepdims=True)
    c = h - mu
    var = jnp.mean(c * c, axis=-1, keepdims=True)
    return c * lax.rsqrt(var + LN_EPS) * g + b


def _sorted_word_row(dest, s):
    return (dest // EXPERT_BLOCK * TOKEN_ROWS + s) * EXPERT_BLOCK + dest % EXPERT_BLOCK


def _pack_words(val):
    half = val.shape[1] // 2
    assert half == TOKEN_ROWS * LANES
    bits = lax.bitcast_convert_type(val.astype(BF16).astype(F32), jnp.uint32)
    words = (bits[:, :half] >> 16) | bits[:, half:]
    return [words[:, s * LANES:(s + 1) * LANES] for s in range(TOKEN_ROWS)]


def _unpack_words(words):
    lo = [lax.bitcast_convert_type(w << 16, F32) for w in words]
    hi = [lax.bitcast_convert_type(w & jnp.uint32(0xFFFF0000), F32) for w in words]
    return jnp.concatenate(lo + hi, axis=1)


def _mix_kernel(alpha, o_ref, mf_ref, g_ref, x_ref, wap_ref, wo_ref, lg_ref, lb_ref,
                x1_ref, x1w_ref):
    hm = o_ref.shape[0] // 2
    rows = [pl.ds(h * hm, hm) for h in range(2)]
    ys = [_dot(o_ref[r, :], wap_ref[...]) for r in rows]
    merged = [(mf_ref[r, :].astype(F32) + g_ref[r, :].astype(F32) * y).astype(BF16)
              for r, y in zip(rows, ys)]
    mixes = [_dot(m, wo_ref[...]) for m in merged]
    for r, mix in zip(rows, mixes):
        x1 = _layer_norm(alpha * x_ref[r, :] + mix, lg_ref[...], lb_ref[...])
        x1_ref[r, :] = x1
        for s, w in enumerate(_pack_words(x1)):
            x1w_ref[s, r, :] = w


def _mix(o2, mf2, g2, x2, w_attn_proj, w_o, ln_g, ln_b, alpha):
    n, d = x2.shape
    assert d == 2 * TOKEN_ROWS * LANES
    tm = _tile(n, 512)
    const = lambda i: (0, 0)
    return pl.pallas_call(
        functools.partial(_mix_kernel, alpha),
        grid=(n // tm,),
        in_specs=[
            pl.BlockSpec((tm, ATTN_WIDTH), lambda i: (i, 0)),
            pl.BlockSpec((tm, d), lambda i: (i, 0)),
            pl.BlockSpec((tm, d), lambda i: (i, 1)),
            pl.BlockSpec((tm, d), lambda i: (i, 0)),
            pl.BlockSpec((ATTN_WIDTH, d), const),
            pl.BlockSpec((d, d), const),
            pl.BlockSpec((1, d), const),
            pl.BlockSpec((1, d), const),
        ],
        out_specs=[
            pl.BlockSpec((tm, d), lambda i: (i, 0)),
            pl.BlockSpec((TOKEN_ROWS, tm, LANES), lambda i: (0, i, 0)),
        ],
        out_shape=[
            jax.ShapeDtypeStruct((n, d), F32),
            jax.ShapeDtypeStruct((TOKEN_ROWS, n, LANES), jnp.uint32),
        ],
        compiler_params=_params("parallel"),
        name="mix",
    )(o2, mf2, g2, x2, w_attn_proj.astype(BF16), w_o.astype(BF16),
      ln_g.astype(F32)[None, :], ln_b.astype(F32)[None, :])


def _route_kernel(x_ref, wh_ref, wl_ref, eb_ref, tri_ref,
                  eidx_ref, rank_ref, w_ref, cnt_ref, carry_ref):
    tm = x_ref.shape[0]

    @pl.when(pl.program_id(0) == 0)
    def _():
        carry_ref[...] = jnp.zeros_like(carry_ref)

    x = x_ref[...]
    xh = x.astype(BF16)
    xl = (x - xh.astype(F32)).astype(BF16)
    wh = wh_ref[...]
    logits = _dot_nt(wh, xh) + _dot_nt(wh, xl) + _dot_nt(wl_ref[...], xh)
    scores = _sigmoid(logits)
    biased = scores + eb_ref[:, 0:1]
    neg = -jnp.inf

    sub_iota = lax.broadcasted_iota(jnp.int32, (GROUP_SIZE, tm), 0).astype(F32)
    gs = []
    for g in range(N_EXPERT_GROUPS):
        blk = biased[g * GROUP_SIZE:(g + 1) * GROUP_SIZE, :]
        m1 = jnp.max(blk, axis=0, keepdims=True)
        a1 = jnp.min(jnp.where(blk == m1, sub_iota, float(GROUP_SIZE)), axis=0, keepdims=True)
        m2 = jnp.max(jnp.where(sub_iota == a1, neg, blk), axis=0, keepdims=True)
        gs.append(m1 + m2)

    masked = []
    for g in range(N_EXPERT_GROUPS):
        beat = jnp.zeros((1, tm), F32)
        for h in range(N_EXPERT_GROUPS):
            if h == g:
                continue
            wins = (gs[h] >= gs[g]) if h < g else (gs[h] > gs[g])
            beat = beat + jnp.where(wins, 1.0, 0.0)
        keep = beat < float(TOPK_GROUPS)
        blk = biased[g * GROUP_SIZE:(g + 1) * GROUP_SIZE, :]
        masked.append(jnp.where(keep, blk, neg))
    masked = jnp.concatenate(masked, axis=0)

    e_iota = lax.broadcasted_iota(jnp.int32, (N_EXPERTS, tm), 0).astype(F32)
    sel = jnp.zeros((N_EXPERTS, tm), F32)
    idxs, ws = [], []
    for _ in range(TOP_K):
        mx = jnp.max(masked, axis=0, keepdims=True)
        idx = jnp.min(jnp.where(masked == mx, e_iota, float(N_EXPERTS)), axis=0, keepdims=True)
        hit = e_iota == idx
        masked = jnp.where(hit, neg, masked)
        sel = jnp.where(hit, 1.0, sel)
        idxs.append(idx)
        ws.append(jnp.sum(jnp.where(hit, scores, 0.0), axis=0, keepdims=True))

    carry = carry_ref[...]
    selb = sel.astype(BF16)
    prefix = _dot(selb, tri_ref[...])
    rank_all = prefix + jnp.concatenate([carry] * (tm // LANES), axis=1)
    total = carry + _dot(selb, jnp.ones((tm, LANES), BF16))
    carry_ref[...] = total
    cnt_ref[...] = total

    wsum = ws[0]
    for j in range(1, TOP_K):
        wsum = wsum + ws[j]
    for j in range(TOP_K):
        eidx_ref[j:j + 1, :] = idxs[j].astype(jnp.int32)
        r = jnp.sum(jnp.where(e_iota == idxs[j], rank_all, 0.0), axis=0, keepdims=True)
        rank_ref[j:j + 1, :] = r.astype(jnp.int32)
        w_ref[j:j + 1, :] = ws[j] / wsum * ROUTED_SCALE


def _route(x1, w_router, e_bias):
    n, d = x1.shape
    tm = _tile(n, 512)
    wt = w_router.astype(F32).T
    wh = wt.astype(BF16)
    wl = (wt - wh.astype(F32)).astype(BF16)
    eb = jnp.broadcast_to(e_bias.astype(F32)[:, None], (N_EXPERTS, LANES))
    tri = jnp.asarray(np.triu(np.ones((tm, tm)), k=1), BF16)
    const = lambda i: (0, 0)
    return pl.pallas_call(
        _route_kernel,
        grid=(n // tm,),
        in_specs=[
            pl.BlockSpec((tm, d), lambda i: (i, 0)),
            pl.BlockSpec((N_EXPERTS, d), const),
            pl.BlockSpec((N_EXPERTS, d), const),
            pl.BlockSpec((N_EXPERTS, LANES), const),
            pl.BlockSpec((tm, tm), const),
        ],
        out_specs=[
            pl.BlockSpec((TOP_K, tm), lambda i: (0, i)),
            pl.BlockSpec((TOP_K, tm), lambda i: (0, i)),
            pl.BlockSpec((TOP_K, tm), lambda i: (0, i)),
            pl.BlockSpec((N_EXPERTS, LANES), const),
        ],
        out_shape=[
            jax.ShapeDtypeStruct((TOP_K, n), jnp.int32),
            jax.ShapeDtypeStruct((TOP_K, n), jnp.int32),
            jax.ShapeDtypeStruct((TOP_K, n), F32),
            jax.ShapeDtypeStruct((N_EXPERTS, LANES), F32),
        ],
        scratch_shapes=[pltpu.VMEM((N_EXPERTS, LANES), F32)],
        compiler_params=_params("arbitrary"),
        name="route",
    )(x1, wh, wl, eb, tri)


def _dest_kernel(eidx_ref, rank_ref, ps_ref, dest_ref):
    tm = eidx_ref.shape[1]
    e_iota = lax.broadcasted_iota(jnp.int32, (N_EXPERTS, tm), 0)
    ps = jnp.concatenate([ps_ref[...]] * (tm // LANES), axis=1)
    for j in range(TOP_K):
        hit = e_iota == eidx_ref[j:j + 1, :]
        start = jnp.sum(jnp.where(hit, ps, 0.0), axis=0, keepdims=True)
        dest_ref[j:j + 1, :] = start.astype(jnp.int32) + rank_ref[j:j + 1, :]


def _dest(eidx, rank, pad_start):
    n = eidx.shape[1]
    tm = _tile(n, 512)
    ps = jnp.broadcast_to(pad_start.astype(F32)[:, None], (N_EXPERTS, LANES))
    return pl.pallas_call(
        _dest_kernel,
        grid=(n // tm,),
        in_specs=[
            pl.BlockSpec((TOP_K, tm), lambda i: (0, i)),
            pl.BlockSpec((TOP_K, tm), lambda i: (0, i)),
            pl.BlockSpec((N_EXPERTS, LANES), lambda i: (0, 0)),
        ],
        out_specs=pl.BlockSpec((TOP_K, tm), lambda i: (0, i)),
        out_shape=jax.ShapeDtypeStruct((TOP_K, n), jnp.int32),
        compiler_params=_params("parallel"),
        name="dest",
    )(eidx, rank, ps)


def _tails_kernel(tail_ref, xs_ref, zero_ref, sem):
    zero_ref[...] = jnp.zeros_like(zero_ref)

    def tail_copy(e):
        rows = EXPERT_BLOCK * TOKEN_ROWS
        row0 = pl.multiple_of(tail_ref[e] * rows, rows)
        return pltpu.make_async_copy(zero_ref, xs_ref.at[pl.ds(row0, rows), :], sem)

    def zstart(e, c):
        @pl.when(tail_ref[e] >= 0)
        def _():
            tail_copy(e).start()
        return c

    def zwait(e, c):
        @pl.when(tail_ref[e] >= 0)
        def _():
            tail_copy(e).wait()
        return c

    lax.fori_loop(0, N_EXPERTS, zstart, 0)
    lax.fori_loop(0, N_EXPERTS, zwait, 0)


def _sc_scatter_rows(xs_init, src, idx):
    nr = src.shape[0]
    slots = idx.shape[0]
    mesh = plsc.VectorSubcoreMesh(core_axis_name="core", subcore_axis_name="subcore")

    @functools.partial(pl.kernel, mesh=mesh, scratch_types=[], out_type=())
    def scatter(src_hbm, idx_hbm, xs_hbm):
        def window(src_vmem, *idx_vmems):
            for idx_vmem in idx_vmems:
                pltpu.sync_copy(src_vmem, xs_hbm.at[idx_vmem.at[0]])

        pltpu.emit_pipeline(
            window,
            grid=(nr // SC_WINDOW,),
            in_specs=[pl.BlockSpec((SC_WINDOW, LANES), index_map=lambda i: (i, 0))]
                     + [pl.BlockSpec((1, SC_WINDOW), index_map=lambda i, j=j: (j, i))
                        for j in range(slots)],
            out_specs=[],
            core_axis_name=("core", "subcore"),
            dimension_semantics=(pltpu.PARALLEL,),
        )(src_hbm, *([idx_hbm] * slots))

    xs_ref = jax.new_ref(xs_init)
    scatter(src, idx, xs_ref)
    return xs_ref[...]


def _dispatch(tail_blk, dest, x1w, n_rows):
    n = dest.shape[1]
    xs = pl.pallas_call(
        _tails_kernel,
        grid_spec=pltpu.PrefetchScalarGridSpec(
            num_scalar_prefetch=1,
            grid=(1,),
            in_specs=[],
            out_specs=pl.BlockSpec(memory_space=pl.ANY),
            scratch_shapes=[pltpu.VMEM((EXPERT_BLOCK * TOKEN_ROWS, LANES), jnp.uint32),
                            pltpu.SemaphoreType.DMA],
        ),
        out_shape=jax.ShapeDtypeStruct((n_rows * TOKEN_ROWS, LANES), jnp.uint32),
        compiler_params=_params("arbitrary"),
        name="tails",
    )(tail_blk)
    word_rows = _sorted_word_row(dest[:, None, :], jnp.arange(TOKEN_ROWS, dtype=jnp.int32)[None, :, None])
    word_rows = word_rows.reshape(TOP_K, TOKEN_ROWS * n)
    return _sc_scatter_rows(xs, x1w.reshape(TOKEN_ROWS * n, LANES), word_rows)


def _swiglu(xb, w_in, w_down):
    h = _dot(xb, w_in)
    half = h.shape[1] // 2
    g = h[:, :half]
    act = g * _sigmoid(g) * h[:, half:]
    return _dot(act.astype(BF16), w_down)


def _experts_kernel(be_ref, nu_ref, fresh_ref, slot_ref, nxt_ref, x_ref, wi_hbm, wd_hbm, o_ref,
                    wi_buf, wd_buf, wib_ref, wdb_ref, sem):
    i = pl.program_id(0)
    used = i < nu_ref[0]
    blk = o_ref.shape[0] // TOKEN_ROWS

    def fetch(e, s):
        return (pltpu.make_async_copy(wi_hbm.at[e], wi_buf.at[s], sem.at[0, s]),
                pltpu.make_async_copy(wd_hbm.at[e], wd_buf.at[s], sem.at[1, s]))

    @pl.when(jnp.logical_and(used, fresh_ref[i] == 1))
    def _():
        s = slot_ref[i]

        @pl.when(i == 0)
        def _():
            for cp in fetch(be_ref[i], s):
                cp.start()

        for cp in fetch(be_ref[i], s):
            cp.wait()

        @pl.when(nxt_ref[i] >= 0)
        def _():
            for cp in fetch(nxt_ref[i], 1 - s):
                cp.start(priority=1)

        wib_ref[...] = wi_buf[s].astype(BF16)
        wdb_ref[...] = wd_buf[s].astype(BF16)

    @pl.when(used)
    def _():
        hb = blk // 2
        wi, wd = wib_ref[...], wdb_ref[...]
        plane = lambda s, h: pl.ds(s * blk + h * hb, hb)
        xbs = [_unpack_words([x_ref[plane(s, h), :] for s in range(TOKEN_ROWS)]).astype(BF16)
               for h in range(2)]
        hs = [_dot(xb, wi) for xb in xbs]
        half = wi.shape[1] // 2
        acts = [(h[:, :half] * _sigmoid(h[:, :half]) * h[:, half:]).astype(BF16) for h in hs]
        for h, act in enumerate(acts):
            for s, w in enumerate(_pack_words(_dot(act, wd))):
                o_ref[plane(s, h), :] = w

    @pl.when(jnp.logical_not(used))
    def _():
        o_ref[...] = jnp.zeros_like(o_ref)


def _experts(sched, xs, w_e_in, w_e_down):
    n_blocks = sched[0].shape[0]
    _, d, h2 = w_e_in.shape
    hdim = w_e_down.shape[1]
    rows = EXPERT_BLOCK * TOKEN_ROWS

    def x_map(i, be, nu, *_):
        return (jnp.minimum(i, nu[0] - 1), 0)

    return pl.pallas_call(
        _experts_kernel,
        grid_spec=pltpu.PrefetchScalarGridSpec(
            num_scalar_prefetch=len(sched),
            grid=(n_blocks,),
            in_specs=[
                pl.BlockSpec((rows, LANES), x_map),
                pl.BlockSpec(memory_space=pl.ANY),
                pl.BlockSpec(memory_space=pl.ANY),
            ],
            out_specs=pl.BlockSpec((rows, LANES), lambda i, *_: (i, 0)),
            scratch_shapes=[
                pltpu.VMEM((2, d, h2), F32), pltpu.VMEM((2, hdim, d), F32),
                pltpu.VMEM((d, h2), BF16), pltpu.VMEM((hdim, d), BF16),
                pltpu.SemaphoreType.DMA((2, 2)),
            ],
        ),
        out_shape=jax.ShapeDtypeStruct(xs.shape, jnp.uint32),
        compiler_params=_params("arbitrary"),
        name="experts",
    )(*sched, xs, w_e_in, w_e_down)


def _sc_gather_rows(table, idx):
    ni = idx.shape[0]
    mesh = plsc.VectorSubcoreMesh(core_axis_name="core", subcore_axis_name="subcore")

    @functools.partial(pl.kernel, mesh=mesh, scratch_types=[],
                       out_type=jax.ShapeDtypeStruct((ni, LANES), table.dtype))
    def gather(table_hbm, idx_hbm, out_hbm):
        def window(idx_vmem, out_vmem):
            pltpu.sync_copy(table_hbm.at[idx_vmem.at[0]], out_vmem)

        pltpu.emit_pipeline(
            window,
            grid=(ni // SC_WINDOW,),
            in_specs=[pl.BlockSpec((1, SC_WINDOW), index_map=lambda i: (0, i))],
            out_specs=[pl.BlockSpec((SC_WINDOW, LANES), index_map=lambda i: (i, 0))],
            core_axis_name=("core", "subcore"),
            dimension_semantics=(pltpu.PARALLEL,),
        )(idx_hbm, out_hbm)

    return gather(table, idx.reshape(1, ni))


def _shared_kernel(alpha, x1_ref, wsi_ref, wsd_ref, part_ref):
    x1 = x1_ref[...]
    part_ref[...] = alpha * x1 + _swiglu(x1.astype(BF16), wsi_ref[...], wsd_ref[...])


def _shared(x1, w_sh_in, w_sh_down, alpha):
    n, d = x1.shape
    tm = _tile(n, 512)
    const = lambda i: (0, 0)
    return pl.pallas_call(
        functools.partial(_shared_kernel, alpha),
        grid=(n // tm,),
        in_specs=[
            pl.BlockSpec((tm, d), lambda i: (i, 0)),
            pl.BlockSpec(w_sh_in.shape, const),
            pl.BlockSpec(w_sh_down.shape, const),
        ],
        out_specs=pl.BlockSpec((tm, d), lambda i: (i, 0)),
        out_shape=jax.ShapeDtypeStruct((n, d), F32),
        compiler_params=_params("parallel"),
        name="shared",
    )(x1, w_sh_in.astype(BF16), w_sh_down.astype(BF16))


def _finish_kernel(acc_ref, wt_ref, st_ref, lg_ref, lb_ref, out_ref):
    acc = acc_ref[...]
    for j in range(TOP_K):
        words = [st_ref[s, j] for s in range(TOKEN_ROWS)]
        acc = acc + wt_ref[:, j:j + 1] * _unpack_words(words)
    out_ref[...] = _layer_norm(acc, lg_ref[...], lb_ref[...])


def _combine(dest, part, wt, ln_g, ln_b, os):
    n, d = part.shape
    nc = n // COMBINE_CHUNKS
    tf = _tile(nc, 512)
    steps = nc // tf
    const = lambda i: (0, 0)
    out = part
    for c in range(COMBINE_CHUNKS):
        word_rows = _sorted_word_row(dest[None, :, c * nc:(c + 1) * nc],
                                     jnp.arange(TOKEN_ROWS, dtype=jnp.int32)[:, None, None]).reshape(-1)
        staged = _sc_gather_rows(os, word_rows).reshape(TOKEN_ROWS, TOP_K, nc, LANES)
        tile = lambda i, c=c: (c * steps + i, 0)
        out = pl.pallas_call(
            _finish_kernel,
            grid=(steps,),
            in_specs=[
                pl.BlockSpec((tf, d), tile),
                pl.BlockSpec((tf, TOP_K), tile),
                pl.BlockSpec((TOKEN_ROWS, TOP_K, tf, LANES), lambda i: (0, 0, i, 0)),
                pl.BlockSpec((1, d), const),
                pl.BlockSpec((1, d), const),
            ],
            out_specs=pl.BlockSpec((tf, d), tile),
            out_shape=jax.ShapeDtypeStruct((n, d), F32),
            input_output_aliases={0: 0},
            compiler_params=_params("arbitrary"),
            name="finish",
        )(out, wt, staged, ln_g.astype(F32)[None, :], ln_b.astype(F32)[None, :])
    return out


def _block_layout(counts, n_assign):
    n_blocks = (n_assign + N_EXPERTS * (EXPERT_BLOCK - 1) + EXPERT_BLOCK - 1) // EXPERT_BLOCK
    nblk = (counts + EXPERT_BLOCK - 1) // EXPERT_BLOCK
    blk_end = jnp.cumsum(nblk)
    pad_start = (blk_end - nblk) * EXPERT_BLOCK
    blk_e = jnp.sum(blk_end[None, :] <= jnp.arange(n_blocks, dtype=jnp.int32)[:, None], axis=1)
    blk_e = jnp.minimum(blk_e, N_EXPERTS - 1).astype(jnp.int32)
    n_used = blk_end[-1:].astype(jnp.int32)
    tail_blk = jnp.where(nblk > 0, blk_end - 1, -1).astype(jnp.int32)
    has = nblk > 0
    e_ids = jnp.arange(N_EXPERTS, dtype=jnp.int32)
    nxt_ge = lax.cummin(jnp.where(has, e_ids, N_EXPERTS)[::-1])[::-1]
    nxt_e = jnp.concatenate([nxt_ge[1:], jnp.full((1,), N_EXPERTS, jnp.int32)])
    nxt_e = jnp.where(nxt_e < N_EXPERTS, nxt_e, -1)
    slot_e = (jnp.cumsum(has.astype(jnp.int32)) - 1) % 2
    blk_ids = jnp.arange(n_blocks, dtype=jnp.int32)
    fresh = (blk_ids == (blk_end - nblk)[blk_e]).astype(jnp.int32)
    sched = (blk_e, n_used, fresh, slot_e[blk_e].astype(jnp.int32), nxt_e[blk_e].astype(jnp.int32))
    return n_blocks, pad_start.astype(jnp.int32), sched, tail_blk


def _layer(x, w_in, b_gate, q_g, k_g, w_four_proj, w_attn_proj, w_o, ln1_g, ln1_b,
           w_router, e_bias, w_e_in, w_e_down, w_sh_in, w_sh_down, ln2_g, ln2_b, alpha):
    batch, seq, d = x.shape
    n = batch * seq
    x2 = x.reshape(n, d)

    u, q4, k4, vt4, gates, score_bound = _inproj(x2, w_in, b_gate, q_g, k_g, batch, seq)
    mf = _fourier(u.reshape(batch, seq, FOURIER_WIDTH), gates.reshape(batch, seq, -1), w_four_proj)
    o = _attention(q4, k4, vt4, score_bound)
    x1, x1w = _mix(o.reshape(n, ATTN_WIDTH), mf.reshape(n, d), gates, x2,
                   w_attn_proj, w_o, ln1_g, ln1_b, alpha)

    eidx, rank, wts, cnt = _route(x1, w_router, e_bias)
    counts = cnt[:, 0].astype(jnp.int32)
    n_blocks, pad_start, sched, tail_blk = _block_layout(counts, n * TOP_K)
    dest = _dest(eidx, rank, pad_start)

    xs = _dispatch(tail_blk, dest, x1w, n_blocks * EXPERT_BLOCK)
    part = _shared(x1, w_sh_in, w_sh_down, alpha)
    xs, part = lax.optimization_barrier((xs, part))
    os = _experts(sched, xs, w_e_in, w_e_down)
    out = _combine(dest, part, wts.T, ln2_g, ln2_b, os)
    return out.reshape(batch, seq, d)


def kernel(x, w_in, b_gate, q_norm_g, k_norm_g, w_four_proj, w_attn_proj, w_o, ln1_g, ln1_b, w_router, e_bias, w_e_in, w_e_down, w_sh_in, w_sh_down, ln2_g, ln2_b):
    depth = w_in.shape[0]
    alpha = (2 * depth) ** 0.25
    for l in range(depth):
        x = _layer(x, w_in[l], b_gate[l], q_norm_g[l], k_norm_g[l], w_four_proj[l],
                   w_attn_proj[l], w_o[l], ln1_g[l], ln1_b[l], w_router[l], e_bias[l],
                   w_e_in[l], w_e_down[l], w_sh_in[l], w_sh_down[l], ln2_g[l], ln2_b[l], alpha)
    return x
```

```python
import functools
import math

import numpy as np
import jax
import jax.numpy as jnp
from jax import lax
from jax.experimental import pallas as pl
from jax.experimental.pallas import tpu as pltpu
from jax.experimental.pallas import tpu_sc as plsc

F32 = jnp.float32
BF16 = jnp.bfloat16

GRID_W = 64
N_FOURIER_GROUPS = 8
FOURIER_GROUP_DIM = 64
FOURIER_WIDTH = N_FOURIER_GROUPS * FOURIER_GROUP_DIM
N_Q_HEADS = 16
N_KV_HEADS = 4
HEAD_DIM = 64
Q_GROUP = N_Q_HEADS // N_KV_HEADS
ATTN_WIDTH = N_Q_HEADS * HEAD_DIM
KV_WIDTH = N_KV_HEADS * HEAD_DIM
ROPE_THETA = 10000.0
QK_EPS = 1e-6
OFF_Q = FOURIER_WIDTH
OFF_K = OFF_Q + ATTN_WIDTH
OFF_V = OFF_K + KV_WIDTH
OFF_G = OFF_V + KV_WIDTH
N_EXPERTS = 256
TOP_K = 8
N_EXPERT_GROUPS = 8
GROUP_SIZE = N_EXPERTS // N_EXPERT_GROUPS
TOPK_GROUPS = 4
ROUTED_SCALE = 2.5
LN_EPS = 1e-5

LANES = 128
SUBLANES = 8
MXU_DIM = 256
VMEM_LIMIT = 56 * 1024 * 1024

MAX_EXP2_RANGE = 100.0

EXPERT_BLOCK = 512
SC_WINDOW = 128
COMBINE_CHUNKS = 8
TOKEN_ROWS = 4

NT_DIMS = (((1,), (1,)), ((), ()))


def _dot(a, b):
    return jnp.dot(a, b, preferred_element_type=F32)


def _dot_nt(a, b):
    return lax.dot_general(a, b, NT_DIMS, preferred_element_type=F32)


def _sigmoid(x):
    return 1.0 / (1.0 + jnp.exp(-x))


def _params(*sem):
    return pltpu.CompilerParams(dimension_semantics=sem, vmem_limit_bytes=VMEM_LIMIT)


def _tile(n, pref):
    t = min(n, pref)
    assert n % t == 0, (n, t)
    return t


def _rope_tables(seq):
    lane = np.arange(MXU_DIM)
    d = lane % HEAD_DIM
    sub = d % 32
    j = sub % 16
    t = np.arange(seq)[:, None]
    pos = np.where(d[None, :] < 32, t // GRID_W, t % GRID_W).astype(np.float64)
    freq = ROPE_THETA ** (-(j.astype(np.float64)) / 16.0)
    ang = pos * freq[None, :]
    cos = np.cos(ang)
    sin = np.sin(ang) * np.where(sub < 16, -1.0, 1.0)[None, :]
    return jnp.asarray(cos, F32), jnp.asarray(sin, F32)


def _head_mean_matrix():
    i = np.arange(MXU_DIM)
    m = (i[:, None] // HEAD_DIM == i[None, :] // HEAD_DIM).astype(np.float64) / HEAD_DIM
    return jnp.asarray(m, BF16)


def _dft_tables(seq):
    c = np.arange(FOURIER_GROUP_DIM)
    ang_c = 2.0 * np.pi * ((c[:, None] * c[None, :]) % FOURIER_GROUP_DIM) / FOURIER_GROUP_DIM
    sc = 1.0 / math.sqrt(FOURIER_GROUP_DIM)
    eye = np.eye(N_FOURIER_GROUPS)
    cc = np.kron(eye, np.cos(ang_c) * sc)
    ss = np.kron(eye, np.sin(ang_c) * sc)
    chan = np.concatenate([cc, ss], axis=1)
    s = np.arange(seq)
    ang_s = 2.0 * np.pi * ((s[:, None] * s[None, :]) % seq) / seq
    ssc = 1.0 / math.sqrt(seq)
    seqm = np.concatenate([np.cos(ang_s) * ssc, -np.sin(ang_s) * ssc], axis=1)
    return jnp.asarray(chan, BF16), jnp.asarray(seqm, BF16)


def _norm_rope(z, gain, mean_mat, cos, sin, lo_mask):
    ms = _dot((z * z).astype(BF16), mean_mat)
    y = z * lax.rsqrt(ms + QK_EPS) * gain
    outs = []
    for c in range(MXU_DIM // LANES):
        yc = y[:, c * LANES:(c + 1) * LANES]
        up = pltpu.roll(yc, LANES - 16, 1)
        dn = pltpu.roll(yc, 16, 1)
        partner = jnp.where(lo_mask, up, dn)
        sl = slice(c * LANES, (c + 1) * LANES)
        outs.append(yc * cos[:, sl] + partner * sin[:, sl])
    return jnp.concatenate(outs, axis=1)


def _inproj_kernel(x_ref, w_ref, bg_ref, gq_ref, gk_ref, mm_ref, cos_ref, sin_ref,
                   u_ref, q_ref, k_ref, v_ref, g_ref):
    xb = x_ref[...].astype(BF16)

    lane = lax.broadcasted_iota(jnp.int32, (1, LANES), 1)
    lo_mask = (lane & 16) == 0
    mean_mat = mm_ref[...]
    cos = cos_ref[...]
    sin = sin_ref[...]
    heads = MXU_DIM // HEAD_DIM

    def put_u(z):
        u_ref[...] = z.astype(BF16)

    def put_q(c, z):
        q = _norm_rope(z, gq_ref[...], mean_mat, cos, sin, lo_mask).astype(BF16)
        for j in range(heads):
            q_ref[0, c * heads + j] = q[:, j * HEAD_DIM:(j + 1) * HEAD_DIM]

    def put_k(z):
        k = _norm_rope(z, gk_ref[...], mean_mat, cos, sin, lo_mask).astype(BF16)
        for j in range(N_KV_HEADS):
            k_ref[0, j] = k[:, j * HEAD_DIM:(j + 1) * HEAD_DIM]

    def put_v(z):
        vt = z.T.astype(BF16)
        for j in range(N_KV_HEADS):
            v_ref[0, j] = vt[j * HEAD_DIM:(j + 1) * HEAD_DIM, :]

    def put_g(lo, hi, z):
        g_ref[:, lo:hi] = _sigmoid(z + bg_ref[:, lo:hi]).astype(BF16)

    stages = [((0, OFF_Q), put_u)]
    for c in range(ATTN_WIDTH // MXU_DIM):
        stages.append(((OFF_Q + c * MXU_DIM, OFF_Q + (c + 1) * MXU_DIM), functools.partial(put_q, c)))
    stages.append(((OFF_K, OFF_V), put_k))
    stages.append(((OFF_V, OFF_G), put_v))
    gw = 512
    for lo in range(0, w_ref.shape[1] - OFF_G, gw):
        stages.append(((OFF_G + lo, OFF_G + lo + gw), functools.partial(put_g, lo, lo + gw)))

    z_next = _dot(xb, w_ref[:, stages[0][0][0]:stages[0][0][1]])
    for s, (_, put) in enumerate(stages):
        z = z_next
        if s + 1 < len(stages):
            lo, hi = stages[s + 1][0]
            z_next = _dot(xb, w_ref[:, lo:hi])
        put(z)


def _inproj(x2, w_in, b_gate, q_g, k_g, batch, seq):
    n, d = x2.shape
    tm = _tile(seq, 512)
    spb = seq // tm
    in_width = w_in.shape[1]
    gate_w = in_width - OFF_G
    cos, sin = _rope_tables(seq)
    mean_mat = _head_mean_matrix()
    scale = HEAD_DIM ** -0.5 * math.log2(math.e)
    gq =jnp.tile(q_g.astype(F32) * scale, MXU_DIM // HEAD_DIM)[None, :]
    gk = jnp.tile(k_g.astype(F32), MXU_DIM // HEAD_DIM)[None, :]
    score_bound = (HEAD_DIM * jnp.max(jnp.abs(gq)) * jnp.max(jnp.abs(gk))).reshape(1)
    const = lambda i: (0, 0)
    outs = pl.pallas_call(
        _inproj_kernel,
        grid=(n // tm,),
        in_specs=[
            pl.BlockSpec((tm, d), lambda i: (i, 0)),
            pl.BlockSpec((d, in_width), const),
            pl.BlockSpec((1, gate_w), const),
            pl.BlockSpec((1, MXU_DIM), const),
            pl.BlockSpec((1, MXU_DIM), const),
            pl.BlockSpec((MXU_DIM, MXU_DIM), const),
            pl.BlockSpec((tm, MXU_DIM), lambda i: (i % spb, 0)),
            pl.BlockSpec((tm, MXU_DIM), lambda i: (i % spb, 0)),
        ],
        out_specs=[
            pl.BlockSpec((tm, FOURIER_WIDTH), lambda i: (i, 0)),
            pl.BlockSpec((1, N_Q_HEADS, tm, HEAD_DIM), lambda i: (i // spb, 0, i % spb, 0)),
            pl.BlockSpec((1, N_KV_HEADS, tm, HEAD_DIM), lambda i: (i // spb, 0, i % spb, 0)),
            pl.BlockSpec((1, N_KV_HEADS, HEAD_DIM, tm), lambda i: (i // spb, 0, 0, i % spb)),
            pl.BlockSpec((tm, gate_w), lambda i: (i, 0)),
        ],
        out_shape=[
            jax.ShapeDtypeStruct((n, FOURIER_WIDTH), BF16),
            jax.ShapeDtypeStruct((batch, N_Q_HEADS, seq, HEAD_DIM), BF16),
            jax.ShapeDtypeStruct((batch, N_KV_HEADS, seq, HEAD_DIM), BF16),
            jax.ShapeDtypeStruct((batch, N_KV_HEADS, HEAD_DIM, seq), BF16),
            jax.ShapeDtypeStruct((n, gate_w), BF16),
        ],
        compiler_params=_params("parallel"),
        name="inproj",
    )(x2, w_in.astype(BF16), b_gate.astype(F32)[None, :], gq, gk, mean_mat, cos, sin)
    return (*outs, score_bound)


def _fourier_kernel(u_ref, chan_ref, seqm_ref, wp_ref, g_ref, o_ref, ab_ref):
    seq = u_ref.shape[1]

    @pl.when(pl.program_id(1) == 0)
    def _():
        ab = _dot(u_ref[0], chan_ref[...])
        ab_ref[0:seq, :] = ab[:, 0:FOURIER_WIDTH].astype(BF16)
        ab_ref[seq:2 * seq, :] = ab[:, FOURIER_WIDTH:].astype(BF16)

    f = _dot(seqm_ref[...], ab_ref[...]).astype(BF16)
    y = _dot(f, wp_ref[...])
    o_ref[0] = (g_ref[0].astype(F32) * y).astype(BF16)


def _fourier(u3, g3, w_four_proj):
    batch, seq, _ = u3.shape
    d = w_four_proj.shape[1]
    tr = _tile(seq, 512)
    chan, seqm = _dft_tables(seq)
    return pl.pallas_call(
        _fourier_kernel,
        grid=(batch, seq // tr),
        in_specs=[
            pl.BlockSpec((1, seq, FOURIER_WIDTH), lambda b, r: (b, 0, 0)),
            pl.BlockSpec((FOURIER_WIDTH, 2 * FOURIER_WIDTH), lambda b, r: (0, 0)),
            pl.BlockSpec((tr, 2 * seq), lambda b, r: (r, 0)),
            pl.BlockSpec((FOURIER_WIDTH, d), lambda b, r: (0, 0)),
            pl.BlockSpec((1, tr, d), lambda b, r: (b, r, 0)),
        ],
        out_specs=pl.BlockSpec((1, tr, d), lambda b, r: (b, r, 0)),
        out_shape=jax.ShapeDtypeStruct((batch, seq, d), BF16),
        scratch_shapes=[pltpu.VMEM((2 * seq, FOURIER_WIDTH), BF16)],
        compiler_params=_params("parallel", "arbitrary"),
        name="fourier",
    )(u3, chan, seqm, w_four_proj.astype(BF16), g3)


def _attention_kernel(bounded, sb_ref, q_ref, k_ref, vt_ref, o_ref, vone_ref):
    seq = k_ref.shape[2]

    @pl.when(pl.program_id(2) == 0)
    def _():
        vone_ref[0:HEAD_DIM, :] = vt_ref[0, 0]
        vone_ref[HEAD_DIM:, :] = jnp.ones((HEAD_DIM, seq), BF16)

    k = k_ref[0, 0]
    vone = vone_ref[...]
    outs = []
    st_next = _dot_nt(k, q_ref[0, 0])
    for g in range(Q_GROUP):
        st = st_next
        if g + 1 < Q_GROUP:
            st_next = _dot_nt(k, q_ref[0, g + 1])
        if bounded:
            m = sb_ref[0]
        else:
            m = jnp.max(st, axis=0, keepdims=True)
        pt = jnp.exp2(st - m).astype(BF16)
        ol = _dot(vone, pt)
        ot = ol[0:HEAD_DIM, :] / ol[HEAD_DIM:HEAD_DIM + 1, :]
        outs.append(ot.T.astype(BF16))
    o_ref[0] = jnp.concatenate(outs, axis=1)


def _attention(q4, k4, vt4, score_bound):
    batch, _, seq, _ = q4.shape
    tq = _tile(seq, 1024)

    def call(bounded):
        return pl.pallas_call(
            functools.partial(_attention_kernel, bounded),
            grid_spec=pltpu.PrefetchScalarGridSpec(
                num_scalar_prefetch=1,
                grid=(batch, N_KV_HEADS, seq // tq),
                in_specs=[
                    pl.BlockSpec((1, Q_GROUP, tq, HEAD_DIM), lambda b, h, i, sb: (b, h, i, 0)),
                    pl.BlockSpec((1, 1, seq, HEAD_DIM), lambda b, h, i, sb: (b, h, 0, 0)),
                    pl.BlockSpec((1, 1, HEAD_DIM, seq), lambda b, h, i, sb: (b, h, 0, 0)),
                ],
                out_specs=pl.BlockSpec((1, tq, Q_GROUP * HEAD_DIM), lambda b, h, i, sb: (b, i, h)),
                scratch_shapes=[pltpu.VMEM((2 * HEAD_DIM, seq), BF16)],
            ),
            out_shape=jax.ShapeDtypeStruct((batch, seq, ATTN_WIDTH), BF16),
            compiler_params=_params("parallel", "parallel", "arbitrary"),
            name="attention_bounded" if bounded else "attention",
        )(score_bound, q4, k4, vt4)

    return lax.cond(2.0 * score_bound[0] <= MAX_EXP2_RANGE,
                    lambda: call(True), lambda: call(False))


def _layer_norm(h, g, b):
    mu = jnp.mean(h, axis=-1, keepdims=True)
    c = h - mu
    var = jnp.mean(c * c, axis=-1, keepdims=True)
    return c * lax.rsqrt(var + LN_EPS) * g + b


def _sorted_word_row(dest, s):
    return (dest // EXPERT_BLOCK * TOKEN_ROWS + s) * EXPERT_BLOCK + dest % EXPERT_BLOCK


def _pack_words(val):
    half = val.shape[1] // 2
    assert half == TOKEN_ROWS * LANES
    bits = lax.bitcast_convert_type(val.astype(BF16).astype(F32), jnp.uint32)
    words = (bits[:, :half] >> 16) | bits[:, half:]
    return [words[:, s * LANES:(s + 1) * LANES] for s in range(TOKEN_ROWS)]


def _unpack_words(words):
    lo = [lax.bitcast_convert_type(w << 16, F32) for w in words]
    hi = [lax.bitcast_convert_type(w & jnp.uint32(0xFFFF0000), F32) for w in words]
    return jnp.concatenate(lo + hi, axis=1)


def _mix_kernel(alpha, o_ref, mf_ref, g_ref, x_ref, wap_ref, wo_ref, lg_ref, lb_ref,
                x1_ref, x1w_ref):
    hm = o_ref.shape[0] // 2
    rows = [pl.ds(h * hm, hm) for h in range(2)]
    ys = [_dot(o_ref[r, :], wap_ref[...]) for r in rows]
    merged = [(mf_ref[r, :].astype(F32) + g_ref[r, :].astype(F32) * y).astype(BF16)
              for r, y in zip(rows, ys)]
    mixes = [_dot(m, wo_ref[...]) for m in merged]
    for r, mix in zip(rows, mixes):
        x1 = _layer_norm(alpha * x_ref[r, :] + mix, lg_ref[...], lb_ref[...])
        x1_ref[r, :] = x1
        for s, w in enumerate(_pack_words(x1)):
            x1w_ref[s, r, :] = w


def _mix(o2, mf2, g2, x2, w_attn_proj, w_o, ln_g, ln_b, alpha):
    n, d = x2.shape
    assert d == 2 * TOKEN_ROWS * LANES
    tm = _tile(n, 512)
    const = lambda i: (0, 0)
    return pl.pallas_call(
        functools.partial(_mix_kernel, alpha),
        grid=(n // tm,),
        in_specs=[
            pl.BlockSpec((tm, ATTN_WIDTH), lambda i: (i, 0)),
            pl.BlockSpec((tm, d), lambda i: (i, 0)),
            pl.BlockSpec((tm, d), lambda i: (i, 1)),
            pl.BlockSpec((tm, d), lambda i: (i, 0)),
            pl.BlockSpec((ATTN_WIDTH, d), const),
            pl.BlockSpec((d, d), const),
            pl.BlockSpec((1, d), const),
            pl.BlockSpec((1, d), const),
        ],
        out_specs=[
            pl.BlockSpec((tm, d), lambda i: (i, 0)),
            pl.BlockSpec((TOKEN_ROWS, tm, LANES), lambda i: (0, i, 0)),
        ],
        out_shape=[
            jax.ShapeDtypeStruct((n, d), F32),
            jax.ShapeDtypeStruct((TOKEN_ROWS, n, LANES), jnp.uint32),
        ],
        compiler_params=_params("parallel"),
        name="mix",
    )(o2, mf2, g2, x2, w_attn_proj.astype(BF16), w_o.astype(BF16),
      ln_g.astype(F32)[None, :], ln_b.astype(F32)[None, :])


def _route_kernel(x_ref, wh_ref, wl_ref, eb_ref, tri_ref,
                  eidx_ref, rank_ref, w_ref, cnt_ref, carry_ref):
    tm = x_ref.shape[0]

    @pl.when(pl.program_id(0) == 0)
    def _():
        carry_ref[...] = jnp.zeros_like(carry_ref)

    x = x_ref[...]
    xh = x.astype(BF16)
    xl = (x - xh.astype(F32)).astype(BF16)
    wh = wh_ref[...]
    logits = _dot_nt(wh, xh) + _dot_nt(wh, xl) + _dot_nt(wl_ref[...], xh)
    scores = _sigmoid(logits)
    biased = scores + eb_ref[:, 0:1]
    neg = -jnp.inf

    sub_iota = lax.broadcasted_iota(jnp.int32, (GROUP_SIZE, tm), 0).astype(F32)
    gs = []
    for g in range(N_EXPERT_GROUPS):
        blk = biased[g * GROUP_SIZE:(g + 1) * GROUP_SIZE, :]
        m1 = jnp.max(blk, axis=0, keepdims=True)
        a1 = jnp.min(jnp.where(blk == m1, sub_iota, float(GROUP_SIZE)), axis=0, keepdims=True)
        m2 = jnp.max(jnp.where(sub_iota == a1, neg, blk), axis=0, keepdims=True)
        gs.append(m1 + m2)

    masked = []
    for g in range(N_EXPERT_GROUPS):
        beat = jnp.zeros((1, tm), F32)
        for h in range(N_EXPERT_GROUPS):
            if h == g:
                continue
            wins = (gs[h] >= gs[g]) if h < g else (gs[h] > gs[g])
            beat = beat + jnp.where(wins, 1.0, 0.0)
        keep = beat < float(TOPK_GROUPS)
        blk = biased[g * GROUP_SIZE:(g + 1) * GROUP_SIZE, :]
        masked.append(jnp.where(keep, blk, neg))
    masked = jnp.concatenate(masked, axis=0)

    e_iota = lax.broadcasted_iota(jnp.int32, (N_EXPERTS, tm), 0).astype(F32)
    sel = jnp.zeros((N_EXPERTS, tm), F32)
    idxs, ws = [], []
    for _ in range(TOP_K):
        mx = jnp.max(masked, axis=0, keepdims=True)
        idx = jnp.min(jnp.where(masked == mx, e_iota, float(N_EXPERTS)), axis=0, keepdims=True)
        hit = e_iota == idx
        masked = jnp.where(hit, neg, masked)
        sel = jnp.where(hit, 1.0, sel)
        idxs.append(idx)
        ws.append(jnp.sum(jnp.where(hit, scores, 0.0), axis=0, keepdims=True))

    carry = carry_ref[...]
    selb = sel.astype(BF16)
    prefix = _dot(selb, tri_ref[...])
    rank_all = prefix + jnp.concatenate([carry] * (tm // LANES), axis=1)
    total = carry + _dot(selb, jnp.ones((tm, LANES), BF16))
    carry_ref[...] = total
    cnt_ref[...] = total

    wsum = ws[0]
    for j in range(1, TOP_K):
        wsum = wsum + ws[j]
    for j in range(TOP_K):
        eidx_ref[j:j + 1, :] = idxs[j].astype(jnp.int32)
        r = jnp.sum(jnp.where(e_iota == idxs[j], rank_all, 0.0), axis=0, keepdims=True)
        rank_ref[j:j + 1, :] = r.astype(jnp.int32)
        w_ref[j:j + 1, :] = ws[j] / wsum * ROUTED_SCALE


def _route(x1, w_router, e_bias):
    n, d = x1.shape
    tm = _tile(n, 512)
    wt = w_router.astype(F32).T
    wh = wt.astype(BF16)
    wl = (wt - wh.astype(F32)).astype(BF16)
    eb = jnp.broadcast_to(e_bias.astype(F32)[:, None], (N_EXPERTS, LANES))
    tri = jnp.asarray(np.triu(np.ones((tm, tm)), k=1), BF16)
    const = lambda i: (0, 0)
    return pl.pallas_call(
        _route_kernel,
        grid=(n // tm,),
        in_specs=[
            pl.BlockSpec((tm, d), lambda i: (i, 0)),
            pl.BlockSpec((N_EXPERTS, d), const),
            pl.BlockSpec((N_EXPERTS, d), const),
            pl.BlockSpec((N_EXPERTS, LANES), const),
            pl.BlockSpec((tm, tm), const),
        ],
        out_specs=[
            pl.BlockSpec((TOP_K, tm), lambda i: (0, i)),
            pl.BlockSpec((TOP_K, tm), lambda i: (0, i)),
            pl.BlockSpec((TOP_K, tm), lambda i: (0, i)),
            pl.BlockSpec((N_EXPERTS, LANES), const),
        ],
        out_shape=[
            jax.ShapeDtypeStruct((TOP_K, n), jnp.int32),
            jax.ShapeDtypeStruct((TOP_K, n), jnp.int32),
            jax.ShapeDtypeStruct((TOP_K, n), F32),
            jax.ShapeDtypeStruct((N_EXPERTS, LANES), F32),
        ],
        scratch_shapes=[pltpu.VMEM((N_EXPERTS, LANES), F32)],
        compiler_params=_params("arbitrary"),
        name="route",
    )(x1, wh, wl, eb, tri)


def _dest_kernel(eidx_ref, rank_ref, ps_ref, dest_ref):
    tm = eidx_ref.shape[1]
    e_iota = lax.broadcasted_iota(jnp.int32, (N_EXPERTS, tm), 0)
    ps = jnp.concatenate([ps_ref[...]] * (tm // LANES), axis=1)
    for j in range(TOP_K):
        hit = e_iota == eidx_ref[j:j + 1, :]
        start = jnp.sum(jnp.where(hit, ps, 0.0), axis=0, keepdims=True)
        dest_ref[j:j + 1, :] = start.astype(jnp.int32) + rank_ref[j:j + 1, :]


def _dest(eidx, rank, pad_start):
    n = eidx.shape[1]
    tm = _tile(n, 512)
    ps = jnp.broadcast_to(pad_start.astype(F32)[:, None], (N_EXPERTS, LANES))
    return pl.pallas_call(
        _dest_kernel,
        grid=(n // tm,),
        in_specs=[
            pl.BlockSpec((TOP_K, tm), lambda i: (0, i)),
            pl.BlockSpec((TOP_K, tm), lambda i: (0, i)),
            pl.BlockSpec((N_EXPERTS, LANES), lambda i: (0, 0)),
        ],
        out_specs=pl.BlockSpec((TOP_K, tm), lambda i: (0, i)),
        out_shape=jax.ShapeDtypeStruct((TOP_K, n), jnp.int32),
        compiler_params=_params("parallel"),
        name="dest",
    )(eidx, rank, ps)


def _tails_kernel(tail_ref, xs_ref, zero_ref, sem):
    zero_ref[...] = jnp.zeros_like(zero_ref)

    def tail_copy(e):
        rows = EXPERT_BLOCK * TOKEN_ROWS
        row0 = pl.multiple_of(tail_ref[e] * rows, rows)
        return pltpu.make_async_copy(zero_ref, xs_ref.at[pl.ds(row0, rows), :], sem)

    def zstart(e, c):
        @pl.when(tail_ref[e] >= 0)
        def _():
            tail_copy(e).start()
        return c

    def zwait(e, c):
        @pl.when(tail_ref[e] >= 0)
        def _():
            tail_copy(e).wait()
        return c

    lax.fori_loop(0, N_EXPERTS, zstart, 0)
    lax.fori_loop(0, N_EXPERTS, zwait, 0)


def _sc_scatter_rows(xs_init, src, idx):
    nr = src.shape[0]
    slots = idx.shape[0]
    mesh = plsc.VectorSubcoreMesh(core_axis_name="core", subcore_axis_name="subcore")

    @functools.partial(pl.kernel, mesh=mesh, scratch_types=[], out_type=())
    def scatter(src_hbm, idx_hbm, xs_hbm):
        def window(src_vmem, *idx_vmems):
            for idx_vmem in idx_vmems:
                pltpu.sync_copy(src_vmem, xs_hbm.at[idx_vmem.at[0]])

        pltpu.emit_pipeline(
            window,
            grid=(nr // SC_WINDOW,),
            in_specs=[pl.BlockSpec((SC_WINDOW, LANES), index_map=lambda i: (i, 0))]
                     + [pl.BlockSpec((1, SC_WINDOW), index_map=lambda i, j=j: (j, i))
                        for j in range(slots)],
            out_specs=[],
            core_axis_name=("core", "subcore"),
            dimension_semantics=(pltpu.PARALLEL,),
        )(src_hbm, *([idx_hbm] * slots))

    xs_ref = jax.new_ref(xs_init)
    scatter(src, idx, xs_ref)
    return xs_ref[...]


def _dispatch(tail_blk, dest, x1w, n_rows):
    n = dest.shape[1]
    xs = pl.pallas_call(
        _tails_kernel,
        grid_spec=pltpu.PrefetchScalarGridSpec(
            num_scalar_prefetch=1,
            grid=(1,),
            in_specs=[],
            out_specs=pl.BlockSpec(memory_space=pl.ANY),
            scratch_shapes=[pltpu.VMEM((EXPERT_BLOCK * TOKEN_ROWS, LANES), jnp.uint32),
                            pltpu.SemaphoreType.DMA],
        ),
        out_shape=jax.ShapeDtypeStruct((n_rows * TOKEN_ROWS, LANES), jnp.uint32),
        compiler_params=_params("arbitrary"),
        name="tails",
    )(tail_blk)
    word_rows = _sorted_word_row(dest[:, None, :], jnp.arange(TOKEN_ROWS, dtype=jnp.int32)[None, :, None])
    word_rows = word_rows.reshape(TOP_K, TOKEN_ROWS * n)
    return _sc_scatter_rows(xs, x1w.reshape(TOKEN_ROWS * n, LANES), word_rows)


def _swiglu(xb, w_in, w_down):
    h = _dot(xb, w_in)
    half = h.shape[1] // 2
    g = h[:, :half]
    act = g * _sigmoid(g) * h[:, half:]
    return _dot(act.astype(BF16), w_down)


def _experts_kernel(be_ref, nu_ref, fresh_ref, slot_ref, nxt_ref, x_ref, wi_hbm, wd_hbm, o_ref,
                    wi_buf, wd_buf, wib_ref, wdb_ref, sem):
    i = pl.program_id(0)
    used = i < nu_ref[0]
    blk = o_ref.shape[0] // TOKEN_ROWS

    def fetch(e, s):
        return (pltpu.make_async_copy(wi_hbm.at[e], wi_buf.at[s], sem.at[0, s]),
                pltpu.make_async_copy(wd_hbm.at[e], wd_buf.at[s], sem.at[1, s]))

    @pl.when(jnp.logical_and(used, fresh_ref[i] == 1))
    def _():
        s = slot_ref[i]

        @pl.when(i == 0)
        def _():
            for cp in fetch(be_ref[i], s):
                cp.start()

        for cp in fetch(be_ref[i], s):
            cp.wait()

        @pl.when(nxt_ref[i] >= 0)
        def _():
            for cp in fetch(nxt_ref[i], 1 - s):
                cp.start(priority=1)

        wib_ref[...] = wi_buf[s].astype(BF16)
        wdb_ref[...] = wd_buf[s].astype(BF16)

    @pl.when(used)
    def _():
        hb = blk // 2
        wi, wd = wib_ref[...], wdb_ref[...]
        plane = lambda s, h: pl.ds(s * blk + h * hb, hb)
        xbs = [_unpack_words([x_ref[plane(s, h), :] for s in range(TOKEN_ROWS)]).astype(BF16)
               for h in range(2)]
        hs = [_dot(xb, wi) for xb in xbs]
        half = wi.shape[1] // 2
        acts = [(h[:, :half] * _sigmoid(h[:, :half]) * h[:, half:]).astype(BF16) for h in hs]
        for h, act in enumerate(acts):
            for s, w in enumerate(_pack_words(_dot(act, wd))):
                o_ref[plane(s, h), :] = w


def _experts(sched, xs, w_e_in, w_e_down):
    n_blocks = sched[0].shape[0]
    _, d, h2 = w_e_in.shape
    hdim = w_e_down.shape[1]
    rows = EXPERT_BLOCK * TOKEN_ROWS

    def x_map(i, be, nu, *_):
        return (jnp.minimum(i, nu[0] - 1), 0)

    return pl.pallas_call(
        _experts_kernel,
        grid_spec=pltpu.PrefetchScalarGridSpec(
            num_scalar_prefetch=len(sched),
            grid=(n_blocks,),
            in_specs=[
                pl.BlockSpec((rows, LANES), x_map),
                pl.BlockSpec(memory_space=pl.ANY),
                pl.BlockSpec(memory_space=pl.ANY),
            ],
            out_specs=pl.BlockSpec((rows, LANES), x_map),
            scratch_shapes=[
                pltpu.VMEM((2, d, h2), F32), pltpu.VMEM((2, hdim, d), F32),
                pltpu.VMEM((d, h2), BF16), pltpu.VMEM((hdim, d), BF16),
                pltpu.SemaphoreType.DMA((2, 2)),
            ],
        ),
        out_shape=jax.ShapeDtypeStruct(xs.shape, jnp.uint32),
        input_output_aliases={len(sched): 0},
        compiler_params=_params("arbitrary"),
        name="experts",
    )(*sched, xs, w_e_in, w_e_down)


def _sc_gather_rows(table, idx):
    ni = idx.shape[0]
    mesh = plsc.VectorSubcoreMesh(core_axis_name="core", subcore_axis_name="subcore")

    @functools.partial(pl.kernel, mesh=mesh, scratch_types=[],
                       out_type=jax.ShapeDtypeStruct((ni, LANES), table.dtype))
    def gather(table_hbm, idx_hbm, out_hbm):
        def window(idx_vmem, out_vmem):
            pltpu.sync_copy(table_hbm.at[idx_vmem.at[0]], out_vmem)

        pltpu.emit_pipeline(
            window,
            grid=(ni // SC_WINDOW,),
            in_specs=[pl.BlockSpec((1, SC_WINDOW), index_map=lambda i: (0, i))],
            out_specs=[pl.BlockSpec((SC_WINDOW, LANES), index_map=lambda i: (i, 0))],
            core_axis_name=("core", "subcore"),
            dimension_semantics=(pltpu.PARALLEL,),
        )(idx_hbm, out_hbm)

    return gather(table, idx.reshape(1, ni))


def _shared_kernel(alpha, x1_ref, wsi_ref, wsd_ref, part_ref):
    x1 = x1_ref[...]
    part_ref[...] = alpha * x1 + _swiglu(x1.astype(BF16), wsi_ref[...], wsd_ref[...])


def _shared(x1, w_sh_in, w_sh_down, alpha):
    n, d = x1.shape
    tm = _tile(n, 512)
    const = lambda i: (0, 0)
    return pl.pallas_call(
        functools.partial(_shared_kernel, alpha),
        grid=(n // tm,),
        in_specs=[
            pl.BlockSpec((tm, d), lambda i: (i, 0)),
            pl.BlockSpec(w_sh_in.shape, const),
            pl.BlockSpec(w_sh_down.shape, const),
        ],
        out_specs=pl.BlockSpec((tm, d), lambda i: (i, 0)),
        out_shape=jax.ShapeDtypeStruct((n, d), F32),
        compiler_params=_params("parallel"),
        name="shared",
    )(x1, w_sh_in.astype(BF16), w_sh_down.astype(BF16))


def _finish_kernel(acc_ref, wt_ref, st_ref, lg_ref, lb_ref, out_ref):
    acc = acc_ref[...]
    for j in range(TOP_K):
        words = [st_ref[s, j] for s in range(TOKEN_ROWS)]
        acc = acc + wt_ref[:, j:j + 1] * _unpack_words(words)
    out_ref[...] = _layer_norm(acc, lg_ref[...], lb_ref[...])


def _combine(dest, part, wt, ln_g, ln_b, os):
    n, d = part.shape
    nc = n // COMBINE_CHUNKS
    tf = _tile(nc, 512)
    steps = nc // tf
    const = lambda i: (0, 0)
    out = part
    for c in range(COMBINE_CHUNKS):
        word_rows = _sorted_word_row(dest[None, :, c * nc:(c + 1) * nc],
                                     jnp.arange(TOKEN_ROWS, dtype=jnp.int32)[:, None, None]).reshape(-1)
        staged = _sc_gather_rows(os, word_rows).reshape(TOKEN_ROWS, TOP_K, nc, LANES)
        tile = lambda i, c=c: (c * steps + i, 0)
        out = pl.pallas_call(
            _finish_kernel,
            grid=(steps,),
            in_specs=[
                pl.BlockSpec((tf, d), tile),
                pl.BlockSpec((tf, TOP_K), tile),
                pl.BlockSpec((TOKEN_ROWS, TOP_K, tf, LANES), lambda i: (0, 0, i, 0)),
                pl.BlockSpec((1, d), const),
                pl.BlockSpec((1, d), const),
            ],
            out_specs=pl.BlockSpec((tf, d), tile),
            out_shape=jax.ShapeDtypeStruct((n, d), F32),
            input_output_aliases={0: 0},
            compiler_params=_params("arbitrary"),
            name="finish",
        )(out, wt, staged, ln_g.astype(F32)[None, :], ln_b.astype(F32)[None, :])
    return out


def _block_layout(counts, n_assign):
    n_blocks = (n_assign + N_EXPERTS * (EXPERT_BLOCK - 1) + EXPERT_BLOCK - 1) // EXPERT_BLOCK
    nblk = (counts + EXPERT_BLOCK - 1) // EXPERT_BLOCK
    blk_end = jnp.cumsum(nblk)
    pad_start = (blk_end - nblk) * EXPERT_BLOCK
    blk_e = jnp.sum(blk_end[None, :] <= jnp.arange(n_blocks, dtype=jnp.int32)[:, None], axis=1)
    blk_e = jnp.minimum(blk_e, N_EXPERTS - 1).astype(jnp.int32)
    n_used = blk_end[-1:].astype(jnp.int32)
    tail_blk = jnp.where(nblk > 0, blk_end - 1, -1).astype(jnp.int32)
    has = nblk > 0
    e_ids = jnp.arange(N_EXPERTS, dtype=jnp.int32)
    nxt_ge = lax.cummin(jnp.where(has, e_ids, N_EXPERTS)[::-1])[::-1]
    nxt_e = jnp.concatenate([nxt_ge[1:], jnp.full((1,), N_EXPERTS, jnp.int32)])
    nxt_e = jnp.where(nxt_e < N_EXPERTS, nxt_e, -1)
    slot_e = (jnp.cumsum(has.astype(jnp.int32)) - 1) % 2
    blk_ids = jnp.arange(n_blocks, dtype=jnp.int32)
    fresh = (blk_ids == (blk_end - nblk)[blk_e]).astype(jnp.int32)
    sched = (blk_e, n_used, fresh, slot_e[blk_e].astype(jnp.int32), nxt_e[blk_e].astype(jnp.int32))
    return n_blocks, pad_start.astype(jnp.int32), sched, tail_blk


def _layer(x, w_in, b_gate, q_g, k_g, w_four_proj, w_attn_proj, w_o, ln1_g, ln1_b,
           w_router, e_bias, w_e_in, w_e_down, w_sh_in, w_sh_down, ln2_g, ln2_b, alpha):
    batch, seq, d = x.shape
    n = batch * seq
    x2 = x.reshape(n, d)

    u, q4, k4, vt4, gates, score_bound = _inproj(x2, w_in, b_gate, q_g, k_g, batch, seq)
    mf = _fourier(u.reshape(batch, seq, FOURIER_WIDTH), gates.reshape(batch, seq, -1), w_four_proj)
    o = _attention(q4, k4, vt4, score_bound)
    x1, x1w = _mix(o.reshape(n, ATTN_WIDTH), mf.reshape(n, d), gates, x2,
                   w_attn_proj, w_o, ln1_g, ln1_b, alpha)

    eidx, rank, wts, cnt = _route(x1, w_router, e_bias)
    counts = cnt[:, 0].astype(jnp.int32)
    n_blocks, pad_start, sched, tail_blk = _block_layout(counts, n * TOP_K)
    dest = _dest(eidx, rank, pad_start)

    xs = _dispatch(tail_blk, dest, x1w, n_blocks * EXPERT_BLOCK)
    part = _shared(x1, w_sh_in, w_sh_down, alpha)
    xs, part = lax.optimization_barrier((xs, part))
    os = _experts(sched, xs, w_e_in, w_e_down)
    out = _combine(dest, part, wts.T, ln2_g, ln2_b, os)
    return out.reshape(batch, seq, d)


def kernel(x, w_in, b_gate, q_norm_g, k_norm_g, w_four_proj, w_attn_proj, w_o, ln1_g, ln1_b, w_router, e_bias, w_e_in, w_e_down, w_sh_in, w_sh_down, ln2_g, ln2_b):
    depth = w_in.shape[0]
    alpha = (2 * depth) ** 0.25
    for l in range(depth):
        x = _layer(x, w_in[l], b_gate[l], q_norm_g[l], k_norm_g[l], w_four_proj[l],
                   w_attn_proj[l], w_o[l], ln1_g[l], ln1_b[l], w_router[l], e_bias[l],
                   w_e_in[l], w_e_down[l], w_sh_in[l], w_sh_down[l], ln2_g[l], ln2_b[l], alpha)
    return x
```

```python
import functools
import math

import numpy as np
import jax
import jax.numpy as jnp
from jax import lax
from jax.experimental import pallas as pl
from jax.experimental.pallas import tpu as pltpu
from jax.experimental.pallas import tpu_sc as plsc

F32 = jnp.float32
BF16 = jnp.bfloat16

GRID_W = 64
N_FOURIER_GROUPS = 8
FOURIER_GROUP_DIM = 64
FOURIER_WIDTH = N_FOURIER_GROUPS * FOURIER_GROUP_DIM
N_Q_HEADS = 16
N_KV_HEADS = 4
HEAD_DIM = 64
Q_GROUP = N_Q_HEADS // N_KV_HEADS
ATTN_WIDTH = N_Q_HEADS * HEAD_DIM
KV_WIDTH = N_KV_HEADS * HEAD_DIM
ROPE_THETA = 10000.0
QK_EPS = 1e-6
OFF_Q = FOURIER_WIDTH
OFF_K = OFF_Q + ATTN_WIDTH
OFF_V = OFF_K + KV_WIDTH
OFF_G = OFF_V + KV_WIDTH
N_EXPERTS = 256
TOP_K = 8
N_EXPERT_GROUPS = 8
GROUP_SIZE = N_EXPERTS // N_EXPERT_GROUPS
TOPK_GROUPS = 4
ROUTED_SCALE = 2.5
LN_EPS = 1e-5

LANES = 128
SUBLANES = 8
MXU_DIM = 256
VMEM_LIMIT = 56 * 1024 * 1024

MAX_EXP2_RANGE = 100.0

EXPERT_BLOCK = 256
SC_WINDOW = 128
COMBINE_CHUNKS = 16
TOKEN_ROWS = 4

NT_DIMS = (((1,), (1,)), ((), ()))


def _dot(a, b):
    return jnp.dot(a, b, preferred_element_type=F32)


def _dot_nt(a, b):
    return lax.dot_general(a, b, NT_DIMS, preferred_element_type=F32)


def _sigmoid(x):
    return 1.0 / (1.0 + jnp.exp(-x))


def _params(*sem):
    return pltpu.CompilerParams(dimension_semantics=sem, vmem_limit_bytes=VMEM_LIMIT)


def _tile(n, pref):
    t = min(n, pref)
    assert n % t == 0, (n, t)
    return t


def _rope_tables(seq):
    lane = np.arange(MXU_DIM)
    d = lane % HEAD_DIM
    sub = d % 32
    j = sub % 16
    t = np.arange(seq)[:, None]
    pos = np.where(d[None, :] < 32, t // GRID_W, t % GRID_W).astype(np.float64)
    freq = ROPE_THETA ** (-(j.astype(np.float64)) / 16.0)
    ang = pos * freq[None, :]
    cos = np.cos(ang)
    sin = np.sin(ang) * np.where(sub < 16, -1.0, 1.0)[None, :]
    return jnp.asarray(cos, F32), jnp.asarray(sin, F32)


def _head_mean_matrix():
    i = np.arange(MXU_DIM)
    m = (i[:, None] // HEAD_DIM == i[None, :] // HEAD_DIM).astype(np.float64) / HEAD_DIM
    return jnp.asarray(m, BF16)


def _dft_tables(seq):
    c = np.arange(FOURIER_GROUP_DIM)
    ang_c = 2.0 * np.pi * ((c[:, None] * c[None, :]) % FOURIER_GROUP_DIM) / FOURIER_GROUP_DIM
    sc = 1.0 / math.sqrt(FOURIER_GROUP_DIM)
    eye = np.eye(N_FOURIER_GROUPS)
    cc = np.kron(eye, np.cos(ang_c) * sc)
    ss = np.kron(eye, np.sin(ang_c) * sc)
    chan = np.concatenate([cc, ss], axis=1)
    s = np.arange(seq)
    ang_s = 2.0 * np.pi * ((s[:, None] * s[None, :]) % seq) / seq
    ssc = 1.0 / math.sqrt(seq)
    seqm = np.concatenate([np.cos(ang_s) * ssc, -np.sin(ang_s) * ssc], axis=1)
    return jnp.asarray(chan, BF16), jnp.asarray(seqm, BF16)


def _norm_rope(z, gain, mean_mat, cos, sin, lo_mask):
    ms = _dot((z * z).astype(BF16), mean_mat)
    y = z * lax.rsqrt(ms + QK_EPS) * gain
    outs = []
    for c in range(MXU_DIM // LANES):
        yc = y[:, c * LANES:(c + 1) * LANES]
        up = pltpu.roll(yc, LANES - 16, 1)
        dn = pltpu.roll(yc, 16, 1)
        partner = jnp.where(lo_mask, up, dn)
        sl = slice(c * LANES, (c + 1) * LANES)
        outs.append(yc * cos[:, sl] + partner * sin[:, sl])
    return jnp.concatenate(outs, axis=1)


def _inproj_kernel(x_ref, w_ref, bg_ref, gq_ref, gk_ref, mm_ref, cos_ref, sin_ref,
                   u_ref, q_ref, k_ref, v_ref, g_ref):
    xb = x_ref[...].astype(BF16)

    lane = lax.broadcasted_iota(jnp.int32, (1, LANES), 1)
    lo_mask = (lane & 16) == 0
    mean_mat = mm_ref[...]
    cos = cos_ref[...]
    sin = sin_ref[...]
    heads = MXU_DIM // HEAD_DIM

    def put_u(z):
        u_ref[...] = z.astype(BF16)

    def put_q(c, z):
        q = _norm_rope(z, gq_ref[...], mean_mat, cos, sin, lo_mask).astype(BF16)
        for j in range(heads):
            q_ref[0, c * heads + j] = q[:, j * HEAD_DIM:(j + 1) * HEAD_DIM]

    def put_k(z):
        k = _norm_rope(z, gk_ref[...], mean_mat, cos, sin, lo_mask).astype(BF16)
        for j in range(N_KV_HEADS):
            k_ref[0, j] = k[:, j * HEAD_DIM:(j + 1) * HEAD_DIM]

    def put_v(z):
        vt = z.T.astype(BF16)
        for j in range(N_KV_HEADS):
            v_ref[0, j] = vt[j * HEAD_DIM:(j + 1) * HEAD_DIM, :]

    def put_g(lo, hi, z):
        g_ref[:, lo:hi] = _sigmoid(z + bg_ref[:, lo:hi]).astype(BF16)

    stages = [((0, OFF_Q), put_u)]
    for c in range(ATTN_WIDTH // MXU_DIM):
        stages.append(((OFF_Q + c * MXU_DIM, OFF_Q + (c + 1) * MXU_DIM), functools.partial(put_q, c)))
    stages.append(((OFF_K, OFF_V), put_k))
    stages.append(((OFF_V, OFF_G), put_v))
    gw = 512
    for lo in range(0, w_ref.shape[1] - OFF_G, gw):
        stages.append(((OFF_G + lo, OFF_G + lo + gw), functools.partial(put_g, lo, lo + gw)))

    z_next = _dot(xb, w_ref[:, stages[0][0][0]:stages[0][0][1]])
    for s, (_, put) in enumerate(stages):
        z = z_next
        if s + 1 < len(stages):
            lo, hi = stages[s + 1][0]
            z_next = _dot(xb, w_ref[:, lo:hi])
        put(z)


def _inproj(x2, w_in, b_gate, q_g, k_g, batch, seq):
    n, d = x2.shape
    tm = _tile(seq, 512)
    spb = seq // tm
    in_width = w_in.shape[1]
    gate_w = in_width - OFF_G
    cos, sin = _rope_tables(seq)
    mean_mat = _head_mean_matrix()
    scale = HEAD_DIM ** -0.5 * math.log2(math.e)
    gq =jnp.tile(q_g.astype(F32) * scale, MXU_DIM // HEAD_DIM)[None, :]
    gk = jnp.tile(k_g.astype(F32), MXU_DIM // HEAD_DIM)[None, :]
    score_bound = (HEAD_DIM * jnp.max(jnp.abs(gq)) * jnp.max(jnp.abs(gk))).reshape(1)
    const = lambda i: (0, 0)
    outs = pl.pallas_call(
        _inproj_kernel,
        grid=(n // tm,),
        in_specs=[
            pl.BlockSpec((tm, d), lambda i: (i, 0)),
            pl.BlockSpec((d, in_width), const),
            pl.BlockSpec((1, gate_w), const),
            pl.BlockSpec((1, MXU_DIM), const),
            pl.BlockSpec((1, MXU_DIM), const),
            pl.BlockSpec((MXU_DIM, MXU_DIM), const),
            pl.BlockSpec((tm, MXU_DIM), lambda i: (i % spb, 0)),
            pl.BlockSpec((tm, MXU_DIM), lambda i: (i % spb, 0)),
        ],
        out_specs=[
            pl.BlockSpec((tm, FOURIER_WIDTH), lambda i: (i, 0)),
            pl.BlockSpec((1, N_Q_HEADS, tm, HEAD_DIM), lambda i: (i // spb, 0, i % spb, 0)),
            pl.BlockSpec((1, N_KV_HEADS, tm, HEAD_DIM), lambda i: (i // spb, 0, i % spb, 0)),
            pl.BlockSpec((1, N_KV_HEADS, HEAD_DIM, tm), lambda i: (i // spb, 0, 0, i % spb)),
            pl.BlockSpec((tm, gate_w), lambda i: (i, 0)),
        ],
        out_shape=[
            jax.ShapeDtypeStruct((n, FOURIER_WIDTH), BF16),
            jax.ShapeDtypeStruct((batch, N_Q_HEADS, seq, HEAD_DIM), BF16),
            jax.ShapeDtypeStruct((batch, N_KV_HEADS, seq, HEAD_DIM), BF16),
            jax.ShapeDtypeStruct((batch, N_KV_HEADS, HEAD_DIM, seq), BF16),
            jax.ShapeDtypeStruct((n, gate_w), BF16),
        ],
        compiler_params=_params("parallel"),
        name="inproj",
    )(x2, w_in.astype(BF16), b_gate.astype(F32)[None, :], gq, gk, mean_mat, cos, sin)
    return (*outs, score_bound)


def _fourier_kernel(u_ref, chan_ref, seqm_ref, wp_ref, g_ref, o_ref, ab_ref):
    seq = u_ref.shape[1]

    @pl.when(pl.program_id(1) == 0)
    def _():
        ab = _dot(u_ref[0], chan_ref[...])
        ab_ref[0:seq, :] = ab[:, 0:FOURIER_WIDTH].astype(BF16)
        ab_ref[seq:2 * seq, :] = ab[:, FOURIER_WIDTH:].astype(BF16)

    f = _dot(seqm_ref[...], ab_ref[...]).astype(BF16)
    y = _dot(f, wp_ref[...])
    o_ref[0] = (g_ref[0].astype(F32) * y).astype(BF16)


def _fourier(u3, g3, w_four_proj):
    batch, seq, _ = u3.shape
    d = w_four_proj.shape[1]
    tr = _tile(seq, 512)
    chan, seqm = _dft_tables(seq)
    return pl.pallas_call(
        _fourier_kernel,
        grid=(batch, seq // tr),
        in_specs=[
            pl.BlockSpec((1, seq, FOURIER_WIDTH), lambda b, r: (b, 0, 0)),
            pl.BlockSpec((FOURIER_WIDTH, 2 * FOURIER_WIDTH), lambda b, r: (0, 0)),
            pl.BlockSpec((tr, 2 * seq), lambda b, r: (r, 0)),
            pl.BlockSpec((FOURIER_WIDTH, d), lambda b, r: (0, 0)),
            pl.BlockSpec((1, tr, d), lambda b, r: (b, r, 0)),
        ],
        out_specs=pl.BlockSpec((1, tr, d), lambda b, r: (b, r, 0)),
        out_shape=jax.ShapeDtypeStruct((batch, seq, d), BF16),
        scratch_shapes=[pltpu.VMEM((2 * seq, FOURIER_WIDTH), BF16)],
        compiler_params=_params("parallel", "arbitrary"),
        name="fourier",
    )(u3, chan, seqm, w_four_proj.astype(BF16), g3)


def _attention_kernel(bounded, sb_ref, q_ref, k_ref, vt_ref, o_ref, vone_ref):
    seq = k_ref.shape[2]

    @pl.when(pl.program_id(2) == 0)
    def _():
        vone_ref[0:HEAD_DIM, :] = vt_ref[0, 0]
        vone_ref[HEAD_DIM:, :] = jnp.ones((HEAD_DIM, seq), BF16)

    k = k_ref[0, 0]
    vone = vone_ref[...]
    outs = []
    st_next = _dot_nt(k, q_ref[0, 0])
    for g in range(Q_GROUP):
        st = st_next
        if g + 1 < Q_GROUP:
            st_next = _dot_nt(k, q_ref[0, g + 1])
        if bounded:
            m = sb_ref[0]
        else:
            m = jnp.max(st, axis=0, keepdims=True)
        pt = jnp.exp2(st - m).astype(BF16)
        ol = _dot(vone, pt)
        ot = ol[0:HEAD_DIM, :] / ol[HEAD_DIM:HEAD_DIM + 1, :]
        outs.append(ot.T.astype(BF16))
    o_ref[0] = jnp.concatenate(outs, axis=1)


def _attention(q4, k4, vt4, score_bound):
    batch, _, seq, _ = q4.shape
    tq = _tile(seq, 1024)

    def call(bounded):
        return pl.pallas_call(
            functools.partial(_attention_kernel, bounded),
            grid_spec=pltpu.PrefetchScalarGridSpec(
                num_scalar_prefetch=1,
                grid=(batch, N_KV_HEADS, seq // tq),
                in_specs=[
                    pl.BlockSpec((1, Q_GROUP, tq, HEAD_DIM), lambda b, h, i, sb: (b, h, i, 0)),
                    pl.BlockSpec((1, 1, seq, HEAD_DIM), lambda b, h, i, sb: (b, h, 0, 0)),
                    pl.BlockSpec((1, 1, HEAD_DIM, seq), lambda b, h, i, sb: (b, h, 0, 0)),
                ],
                out_specs=pl.BlockSpec((1, tq, Q_GROUP * HEAD_DIM), lambda b, h, i, sb: (b, i, h)),
                scratch_shapes=[pltpu.VMEM((2 * HEAD_DIM, seq), BF16)],
            ),
            out_shape=jax.ShapeDtypeStruct((batch, seq, ATTN_WIDTH), BF16),
            compiler_params=_params("parallel", "parallel", "arbitrary"),
            name="attention_bounded" if bounded else "attention",
        )(score_bound, q4, k4, vt4)

    return lax.cond(2.0 * score_bound[0] <= MAX_EXP2_RANGE,
                    lambda: call(True), lambda: call(False))


def _layer_norm(h, g, b):
    mu = jnp.mean(h, axis=-1, keepdims=True)
    c = h - mu
    var = jnp.mean(c * c, axis=-1, keepdims=True)
    return c * lax.rsqrt(var + LN_EPS) * g + b


def _sorted_word_row(dest, s):
    return (dest // EXPERT_BLOCK * TOKEN_ROWS + s) * EXPERT_BLOCK + dest % EXPERT_BLOCK


def _pack_words(val):
    half = val.shape[1] // 2
    assert half == TOKEN_ROWS * LANES
    bits = lax.bitcast_convert_type(val.astype(BF16).astype(F32), jnp.uint32)
    words = (bits[:, :half] >> 16) | bits[:, half:]
    return [words[:, s * LANES:(s + 1) * LANES] for s in range(TOKEN_ROWS)]


def _unpack_words(words):
    lo = [lax.bitcast_convert_type(w << 16, F32) for w in words]
    hi = [lax.bitcast_convert_type(w & jnp.uint32(0xFFFF0000), F32) for w in words]
    return jnp.concatenate(lo + hi, axis=1)


def _mix_kernel(alpha, o_ref, mf_ref, g_ref, x_ref, wap_ref, wo_ref, lg_ref, lb_ref,
                x1_ref, x1w_ref):
    hm = o_ref.shape[0] // 2
    rows = [pl.ds(h * hm, hm) for h in range(2)]
    ys = [_dot(o_ref[r, :], wap_ref[...]) for r in rows]
    merged = [(mf_ref[r, :].astype(F32) + g_ref[r, :].astype(F32) * y).astype(BF16)
              for r, y in zip(rows, ys)]
    mixes = [_dot(m, wo_ref[...]) for m in merged]
    for r, mix in zip(rows, mixes):
        x1 = _layer_norm(alpha * x_ref[r, :] + mix, lg_ref[...], lb_ref[...])
        x1_ref[r, :] = x1
        for s, w in enumerate(_pack_words(x1)):
            x1w_ref[s, r, :] = w


def _mix(o2, mf2, g2, x2, w_attn_proj, w_o, ln_g, ln_b, alpha):
    n, d = x2.shape
    assert d == 2 * TOKEN_ROWS * LANES
    tm = _tile(n, 512)
    const = lambda i: (0, 0)
    return pl.pallas_call(
        functools.partial(_mix_kernel, alpha),
        grid=(n // tm,),
        in_specs=[
            pl.BlockSpec((tm, ATTN_WIDTH), lambda i: (i, 0)),
            pl.BlockSpec((tm, d), lambda i: (i, 0)),
            pl.BlockSpec((tm, d), lambda i: (i, 1)),
            pl.BlockSpec((tm, d), lambda i: (i, 0)),
            pl.BlockSpec((ATTN_WIDTH, d), const),
            pl.BlockSpec((d, d), const),
            pl.BlockSpec((1, d), const),
            pl.BlockSpec((1, d), const),
        ],
        out_specs=[
            pl.BlockSpec((tm, d), lambda i: (i, 0)),
            pl.BlockSpec((TOKEN_ROWS, tm, LANES), lambda i: (0, i, 0)),
        ],
        out_shape=[
            jax.ShapeDtypeStruct((n, d), F32),
            jax.ShapeDtypeStruct((TOKEN_ROWS, n, LANES), jnp.uint32),
        ],
        compiler_params=_params("parallel"),
        name="mix",
    )(o2, mf2, g2, x2, w_attn_proj.astype(BF16), w_o.astype(BF16),
      ln_g.astype(F32)[None, :], ln_b.astype(F32)[None, :])


def _route_kernel(x_ref, wh_ref, wl_ref, eb_ref, tri_ref,
                  eidx_ref, rank_ref, w_ref, cnt_ref, carry_ref):
    tm = x_ref.shape[0]

    @pl.when(pl.program_id(0) == 0)
    def _():
        carry_ref[...] = jnp.zeros_like(carry_ref)

    x = x_ref[...]
    xh = x.astype(BF16)
    xl = (x - xh.astype(F32)).astype(BF16)
    wh = wh_ref[...]
    logits = _dot_nt(wh, xh) + _dot_nt(wh, xl) + _dot_nt(wl_ref[...], xh)
    scores = _sigmoid(logits)
    biased = scores + eb_ref[:, 0:1]
    neg = -jnp.inf

    sub_iota = lax.broadcasted_iota(jnp.int32, (GROUP_SIZE, tm), 0).astype(F32)
    gs = []
    for g in range(N_EXPERT_GROUPS):
        blk = biased[g * GROUP_SIZE:(g + 1) * GROUP_SIZE, :]
        m1 = jnp.max(blk, axis=0, keepdims=True)
        a1 = jnp.min(jnp.where(blk == m1, sub_iota, float(GROUP_SIZE)), axis=0, keepdims=True)
        m2 = jnp.max(jnp.where(sub_iota == a1, neg, blk), axis=0, keepdims=True)
        gs.append(m1 + m2)

    masked = []
    for g in range(N_EXPERT_GROUPS):
        beat = jnp.zeros((1, tm), F32)
        for h in range(N_EXPERT_GROUPS):
            if h == g:
                continue
            wins = (gs[h] >= gs[g]) if h < g else (gs[h] > gs[g])
            beat = beat + jnp.where(wins, 1.0, 0.0)
        keep = beat < float(TOPK_GROUPS)
        blk = biased[g * GROUP_SIZE:(g + 1) * GROUP_SIZE, :]
        masked.append(jnp.where(keep, blk, neg))
    masked = jnp.concatenate(masked, axis=0)

    e_iota = lax.broadcasted_iota(jnp.int32, (N_EXPERTS, tm), 0).astype(F32)
    sel = jnp.zeros((N_EXPERTS, tm), F32)
    idxs, ws = [], []
    for _ in range(TOP_K):
        mx = jnp.max(masked, axis=0, keepdims=True)
        idx = jnp.min(jnp.where(masked == mx, e_iota, float(N_EXPERTS)), axis=0, keepdims=True)
        hit = e_iota == idx
        masked = jnp.where(hit, neg, masked)
        sel = jnp.where(hit, 1.0, sel)
        idxs.append(idx)
        ws.append(jnp.sum(jnp.where(hit, scores, 0.0), axis=0, keepdims=True))

    carry = carry_ref[...]
    selb = sel.astype(BF16)
    prefix = _dot(selb, tri_ref[...])
    rank_all = prefix + jnp.concatenate([carry] * (tm // LANES), axis=1)
    total = carry + _dot(selb, jnp.ones((tm, LANES), BF16))
    carry_ref[...] = total
    cnt_ref[...] = total

    wsum = ws[0]
    for j in range(1, TOP_K):
        wsum = wsum + ws[j]
    for j in range(TOP_K):
        eidx_ref[j:j + 1, :] = idxs[j].astype(jnp.int32)
        r = jnp.sum(jnp.where(e_iota == idxs[j], rank_all, 0.0), axis=0, keepdims=True)
        rank_ref[j:j + 1, :] = r.astype(jnp.int32)
        w_ref[j:j + 1, :] = ws[j] / wsum * ROUTED_SCALE


def _route(x1, w_router, e_bias):
    n, d = x1.shape
    tm = _tile(n, 512)
    wt = w_router.astype(F32).T
    wh = wt.astype(BF16)
    wl = (wt - wh.astype(F32)).astype(BF16)
    eb = jnp.broadcast_to(e_bias.astype(F32)[:, None], (N_EXPERTS, LANES))
    tri = jnp.asarray(np.triu(np.ones((tm, tm)), k=1), BF16)
    const = lambda i: (0, 0)
    return pl.pallas_call(
        _route_kernel,
        grid=(n // tm,),
        in_specs=[
            pl.BlockSpec((tm, d), lambda i: (i, 0)),
            pl.BlockSpec((N_EXPERTS, d), const),
            pl.BlockSpec((N_EXPERTS, d), const),
            pl.BlockSpec((N_EXPERTS, LANES), const),
            pl.BlockSpec((tm, tm), const),
        ],
        out_specs=[
            pl.BlockSpec((TOP_K, tm), lambda i: (0, i)),
            pl.BlockSpec((TOP_K, tm), lambda i: (0, i)),
            pl.BlockSpec((TOP_K, tm), lambda i: (0, i)),
            pl.BlockSpec((N_EXPERTS, LANES), const),
        ],
        out_shape=[
            jax.ShapeDtypeStruct((TOP_K, n), jnp.int32),
            jax.ShapeDtypeStruct((TOP_K, n), jnp.int32),
            jax.ShapeDtypeStruct((TOP_K, n), F32),
            jax.ShapeDtypeStruct((N_EXPERTS, LANES), F32),
        ],
        scratch_shapes=[pltpu.VMEM((N_EXPERTS, LANES), F32)],
        compiler_params=_params("arbitrary"),
        name="route",
    )(x1, wh, wl, eb, tri)


def _dest_kernel(eidx_ref, rank_ref, ps_ref, dest_ref):
    tm = eidx_ref.shape[1]
    e_iota = lax.broadcasted_iota(jnp.int32, (N_EXPERTS, tm), 0)
    ps = jnp.concatenate([ps_ref[...]] * (tm // LANES), axis=1)
    for j in range(TOP_K):
        hit = e_iota == eidx_ref[j:j + 1, :]
        start = jnp.sum(jnp.where(hit, ps, 0.0), axis=0, keepdims=True)
        dest_ref[j:j + 1, :] = start.astype(jnp.int32) + rank_ref[j:j + 1, :]


def _dest(eidx, rank, pad_start):
    n = eidx.shape[1]
    tm = _tile(n, 512)
    ps = jnp.broadcast_to(pad_start.astype(F32)[:, None], (N_EXPERTS, LANES))
    return pl.pallas_call(
        _dest_kernel,
        grid=(n // tm,),
        in_specs=[
            pl.BlockSpec((TOP_K, tm), lambda i: (0, i)),
            pl.BlockSpec((TOP_K, tm), lambda i: (0, i)),
            pl.BlockSpec((N_EXPERTS, LANES), lambda i: (0, 0)),
        ],
        out_specs=pl.BlockSpec((TOP_K, tm), lambda i: (0, i)),
        out_shape=jax.ShapeDtypeStruct((TOP_K, n), jnp.int32),
        compiler_params=_params("parallel"),
        name="dest",
    )(eidx, rank, ps)


def _tails_kernel(tail_ref, xs_ref, zero_ref, sem):
    zero_ref[...] = jnp.zeros_like(zero_ref)

    def tail_copy(e):
        rows = EXPERT_BLOCK * TOKEN_ROWS
        row0 = pl.multiple_of(tail_ref[e] * rows, rows)
        return pltpu.make_async_copy(zero_ref, xs_ref.at[pl.ds(row0, rows), :], sem)

    def zstart(e, c):
        @pl.when(tail_ref[e] >= 0)
        def _():
            tail_copy(e).start()
        return c

    def zwait(e, c):
        @pl.when(tail_ref[e] >= 0)
        def _():
            tail_copy(e).wait()
        return c

    lax.fori_loop(0, N_EXPERTS, zstart, 0)
    lax.fori_loop(0, N_EXPERTS, zwait, 0)


def _sc_scatter_rows(xs_init, src, idx):
    nr = src.shape[0]
    slots = idx.shape[0]
    mesh = plsc.VectorSubcoreMesh(core_axis_name="core", subcore_axis_name="subcore")

    @functools.partial(pl.kernel, mesh=mesh, scratch_types=[], out_type=())
    def scatter(src_hbm, idx_hbm, xs_hbm):
        def window(src_vmem, *idx_vmems):
            for idx_vmem in idx_vmems:
                pltpu.sync_copy(src_vmem, xs_hbm.at[idx_vmem.at[0]])

        pltpu.emit_pipeline(
            window,
            grid=(nr // SC_WINDOW,),
            in_specs=[pl.BlockSpec((SC_WINDOW, LANES), index_map=lambda i: (i, 0))]
                     + [pl.BlockSpec((1, SC_WINDOW), index_map=lambda i, j=j: (j, i))
                        for j in range(slots)],
            out_specs=[],
            core_axis_name=("core", "subcore"),
            dimension_semantics=(pltpu.PARALLEL,),
        )(src_hbm, *([idx_hbm] * slots))

    xs_ref = jax.new_ref(xs_init)
    scatter(src, idx, xs_ref)
    return xs_ref[...]


def _dispatch(tail_blk, dest, x1w, n_rows):
    n = dest.shape[1]
    xs = pl.pallas_call(
        _tails_kernel,
        grid_spec=pltpu.PrefetchScalarGridSpec(
            num_scalar_prefetch=1,
            grid=(1,),
            in_specs=[],
            out_specs=pl.BlockSpec(memory_space=pl.ANY),
            scratch_shapes=[pltpu.VMEM((EXPERT_BLOCK * TOKEN_ROWS, LANES), jnp.uint32),
                            pltpu.SemaphoreType.DMA],
        ),
        out_shape=jax.ShapeDtypeStruct((n_rows * TOKEN_ROWS, LANES), jnp.uint32),
        compiler_params=_params("arbitrary"),
        name="tails",
    )(tail_blk)
    word_rows = _sorted_word_row(dest[:, None, :], jnp.arange(TOKEN_ROWS, dtype=jnp.int32)[None, :, None])
    word_rows = word_rows.reshape(TOP_K, TOKEN_ROWS * n)
    return _sc_scatter_rows(xs, x1w.reshape(TOKEN_ROWS * n, LANES), word_rows)


def _swiglu(xb, w_in, w_down):
    h = _dot(xb, w_in)
    half = h.shape[1] // 2
    g = h[:, :half]
    act = g * _sigmoid(g) * h[:, half:]
    return _dot(act.astype(BF16), w_down)


def _experts_kernel(be_ref, nu_ref, fresh_ref, slot_ref, nxt_ref, x_ref, wi_hbm, wd_hbm, o_ref,
                    wi_buf, wd_buf, wib_ref, wdb_ref, sem):
    i = pl.program_id(0)
    used = i < nu_ref[0]
    blk = o_ref.shape[0] // TOKEN_ROWS

    def fetch(e, s):
        return (pltpu.make_async_copy(wi_hbm.at[e], wi_buf.at[s], sem.at[0, s]),
                pltpu.make_async_copy(wd_hbm.at[e], wd_buf.at[s], sem.at[1, s]))

    @pl.when(jnp.logical_and(used, fresh_ref[i] == 1))
    def _():
        s = slot_ref[i]

        @pl.when(i == 0)
        def _():
            for cp in fetch(be_ref[i], s):
                cp.start()

        for cp in fetch(be_ref[i], s):
            cp.wait()

        @pl.when(nxt_ref[i] >= 0)
        def _():
            for cp in fetch(nxt_ref[i], 1 - s):
                cp.start(priority=1)

        wib_ref[...] = wi_buf[s].astype(BF16)
        wdb_ref[...] = wd_buf[s].astype(BF16)

    @pl.when(used)
    def _():
        hb = blk // 2
        wi, wd = wib_ref[...], wdb_ref[...]
        plane = lambda s, h: pl.ds(s * blk + h * hb, hb)
        xbs = [_unpack_words([x_ref[plane(s, h), :] for s in range(TOKEN_ROWS)]).astype(BF16)
               for h in range(2)]
        hs = [_dot(xb, wi) for xb in xbs]
        half = wi.shape[1] // 2
        acts = [(h[:, :half] * _sigmoid(h[:, :half]) * h[:, half:]).astype(BF16) for h in hs]
        for h, act in enumerate(acts):
            for s, w in enumerate(_pack_words(_dot(act, wd))):
                o_ref[plane(s, h), :] = w


def _experts(sched, xs, w_e_in, w_e_down):
    n_blocks = sched[0].shape[0]
    _, d, h2 = w_e_in.shape
    hdim = w_e_down.shape[1]
    rows = EXPERT_BLOCK * TOKEN_ROWS

    def x_map(i, be, nu, *_):
        return (jnp.minimum(i, nu[0] - 1), 0)

    return pl.pallas_call(
        _experts_kernel,
        grid_spec=pltpu.PrefetchScalarGridSpec(
            num_scalar_prefetch=len(sched),
            grid=(n_blocks,),
            in_specs=[
                pl.BlockSpec((rows, LANES), x_map),
                pl.BlockSpec(memory_space=pl.ANY),
                pl.BlockSpec(memory_space=pl.ANY),
            ],
            out_specs=pl.BlockSpec((rows, LANES), x_map),
            scratch_shapes=[
                pltpu.VMEM((2, d, h2), F32), pltpu.VMEM((2, hdim, d), F32),
                pltpu.VMEM((d, h2), BF16), pltpu.VMEM((hdim, d), BF16),
                pltpu.SemaphoreType.DMA((2, 2)),
            ],
        ),
        out_shape=jax.ShapeDtypeStruct(xs.shape, jnp.uint32),
        input_output_aliases={len(sched): 0},
        compiler_params=_params("arbitrary"),
        name="experts",
    )(*sched, xs, w_e_in, w_e_down)


def _sc_gather_rows(table, idx):
    ni = idx.shape[0]
    mesh = plsc.VectorSubcoreMesh(core_axis_name="core", subcore_axis_name="subcore")

    @functools.partial(pl.kernel, mesh=mesh, scratch_types=[],
                       out_type=jax.ShapeDtypeStruct((ni, LANES), table.dtype))
    def gather(table_hbm, idx_hbm, out_hbm):
        def window(idx_vmem, out_vmem):
            pltpu.sync_copy(table_hbm.at[idx_vmem.at[0]], out_vmem)

        pltpu.emit_pipeline(
            window,
            grid=(ni // SC_WINDOW,),
            in_specs=[pl.BlockSpec((1, SC_WINDOW), index_map=lambda i: (0, i))],
            out_specs=[pl.BlockSpec((SC_WINDOW, LANES), index_map=lambda i: (i, 0))],
            core_axis_name=("core", "subcore"),
            dimension_semantics=(pltpu.PARALLEL,),
        )(idx_hbm, out_hbm)

    return gather(table, idx.reshape(1, ni))


def _shared_kernel(alpha, x1_ref, wsi_ref, wsd_ref, part_ref):
    x1 = x1_ref[...]
    part_ref[...] = alpha * x1 + _swiglu(x1.astype(BF16), wsi_ref[...], wsd_ref[...])


def _shared(x1, w_sh_in, w_sh_down, alpha):
    n, d = x1.shape
    tm = _tile(n, 512)
    const = lambda i: (0, 0)
    return pl.pallas_call(
        functools.partial(_shared_kernel, alpha),
        grid=(n // tm,),
        in_specs=[
            pl.BlockSpec((tm, d), lambda i: (i, 0)),
            pl.BlockSpec(w_sh_in.shape, const),
            pl.BlockSpec(w_sh_down.shape, const),
        ],
        out_specs=pl.BlockSpec((tm, d), lambda i: (i, 0)),
        out_shape=jax.ShapeDtypeStruct((n, d), F32),
        compiler_params=_params("parallel"),
        name="shared",
    )(x1, w_sh_in.astype(BF16), w_sh_down.astype(BF16))


def _finish_kernel(acc_ref, wt_ref, st_ref, lg_ref, lb_ref, out_ref):
    acc = acc_ref[...]
    for j in range(TOP_K):
        words = [st_ref[s, j] for s in range(TOKEN_ROWS)]
        acc = acc + wt_ref[:, j:j + 1] * _unpack_words(words)
    out_ref[...] = _layer_norm(acc, lg_ref[...], lb_ref[...])


def _combine(dest, part, wt, ln_g, ln_b, os):
    n, d = part.shape
    nc = n // COMBINE_CHUNKS
    tf = _tile(nc, 512)
    steps = nc // tf
    const = lambda i: (0, 0)
    out = part
    for c in range(COMBINE_CHUNKS):
        word_rows = _sorted_word_row(dest[None, :, c * nc:(c + 1) * nc],
                                     jnp.arange(TOKEN_ROWS, dtype=jnp.int32)[:, None, None]).reshape(-1)
        staged = _sc_gather_rows(os, word_rows).reshape(TOKEN_ROWS, TOP_K, nc, LANES)
        tile = lambda i, c=c: (c * steps + i, 0)
        out = pl.pallas_call(
            _finish_kernel,
            grid=(steps,),
            in_specs=[
                pl.BlockSpec((tf, d), tile),
                pl.BlockSpec((tf, TOP_K), tile),
                pl.BlockSpec((TOKEN_ROWS, TOP_K, tf, LANES), lambda i: (0, 0, i, 0)),
                pl.BlockSpec((1, d), const),
                pl.BlockSpec((1, d), const),
            ],
            out_specs=pl.BlockSpec((tf, d), tile),
            out_shape=jax.ShapeDtypeStruct((n, d), F32),
            input_output_aliases={0: 0},
            compiler_params=_params("arbitrary"),
            name="finish",
        )(out, wt, staged, ln_g.astype(F32)[None, :], ln_b.astype(F32)[None, :])
    return out


def _block_layout(counts, n_assign):
    n_blocks = (n_assign + N_EXPERTS * (EXPERT_BLOCK - 1) + EXPERT_BLOCK - 1) // EXPERT_BLOCK
    nblk = (counts + EXPERT_BLOCK - 1) // EXPERT_BLOCK
    blk_end = jnp.cumsum(nblk)
    pad_start = (blk_end - nblk) * EXPERT_BLOCK
    blk_e = jnp.sum(blk_end[None, :] <= jnp.arange(n_blocks, dtype=jnp.int32)[:, None], axis=1)
    blk_e = jnp.minimum(blk_e, N_EXPERTS - 1).astype(jnp.int32)
    n_used = blk_end[-1:].astype(jnp.int32)
    tail_blk = jnp.where(nblk > 0, blk_end - 1, -1).astype(jnp.int32)
    has = nblk > 0
    e_ids = jnp.arange(N_EXPERTS, dtype=jnp.int32)
    nxt_ge = lax.cummin(jnp.where(has, e_ids, N_EXPERTS)[::-1])[::-1]
    nxt_e = jnp.concatenate([nxt_ge[1:], jnp.full((1,), N_EXPERTS, jnp.int32)])
    nxt_e = jnp.where(nxt_e < N_EXPERTS, nxt_e, -1)
    slot_e = (jnp.cumsum(has.astype(jnp.int32)) - 1) % 2
    blk_ids = jnp.arange(n_blocks, dtype=jnp.int32)
    fresh = (blk_ids == (blk_end - nblk)[blk_e]).astype(jnp.int32)
    sched = (blk_e, n_used, fresh, slot_e[blk_e].astype(jnp.int32), nxt_e[blk_e].astype(jnp.int32))
    return n_blocks, pad_start.astype(jnp.int32), sched, tail_blk


def _layer(x, w_in, b_gate, q_g, k_g, w_four_proj, w_attn_proj, w_o, ln1_g, ln1_b,
           w_router, e_bias, w_e_in, w_e_down, w_sh_in, w_sh_down, ln2_g, ln2_b, alpha):
    batch, seq, d = x.shape
    n = batch * seq
    x2 = x.reshape(n, d)

    u, q4, k4, vt4, gates, score_bound = _inproj(x2, w_in, b_gate, q_g, k_g, batch, seq)
    mf = _fourier(u.reshape(batch, seq, FOURIER_WIDTH), gates.reshape(batch, seq, -1), w_four_proj)
    o = _attention(q4, k4, vt4, score_bound)
    x1, x1w = _mix(o.reshape(n, ATTN_WIDTH), mf.reshape(n, d), gates, x2,
                   w_attn_proj, w_o, ln1_g, ln1_b, alpha)

    eidx, rank, wts, cnt = _route(x1, w_router, e_bias)
    counts = cnt[:, 0].astype(jnp.int32)
    n_blocks, pad_start, sched, tail_blk = _block_layout(counts, n * TOP_K)
    dest = _dest(eidx, rank, pad_start)

    xs = _dispatch(tail_blk, dest, x1w, n_blocks * EXPERT_BLOCK)
    part = _shared(x1, w_sh_in, w_sh_down, alpha)
    xs, part = lax.optimization_barrier((xs, part))
    os = _experts(sched, xs, w_e_in, w_e_down)
    out = _combine(dest, part, wts.T, ln2_g, ln2_b, os)
    return out.reshape(batch, seq, d)


def kernel(x, w_in, b_gate, q_norm_g, k_norm_g, w_four_proj, w_attn_proj, w_o, ln1_g, ln1_b, w_router, e_bias, w_e_in, w_e_down, w_sh_in, w_sh_down, ln2_g, ln2_b):
    depth = w_in.shape[0]
    alpha = (2 * depth) ** 0.25
    for l in range(depth):
        x = _layer(x, w_in[l], b_gate[l], q_norm_g[l], k_norm_g[l], w_four_proj[l],
                   w_attn_proj[l], w_o[l], ln1_g[l], ln1_b[l], w_router[l], e_bias[l],
                   w_e_in[l], w_e_down[l], w_sh_in[l], w_sh_down[l], ln2_g[l], ln2_b[l], alpha)
    return x
```

```python
import functools
import math

import numpy as np
import jax
import jax.numpy as jnp
from jax import lax
from jax.experimental import pallas as pl
from jax.experimental.pallas import tpu as pltpu
from jax.experimental.pallas import tpu_sc as plsc

F32 = jnp.float32
BF16 = jnp.bfloat16

GRID_W = 64
N_FOURIER_GROUPS = 8
FOURIER_GROUP_DIM = 64
FOURIER_WIDTH = N_FOURIER_GROUPS * FOURIER_GROUP_DIM
N_Q_HEADS = 16
N_KV_HEADS = 4
HEAD_DIM = 64
Q_GROUP = N_Q_HEADS // N_KV_HEADS
ATTN_WIDTH = N_Q_HEADS * HEAD_DIM
KV_WIDTH = N_KV_HEADS * HEAD_DIM
ROPE_THETA = 10000.0
QK_EPS = 1e-6
OFF_Q = FOURIER_WIDTH
OFF_K = OFF_Q + ATTN_WIDTH
OFF_V = OFF_K + KV_WIDTH
OFF_G = OFF_V + KV_WIDTH
N_EXPERTS = 256
TOP_K = 8
N_EXPERT_GROUPS = 8
GROUP_SIZE = N_EXPERTS // N_EXPERT_GROUPS
TOPK_GROUPS = 4
ROUTED_SCALE = 2.5
LN_EPS = 1e-5

LANES = 128
SUBLANES = 8
MXU_DIM = 256
VMEM_LIMIT = 56 * 1024 * 1024

MAX_EXP2_RANGE = 100.0

ZERO_CHUNK = 128
EXPERT_BLOCK = 1152
SC_WINDOW = 128
COMBINE_CHUNKS = 8
TOKEN_ROWS = 4

NT_DIMS = (((1,), (1,)), ((), ()))


def _dot(a, b):
    return jnp.dot(a, b, preferred_element_type=F32)


def _dot_nt(a, b):
    return lax.dot_general(a, b, NT_DIMS, preferred_element_type=F32)


def _sigmoid(x):
    return 1.0 / (1.0 + jnp.exp(-x))


def _params(*sem):
    return pltpu.CompilerParams(dimension_semantics=sem, vmem_limit_bytes=VMEM_LIMIT)


def _tile(n, pref):
    t = min(n, pref)
    assert n % t == 0, (n, t)
    return t


def _rope_tables(seq):
    lane = np.arange(MXU_DIM)
    d = lane % HEAD_DIM
    sub = d % 32
    j = sub % 16
    t = np.arange(seq)[:, None]
    pos = np.where(d[None, :] < 32, t // GRID_W, t % GRID_W).astype(np.float64)
    freq = ROPE_THETA ** (-(j.astype(np.float64)) / 16.0)
    ang = pos * freq[None, :]
    cos = np.cos(ang)
    sin = np.sin(ang) * np.where(sub < 16, -1.0, 1.0)[None, :]
    return jnp.asarray(cos, F32), jnp.asarray(sin, F32)


def _head_mean_matrix():
    i = np.arange(MXU_DIM)
    m = (i[:, None] // HEAD_DIM == i[None, :] // HEAD_DIM).astype(np.float64) / HEAD_DIM
    return jnp.asarray(m, BF16)


def _dft_tables(seq):
    c = np.arange(FOURIER_GROUP_DIM)
    ang_c = 2.0 * np.pi * ((c[:, None] * c[None, :]) % FOURIER_GROUP_DIM) / FOURIER_GROUP_DIM
    sc = 1.0 / math.sqrt(FOURIER_GROUP_DIM)
    eye = np.eye(N_FOURIER_GROUPS)
    cc = np.kron(eye, np.cos(ang_c) * sc)
    ss = np.kron(eye, np.sin(ang_c) * sc)
    chan = np.concatenate([cc, ss], axis=1)
    s = np.arange(seq)
    ang_s = 2.0 * np.pi * ((s[:, None] * s[None, :]) % seq) / seq
    ssc = 1.0 / math.sqrt(seq)
    seqm = np.concatenate([np.cos(ang_s) * ssc, -np.sin(ang_s) * ssc], axis=1)
    return jnp.asarray(chan, BF16), jnp.asarray(seqm, BF16)


def _norm_rope(z, gain, mean_mat, cos, sin, lo_mask):
    ms = _dot((z * z).astype(BF16), mean_mat)
    y = z * lax.rsqrt(ms + QK_EPS) * gain
    outs = []
    for c in range(MXU_DIM // LANES):
        yc = y[:, c * LANES:(c + 1) * LANES]
        up = pltpu.roll(yc, LANES - 16, 1)
        dn = pltpu.roll(yc, 16, 1)
        partner = jnp.where(lo_mask, up, dn)
        sl = slice(c * LANES, (c + 1) * LANES)
        outs.append(yc * cos[:, sl] + partner * sin[:, sl])
    return jnp.concatenate(outs, axis=1)


def _inproj_kernel(x_ref, w_ref, bg_ref, gq_ref, gk_ref, mm_ref, cos_ref, sin_ref,
                   u_ref, q_ref, k_ref, v_ref, g_ref):
    xb = x_ref[...].astype(BF16)

    lane = lax.broadcasted_iota(jnp.int32, (1, LANES), 1)
    lo_mask = (lane & 16) == 0
    mean_mat = mm_ref[...]
    cos = cos_ref[...]
    sin = sin_ref[...]
    heads = MXU_DIM // HEAD_DIM

    def put_u(z):
        u_ref[...] = z.astype(BF16)

    def put_q(c, z):
        q = _norm_rope(z, gq_ref[...], mean_mat, cos, sin, lo_mask).astype(BF16)
        for j in range(heads):
            q_ref[0, c * heads + j] = q[:, j * HEAD_DIM:(j + 1) * HEAD_DIM]

    def put_k(z):
        k = _norm_rope(z, gk_ref[...], mean_mat, cos, sin, lo_mask).astype(BF16)
        for j in range(N_KV_HEADS):
            k_ref[0, j] = k[:, j * HEAD_DIM:(j + 1) * HEAD_DIM]

    def put_v(z):
        vt = z.T.astype(BF16)
        for j in range(N_KV_HEADS):
            v_ref[0, j] = vt[j * HEAD_DIM:(j + 1) * HEAD_DIM, :]

    def put_g(lo, hi, z):
        g_ref[:, lo:hi] = _sigmoid(z + bg_ref[:, lo:hi]).astype(BF16)

    stages = [((0, OFF_Q), put_u)]
    for c in range(ATTN_WIDTH // MXU_DIM):
        stages.append(((OFF_Q + c * MXU_DIM, OFF_Q + (c + 1) * MXU_DIM), functools.partial(put_q, c)))
    stages.append(((OFF_K, OFF_V), put_k))
    stages.append(((OFF_V, OFF_G), put_v))
    gw = 512
    for lo in range(0, w_ref.shape[1] - OFF_G, gw):
        stages.append(((OFF_G + lo, OFF_G + lo + gw), functools.partial(put_g, lo, lo + gw)))

    z_next = _dot(xb, w_ref[:, stages[0][0][0]:stages[0][0][1]])
    for s, (_, put) in enumerate(stages):
        z = z_next
        if s + 1 < len(stages):
            lo, hi = stages[s + 1][0]
            z_next = _dot(xb, w_ref[:, lo:hi])
        put(z)


def _inproj(x2, w_in, b_gate, q_g, k_g, batch, seq):
    n, d = x2.shape
    tm = _tile(seq, 512)
    spb = seq // tm
    in_width = w_in.shape[1]
    gate_w = in_width - OFF_G
    cos, sin = _rope_tables(seq)
    mean_mat = _head_mean_matrix()
    scale = HEAD_DIM ** -0.5 * math.log2(math.e)
    gq =jnp.tile(q_g.astype(F32) * scale, MXU_DIM // HEAD_DIM)[None, :]
    gk = jnp.tile(k_g.astype(F32), MXU_DIM // HEAD_DIM)[None, :]
    score_bound = (HEAD_DIM * jnp.max(jnp.abs(gq)) * jnp.max(jnp.abs(gk))).reshape(1)
    const = lambda i: (0, 0)
    outs = pl.pallas_call(
        _inproj_kernel,
        grid=(n // tm,),
        in_specs=[
            pl.BlockSpec((tm, d), lambda i: (i, 0)),
            pl.BlockSpec((d, in_width), const),
            pl.BlockSpec((1, gate_w), const),
            pl.BlockSpec((1, MXU_DIM), const),
            pl.BlockSpec((1, MXU_DIM), const),
            pl.BlockSpec((MXU_DIM, MXU_DIM), const),
            pl.BlockSpec((tm, MXU_DIM), lambda i: (i % spb, 0)),
            pl.BlockSpec((tm, MXU_DIM), lambda i: (i % spb, 0)),
        ],
        out_specs=[
            pl.BlockSpec((tm, FOURIER_WIDTH), lambda i: (i, 0)),
            pl.BlockSpec((1, N_Q_HEADS, tm, HEAD_DIM), lambda i: (i // spb, 0, i % spb, 0)),
            pl.BlockSpec((1, N_KV_HEADS, tm, HEAD_DIM), lambda i: (i // spb, 0, i % spb, 0)),
            pl.BlockSpec((1, N_KV_HEADS, HEAD_DIM, tm), lambda i: (i // spb, 0, 0, i % spb)),
            pl.BlockSpec((tm, gate_w), lambda i: (i, 0)),
        ],
        out_shape=[
            jax.ShapeDtypeStruct((n, FOURIER_WIDTH), BF16),
            jax.ShapeDtypeStruct((batch, N_Q_HEADS, seq, HEAD_DIM), BF16),
            jax.ShapeDtypeStruct((batch, N_KV_HEADS, seq, HEAD_DIM), BF16),
            jax.ShapeDtypeStruct((batch, N_KV_HEADS, HEAD_DIM, seq), BF16),
            jax.ShapeDtypeStruct((n, gate_w), BF16),
        ],
        compiler_params=_params("parallel"),
        name="inproj",
    )(x2, w_in.astype(BF16), b_gate.astype(F32)[None, :], gq, gk, mean_mat, cos, sin)
    return (*outs, score_bound)


def _fourier_kernel(u_ref, chan_ref, seqm_ref, wp_ref, g_ref, o_ref, ab_ref):
    seq = u_ref.shape[1]

    @pl.when(pl.program_id(1) == 0)
    def _():
        ab = _dot(u_ref[0], chan_ref[...])
        ab_ref[0:seq, :] = ab[:, 0:FOURIER_WIDTH].astype(BF16)
        ab_ref[seq:2 * seq, :] = ab[:, FOURIER_WIDTH:].astype(BF16)

    f = _dot(seqm_ref[...], ab_ref[...]).astype(BF16)
    y = _dot(f, wp_ref[...])
    o_ref[0] = (g_ref[0].astype(F32) * y).astype(BF16)


def _fourier(u3, g3, w_four_proj):
    batch, seq, _ = u3.shape
    d = w_four_proj.shape[1]
    tr = _tile(seq, 512)
    chan, seqm = _dft_tables(seq)
    return pl.pallas_call(
        _fourier_kernel,
        grid=(batch, seq // tr),
        in_specs=[
            pl.BlockSpec((1, seq, FOURIER_WIDTH), lambda b, r: (b, 0, 0)),
            pl.BlockSpec((FOURIER_WIDTH, 2 * FOURIER_WIDTH), lambda b, r: (0, 0)),
            pl.BlockSpec((tr, 2 * seq), lambda b, r: (r, 0)),
            pl.BlockSpec((FOURIER_WIDTH, d), lambda b, r: (0, 0)),
            pl.BlockSpec((1, tr, d), lambda b, r: (b, r, 0)),
        ],
        out_specs=pl.BlockSpec((1, tr, d), lambda b, r: (b, r, 0)),
        out_shape=jax.ShapeDtypeStruct((batch, seq, d), BF16),
        scratch_shapes=[pltpu.VMEM((2 * seq, FOURIER_WIDTH), BF16)],
        compiler_params=_params("parallel", "arbitrary"),
        name="fourier",
    )(u3, chan, seqm, w_four_proj.astype(BF16), g3)


def _attention_kernel(bounded, sb_ref, q_ref, k_ref, vt_ref, o_ref, vone_ref):
    seq = k_ref.shape[2]

    @pl.when(pl.program_id(2) == 0)
    def _():
        vone_ref[0:HEAD_DIM, :] = vt_ref[0, 0]
        vone_ref[HEAD_DIM:, :] = jnp.ones((HEAD_DIM, seq), BF16)

    k = k_ref[0, 0]
    vone = vone_ref[...]
    outs = []
    st_next = _dot_nt(k, q_ref[0, 0])
    for g in range(Q_GROUP):
        st = st_next
        if g + 1 < Q_GROUP:
            st_next = _dot_nt(k, q_ref[0, g + 1])
        if bounded:
            m = sb_ref[0]
        else:
            m = jnp.max(st, axis=0, keepdims=True)
        pt = jnp.exp2(st - m).astype(BF16)
        ol = _dot(vone, pt)
        ot = ol[0:HEAD_DIM, :] / ol[HEAD_DIM:HEAD_DIM + 1, :]
        outs.append(ot.T.astype(BF16))
    o_ref[0] = jnp.concatenate(outs, axis=1)


def _attention(q4, k4, vt4, score_bound):
    batch, _, seq, _ = q4.shape
    tq = _tile(seq, 1024)

    def call(bounded):
        return pl.pallas_call(
            functools.partial(_attention_kernel, bounded),
            grid_spec=pltpu.PrefetchScalarGridSpec(
                num_scalar_prefetch=1,
                grid=(batch, N_KV_HEADS, seq // tq),
                in_specs=[
                    pl.BlockSpec((1, Q_GROUP, tq, HEAD_DIM), lambda b, h, i, sb: (b, h, i, 0)),
                    pl.BlockSpec((1, 1, seq, HEAD_DIM), lambda b, h, i, sb: (b, h, 0, 0)),
                    pl.BlockSpec((1, 1, HEAD_DIM, seq), lambda b, h, i, sb: (b, h, 0, 0)),
                ],
                out_specs=pl.BlockSpec((1, tq, Q_GROUP * HEAD_DIM), lambda b, h, i, sb: (b, i, h)),
                scratch_shapes=[pltpu.VMEM((2 * HEAD_DIM, seq), BF16)],
            ),
            out_shape=jax.ShapeDtypeStruct((batch, seq, ATTN_WIDTH), BF16),
            compiler_params=_params("parallel", "parallel", "arbitrary"),
            name="attention_bounded" if bounded else "attention",
        )(score_bound, q4, k4, vt4)

    return lax.cond(2.0 * score_bound[0] <= MAX_EXP2_RANGE,
                    lambda: call(True), lambda: call(False))


def _layer_norm(h, g, b):
    mu = jnp.mean(h, axis=-1, keepdims=True)
    c = h - mu
    var = jnp.mean(c * c, axis=-1, keepdims=True)
    return c * lax.rsqrt(var + LN_EPS) * g + b


def _sorted_word_row(dest, s):
    return (dest // EXPERT_BLOCK * TOKEN_ROWS + s) * EXPERT_BLOCK + dest % EXPERT_BLOCK


def _pack_words(val):
    half = val.shape[1] // 2
    assert half == TOKEN_ROWS * LANES
    bits = lax.bitcast_convert_type(val.astype(BF16).astype(F32), jnp.uint32)
    words = (bits[:, :half] >> 16) | bits[:, half:]
    return [words[:, s * LANES:(s + 1) * LANES] for s in range(TOKEN_ROWS)]


def _unpack_words(words):
    lo = [lax.bitcast_convert_type(w << 16, F32) for w in words]
    hi = [lax.bitcast_convert_type(w & jnp.uint32(0xFFFF0000), F32) for w in words]
    return jnp.concatenate(lo + hi, axis=1)


def _mix_kernel(alpha, o_ref, mf_ref, g_ref, x_ref, wap_ref, wo_ref, lg_ref, lb_ref,
                x1_ref, x1w_ref):
    hm = o_ref.shape[0] // 2
    rows = [pl.ds(h * hm, hm) for h in range(2)]
    ys = [_dot(o_ref[r, :], wap_ref[...]) for r in rows]
    merged = [(mf_ref[r, :].astype(F32) + g_ref[r, :].astype(F32) * y).astype(BF16)
              for r, y in zip(rows, ys)]
    mixes = [_dot(m, wo_ref[...]) for m in merged]
    for r, mix in zip(rows, mixes):
        x1 = _layer_norm(alpha * x_ref[r, :] + mix, lg_ref[...], lb_ref[...])
        x1_ref[r, :] = x1
        for s, w in enumerate(_pack_words(x1)):
            x1w_ref[s, r, :] = w


def _mix(o2, mf2, g2, x2, w_attn_proj, w_o, ln_g, ln_b, alpha):
    n, d = x2.shape
    assert d == 2 * TOKEN_ROWS * LANES
    tm = _tile(n, 512)
    const = lambda i: (0, 0)
    return pl.pallas_call(
        functools.partial(_mix_kernel, alpha),
        grid=(n // tm,),
        in_specs=[
            pl.BlockSpec((tm, ATTN_WIDTH), lambda i: (i, 0)),
            pl.BlockSpec((tm, d), lambda i: (i, 0)),
            pl.BlockSpec((tm, d), lambda i: (i, 1)),
            pl.BlockSpec((tm, d), lambda i: (i, 0)),
            pl.BlockSpec((ATTN_WIDTH, d), const),
            pl.BlockSpec((d, d), const),
            pl.BlockSpec((1, d), const),
            pl.BlockSpec((1, d), const),
        ],
        out_specs=[
            pl.BlockSpec((tm, d), lambda i: (i, 0)),
            pl.BlockSpec((TOKEN_ROWS, tm, LANES), lambda i: (0, i, 0)),
        ],
        out_shape=[
            jax.ShapeDtypeStruct((n, d), F32),
            jax.ShapeDtypeStruct((TOKEN_ROWS, n, LANES), jnp.uint32),
        ],
        compiler_params=_params("parallel"),
        name="mix",
    )(o2, mf2, g2, x2, w_attn_proj.astype(BF16), w_o.astype(BF16),
      ln_g.astype(F32)[None, :], ln_b.astype(F32)[None, :])


def _route_kernel(x_ref, wh_ref, wl_ref, eb_ref, tri_ref,
                  eidx_ref, rank_ref, w_ref, cnt_ref, carry_ref):
    tm = x_ref.shape[0]

    @pl.when(pl.program_id(0) == 0)
    def _():
        carry_ref[...] = jnp.zeros_like(carry_ref)

    x = x_ref[...]
    xh = x.astype(BF16)
    xl = (x - xh.astype(F32)).astype(BF16)
    wh = wh_ref[...]
    logits = _dot_nt(wh, xh) + _dot_nt(wh, xl) + _dot_nt(wl_ref[...], xh)
    scores = _sigmoid(logits)
    biased = scores + eb_ref[:, 0:1]
    neg = -jnp.inf

    sub_iota = lax.broadcasted_iota(jnp.int32, (GROUP_SIZE, tm), 0).astype(F32)
    gs = []
    for g in range(N_EXPERT_GROUPS):
        blk = biased[g * GROUP_SIZE:(g + 1) * GROUP_SIZE, :]
        m1 = jnp.max(blk, axis=0, keepdims=True)
        a1 = jnp.min(jnp.where(blk == m1, sub_iota, float(GROUP_SIZE)), axis=0, keepdims=True)
        m2 = jnp.max(jnp.where(sub_iota == a1, neg, blk), axis=0, keepdims=True)
        gs.append(m1 + m2)

    masked = []
    for g in range(N_EXPERT_GROUPS):
        beat = jnp.zeros((1, tm), F32)
        for h in range(N_EXPERT_GROUPS):
            if h == g:
                continue
            wins = (gs[h] >= gs[g]) if h < g else (gs[h] > gs[g])
            beat = beat + jnp.where(wins, 1.0, 0.0)
        keep = beat < float(TOPK_GROUPS)
        blk = biased[g * GROUP_SIZE:(g + 1) * GROUP_SIZE, :]
        masked.append(jnp.where(keep, blk, neg))
    masked = jnp.concatenate(masked, axis=0)

    e_iota = lax.broadcasted_iota(jnp.int32, (N_EXPERTS, tm), 0).astype(F32)
    sel = jnp.zeros((N_EXPERTS, tm), F32)
    idxs, ws = [], []
    for _ in range(TOP_K):
        mx = jnp.max(masked, axis=0, keepdims=True)
        idx = jnp.min(jnp.where(masked == mx, e_iota, float(N_EXPERTS)), axis=0, keepdims=True)
        hit = e_iota == idx
        masked = jnp.where(hit, neg, masked)
        sel = jnp.where(hit, 1.0, sel)
        idxs.append(idx)
        ws.append(jnp.sum(jnp.where(hit, scores, 0.0), axis=0, keepdims=True))

    carry = carry_ref[...]
    selb = sel.astype(BF16)
    prefix = _dot(selb, tri_ref[...])
    rank_all = prefix + jnp.concatenate([carry] * (tm // LANES), axis=1)
    total = carry + _dot(selb, jnp.ones((tm, LANES), BF16))
    carry_ref[...] = total
    cnt_ref[...] = total

    wsum = ws[0]
    for j in range(1, TOP_K):
        wsum = wsum + ws[j]
    for j in range(TOP_K):
        eidx_ref[j:j + 1, :] = idxs[j].astype(jnp.int32)
        r = jnp.sum(jnp.where(e_iota == idxs[j], rank_all, 0.0), axis=0, keepdims=True)
        rank_ref[j:j + 1, :] = r.astype(jnp.int32)
        w_ref[j:j + 1, :] = ws[j] / wsum * ROUTED_SCALE


def _route(x1, w_router, e_bias):
    n, d = x1.shape
    tm = _tile(n, 512)
    wt = w_router.astype(F32).T
    wh = wt.astype(BF16)
    wl = (wt - wh.astype(F32)).astype(BF16)
    eb = jnp.broadcast_to(e_bias.astype(F32)[:, None], (N_EXPERTS, LANES))
    tri = jnp.asarray(np.triu(np.ones((tm, tm)), k=1), BF16)
    const = lambda i: (0, 0)
    return pl.pallas_call(
        _route_kernel,
        grid=(n // tm,),
        in_specs=[
            pl.BlockSpec((tm, d), lambda i: (i, 0)),
            pl.BlockSpec((N_EXPERTS, d), const),
            pl.BlockSpec((N_EXPERTS, d), const),
            pl.BlockSpec((N_EXPERTS, LANES), const),
            pl.BlockSpec((tm, tm), const),
        ],
        out_specs=[
            pl.BlockSpec((TOP_K, tm), lambda i: (0, i)),
            pl.BlockSpec((TOP_K, tm), lambda i: (0, i)),
            pl.BlockSpec((TOP_K, tm), lambda i: (0, i)),
            pl.BlockSpec((N_EXPERTS, LANES), const),
        ],
        out_shape=[
            jax.ShapeDtypeStruct((TOP_K, n), jnp.int32),
            jax.ShapeDtypeStruct((TOP_K, n), jnp.int32),
            jax.ShapeDtypeStruct((TOP_K, n), F32),
            jax.ShapeDtypeStruct((N_EXPERTS, LANES), F32),
        ],
        scratch_shapes=[pltpu.VMEM((N_EXPERTS, LANES), F32)],
        compiler_params=_params("arbitrary"),
        name="route",
    )(x1, wh, wl, eb, tri)


def _dest_kernel(eidx_ref, rank_ref, ps_ref, dest_ref):
    tm = eidx_ref.shape[1]
    e_iota = lax.broadcasted_iota(jnp.int32, (N_EXPERTS, tm), 0)
    ps = jnp.concatenate([ps_ref[...]] * (tm // LANES), axis=1)
    for j in range(TOP_K):
        hit = e_iota == eidx_ref[j:j + 1, :]
        start = jnp.sum(jnp.where(hit, ps, 0.0), axis=0, keepdims=True)
        dest_ref[j:j + 1, :] = start.astype(jnp.int32) + rank_ref[j:j + 1, :]


def _dest(eidx, rank, pad_start):
    n = eidx.shape[1]
    tm = _tile(n, 512)
    ps = jnp.broadcast_to(pad_start.astype(F32)[:, None], (N_EXPERTS, LANES))
    return pl.pallas_call(
        _dest_kernel,
        grid=(n // tm,),
        in_specs=[
            pl.BlockSpec((TOP_K, tm), lambda i: (0, i)),
            pl.BlockSpec((TOP_K, tm), lambda i: (0, i)),
            pl.BlockSpec((N_EXPERTS, LANES), lambda i: (0, 0)),
        ],
        out_specs=pl.BlockSpec((TOP_K, tm), lambda i: (0, i)),
        out_shape=jax.ShapeDtypeStruct((TOP_K, n), jnp.int32),
        compiler_params=_params("parallel"),
        name="dest",
    )(eidx, rank, ps)


def _tails_kernel(tail_ref, first_ref, xs_ref, zero_ref, sem):
    zero_ref[...] = jnp.zeros_like(zero_ref)

    def piece_copy(e, q, s):
        row0 = (tail_ref[e] * TOKEN_ROWS + s) * EXPERT_BLOCK + q * ZERO_CHUNK
        row0 = pl.multiple_of(row0, ZERO_CHUNK)
        return pltpu.make_async_copy(zero_ref, xs_ref.at[pl.ds(row0, ZERO_CHUNK), :], sem)

    def each_piece(act):
        def body(e, c):
            for q in range(EXPERT_BLOCK // ZERO_CHUNK):
                @pl.when(jnp.logical_and(tail_ref[e] >= 0, q >= first_ref[e]))
                def _():
                    for s in range(TOKEN_ROWS):
                        act(piece_copy(e, q, s))
            return c
        lax.fori_loop(0, N_EXPERTS, body, 0)

    each_piece(lambda cp: cp.start())
    each_piece(lambda cp: cp.wait())


def _sc_scatter_rows(xs_init, src, idx):
    nr = src.shape[0]
    slots = idx.shape[0]
    mesh = plsc.VectorSubcoreMesh(core_axis_name="core", subcore_axis_name="subcore")

    @functools.partial(pl.kernel, mesh=mesh, scratch_types=[], out_type=())
    def scatter(src_hbm, idx_hbm, xs_hbm):
        def window(src_vmem, *idx_vmems):
            for idx_vmem in idx_vmems:
                pltpu.sync_copy(src_vmem, xs_hbm.at[idx_vmem.at[0]])

        pltpu.emit_pipeline(
            window,
            grid=(nr // SC_WINDOW,),
            in_specs=[pl.BlockSpec((SC_WINDOW, LANES), index_map=lambda i: (i, 0))]
                     + [pl.BlockSpec((1, SC_WINDOW), index_map=lambda i, j=j: (j, i))
                        for j in range(slots)],
            out_specs=[],
            core_axis_name=("core", "subcore"),
            dimension_semantics=(pltpu.PARALLEL,),
        )(src_hbm, *([idx_hbm] * slots))

    xs_ref = jax.new_ref(xs_init)
    scatter(src, idx, xs_ref)
    return xs_ref[...]


def _dispatch(tail, dest, x1w, n_rows):
    n = dest.shape[1]
    xs = pl.pallas_call(
        _tails_kernel,
        grid_spec=pltpu.PrefetchScalarGridSpec(
            num_scalar_prefetch=len(tail),
            grid=(1,),
            in_specs=[],
            out_specs=pl.BlockSpec(memory_space=pl.ANY),
            scratch_shapes=[pltpu.VMEM((ZERO_CHUNK, LANES), jnp.uint32), pltpu.SemaphoreType.DMA],
        ),
        out_shape=jax.ShapeDtypeStruct((n_rows * TOKEN_ROWS, LANES), jnp.uint32),
        compiler_params=_params("arbitrary"),
        name="tails",
    )(*tail)
    word_rows = _sorted_word_row(dest[:, None, :], jnp.arange(TOKEN_ROWS, dtype=jnp.int32)[None, :, None])
    word_rows = word_rows.reshape(TOP_K, TOKEN_ROWS * n)
    return _sc_scatter_rows(xs, x1w.reshape(TOKEN_ROWS * n, LANES), word_rows)


def _swiglu(xb, w_in, w_down):
    h = _dot(xb, w_in)
    half = h.shape[1] // 2
    g = h[:, :half]
    act = g * _sigmoid(g) * h[:, half:]
    return _dot(act.astype(BF16), w_down)


def _experts_kernel(be_ref, nu_ref, fresh_ref, slot_ref, nxt_ref, x_ref, wi_hbm, wd_hbm, o_ref,
                    wi_buf, wd_buf, wib_ref, wdb_ref, sem):
    i = pl.program_id(0)
    used = i < nu_ref[0]
    blk = o_ref.shape[0] // TOKEN_ROWS

    def fetch(e, s):
        return (pltpu.make_async_copy(wi_hbm.at[e], wi_buf.at[s], sem.at[0, s]),
                pltpu.make_async_copy(wd_hbm.at[e], wd_buf.at[s], sem.at[1, s]))

    @pl.when(jnp.logical_and(used, fresh_ref[i] == 1))
    def _():
        s = slot_ref[i]

        @pl.when(i == 0)
        def _():
            for cp in fetch(be_ref[i], s):
                cp.start()

        for cp in fetch(be_ref[i], s):
            cp.wait()

        @pl.when(nxt_ref[i] >= 0)
        def _():
            for cp in fetch(nxt_ref[i], 1 - s):
                cp.start(priority=1)

        wib_ref[...] = wi_buf[s].astype(BF16)
        wdb_ref[...] = wd_buf[s].astype(BF16)

    @pl.when(used)
    def _():
        hb = blk // 2
        wi, wd = wib_ref[...], wdb_ref[...]
        plane = lambda s, h: pl.ds(s * blk + h * hb, hb)
        xbs = [_unpack_words([x_ref[plane(s, h), :] for s in range(TOKEN_ROWS)]).astype(BF16)
               for h in range(2)]
        hs = [_dot(xb, wi) for xb in xbs]
        half = wi.shape[1] // 2
        acts = [(h[:, :half] * _sigmoid(h[:, :half]) * h[:, half:]).astype(BF16) for h in hs]
        for h, act in enumerate(acts):
            for s, w in enumerate(_pack_words(_dot(act, wd))):
                o_ref[plane(s, h), :] = w


def _experts(sched, xs, w_e_in, w_e_down):
    n_blocks = sched[0].shape[0]
    _, d, h2 = w_e_in.shape
    hdim = w_e_down.shape[1]
    rows = EXPERT_BLOCK * TOKEN_ROWS

    def x_map(i, be, nu, *_):
        return (jnp.minimum(i, nu[0] - 1), 0)

    return pl.pallas_call(
        _experts_kernel,
        grid_spec=pltpu.PrefetchScalarGridSpec(
            num_scalar_prefetch=len(sched),
            grid=(n_blocks,),
            in_specs=[
                pl.BlockSpec((rows, LANES), x_map),
                pl.BlockSpec(memory_space=pl.ANY),
                pl.BlockSpec(memory_space=pl.ANY),
            ],
            out_specs=pl.BlockSpec((rows, LANES), x_map),
            scratch_shapes=[
                pltpu.VMEM((2, d, h2), F32), pltpu.VMEM((2, hdim, d), F32),
                pltpu.VMEM((d, h2), BF16), pltpu.VMEM((hdim, d), BF16),
                pltpu.SemaphoreType.DMA((2, 2)),
            ],
        ),
        out_shape=jax.ShapeDtypeStruct(xs.shape, jnp.uint32),
        input_output_aliases={len(sched): 0},
        compiler_params=_params("arbitrary"),
        name="experts",
    )(*sched, xs, w_e_in, w_e_down)


def _sc_gather_rows(table, idx):
    ni = idx.shape[0]
    mesh = plsc.VectorSubcoreMesh(core_axis_name="core", subcore_axis_name="subcore")

    @functools.partial(pl.kernel, mesh=mesh, scratch_types=[],
                       out_type=jax.ShapeDtypeStruct((ni, LANES), table.dtype))
    def gather(table_hbm, idx_hbm, out_hbm):
        def window(idx_vmem, out_vmem):
            pltpu.sync_copy(table_hbm.at[idx_vmem.at[0]], out_vmem)

        pltpu.emit_pipeline(
            window,
            grid=(ni // SC_WINDOW,),
            in_specs=[pl.BlockSpec((1, SC_WINDOW), index_map=lambda i: (0, i))],
            out_specs=[pl.BlockSpec((SC_WINDOW, LANES), index_map=lambda i: (i, 0))],
            core_axis_name=("core", "subcore"),
            dimension_semantics=(pltpu.PARALLEL,),
        )(idx_hbm, out_hbm)

    return gather(table, idx.reshape(1, ni))


def _shared_kernel(alpha, x1_ref, wsi_ref, wsd_ref, part_ref):
    x1 = x1_ref[...]
    part_ref[...] = alpha * x1 + _swiglu(x1.astype(BF16), wsi_ref[...], wsd_ref[...])


def _shared(x1, w_sh_in, w_sh_down, alpha):
    n, d = x1.shape
    tm = _tile(n, 512)
    const = lambda i: (0, 0)
    return pl.pallas_call(
        functools.partial(_shared_kernel, alpha),
        grid=(n // tm,),
        in_specs=[
            pl.BlockSpec((tm, d), lambda i: (i, 0)),
            pl.BlockSpec(w_sh_in.shape, const),
            pl.BlockSpec(w_sh_down.shape, const),
        ],
        out_specs=pl.BlockSpec((tm, d), lambda i: (i, 0)),
        out_shape=jax.ShapeDtypeStruct((n, d), F32),
        compiler_params=_params("parallel"),
        name="shared",
    )(x1, w_sh_in.astype(BF16), w_sh_down.astype(BF16))


def _finish_kernel(acc_ref, wt_ref, st_ref, lg_ref, lb_ref, out_ref):
    acc = acc_ref[...]
    for j in range(TOP_K):
        words = [st_ref[s, j] for s in range(TOKEN_ROWS)]
        acc = acc + wt_ref[:, j:j + 1] * _unpack_words(words)
    out_ref[...] = _layer_norm(acc, lg_ref[...], lb_ref[...])


def _combine(dest, part, wt, ln_g, ln_b, os):
    n, d = part.shape
    nc = n // COMBINE_CHUNKS
    tf = _tile(nc, 512)
    steps = nc // tf
    const = lambda i: (0, 0)
    out = part
    for c in range(COMBINE_CHUNKS):
        word_rows = _sorted_word_row(dest[None, :, c * nc:(c + 1) * nc],
                                     jnp.arange(TOKEN_ROWS, dtype=jnp.int32)[:, None, None]).reshape(-1)
        staged = _sc_gather_rows(os, word_rows).reshape(TOKEN_ROWS, TOP_K, nc, LANES)
        tile = lambda i, c=c: (c * steps + i, 0)
        out = pl.pallas_call(
            _finish_kernel,
            grid=(steps,),
            in_specs=[
                pl.BlockSpec((tf, d), tile),
                pl.BlockSpec((tf, TOP_K), tile),
                pl.BlockSpec((TOKEN_ROWS, TOP_K, tf, LANES), lambda i: (0, 0, i, 0)),
                pl.BlockSpec((1, d), const),
                pl.BlockSpec((1, d), const),
            ],
            out_specs=pl.BlockSpec((tf, d), tile),
            out_shape=jax.ShapeDtypeStruct((n, d), F32),
            input_output_aliases={0: 0},
            compiler_params=_params("arbitrary"),
            name="finish",
        )(out, wt, staged, ln_g.astype(F32)[None, :], ln_b.astype(F32)[None, :])
    return out


def _block_layout(counts, n_assign):
    n_blocks = (n_assign + N_EXPERTS * (EXPERT_BLOCK - 1) + EXPERT_BLOCK - 1) // EXPERT_BLOCK
    nblk = (counts + EXPERT_BLOCK - 1) // EXPERT_BLOCK
    blk_end = jnp.cumsum(nblk)
    pad_start = (blk_end - nblk) * EXPERT_BLOCK
    blk_e = jnp.sum(blk_end[None, :] <= jnp.arange(n_blocks, dtype=jnp.int32)[:, None], axis=1)
    blk_e = jnp.minimum(blk_e, N_EXPERTS - 1).astype(jnp.int32)
    n_used = blk_end[-1:].astype(jnp.int32)
    tail_blk = jnp.where(nblk > 0, blk_end - 1, -1).astype(jnp.int32)
    tail_rows = counts - (nblk - 1) * EXPERT_BLOCK
    tail_first = (tail_rows // ZERO_CHUNK).astype(jnp.int32)
    tail = (tail_blk, tail_first)
    has = nblk > 0
    e_ids = jnp.arange(N_EXPERTS, dtype=jnp.int32)
    nxt_ge = lax.cummin(jnp.where(has, e_ids, N_EXPERTS)[::-1])[::-1]
    nxt_e = jnp.concatenate([nxt_ge[1:], jnp.full((1,), N_EXPERTS, jnp.int32)])
    nxt_e = jnp.where(nxt_e < N_EXPERTS, nxt_e, -1)
    slot_e = (jnp.cumsum(has.astype(jnp.int32)) - 1) % 2
    blk_ids = jnp.arange(n_blocks, dtype=jnp.int32)
    fresh = (blk_ids == (blk_end - nblk)[blk_e]).astype(jnp.int32)
    sched = (blk_e, n_used, fresh, slot_e[blk_e].astype(jnp.int32), nxt_e[blk_e].astype(jnp.int32))
    return n_blocks, pad_start.astype(jnp.int32), sched, tail


def _layer(x, w_in, b_gate, q_g, k_g, w_four_proj, w_attn_proj, w_o, ln1_g, ln1_b,
           w_router, e_bias, w_e_in, w_e_down, w_sh_in, w_sh_down, ln2_g, ln2_b, alpha):
    batch, seq, d = x.shape
    n = batch * seq
    x2 = x.reshape(n, d)

    u, q4, k4, vt4, gates, score_bound = _inproj(x2, w_in, b_gate, q_g, k_g, batch, seq)
    mf = _fourier(u.reshape(batch, seq, FOURIER_WIDTH), gates.reshape(batch, seq, -1), w_four_proj)
    o = _attention(q4, k4, vt4, score_bound)
    x1, x1w = _mix(o.reshape(n, ATTN_WIDTH), mf.reshape(n, d), gates, x2,
                   w_attn_proj, w_o, ln1_g, ln1_b, alpha)

    eidx, rank, wts, cnt = _route(x1, w_router, e_bias)
    counts = cnt[:, 0].astype(jnp.int32)
    n_blocks, pad_start, sched, tail = _block_layout(counts, n * TOP_K)
    dest = _dest(eidx, rank, pad_start)

    xs = _dispatch(tail, dest, x1w, n_blocks * EXPERT_BLOCK)
    part = _shared(x1, w_sh_in, w_sh_down, alpha)
    xs, part = lax.optimization_barrier((xs, part))
    os = _experts(sched, xs, w_e_in, w_e_down)
    out = _combine(dest, part, wts.T, ln2_g, ln2_b, os)
    return out.reshape(batch, seq, d)


def kernel(x, w_in, b_gate, q_norm_g, k_norm_g, w_four_proj, w_attn_proj, w_o, ln1_g, ln1_b, w_router, e_bias, w_e_in, w_e_down, w_sh_in, w_sh_down, ln2_g, ln2_b):
    depth = w_in.shape[0]
    alpha = (2 * depth) ** 0.25
    for l in range(depth):
        x = _layer(x, w_in[l], b_gate[l], q_norm_g[l], k_norm_g[l], w_four_proj[l],
                   w_attn_proj[l], w_o[l], ln1_g[l], ln1_b[l], w_router[l], e_bias[l],
                   w_e_in[l], w_e_down[l], w_sh_in[l], w_sh_down[l], ln2_g[l], ln2_b[l], alpha)
    return x
```

```python
import functools
import math

import numpy as np
import jax
import jax.numpy as jnp
from jax import lax
from jax.experimental import pallas as pl
from jax.experimental.pallas import tpu as pltpu
from jax.experimental.pallas import tpu_sc as plsc

F32 = jnp.float32
BF16 = jnp.bfloat16

GRID_W = 64
N_FOURIER_GROUPS = 8
FOURIER_GROUP_DIM = 64
FOURIER_WIDTH = N_FOURIER_GROUPS * FOURIER_GROUP_DIM
N_Q_HEADS = 16
N_KV_HEADS = 4
HEAD_DIM = 64
Q_GROUP = N_Q_HEADS // N_KV_HEADS
ATTN_WIDTH = N_Q_HEADS * HEAD_DIM
KV_WIDTH = N_KV_HEADS * HEAD_DIM
ROPE_THETA = 10000.0
QK_EPS = 1e-6
OFF_Q = FOURIER_WIDTH
OFF_K = OFF_Q + ATTN_WIDTH
OFF_V = OFF_K + KV_WIDTH
OFF_G = OFF_V + KV_WIDTH
N_EXPERTS = 256
TOP_K = 8
N_EXPERT_GROUPS = 8
GROUP_SIZE = N_EXPERTS // N_EXPERT_GROUPS
TOPK_GROUPS = 4
ROUTED_SCALE = 2.5
LN_EPS = 1e-5

LANES = 128
MXU_DIM = 256
VMEM_LIMIT = 56 * 1024 * 1024

MAX_EXP2_RANGE = 100.0

ZERO_CHUNK = 128
EXPERT_BLOCK = 1152
SC_WINDOW = 128
COMBINE_CHUNKS = 4
TOKEN_ROWS = 4

NT_DIMS = (((1,), (1,)), ((), ()))


def _dot(a, b):
    return jnp.dot(a, b, preferred_element_type=F32)


def _dot_nt(a, b):
    return lax.dot_general(a, b, NT_DIMS, preferred_element_type=F32)


def _sigmoid(x):
    return 1.0 / (1.0 + jnp.exp(-x))


def _params(*sem):
    return pltpu.CompilerParams(dimension_semantics=sem, vmem_limit_bytes=VMEM_LIMIT)


def _tile(n, pref):
    t = min(n, pref)
    assert n % t == 0, (n, t)
    return t


def _rope_tables(seq):
    lane = np.arange(MXU_DIM)
    d = lane % HEAD_DIM
    sub = d % 32
    j = sub % 16
    t = np.arange(seq)[:, None]
    pos = np.where(d[None, :] < 32, t // GRID_W, t % GRID_W).astype(np.float64)
    freq = ROPE_THETA ** (-(j.astype(np.float64)) / 16.0)
    ang = pos * freq[None, :]
    cos = np.cos(ang)
    sin = np.sin(ang) * np.where(sub < 16, -1.0, 1.0)[None, :]
    return jnp.asarray(cos, F32), jnp.asarray(sin, F32)


def _head_mean_matrix():
    i = np.arange(MXU_DIM)
    m = (i[:, None] // HEAD_DIM == i[None, :] // HEAD_DIM).astype(np.float64) / HEAD_DIM
    return jnp.asarray(m, BF16)


def _dft_tables(seq):
    c = np.arange(FOURIER_GROUP_DIM)
    ang_c = 2.0 * np.pi * ((c[:, None] * c[None, :]) % FOURIER_GROUP_DIM) / FOURIER_GROUP_DIM
    sc = 1.0 / math.sqrt(FOURIER_GROUP_DIM)
    eye = np.eye(MXU_DIM // FOURIER_GROUP_DIM)
    block = np.concatenate([np.kron(eye, np.cos(ang_c) * sc), np.kron(eye, np.sin(ang_c) * sc)], axis=1)
    chan = np.stack([block] * (FOURIER_WIDTH // MXU_DIM))
    s = np.arange(seq)
    ang_s = 2.0 * np.pi * ((s[:, None] * s[None, :]) % seq) / seq
    ssc = 1.0 / math.sqrt(seq)
    seqm = np.concatenate([np.cos(ang_s) * ssc, -np.sin(ang_s) * ssc], axis=1)
    return jnp.asarray(chan, BF16), jnp.asarray(seqm, BF16)


def _norm_rope(z, gain, mean_mat, cos, sin, lo_mask):
    ms = _dot((z * z).astype(BF16), mean_mat)
    y = z * lax.rsqrt(ms + QK_EPS) * gain
    outs = []
    for c in range(MXU_DIM // LANES):
        yc = y[:, c * LANES:(c + 1) * LANES]
        up = pltpu.roll(yc, LANES - 16, 1)
        dn = pltpu.roll(yc, 16, 1)
        partner = jnp.where(lo_mask, up, dn)
        sl = slice(c * LANES, (c + 1) * LANES)
        outs.append(yc * cos[:, sl] + partner * sin[:, sl])
    return jnp.concatenate(outs, axis=1)


def _inproj_kernel(x_ref, w_ref, bg_ref, gq_ref, gk_ref, mm_ref, cos_ref, sin_ref,
                   u_ref, q_ref, k_ref, v_ref, g_ref):
    xb = x_ref[...].astype(BF16)

    lane = lax.broadcasted_iota(jnp.int32, (1, LANES), 1)
    lo_mask = (lane & 16) == 0
    mean_mat = mm_ref[...]
    cos = cos_ref[...]
    sin = sin_ref[...]
    heads = MXU_DIM // HEAD_DIM

    def put_u(z):
        u_ref[...] = z.astype(BF16)

    def put_q(c, z):
        q = _norm_rope(z, gq_ref[...], mean_mat, cos, sin, lo_mask).astype(BF16)
        for j in range(heads):
            q_ref[0, c * heads + j] = q[:, j * HEAD_DIM:(j + 1) * HEAD_DIM]

    def put_k(z):
        k = _norm_rope(z, gk_ref[...], mean_mat, cos, sin, lo_mask).astype(BF16)
        for j in range(N_KV_HEADS):
            k_ref[0, j] = k[:, j * HEAD_DIM:(j + 1) * HEAD_DIM]

    def put_v(z):
        vt = z.T.astype(BF16)
        for j in range(N_KV_HEADS):
            v_ref[0, j] = vt[j * HEAD_DIM:(j + 1) * HEAD_DIM, :]

    def put_g(lo, hi, z):
        g_ref[:, lo:hi] = _sigmoid(z + bg_ref[:, lo:hi]).astype(BF16)

    stages = [((0, OFF_Q), put_u)]
    for c in range(ATTN_WIDTH // MXU_DIM):
        stages.append(((OFF_Q + c * MXU_DIM, OFF_Q + (c + 1) * MXU_DIM), functools.partial(put_q, c)))
    stages.append(((OFF_K, OFF_V), put_k))
    stages.append(((OFF_V, OFF_G), put_v))
    gw = 512
    for lo in range(0, w_ref.shape[1] - OFF_G, gw):
        stages.append(((OFF_G + lo, OFF_G + lo + gw), functools.partial(put_g, lo, lo + gw)))

    z_next = _dot(xb, w_ref[:, stages[0][0][0]:stages[0][0][1]])
    for s, (_, put) in enumerate(stages):
        z = z_next
        if s + 1 < len(stages):
            lo, hi = stages[s + 1][0]
            z_next = _dot(xb, w_ref[:, lo:hi])
        put(z)


def _inproj(x2, w_in, b_gate, q_g, k_g, batch, seq):
    n, d = x2.shape
    tm = _tile(seq, 512)
    spb = seq // tm
    in_width = w_in.shape[1]
    gate_w = in_width - OFF_G
    cos, sin = _rope_tables(seq)
    mean_mat = _head_mean_matrix()
    scale = HEAD_DIM ** -0.5 * math.log2(math.e)
    gq = jnp.tile(q_g.astype(F32) * scale, MXU_DIM // HEAD_DIM)[None, :]
    gk = jnp.tile(k_g.astype(F32), MXU_DIM // HEAD_DIM)[None, :]
    score_bound = (HEAD_DIM * jnp.max(jnp.abs(gq)) * jnp.max(jnp.abs(gk))).reshape(1)
    const = lambda i: (0, 0)
    outs = pl.pallas_call(
        _inproj_kernel,
        grid=(n // tm,),
        in_specs=[
            pl.BlockSpec((tm, d), lambda i: (i, 0)),
            pl.BlockSpec((d, in_width), const),
            pl.BlockSpec((1, gate_w), const),
            pl.BlockSpec((1, MXU_DIM), const),
            pl.BlockSpec((1, MXU_DIM), const),
            pl.BlockSpec((MXU_DIM, MXU_DIM), const),
            pl.BlockSpec((tm, MXU_DIM), lambda i: (i % spb, 0)),
            pl.BlockSpec((tm, MXU_DIM), lambda i: (i % spb, 0)),
        ],
        out_specs=[
            pl.BlockSpec((tm, FOURIER_WIDTH), lambda i: (i, 0)),
            pl.BlockSpec((1, N_Q_HEADS, tm, HEAD_DIM), lambda i: (i // spb, 0, i % spb, 0)),
            pl.BlockSpec((1, N_KV_HEADS, tm, HEAD_DIM), lambda i: (i // spb, 0, i % spb, 0)),
            pl.BlockSpec((1, N_KV_HEADS, HEAD_DIM, tm), lambda i: (i // spb, 0, 0, i % spb)),
            pl.BlockSpec((tm, gate_w), lambda i: (i, 0)),
        ],
        out_shape=[
            jax.ShapeDtypeStruct((n, FOURIER_WIDTH), BF16),
            jax.ShapeDtypeStruct((batch, N_Q_HEADS, seq, HEAD_DIM), BF16),
            jax.ShapeDtypeStruct((batch, N_KV_HEADS, seq, HEAD_DIM), BF16),
            jax.ShapeDtypeStruct((batch, N_KV_HEADS, HEAD_DIM, seq), BF16),
            jax.ShapeDtypeStruct((n, gate_w), BF16),
        ],
        compiler_params=_params("parallel"),
        name="inproj",
    )(x2, w_in.astype(BF16), b_gate.astype(F32)[None, :], gq, gk, mean_mat, cos, sin)
    return (*outs, score_bound)


def _fourier_kernel(u_ref, chan_ref, seqm_ref, wp_ref, g_ref, o_ref, ab_ref):
    seq = u_ref.shape[1]

    @pl.when(pl.program_id(1) == 0)
    def _():
        for h in range(FOURIER_WIDTH // MXU_DIM):
            cols = slice(h * MXU_DIM, (h + 1) * MXU_DIM)
            ab = _dot(u_ref[0, :, cols], chan_ref[h])
            ab_ref[0:seq, cols] = ab[:, 0:MXU_DIM].astype(BF16)
            ab_ref[seq:2 * seq, cols] = ab[:, MXU_DIM:].astype(BF16)

    f = _dot(seqm_ref[...], ab_ref[...]).astype(BF16)
    y = _dot(f, wp_ref[...])
    o_ref[0] = (g_ref[0].astype(F32) * y).astype(BF16)


def _fourier(u3, g3, w_four_proj):
    batch, seq, _ = u3.shape
    d = w_four_proj.shape[1]
    tr = _tile(seq, 512)
    chan, seqm = _dft_tables(seq)
    return pl.pallas_call(
        _fourier_kernel,
        grid=(batch, seq // tr),
        in_specs=[
            pl.BlockSpec((1, seq, FOURIER_WIDTH), lambda b, r: (b, 0, 0)),
            pl.BlockSpec(chan.shape, lambda b, r: (0, 0, 0)),
            pl.BlockSpec((tr, 2 * seq), lambda b, r: (r, 0)),
            pl.BlockSpec((FOURIER_WIDTH, d), lambda b, r: (0, 0)),
            pl.BlockSpec((1, tr, d), lambda b, r: (b, r, 0)),
        ],
        out_specs=pl.BlockSpec((1, tr, d), lambda b, r: (b, r, 0)),
        out_shape=jax.ShapeDtypeStruct((batch, seq, d), BF16),
        scratch_shapes=[pltpu.VMEM((2 * seq, FOURIER_WIDTH), BF16)],
        compiler_params=_params("parallel", "arbitrary"),
        name="fourier",
    )(u3, chan, seqm, w_four_proj.astype(BF16), g3)


def _attention_kernel(bounded, sb_ref, q_ref, k_ref, vt_ref, o_ref, vone_ref):
    seq = k_ref.shape[2]

    @pl.when(pl.program_id(2) == 0)
    def _():
        vone_ref[0:HEAD_DIM, :] = vt_ref[0, 0]
        vone_ref[HEAD_DIM:, :] = jnp.ones((HEAD_DIM, seq), BF16)

    k = k_ref[0, 0]
    vone = vone_ref[...]
    outs = []
    st_next = _dot_nt(k, q_ref[0, 0])
    for g in range(Q_GROUP):
        st = st_next
        if g + 1 < Q_GROUP:
            st_next = _dot_nt(k, q_ref[0, g + 1])
        if bounded:
            m = sb_ref[0]
        else:
            m = jnp.max(st, axis=0, keepdims=True)
        pt = jnp.exp2(st - m).astype(BF16)
        ol = _dot(vone, pt)
        ot = ol[0:HEAD_DIM, :] / ol[HEAD_DIM:HEAD_DIM + 1, :]
        outs.append(ot.T.astype(BF16))
    o_ref[0] = jnp.concatenate(outs, axis=1)


def _attention(q4, k4, vt4, score_bound):
    batch, _, seq, _ = q4.shape
    tq = _tile(seq, 1024)

    def call(bounded):
        return pl.pallas_call(
            functools.partial(_attention_kernel, bounded),
            grid_spec=pltpu.PrefetchScalarGridSpec(
                num_scalar_prefetch=1,
                grid=(batch, N_KV_HEADS, seq // tq),
                in_specs=[
                    pl.BlockSpec((1, Q_GROUP, tq, HEAD_DIM), lambda b, h, i, sb: (b, h, i, 0)),
                    pl.BlockSpec((1, 1, seq, HEAD_DIM), lambda b, h, i, sb: (b, h, 0, 0)),
                    pl.BlockSpec((1, 1, HEAD_DIM, seq), lambda b, h, i, sb: (b, h, 0, 0)),
                ],
                out_specs=pl.BlockSpec((1, tq, Q_GROUP * HEAD_DIM), lambda b, h, i, sb: (b, i, h)),
                scratch_shapes=[pltpu.VMEM((2 * HEAD_DIM, seq), BF16)],
            ),
            out_shape=jax.ShapeDtypeStruct((batch, seq, ATTN_WIDTH), BF16),
            compiler_params=_params("parallel", "parallel", "arbitrary"),
            name="attention_bounded" if bounded else "attention",
        )(score_bound, q4, k4, vt4)

    return lax.cond(2.0 * score_bound[0] <= MAX_EXP2_RANGE,
                    lambda: call(True), lambda: call(False))


def _layer_norm(h, g, b):
    mu = jnp.mean(h, axis=-1, keepdims=True)
    c = h - mu
    var = jnp.mean(c * c, axis=-1, keepdims=True)
    return c * lax.rsqrt(var + LN_EPS) * g + b


def _sorted_word_row(dest, s):
    return (dest // EXPERT_BLOCK * TOKEN_ROWS + s) * EXPERT_BLOCK + dest % EXPERT_BLOCK


def _pack_words(val):
    half = val.shape[1] // 2
    assert half == TOKEN_ROWS * LANES
    bits = lax.bitcast_convert_type(val.astype(BF16).astype(F32), jnp.uint32)
    words = (bits[:, :half] >> 16) | bits[:, half:]
    return [words[:, s * LANES:(s + 1) * LANES] for s in range(TOKEN_ROWS)]


def _unpack_words(words):
    lo = [lax.bitcast_convert_type(w << 16, F32) for w in words]
    hi = [lax.bitcast_convert_type(w & jnp.uint32(0xFFFF0000), F32) for w in words]
    return jnp.concatenate(lo + hi, axis=1)


def _mix_kernel(alpha, o_ref, mf_ref, g_ref, x_ref, wap_ref, wo_ref, lg_ref, lb_ref,
                x1_ref, x1w_ref):
    hm = o_ref.shape[0] // 2
    rows = [pl.ds(h * hm, hm) for h in range(2)]
    ys = [_dot(o_ref[r, :], wap_ref[...]) for r in rows]
    merged = [(mf_ref[r, :].astype(F32) + g_ref[r, :].astype(F32) * y).astype(BF16)
              for r, y in zip(rows, ys)]
    mixes = [_dot(m, wo_ref[...]) for m in merged]
    for r, mix in zip(rows, mixes):
        x1 = _layer_norm(alpha * x_ref[r, :] + mix, lg_ref[...], lb_ref[...])
        x1_ref[r, :] = x1
        for s, w in enumerate(_pack_words(x1)):
            x1w_ref[s, r, :] = w


def _mix(o2, mf2, g2, x2, w_attn_proj, w_o, ln_g, ln_b, alpha):
    n, d = x2.shape
    assert d == 2 * TOKEN_ROWS * LANES
    tm = _tile(n, 512)
    const = lambda i: (0, 0)
    return pl.pallas_call(
        functools.partial(_mix_kernel, alpha),
        grid=(n // tm,),
        in_specs=[
            pl.BlockSpec((tm, ATTN_WIDTH), lambda i: (i, 0)),
            pl.BlockSpec((tm, d), lambda i: (i, 0)),
            pl.BlockSpec((tm, d), lambda i: (i, 1)),
            pl.BlockSpec((tm, d), lambda i: (i, 0)),
            pl.BlockSpec((ATTN_WIDTH, d), const),
            pl.BlockSpec((d, d), const),
            pl.BlockSpec((1, d), const),
            pl.BlockSpec((1, d), const),
        ],
        out_specs=[
            pl.BlockSpec((tm, d), lambda i: (i, 0)),
            pl.BlockSpec((TOKEN_ROWS, tm, LANES), lambda i: (0, i, 0)),
        ],
        out_shape=[
            jax.ShapeDtypeStruct((n, d), F32),
            jax.ShapeDtypeStruct((TOKEN_ROWS, n, LANES), jnp.uint32),
        ],
        compiler_params=_params("parallel"),
        name="mix",
    )(o2, mf2, g2, x2, w_attn_proj.astype(BF16), w_o.astype(BF16),
      ln_g.astype(F32)[None, :], ln_b.astype(F32)[None, :])


def _route_kernel(x_ref, wh_ref, wl_ref, eb_ref, tri_ref,
                  eidx_ref, rank_ref, w_ref, cnt_ref, carry_ref):
    tm = x_ref.shape[0]

    @pl.when(pl.program_id(0) == 0)
    def _():
        carry_ref[...] = jnp.zeros_like(carry_ref)

    x = x_ref[...]
    xh = x.astype(BF16)
    xl = (x - xh.astype(F32)).astype(BF16)
    wh = wh_ref[...]
    logits = _dot_nt(wh, xh) + _dot_nt(wh, xl) + _dot_nt(wl_ref[...], xh)
    scores = _sigmoid(logits)
    biased = scores + eb_ref[:, 0:1]
    neg = -jnp.inf

    sub_iota = lax.broadcasted_iota(jnp.int32, (GROUP_SIZE, tm), 0).astype(F32)
    gs = []
    for g in range(N_EXPERT_GROUPS):
        blk = biased[g * GROUP_SIZE:(g + 1) * GROUP_SIZE, :]
        m1 = jnp.max(blk, axis=0, keepdims=True)
        a1 = jnp.min(jnp.where(blk == m1, sub_iota, float(GROUP_SIZE)), axis=0, keepdims=True)
        m2 = jnp.max(jnp.where(sub_iota == a1, neg, blk), axis=0, keepdims=True)
        gs.append(m1 + m2)

    masked = []
    for g in range(N_EXPERT_GROUPS):
        beat = jnp.zeros((1, tm), F32)
        for h in range(N_EXPERT_GROUPS):
            if h == g:
                continue
            wins = (gs[h] >= gs[g]) if h < g else (gs[h] > gs[g])
            beat = beat + jnp.where(wins, 1.0, 0.0)
        keep = beat < float(TOPK_GROUPS)
        blk = biased[g * GROUP_SIZE:(g + 1) * GROUP_SIZE, :]
        masked.append(jnp.where(keep, blk, neg))
    masked = jnp.concatenate(masked, axis=0)

    e_iota = lax.broadcasted_iota(jnp.int32, (N_EXPERTS, tm), 0).astype(F32)
    sel = jnp.zeros((N_EXPERTS, tm), F32)
    idxs, ws = [], []
    for _ in range(TOP_K):
        mx = jnp.max(masked, axis=0, keepdims=True)
        idx = jnp.min(jnp.where(masked == mx, e_iota, float(N_EXPERTS)), axis=0, keepdims=True)
        hit = e_iota == idx
        masked = jnp.where(hit, neg, masked)
        sel = jnp.where(hit, 1.0, sel)
        idxs.append(idx)
        ws.append(jnp.sum(jnp.where(hit, scores, 0.0), axis=0, keepdims=True))

    carry = carry_ref[...]
    selb = sel.astype(BF16)
    prefix = _dot(selb, tri_ref[...])
    rank_all = prefix + jnp.concatenate([carry] * (tm // LANES), axis=1)
    total = carry + _dot(selb, jnp.ones((tm, LANES), BF16))
    carry_ref[...] = total
    cnt_ref[...] = total

    wsum = ws[0]
    for j in range(1, TOP_K):
        wsum = wsum + ws[j]
    for j in range(TOP_K):
        eidx_ref[j:j + 1, :] = idxs[j].astype(jnp.int32)
        r = jnp.sum(jnp.where(e_iota == idxs[j], rank_all, 0.0), axis=0, keepdims=True)
        rank_ref[j:j + 1, :] = r.astype(jnp.int32)
        w_ref[j:j + 1, :] = ws[j] / wsum * ROUTED_SCALE


def _route(x1, w_router, e_bias):
    n, d = x1.shape
    tm = _tile(n, 512)
    wt = w_router.astype(F32).T
    wh = wt.astype(BF16)
    wl = (wt - wh.astype(F32)).astype(BF16)
    eb = jnp.broadcast_to(e_bias.astype(F32)[:, None], (N_EXPERTS, LANES))
    tri = jnp.asarray(np.triu(np.ones((tm, tm)), k=1), BF16)
    const = lambda i: (0, 0)
    return pl.pallas_call(
        _route_kernel,
        grid=(n // tm,),
        in_specs=[
            pl.BlockSpec((tm, d), lambda i: (i, 0)),
            pl.BlockSpec((N_EXPERTS, d), const),
            pl.BlockSpec((N_EXPERTS, d), const),
            pl.BlockSpec((N_EXPERTS, LANES), const),
            pl.BlockSpec((tm, tm), const),
        ],
        out_specs=[
            pl.BlockSpec((TOP_K, tm), lambda i: (0, i)),
            pl.BlockSpec((TOP_K, tm), lambda i: (0, i)),
            pl.BlockSpec((TOP_K, tm), lambda i: (0, i)),
            pl.BlockSpec((N_EXPERTS, LANES), const),
        ],
        out_shape=[
            jax.ShapeDtypeStruct((TOP_K, n), jnp.int32),
            jax.ShapeDtypeStruct((TOP_K, n), jnp.int32),
            jax.ShapeDtypeStruct((TOP_K, n), F32),
            jax.ShapeDtypeStruct((N_EXPERTS, LANES), F32),
        ],
        scratch_shapes=[pltpu.VMEM((N_EXPERTS, LANES), F32)],
        compiler_params=_params("arbitrary"),
        name="route",
    )(x1, wh, wl, eb, tri)


def _dest_kernel(tail_ref, first_ref, eidx_ref, rank_ref, ps_ref, dest_ref, xs_ref, zero_ref, sem):
    i = pl.program_id(0)

    def piece_copy(e, q, s):
        row0 = (tail_ref[e] * TOKEN_ROWS + s) * EXPERT_BLOCK + q * ZERO_CHUNK
        row0 = pl.multiple_of(row0, ZERO_CHUNK)
        return pltpu.make_async_copy(zero_ref, xs_ref.at[pl.ds(row0, ZERO_CHUNK), :], sem)

    def each_piece(act):
        def body(e, c):
            for q in range(EXPERT_BLOCK // ZERO_CHUNK):
                @pl.when(jnp.logical_and(tail_ref[e] >= 0, q >= first_ref[e]))
                def _():
                    for s in range(TOKEN_ROWS):
                        act(piece_copy(e, q, s))
            return c
        lax.fori_loop(0, N_EXPERTS, body, 0)

    @pl.when(i == 0)
    def _():
        zero_ref[...] = jnp.zeros_like(zero_ref)
        each_piece(lambda cp: cp.start())

    tm = eidx_ref.shape[1]
    e_iota = lax.broadcasted_iota(jnp.int32, (N_EXPERTS, tm), 0)
    ps = jnp.concatenate([ps_ref[...]] * (tm // LANES), axis=1)
    for j in range(TOP_K):
        hit = e_iota == eidx_ref[j:j + 1, :]
        start = jnp.sum(jnp.where(hit, ps, 0.0), axis=0, keepdims=True)
        dest_ref[j:j + 1, :] = start.astype(jnp.int32) + rank_ref[j:j + 1, :]

    @pl.when(i == pl.num_programs(0) - 1)
    def _():
        each_piece(lambda cp: cp.wait())


def _dest(eidx, rank, pad_start, tail, n_rows):
    n = eidx.shape[1]
    tm = _tile(n, 512)
    ps = jnp.broadcast_to(pad_start.astype(F32)[:, None], (N_EXPERTS, LANES))
    tok = lambda i, *_: (0, i)
    return pl.pallas_call(
        _dest_kernel,
        grid_spec=pltpu.PrefetchScalarGridSpec(
            num_scalar_prefetch=len(tail),
            grid=(n // tm,),
            in_specs=[
                pl.BlockSpec((TOP_K, tm), tok),
                pl.BlockSpec((TOP_K, tm), tok),
                pl.BlockSpec((N_EXPERTS, LANES), lambda i, *_: (0, 0)),
            ],
            out_specs=[pl.BlockSpec((TOP_K, tm), tok), pl.BlockSpec(memory_space=pl.ANY)],
            scratch_shapes=[pltpu.VMEM((ZERO_CHUNK, LANES), jnp.uint32), pltpu.SemaphoreType.DMA],
        ),
        out_shape=[jax.ShapeDtypeStruct((TOP_K, n), jnp.int32),
                   jax.ShapeDtypeStruct((n_rows * TOKEN_ROWS, LANES), jnp.uint32)],
        compiler_params=_params("arbitrary"),
        name="dest",
    )(*tail, eidx, rank, ps)


def _sc_scatter_rows(xs_init, src, idx):
    nr = src.shape[0]
    slots = idx.shape[0]
    mesh = plsc.VectorSubcoreMesh(core_axis_name="core", subcore_axis_name="subcore")

    @functools.partial(pl.kernel, mesh=mesh, scratch_types=[], out_type=())
    def scatter(src_hbm, idx_hbm, xs_hbm):
        def window(src_vmem, *idx_vmems):
            for idx_vmem in idx_vmems:
                pltpu.sync_copy(src_vmem, xs_hbm.at[idx_vmem.at[0]])

        pltpu.emit_pipeline(
            window,
            grid=(nr // SC_WINDOW,),
            in_specs=[pl.BlockSpec((SC_WINDOW, LANES), index_map=lambda i: (i, 0))]
                     + [pl.BlockSpec((1, SC_WINDOW), index_map=lambda i, j=j: (j, i))
                        for j in range(slots)],
            out_specs=[],
            core_axis_name=("core", "subcore"),
            dimension_semantics=(pltpu.PARALLEL,),
        )(src_hbm, *([idx_hbm] * slots))

    xs_ref = jax.new_ref(xs_init)
    scatter(src, idx, xs_ref)
    return xs_ref[...]


def _dispatch(xs, dest, x1w):
    n = dest.shape[1]
    word_rows = _sorted_word_row(dest[:, None, :], jnp.arange(TOKEN_ROWS, dtype=jnp.int32)[None, :, None])
    word_rows = word_rows.reshape(TOP_K, TOKEN_ROWS * n)
    return _sc_scatter_rows(xs, x1w.reshape(TOKEN_ROWS * n, LANES), word_rows)


def _swiglu(xb, w_in, w_down):
    h = _dot(xb, w_in)
    half = h.shape[1] // 2
    g = h[:, :half]
    act = g * _sigmoid(g) * h[:, half:]
    return _dot(act.astype(BF16), w_down)


def _experts_kernel(be_ref, nu_ref, fresh_ref, slot_ref, nxt_ref, x_ref, wi_hbm, wd_hbm, o_ref,
                    wi_buf, wd_buf, wib_ref, wdb_ref, sem):
    i = pl.program_id(0)
    used = i < nu_ref[0]
    blk = o_ref.shape[0] // TOKEN_ROWS

    def fetch(e, s):
        return (pltpu.make_async_copy(wi_hbm.at[e], wi_buf.at[s], sem.at[0, s]),
                pltpu.make_async_copy(wd_hbm.at[e], wd_buf.at[s], sem.at[1, s]))

    @pl.when(jnp.logical_and(used, fresh_ref[i] == 1))
    def _():
        s = slot_ref[i]

        @pl.when(i == 0)
        def _():
            for cp in fetch(be_ref[i], s):
                cp.start()

        for cp in fetch(be_ref[i], s):
            cp.wait()

        @pl.when(nxt_ref[i] >= 0)
        def _():
            for cp in fetch(nxt_ref[i], 1 - s):
                cp.start(priority=1)

        wib_ref[...] = wi_buf[s].astype(BF16)
        wdb_ref[...] = wd_buf[s].astype(BF16)

    @pl.when(used)
    def _():
        hb = blk // 2
        wi, wd = wib_ref[...], wdb_ref[...]
        plane = lambda s, h: pl.ds(s * blk + h * hb, hb)
        xbs = [_unpack_words([x_ref[plane(s, h), :] for s in range(TOKEN_ROWS)]).astype(BF16)
               for h in range(2)]
        hs = [_dot(xb, wi) for xb in xbs]
        half = wi.shape[1] // 2
        acts = [(h[:, :half] * _sigmoid(h[:, :half]) * h[:, half:]).astype(BF16) for h in hs]
        for h, act in enumerate(acts):
            for s, w in enumerate(_pack_words(_dot(act, wd))):
                o_ref[plane(s, h), :] = w


def _experts(sched, xs, w_e_in, w_e_down):
    n_blocks = sched[0].shape[0]
    _, d, h2 = w_e_in.shape
    hdim = w_e_down.shape[1]
    rows = EXPERT_BLOCK * TOKEN_ROWS

    def x_map(i, be, nu, *_):
        return (jnp.minimum(i, nu[0] - 1), 0)

    return pl.pallas_call(
        _experts_kernel,
        grid_spec=pltpu.PrefetchScalarGridSpec(
            num_scalar_prefetch=len(sched),
            grid=(n_blocks,),
            in_specs=[
                pl.BlockSpec((rows, LANES), x_map),
                pl.BlockSpec(memory_space=pl.ANY),
                pl.BlockSpec(memory_space=pl.ANY),
            ],
            out_specs=pl.BlockSpec((rows, LANES), x_map),
            scratch_shapes=[
                pltpu.VMEM((2, d, h2), F32), pltpu.VMEM((2, hdim, d), F32),
                pltpu.VMEM((d, h2), BF16), pltpu.VMEM((hdim, d), BF16),
                pltpu.SemaphoreType.DMA((2, 2)),
            ],
        ),
        out_shape=jax.ShapeDtypeStruct(xs.shape, jnp.uint32),
        input_output_aliases={len(sched): 0},
        compiler_params=_params("arbitrary"),
        name="experts",
    )(*sched, xs, w_e_in, w_e_down)


def _sc_gather_rows(table, idx):
    ni = idx.shape[0]
    mesh = plsc.VectorSubcoreMesh(core_axis_name="core", subcore_axis_name="subcore")

    @functools.partial(pl.kernel, mesh=mesh, scratch_types=[],
                       out_type=jax.ShapeDtypeStruct((ni, LANES), table.dtype))
    def gather(table_hbm, idx_hbm, out_hbm):
        def window(idx_vmem, out_vmem):
            pltpu.sync_copy(table_hbm.at[idx_vmem.at[0]], out_vmem)

        pltpu.emit_pipeline(
            window,
            grid=(ni // SC_WINDOW,),
            in_specs=[pl.BlockSpec((1, SC_WINDOW), index_map=lambda i: (0, i))],
            out_specs=[pl.BlockSpec((SC_WINDOW, LANES), index_map=lambda i: (i, 0))],
            core_axis_name=("core", "subcore"),
            dimension_semantics=(pltpu.PARALLEL,),
        )(idx_hbm, out_hbm)

    return gather(table, idx.reshape(1, ni))


def _shared_kernel(alpha, x1_ref, wsi_ref, wsd_ref, part_ref):
    x1 = x1_ref[...]
    part_ref[...] = alpha * x1 + _swiglu(x1.astype(BF16), wsi_ref[...], wsd_ref[...])


def _shared(x1, w_sh_in, w_sh_down, alpha):
    n, d = x1.shape
    tm = _tile(n, 512)
    const = lambda i: (0, 0)
    return pl.pallas_call(
        functools.partial(_shared_kernel, alpha),
        grid=(n // tm,),
        in_specs=[
            pl.BlockSpec((tm, d), lambda i: (i, 0)),
            pl.BlockSpec(w_sh_in.shape, const),
            pl.BlockSpec(w_sh_down.shape, const),
        ],
        out_specs=pl.BlockSpec((tm, d), lambda i: (i, 0)),
        out_shape=jax.ShapeDtypeStruct((n, d), F32),
        compiler_params=_params("parallel"),
        name="shared",
    )(x1, w_sh_in.astype(BF16), w_sh_down.astype(BF16))


def _finish_kernel(acc_ref, wt_ref, st_ref, lg_ref, lb_ref, out_ref):
    acc = acc_ref[...]
    for j in range(TOP_K):
        words = [st_ref[s, j] for s in range(TOKEN_ROWS)]
        acc = acc + wt_ref[:, j:j + 1] * _unpack_words(words)
    out_ref[...] = _layer_norm(acc, lg_ref[...], lb_ref[...])


def _combine(dest, part, wt, ln_g, ln_b, os):
    n, d = part.shape
    nc = n // COMBINE_CHUNKS
    tf = _tile(nc, 512)
    steps = nc // tf
    const = lambda i: (0, 0)
    out = part
    for c in range(COMBINE_CHUNKS):
        word_rows = _sorted_word_row(dest[None, :, c * nc:(c + 1) * nc],
                                     jnp.arange(TOKEN_ROWS, dtype=jnp.int32)[:, None, None]).reshape(-1)
        staged = _sc_gather_rows(os, word_rows).reshape(TOKEN_ROWS, TOP_K, nc, LANES)
        tile = lambda i, c=c: (c * steps + i, 0)
        out = pl.pallas_call(
            _finish_kernel,
            grid=(steps,),
            in_specs=[
                pl.BlockSpec((tf, d), tile),
                pl.BlockSpec((tf, TOP_K), tile),
                pl.BlockSpec((TOKEN_ROWS, TOP_K, tf, LANES), lambda i: (0, 0, i, 0)),
                pl.BlockSpec((1, d), const),
                pl.BlockSpec((1, d), const),
            ],
            out_specs=pl.BlockSpec((tf, d), tile),
            out_shape=jax.ShapeDtypeStruct((n, d), F32),
            input_output_aliases={0: 0},
            compiler_params=_params("arbitrary"),
            name="finish",
        )(out, wt, staged, ln_g.astype(F32)[None, :], ln_b.astype(F32)[None, :])
    return out


def _block_layout(counts, n_assign):
    n_blocks = (n_assign + N_EXPERTS * (EXPERT_BLOCK - 1) + EXPERT_BLOCK - 1) // EXPERT_BLOCK
    nblk = (counts + EXPERT_BLOCK - 1) // EXPERT_BLOCK
    blk_end = jnp.cumsum(nblk)
    pad_start = (blk_end - nblk) * EXPERT_BLOCK
    blk_e = jnp.sum(blk_end[None, :] <= jnp.arange(n_blocks, dtype=jnp.int32)[:, None], axis=1)
    blk_e = jnp.minimum(blk_e, N_EXPERTS - 1).astype(jnp.int32)
    n_used = blk_end[-1:].astype(jnp.int32)
    tail_blk = jnp.where(nblk > 0, blk_end - 1, -1).astype(jnp.int32)
    tail_rows = counts - (nblk - 1) * EXPERT_BLOCK
    tail_first = (tail_rows // ZERO_CHUNK).astype(jnp.int32)
    tail = (tail_blk, tail_first)
    has = nblk > 0
    e_ids = jnp.arange(N_EXPERTS, dtype=jnp.int32)
    nxt_ge = lax.cummin(jnp.where(has, e_ids, N_EXPERTS)[::-1])[::-1]
    nxt_e = jnp.concatenate([nxt_ge[1:], jnp.full((1,), N_EXPERTS, jnp.int32)])
    nxt_e = jnp.where(nxt_e < N_EXPERTS, nxt_e, -1)
    slot_e = (jnp.cumsum(has.astype(jnp.int32)) - 1) % 2
    blk_ids = jnp.arange(n_blocks, dtype=jnp.int32)
    fresh = (blk_ids == (blk_end - nblk)[blk_e]).astype(jnp.int32)
    sched = (blk_e, n_used, fresh, slot_e[blk_e].astype(jnp.int32), nxt_e[blk_e].astype(jnp.int32))
    return n_blocks, pad_start.astype(jnp.int32), sched, tail


def _layer(x, w_in, b_gate, q_g, k_g, w_four_proj, w_attn_proj, w_o, ln1_g, ln1_b,
           w_router, e_bias, w_e_in, w_e_down, w_sh_in, w_sh_down, ln2_g, ln2_b, alpha):
    batch, seq, d = x.shape
    n = batch * seq
    x2 = x.reshape(n, d)

    u, q4, k4, vt4, gates, score_bound = _inproj(x2, w_in, b_gate, q_g, k_g, batch, seq)
    mf = _fourier(u.reshape(batch, seq, FOURIER_WIDTH), gates.reshape(batch, seq, -1), w_four_proj)
    o = _attention(q4, k4, vt4, score_bound)
    x1, x1w = _mix(o.reshape(n, ATTN_WIDTH), mf.reshape(n, d), gates, x2,
                   w_attn_proj, w_o, ln1_g, ln1_b, alpha)

    eidx, rank, wts, cnt = _route(x1, w_router, e_bias)
    counts = cnt[:, 0].astype(jnp.int32)
    n_blocks, pad_start, sched, tail = _block_layout(counts, n * TOP_K)
    dest, xs = _dest(eidx, rank, pad_start, tail, n_blocks * EXPERT_BLOCK)
    xs = _dispatch(xs, dest, x1w)
    part = _shared(x1, w_sh_in, w_sh_down, alpha)
    xs, part = lax.optimization_barrier((xs, part))
    os = _experts(sched, xs, w_e_in, w_e_down)
    out = _combine(dest, part, wts.T, ln2_g, ln2_b, os)
    return out.reshape(batch, seq, d)


def kernel(x, w_in, b_gate, q_norm_g, k_norm_g, w_four_proj, w_attn_proj, w_o, ln1_g, ln1_b, w_router, e_bias, w_e_in, w_e_down, w_sh_in, w_sh_down, ln2_g, ln2_b):
    depth = w_in.shape[0]
    alpha = (2 * depth) ** 0.25
    for l in range(depth):
        x = _layer(x, w_in[l], b_gate[l], q_norm_g[l], k_norm_g[l], w_four_proj[l],
                   w_attn_proj[l], w_o[l], ln1_g[l], ln1_b[l], w_router[l], e_bias[l],
                   w_e_in[l], w_e_down[l], w_sh_in[l], w_sh_down[l], ln2_g[l], ln2_b[l], alpha)
    return x
```

```python
import functools
import math

import numpy as np
import jax
import jax.numpy as jnp
from jax import lax
from jax.experimental import pallas as pl
from jax.experimental.pallas import tpu as pltpu
from jax.experimental.pallas import tpu_sc as plsc

F32 = jnp.float32
BF16 = jnp.bfloat16

GRID_W = 64
N_FOURIER_GROUPS = 8
FOURIER_GROUP_DIM = 64
FOURIER_WIDTH = N_FOURIER_GROUPS * FOURIER_GROUP_DIM
N_Q_HEADS = 16
N_KV_HEADS = 4
HEAD_DIM = 64
Q_GROUP = N_Q_HEADS // N_KV_HEADS
ATTN_WIDTH = N_Q_HEADS * HEAD_DIM
KV_WIDTH = N_KV_HEADS * HEAD_DIM
ROPE_THETA = 10000.0
QK_EPS = 1e-6
OFF_Q = FOURIER_WIDTH
OFF_K = OFF_Q + ATTN_WIDTH
OFF_V = OFF_K + KV_WIDTH
OFF_G = OFF_V + KV_WIDTH
N_EXPERTS = 256
TOP_K = 8
N_EXPERT_GROUPS = 8
GROUP_SIZE = N_EXPERTS // N_EXPERT_GROUPS
TOPK_GROUPS = 4
ROUTED_SCALE = 2.5
LN_EPS = 1e-5

LANES = 128
MXU_DIM = 256
VMEM_LIMIT = 56 * 1024 * 1024

MAX_EXP2_RANGE = 100.0

ZERO_CHUNK = 128
EXPERT_BLOCK = 1152
SC_WINDOW = 128
COMBINE_CHUNKS = 4
TOKEN_ROWS = 4

NT_DIMS = (((1,), (1,)), ((), ()))


def _dot(a, b):
    return jnp.dot(a, b, preferred_element_type=F32)


def _dot_nt(a, b):
    return lax.dot_general(a, b, NT_DIMS, preferred_element_type=F32)


def _sigmoid(x):
    return 1.0 / (1.0 + jnp.exp(-x))


def _params(*sem):
    return pltpu.CompilerParams(dimension_semantics=sem, vmem_limit_bytes=VMEM_LIMIT)


def _tile(n, pref):
    t = min(n, pref)
    assert n % t == 0, (n, t)
    return t


def _rope_tables(seq):
    lane = np.arange(MXU_DIM)
    d = lane % HEAD_DIM
    sub = d % 32
    j = sub % 16
    t = np.arange(seq)[:, None]
    pos = np.where(d[None, :] < 32, t // GRID_W, t % GRID_W).astype(np.float64)
    freq = ROPE_THETA ** (-(j.astype(np.float64)) / 16.0)
    ang = pos * freq[None, :]
    cos = np.cos(ang)
    sin = np.sin(ang) * np.where(sub < 16, -1.0, 1.0)[None, :]
    return jnp.asarray(cos, F32), jnp.asarray(sin, F32)


def _head_mean_matrix():
    i = np.arange(MXU_DIM)
    m = (i[:, None] // HEAD_DIM == i[None, :] // HEAD_DIM).astype(np.float64) / HEAD_DIM
    return jnp.asarray(m, BF16)


def _dft_tables(seq):
    c = np.arange(FOURIER_GROUP_DIM)
    ang_c = 2.0 * np.pi * ((c[:, None] * c[None, :]) % FOURIER_GROUP_DIM) / FOURIER_GROUP_DIM
    sc = 1.0 / math.sqrt(FOURIER_GROUP_DIM)
    eye = np.eye(MXU_DIM // FOURIER_GROUP_DIM)
    block = np.concatenate([np.kron(eye, np.cos(ang_c) * sc), np.kron(eye, np.sin(ang_c) * sc)], axis=1)
    chan = np.stack([block] * (FOURIER_WIDTH // MXU_DIM))
    s = np.arange(seq)
    ang_s = 2.0 * np.pi * ((s[:, None] * s[None, :]) % seq) / seq
    ssc = 1.0 / math.sqrt(seq)
    seqm = np.concatenate([np.cos(ang_s) * ssc, -np.sin(ang_s) * ssc], axis=1)
    return jnp.asarray(chan, BF16), jnp.asarray(seqm, BF16)


def _norm_rope(z, gain, mean_mat, cos, sin, lo_mask):
    ms = _dot((z * z).astype(BF16), mean_mat)
    y = z * lax.rsqrt(ms + QK_EPS) * gain
    outs = []
    for c in range(MXU_DIM // LANES):
        yc = y[:, c * LANES:(c + 1) * LANES]
        up = pltpu.roll(yc, LANES - 16, 1)
        dn = pltpu.roll(yc, 16, 1)
        partner = jnp.where(lo_mask, up, dn)
        sl = slice(c * LANES, (c + 1) * LANES)
        outs.append(yc * cos[:, sl] + partner * sin[:, sl])
    return jnp.concatenate(outs, axis=1)


def _inproj_kernel(x_ref, w_ref, bg_ref, gq_ref, gk_ref, mm_ref, cos_ref, sin_ref,
                   u_ref, q_ref, k_ref, v_ref, g_ref):
    xb = x_ref[...].astype(BF16)

    lane = lax.broadcasted_iota(jnp.int32, (1, LANES), 1)
    lo_mask = (lane & 16) == 0
    mean_mat = mm_ref[...]
    cos = cos_ref[...]
    sin = sin_ref[...]
    heads = MXU_DIM // HEAD_DIM

    def put_u(z):
        u_ref[...] = z.astype(BF16)

    def put_q(c, z):
        q = _norm_rope(z, gq_ref[...], mean_mat, cos, sin, lo_mask).astype(BF16)
        for j in range(heads):
            q_ref[0, c * heads + j] = q[:, j * HEAD_DIM:(j + 1) * HEAD_DIM]

    def put_k(z):
        k = _norm_rope(z, gk_ref[...], mean_mat, cos, sin, lo_mask).astype(BF16)
        for j in range(N_KV_HEADS):
            k_ref[0, j] = k[:, j * HEAD_DIM:(j + 1) * HEAD_DIM]

    def put_v(z):
        vt = z.T.astype(BF16)
        for j in range(N_KV_HEADS):
            v_ref[0, j] = vt[j * HEAD_DIM:(j + 1) * HEAD_DIM, :]

    def put_g(lo, hi, z):
        g_ref[:, lo:hi] = _sigmoid(z + bg_ref[:, lo:hi]).astype(BF16)

    stages = [((0, OFF_Q), put_u)]
    for c in range(ATTN_WIDTH // MXU_DIM):
        stages.append(((OFF_Q + c * MXU_DIM, OFF_Q + (c + 1) * MXU_DIM), functools.partial(put_q, c)))
    stages.append(((OFF_K, OFF_V), put_k))
    stages.append(((OFF_V, OFF_G), put_v))
    gw = 512
    for lo in range(0, w_ref.shape[1] - OFF_G, gw):
        stages.append(((OFF_G + lo, OFF_G + lo + gw), functools.partial(put_g, lo, lo + gw)))

    z_next = _dot(xb, w_ref[:, stages[0][0][0]:stages[0][0][1]])
    for s, (_, put) in enumerate(stages):
        z = z_next
        if s + 1 < len(stages):
            lo, hi = stages[s + 1][0]
            z_next = _dot(xb, w_ref[:, lo:hi])
        put(z)


def _inproj(x2, w_in, b_gate, q_g, k_g, batch, seq):
    n, d = x2.shape
    tm = _tile(seq, 512)
    spb = seq // tm
    in_width = w_in.shape[1]
    gate_w = in_width - OFF_G
    cos, sin = _rope_tables(seq)
    mean_mat = _head_mean_matrix()
    scale = HEAD_DIM ** -0.5 * math.log2(math.e)
    gq = jnp.tile(q_g.astype(F32) * scale, MXU_DIM // HEAD_DIM)[None, :]
    gk = jnp.tile(k_g.astype(F32), MXU_DIM // HEAD_DIM)[None, :]
    score_bound = (HEAD_DIM * jnp.max(jnp.abs(gq)) * jnp.max(jnp.abs(gk))).reshape(1)
    const = lambda i: (0, 0)
    outs = pl.pallas_call(
        _inproj_kernel,
        grid=(n // tm,),
        in_specs=[
            pl.BlockSpec((tm, d), lambda i: (i, 0)),
            pl.BlockSpec((d, in_width), const),
            pl.BlockSpec((1, gate_w), const),
            pl.BlockSpec((1, MXU_DIM), const),
            pl.BlockSpec((1, MXU_DIM), const),
            pl.BlockSpec((MXU_DIM, MXU_DIM), const),
            pl.BlockSpec((tm, MXU_DIM), lambda i: (i % spb, 0)),
            pl.BlockSpec((tm, MXU_DIM), lambda i: (i % spb, 0)),
        ],
        out_specs=[
            pl.BlockSpec((tm, FOURIER_WIDTH), lambda i: (i, 0)),
            pl.BlockSpec((1, N_Q_HEADS, tm, HEAD_DIM), lambda i: (i // spb, 0, i % spb, 0)),
            pl.BlockSpec((1, N_KV_HEADS, tm, HEAD_DIM), lambda i: (i // spb, 0, i % spb, 0)),
            pl.BlockSpec((1, N_KV_HEADS, HEAD_DIM, tm), lambda i: (i // spb, 0, 0, i % spb)),
            pl.BlockSpec((tm, gate_w), lambda i: (i, 0)),
        ],
        out_shape=[
            jax.ShapeDtypeStruct((n, FOURIER_WIDTH), BF16),
            jax.ShapeDtypeStruct((batch, N_Q_HEADS, seq, HEAD_DIM), BF16),
            jax.ShapeDtypeStruct((batch, N_KV_HEADS, seq, HEAD_DIM), BF16),
            jax.ShapeDtypeStruct((batch, N_KV_HEADS, HEAD_DIM, seq), BF16),
            jax.ShapeDtypeStruct((n, gate_w), BF16),
        ],
        compiler_params=_params("parallel"),
        name="inproj",
    )(x2, w_in.astype(BF16), b_gate.astype(F32)[None, :], gq, gk, mean_mat, cos, sin)
    return (*outs, score_bound)


def _fourier_kernel(u_ref, chan_ref, seqm_ref, wp_ref, g_ref, o_ref, ab_ref):
    seq = u_ref.shape[1]

    @pl.when(pl.program_id(1) == 0)
    def _():
        for h in range(FOURIER_WIDTH // MXU_DIM):
            cols = slice(h * MXU_DIM, (h + 1) * MXU_DIM)
            ab = _dot(u_ref[0, :, cols], chan_ref[h])
            ab_ref[0:seq, cols] = ab[:, 0:MXU_DIM].astype(BF16)
            ab_ref[seq:2 * seq, cols] = ab[:, MXU_DIM:].astype(BF16)

    f = _dot(seqm_ref[...], ab_ref[...]).astype(BF16)
    y = _dot(f, wp_ref[...])
    o_ref[0] = (g_ref[0].astype(F32) * y).astype(BF16)


def _fourier(u3, g3, w_four_proj):
    batch, seq, _ = u3.shape
    d = w_four_proj.shape[1]
    tr = _tile(seq, 512)
    chan, seqm = _dft_tables(seq)
    return pl.pallas_call(
        _fourier_kernel,
        grid=(batch, seq // tr),
        in_specs=[
            pl.BlockSpec((1, seq, FOURIER_WIDTH), lambda b, r: (b, 0, 0)),
            pl.BlockSpec(chan.shape, lambda b, r: (0, 0, 0)),
            pl.BlockSpec((tr, 2 * seq), lambda b, r: (r, 0)),
            pl.BlockSpec((FOURIER_WIDTH, d), lambda b, r: (0, 0)),
            pl.BlockSpec((1, tr, d), lambda b, r: (b, r, 0)),
        ],
        out_specs=pl.BlockSpec((1, tr, d), lambda b, r: (b, r, 0)),
        out_shape=jax.ShapeDtypeStruct((batch, seq, d), BF16),
        scratch_shapes=[pltpu.VMEM((2 * seq, FOURIER_WIDTH), BF16)],
        compiler_params=_params("parallel", "arbitrary"),
        name="fourier",
    )(u3, chan, seqm, w_four_proj.astype(BF16), g3)


def _attention_kernel(bounded, sb_ref, q_ref, k_ref, vt_ref, o_ref, vone_ref):
    seq = k_ref.shape[2]

    @pl.when(pl.program_id(2) == 0)
    def _():
        vone_ref[0:HEAD_DIM, :] = vt_ref[0, 0]
        vone_ref[HEAD_DIM:, :] = jnp.ones((HEAD_DIM, seq), BF16)

    k = k_ref[0, 0]
    vone = vone_ref[...]
    outs = []
    st_next = _dot_nt(k, q_ref[0, 0])
    for g in range(Q_GROUP):
        st = st_next
        if g + 1 < Q_GROUP:
            st_next = _dot_nt(k, q_ref[0, g + 1])
        if bounded:
            m = sb_ref[0]
        else:
            m = jnp.max(st, axis=0, keepdims=True)
        pt = jnp.exp2(st - m).astype(BF16)
        ol = _dot(vone, pt)
        ot = ol[0:HEAD_DIM, :] / ol[HEAD_DIM:HEAD_DIM + 1, :]
        outs.append(ot.T.astype(BF16))
    o_ref[0] = jnp.concatenate(outs, axis=1)


def _attention(q4, k4, vt4, score_bound):
    batch, _, seq, _ = q4.shape
    tq = _tile(seq, 1024)

    def call(bounded):
        return pl.pallas_call(
            functools.partial(_attention_kernel, bounded),
            grid_spec=pltpu.PrefetchScalarGridSpec(
                num_scalar_prefetch=1,
                grid=(batch, N_KV_HEADS, seq // tq),
                in_specs=[
                    pl.BlockSpec((1, Q_GROUP, tq, HEAD_DIM), lambda b, h, i, sb: (b, h, i, 0)),
                    pl.BlockSpec((1, 1, seq, HEAD_DIM), lambda b, h, i, sb: (b, h, 0, 0)),
                    pl.BlockSpec((1, 1, HEAD_DIM, seq), lambda b, h, i, sb: (b, h, 0, 0)),
                ],
                out_specs=pl.BlockSpec((1, tq, Q_GROUP * HEAD_DIM), lambda b, h, i, sb: (b, i, h)),
                scratch_shapes=[pltpu.VMEM((2 * HEAD_DIM, seq), BF16)],
            ),
            out_shape=jax.ShapeDtypeStruct((batch, seq, ATTN_WIDTH), BF16),
            compiler_params=_params("parallel", "parallel", "arbitrary"),
            name="attention_bounded" if bounded else "attention",
        )(score_bound, q4, k4, vt4)

    return lax.cond(2.0 * score_bound[0] <= MAX_EXP2_RANGE,
                    lambda: call(True), lambda: call(False))


def _layer_norm(h, g, b):
    mu = jnp.mean(h, axis=-1, keepdims=True)
    c = h - mu
    var = jnp.mean(c * c, axis=-1, keepdims=True)
    return c * lax.rsqrt(var + LN_EPS) * g + b


def _sorted_word_row(dest, s):
    return (dest // EXPERT_BLOCK * TOKEN_ROWS + s) * EXPERT_BLOCK + dest % EXPERT_BLOCK


def _pack_words(val):
    half = val.shape[1] // 2
    assert half == TOKEN_ROWS * LANES
    bits = lax.bitcast_convert_type(val.astype(BF16).astype(F32), jnp.uint32)
    words = (bits[:, :half] >> 16) | bits[:, half:]
    return [words[:, s * LANES:(s + 1) * LANES] for s in range(TOKEN_ROWS)]


def _unpack_words(words):
    lo = [lax.bitcast_convert_type(w << 16, F32) for w in words]
    hi = [lax.bitcast_convert_type(w & jnp.uint32(0xFFFF0000), F32) for w in words]
    return jnp.concatenate(lo + hi, axis=1)


def _mix_kernel(alpha, o_ref, mf_ref, g_ref, x_ref, wap_ref, wo_ref, lg_ref, lb_ref,
                x1_ref, x1w_ref):
    hm = o_ref.shape[0] // 2
    rows = [pl.ds(h * hm, hm) for h in range(2)]
    ys = [_dot(o_ref[r, :], wap_ref[...]) for r in rows]
    merged = [(mf_ref[r, :].astype(F32) + g_ref[r, :].astype(F32) * y).astype(BF16)
              for r, y in zip(rows, ys)]
    mixes = [_dot(m, wo_ref[...]) for m in merged]
    for r, mix in zip(rows, mixes):
        x1 = _layer_norm(alpha * x_ref[r, :] + mix, lg_ref[...], lb_ref[...])
        x1_ref[r, :] = x1
        for s, w in enumerate(_pack_words(x1)):
            x1w_ref[s, r, :] = w


def _mix(o2, mf2, g2, x2, w_attn_proj, w_o, ln_g, ln_b, alpha):
    n, d = x2.shape
    assert d == 2 * TOKEN_ROWS * LANES
    tm = _tile(n, 512)
    const = lambda i: (0, 0)
    return pl.pallas_call(
        functools.partial(_mix_kernel, alpha),
        grid=(n // tm,),
        in_specs=[
            pl.BlockSpec((tm, ATTN_WIDTH), lambda i: (i, 0)),
            pl.BlockSpec((tm, d), lambda i: (i, 0)),
            pl.BlockSpec((tm, d), lambda i: (i, 1)),
            pl.BlockSpec((tm, d), lambda i: (i, 0)),
            pl.BlockSpec((ATTN_WIDTH, d), const),
            pl.BlockSpec((d, d), const),
            pl.BlockSpec((1, d), const),
            pl.BlockSpec((1, d), const),
        ],
        out_specs=[
            pl.BlockSpec((tm, d), lambda i: (i, 0)),
            pl.BlockSpec((TOKEN_ROWS, tm, LANES), lambda i: (0, i, 0)),
        ],
        out_shape=[
            jax.ShapeDtypeStruct((n, d), F32),
            jax.ShapeDtypeStruct((TOKEN_ROWS, n, LANES), jnp.uint32),
        ],
        compiler_params=_params("parallel"),
        name="mix",
    )(o2, mf2, g2, x2, w_attn_proj.astype(BF16), w_o.astype(BF16),
      ln_g.astype(F32)[None, :], ln_b.astype(F32)[None, :])


def _route_kernel(x_ref, wh_ref, wl_ref, eb_ref, tri_ref,
                  eidx_ref, rank_ref, w_ref, cnt_ref, carry_ref):
    tm = x_ref.shape[0]

    @pl.when(pl.program_id(0) == 0)
    def _():
        carry_ref[...] = jnp.zeros_like(carry_ref)

    x = x_ref[...]
    xh = x.astype(BF16)
    xl = (x - xh.astype(F32)).astype(BF16)
    wh = wh_ref[...]
    logits = _dot_nt(wh, xh) + _dot_nt(wh, xl) + _dot_nt(wl_ref[...], xh)
    scores = _sigmoid(logits)
    biased = scores + eb_ref[:, 0:1]
    neg = -jnp.inf

    sub_iota = lax.broadcasted_iota(jnp.int32, (GROUP_SIZE, tm), 0).astype(F32)
    gs = []
    for g in range(N_EXPERT_GROUPS):
        blk = biased[g * GROUP_SIZE:(g + 1) * GROUP_SIZE, :]
        m1 = jnp.max(blk, axis=0, keepdims=True)
        a1 = jnp.min(jnp.where(blk == m1, sub_iota, float(GROUP_SIZE)), axis=0, keepdims=True)
        m2 = jnp.max(jnp.where(sub_iota == a1, neg, blk), axis=0, keepdims=True)
        gs.append(m1 + m2)

    masked = []
    for g in range(N_EXPERT_GROUPS):
        beat = jnp.zeros((1, tm), F32)
        for h in range(N_EXPERT_GROUPS):
            if h == g:
                continue
            wins = (gs[h] >= gs[g]) if h < g else (gs[h] > gs[g])
            beat = beat + jnp.where(wins, 1.0, 0.0)
        keep = beat < float(TOPK_GROUPS)
        blk = biased[g * GROUP_SIZE:(g + 1) * GROUP_SIZE, :]
        masked.append(jnp.where(keep, blk, neg))
    masked = jnp.concatenate(masked, axis=0)

    e_iota = lax.broadcasted_iota(jnp.int32, (N_EXPERTS, tm), 0).astype(F32)
    sel = jnp.zeros((N_EXPERTS, tm), F32)
    idxs, ws = [], []
    for _ in range(TOP_K):
        mx = jnp.max(masked, axis=0, keepdims=True)
        idx = jnp.min(jnp.where(masked == mx, e_iota, float(N_EXPERTS)), axis=0, keepdims=True)
        hit = e_iota == idx
        masked = jnp.where(hit, neg, masked)
        sel = jnp.where(hit, 1.0, sel)
        idxs.append(idx)
        ws.append(jnp.sum(jnp.where(hit, scores, 0.0), axis=0, keepdims=True))

    carry = carry_ref[...]
    selb = sel.astype(BF16)
    prefix = _dot(selb, tri_ref[...])
    rank_all = prefix + jnp.concatenate([carry] * (tm // LANES), axis=1)
    total = carry + _dot(selb, jnp.ones((tm, LANES), BF16))
    carry_ref[...] = total
    cnt_ref[...] = total

    wsum = ws[0]
    for j in range(1, TOP_K):
        wsum = wsum + ws[j]
    for j in range(TOP_K):
        eidx_ref[j:j + 1, :] = idxs[j].astype(jnp.int32)
        r = jnp.sum(jnp.where(e_iota == idxs[j], rank_all, 0.0), axis=0, keepdims=True)
        rank_ref[j:j + 1, :] = r.astype(jnp.int32)
        w_ref[j:j + 1, :] = ws[j] / wsum * ROUTED_SCALE


def _route(x1, w_router, e_bias):
    n, d = x1.shape
    tm = _tile(n, 512)
    wt = w_router.astype(F32).T
    wh = wt.astype(BF16)
    wl = (wt - wh.astype(F32)).astype(BF16)
    eb = jnp.broadcast_to(e_bias.astype(F32)[:, None], (N_EXPERTS, LANES))
    tri = jnp.asarray(np.triu(np.ones((tm, tm)), k=1), BF16)
    const = lambda i: (0, 0)
    return pl.pallas_call(
        _route_kernel,
        grid=(n // tm,),
        in_specs=[
            pl.BlockSpec((tm, d), lambda i: (i, 0)),
            pl.BlockSpec((N_EXPERTS, d), const),
            pl.BlockSpec((N_EXPERTS, d), const),
            pl.BlockSpec((N_EXPERTS, LANES), const),
            pl.BlockSpec((tm, tm), const),
        ],
        out_specs=[
            pl.BlockSpec((TOP_K, tm), lambda i: (0, i)),
            pl.BlockSpec((TOP_K, tm), lambda i: (0, i)),
            pl.BlockSpec((TOP_K, tm), lambda i: (0, i)),
            pl.BlockSpec((N_EXPERTS, LANES), const),
        ],
        out_shape=[
            jax.ShapeDtypeStruct((TOP_K, n), jnp.int32),
            jax.ShapeDtypeStruct((TOP_K, n), jnp.int32),
            jax.ShapeDtypeStruct((TOP_K, n), F32),
            jax.ShapeDtypeStruct((N_EXPERTS, LANES), F32),
        ],
        scratch_shapes=[pltpu.VMEM((N_EXPERTS, LANES), F32)],
        compiler_params=_params("arbitrary"),
        name="route",
    )(x1, wh, wl, eb, tri)


def _dest_kernel(tail_ref, first_ref, eidx_ref, rank_ref, ps_ref, dest_ref, xs_ref, zero_ref, sem):
    i = pl.program_id(0)

    def piece_copy(e, q, s):
        row0 = (tail_ref[e] * TOKEN_ROWS + s) * EXPERT_BLOCK + q * ZERO_CHUNK
        row0 = pl.multiple_of(row0, ZERO_CHUNK)
        return pltpu.make_async_copy(zero_ref, xs_ref.at[pl.ds(row0, ZERO_CHUNK), :], sem)

    def each_piece(act):
        def body(e, c):
            def piece(q, c2):
                for s in range(TOKEN_ROWS):
                    act(piece_copy(e, q, s))
                return c2

            @pl.when(tail_ref[e] >= 0)
            def _():
                lax.fori_loop(first_ref[e], EXPERT_BLOCK // ZERO_CHUNK, piece, 0)
            return c
        lax.fori_loop(0, N_EXPERTS, body, 0)

    @pl.when(i == 0)
    def _():
        zero_ref[...] = jnp.zeros_like(zero_ref)
        each_piece(lambda cp: cp.start())

    tm = eidx_ref.shape[1]
    e_iota = lax.broadcasted_iota(jnp.int32, (N_EXPERTS, tm), 0)
    ps = jnp.concatenate([ps_ref[...]] * (tm // LANES), axis=1)
    for j in range(TOP_K):
        hit = e_iota == eidx_ref[j:j + 1, :]
        start = jnp.sum(jnp.where(hit, ps, 0.0), axis=0, keepdims=True)
        dest_ref[j:j + 1, :] = start.astype(jnp.int32) + rank_ref[j:j + 1, :]

    @pl.when(i == pl.num_programs(0) - 1)
    def _():
        each_piece(lambda cp: cp.wait())


def _dest(eidx, rank, pad_start, tail, n_rows):
    n = eidx.shape[1]
    tm = _tile(n, 512)
    ps = jnp.broadcast_to(pad_start.astype(F32)[:, None], (N_EXPERTS, LANES))
    tok = lambda i, *_: (0, i)
    return pl.pallas_call(
        _dest_kernel,
        grid_spec=pltpu.PrefetchScalarGridSpec(
            num_scalar_prefetch=len(tail),
            grid=(n // tm,),
            in_specs=[
                pl.BlockSpec((TOP_K, tm), tok),
                pl.BlockSpec((TOP_K, tm), tok),
                pl.BlockSpec((N_EXPERTS, LANES), lambda i, *_: (0, 0)),
            ],
            out_specs=[pl.BlockSpec((TOP_K, tm), tok), pl.BlockSpec(memory_space=pl.ANY)],
            scratch_shapes=[pltpu.VMEM((ZERO_CHUNK, LANES), jnp.uint32), pltpu.SemaphoreType.DMA],
        ),
        out_shape=[jax.ShapeDtypeStruct((TOP_K, n), jnp.int32),
                   jax.ShapeDtypeStruct((n_rows * TOKEN_ROWS, LANES), jnp.uint32)],
        compiler_params=_params("arbitrary"),
        name="dest",
    )(*tail, eidx, rank, ps)


def _sc_scatter_rows(xs_init, src, idx):
    nr = src.shape[0]
    slots = idx.shape[0]
    mesh = plsc.VectorSubcoreMesh(core_axis_name="core", subcore_axis_name="subcore")

    @functools.partial(pl.kernel, mesh=mesh, scratch_types=[], out_type=())
    def scatter(src_hbm, idx_hbm, xs_hbm):
        def window(src_vmem, *idx_vmems):
            for idx_vmem in idx_vmems:
                pltpu.sync_copy(src_vmem, xs_hbm.at[idx_vmem.at[0]])

        pltpu.emit_pipeline(
            window,
            grid=(nr // SC_WINDOW,),
            in_specs=[pl.BlockSpec((SC_WINDOW, LANES), index_map=lambda i: (i, 0))]
                     + [pl.BlockSpec((1, SC_WINDOW), index_map=lambda i, j=j: (j, i))
                        for j in range(slots)],
            out_specs=[],
            core_axis_name=("core", "subcore"),
            dimension_semantics=(pltpu.PARALLEL,),
        )(src_hbm, *([idx_hbm] * slots))

    xs_ref = jax.new_ref(xs_init)
    scatter(src, idx, xs_ref)
    return xs_ref[...]


def _dispatch(xs, dest, x1w):
    n = dest.shape[1]
    word_rows = _sorted_word_row(dest[:, None, :], jnp.arange(TOKEN_ROWS, dtype=jnp.int32)[None, :, None])
    word_rows = word_rows.reshape(TOP_K, TOKEN_ROWS * n)
    return _sc_scatter_rows(xs, x1w.reshape(TOKEN_ROWS * n, LANES), word_rows)


def _swiglu(xb, w_in, w_down):
    h = _dot(xb, w_in)
    half = h.shape[1] // 2
    g = h[:, :half]
    act = g * _sigmoid(g) * h[:, half:]
    return _dot(act.astype(BF16), w_down)


def _experts_kernel(be_ref, nu_ref, fresh_ref, slot_ref, nxt_ref, x_ref, wi_hbm, wd_hbm, o_ref,
                    wi_buf, wd_buf, wib_ref, wdb_ref, sem):
    i = pl.program_id(0)
    used = i < nu_ref[0]
    blk = o_ref.shape[0] // TOKEN_ROWS

    def fetch(e, s):
        return (pltpu.make_async_copy(wi_hbm.at[e], wi_buf.at[s], sem.at[0, s]),
                pltpu.make_async_copy(wd_hbm.at[e], wd_buf.at[s], sem.at[1, s]))

    @pl.when(jnp.logical_and(used, fresh_ref[i] == 1))
    def _():
        s = slot_ref[i]

        @pl.when(i == 0)
        def _():
            for cp in fetch(be_ref[i], s):
                cp.start()

        for cp in fetch(be_ref[i], s):
            cp.wait()

        @pl.when(nxt_ref[i] >= 0)
        def _():
            for cp in fetch(nxt_ref[i], 1 - s):
                cp.start(priority=1)

        wib_ref[...] = wi_buf[s].astype(BF16)
        wdb_ref[...] = wd_buf[s].astype(BF16)

    @pl.when(used)
    def _():
        hb = blk // 2
        wi, wd = wib_ref[...], wdb_ref[...]
        plane = lambda s, h: pl.ds(s * blk + h * hb, hb)
        xbs = [_unpack_words([x_ref[plane(s, h), :] for s in range(TOKEN_ROWS)]).astype(BF16)
               for h in range(2)]
        hs = [_dot(xb, wi) for xb in xbs]
        half = wi.shape[1] // 2
        acts = [(h[:, :half] * _sigmoid(h[:, :half]) * h[:, half:]).astype(BF16) for h in hs]
        for h, act in enumerate(acts):
            for s, w in enumerate(_pack_words(_dot(act, wd))):
                o_ref[plane(s, h), :] = w


def _experts(sched, xs, w_e_in, w_e_down):
    n_blocks = sched[0].shape[0]
    _, d, h2 = w_e_in.shape
    hdim = w_e_down.shape[1]
    rows = EXPERT_BLOCK * TOKEN_ROWS

    def x_map(i, be, nu, *_):
        return (jnp.minimum(i, nu[0] - 1), 0)

    return pl.pallas_call(
        _experts_kernel,
        grid_spec=pltpu.PrefetchScalarGridSpec(
            num_scalar_prefetch=len(sched),
            grid=(n_blocks,),
            in_specs=[
                pl.BlockSpec((rows, LANES), x_map),
                pl.BlockSpec(memory_space=pl.ANY),
                pl.BlockSpec(memory_space=pl.ANY),
            ],
            out_specs=pl.BlockSpec((rows, LANES), x_map),
            scratch_shapes=[
                pltpu.VMEM((2, d, h2), F32), pltpu.VMEM((2, hdim, d), F32),
                pltpu.VMEM((d, h2), BF16), pltpu.VMEM((hdim, d), BF16),
                pltpu.SemaphoreType.DMA((2, 2)),
            ],
        ),
        out_shape=jax.ShapeDtypeStruct(xs.shape, jnp.uint32),
        input_output_aliases={len(sched): 0},
        compiler_params=_params("arbitrary"),
        name="experts",
    )(*sched, xs, w_e_in, w_e_down)


def _sc_gather_rows(table, idx):
    ni = idx.shape[0]
    mesh = plsc.VectorSubcoreMesh(core_axis_name="core", subcore_axis_name="subcore")

    @functools.partial(pl.kernel, mesh=mesh, scratch_types=[],
                       out_type=jax.ShapeDtypeStruct((ni, LANES), table.dtype))
    def gather(table_hbm, idx_hbm, out_hbm):
        def window(idx_vmem, out_vmem):
            pltpu.sync_copy(table_hbm.at[idx_vmem.at[0]], out_vmem)

        pltpu.emit_pipeline(
            window,
            grid=(ni // SC_WINDOW,),
            in_specs=[pl.BlockSpec((1, SC_WINDOW), index_map=lambda i: (0, i))],
            out_specs=[pl.BlockSpec((SC_WINDOW, LANES), index_map=lambda i: (i, 0))],
            core_axis_name=("core", "subcore"),
            dimension_semantics=(pltpu.PARALLEL,),
        )(idx_hbm, out_hbm)

    return gather(table, idx.reshape(1, ni))


def _shared_kernel(alpha, x1_ref, wsi_ref, wsd_ref, part_ref):
    x1 = x1_ref[...]
    part_ref[...] = alpha * x1 + _swiglu(x1.astype(BF16), wsi_ref[...], wsd_ref[...])


def _shared(x1, w_sh_in, w_sh_down, alpha):
    n, d = x1.shape
    tm = _tile(n, 512)
    const = lambda i: (0, 0)
    return pl.pallas_call(
        functools.partial(_shared_kernel, alpha),
        grid=(n // tm,),
        in_specs=[
            pl.BlockSpec((tm, d), lambda i: (i, 0)),
            pl.BlockSpec(w_sh_in.shape, const),
            pl.BlockSpec(w_sh_down.shape, const),
        ],
        out_specs=pl.BlockSpec((tm, d), lambda i: (i, 0)),
        out_shape=jax.ShapeDtypeStruct((n, d), F32),
        compiler_params=_params("parallel"),
        name="shared",
    )(x1, w_sh_in.astype(BF16), w_sh_down.astype(BF16))


def _finish_kernel(acc_ref, wt_ref, st_ref, lg_ref, lb_ref, out_ref):
    acc = acc_ref[...]
    for j in range(TOP_K):
        words = [st_ref[s, j] for s in range(TOKEN_ROWS)]
        acc = acc + wt_ref[:, j:j + 1] * _unpack_words(words)
    out_ref[...] = _layer_norm(acc, lg_ref[...], lb_ref[...])


def _combine(dest, part, wt, ln_g, ln_b, os):
    n, d = part.shape
    nc = n // COMBINE_CHUNKS
    tf = _tile(nc, 512)
    steps = nc // tf
    const = lambda i: (0, 0)
    out = part
    for c in range(COMBINE_CHUNKS):
        word_rows = _sorted_word_row(dest[None, :, c * nc:(c + 1) * nc],
                                     jnp.arange(TOKEN_ROWS, dtype=jnp.int32)[:, None, None]).reshape(-1)
        staged = _sc_gather_rows(os, word_rows).reshape(TOKEN_ROWS, TOP_K, nc, LANES)
        tile = lambda i, c=c: (c * steps + i, 0)
        out = pl.pallas_call(
            _finish_kernel,
            grid=(steps,),
            in_specs=[
                pl.BlockSpec((tf, d), tile),
                pl.BlockSpec((tf, TOP_K), tile),
                pl.BlockSpec((TOKEN_ROWS, TOP_K, tf, LANES), lambda i: (0, 0, i, 0)),
                pl.BlockSpec((1, d), const),
                pl.BlockSpec((1, d), const),
            ],
            out_specs=pl.BlockSpec((tf, d), tile),
            out_shape=jax.ShapeDtypeStruct((n, d), F32),
            input_output_aliases={0: 0},
            compiler_params=_params("arbitrary"),
            name="finish",
        )(out, wt, staged, ln_g.astype(F32)[None, :], ln_b.astype(F32)[None, :])
    return out


def _block_layout(counts, n_assign):
    n_blocks = (n_assign + N_EXPERTS * (EXPERT_BLOCK - 1) + EXPERT_BLOCK - 1) // EXPERT_BLOCK
    nblk = (counts + EXPERT_BLOCK - 1) // EXPERT_BLOCK
    blk_end = jnp.cumsum(nblk)
    pad_start = (blk_end - nblk) * EXPERT_BLOCK
    blk_e = jnp.sum(blk_end[None, :] <= jnp.arange(n_blocks, dtype=jnp.int32)[:, None], axis=1)
    blk_e = jnp.minimum(blk_e, N_EXPERTS - 1).astype(jnp.int32)
    n_used = blk_end[-1:].astype(jnp.int32)
    tail_blk = jnp.where(nblk > 0, blk_end - 1, -1).astype(jnp.int32)
    tail_rows = counts - (nblk - 1) * EXPERT_BLOCK
    tail_first = (tail_rows // ZERO_CHUNK).astype(jnp.int32)
    tail = (tail_blk, tail_first)
    has = nblk > 0
    e_ids = jnp.arange(N_EXPERTS, dtype=jnp.int32)
    nxt_ge = lax.cummin(jnp.where(has, e_ids, N_EXPERTS)[::-1])[::-1]
    nxt_e = jnp.concatenate([nxt_ge[1:], jnp.full((1,), N_EXPERTS, jnp.int32)])
    nxt_e = jnp.where(nxt_e < N_EXPERTS, nxt_e, -1)
    slot_e = (jnp.cumsum(has.astype(jnp.int32)) - 1) % 2
    blk_ids = jnp.arange(n_blocks, dtype=jnp.int32)
    fresh = (blk_ids == (blk_end - nblk)[blk_e]).astype(jnp.int32)
    sched = (blk_e, n_used, fresh, slot_e[blk_e].astype(jnp.int32), nxt_e[blk_e].astype(jnp.int32))
    return n_blocks, pad_start.astype(jnp.int32), sched, tail


def _layer(x, w_in, b_gate, q_g, k_g, w_four_proj, w_attn_proj, w_o, ln1_g, ln1_b,
           w_router, e_bias, w_e_in, w_e_down, w_sh_in, w_sh_down, ln2_g, ln2_b, alpha):
    batch, seq, d = x.shape
    n = batch * seq
    x2 = x.reshape(n, d)

    u, q4, k4, vt4, gates, score_bound = _inproj(x2, w_in, b_gate, q_g, k_g, batch, seq)
    mf = _fourier(u.reshape(batch, seq, FOURIER_WIDTH), gates.reshape(batch, seq, -1), w_four_proj)
    o = _attention(q4, k4, vt4, score_bound)
    x1, x1w = _mix(o.reshape(n, ATTN_WIDTH), mf.reshape(n, d), gates, x2,
                   w_attn_proj, w_o, ln1_g, ln1_b, alpha)

    eidx, rank, wts, cnt = _route(x1, w_router, e_bias)
    counts = cnt[:, 0].astype(jnp.int32)
    n_blocks, pad_start, sched, tail = _block_layout(counts, n * TOP_K)
    dest, xs = _dest(eidx, rank, pad_start, tail, n_blocks * EXPERT_BLOCK)
    xs = _dispatch(xs, dest, x1w)
    part = _shared(x1, w_sh_in, w_sh_down, alpha)
    xs, part = lax.optimization_barrier((xs, part))
    os = _experts(sched, xs, w_e_in, w_e_down)
    out = _combine(dest, part, wts.T, ln2_g, ln2_b, os)
    return out.reshape(batch, seq, d)


def kernel(x, w_in, b_gate, q_norm_g, k_norm_g, w_four_proj, w_attn_proj, w_o, ln1_g, ln1_b, w_router, e_bias, w_e_in, w_e_down, w_sh_in, w_sh_down, ln2_g, ln2_b):
    depth = w_in.shape[0]
    alpha = (2 * depth) ** 0.25
    for l in range(depth):
        x = _layer(x, w_in[l], b_gate[l], q_norm_g[l], k_norm_g[l], w_four_proj[l],
                   w_attn_proj[l], w_o[l], ln1_g[l], ln1_b[l], w_router[l], e_bias[l],
                   w_e_in[l], w_e_down[l], w_sh_in[l], w_sh_down[l], ln2_g[l], ln2_b[l], alpha)
    return x
```

```python
import functools
import math

import numpy as np
import jax
import jax.numpy as jnp
from jax import lax
from jax.experimental import pallas as pl
from jax.experimental.pallas import tpu as pltpu
from jax.experimental.pallas import tpu_sc as plsc

F32 = jnp.float32
BF16 = jnp.bfloat16

GRID_W = 64
N_FOURIER_GROUPS = 8
FOURIER_GROUP_DIM = 64
FOURIER_WIDTH = N_FOURIER_GROUPS * FOURIER_GROUP_DIM
N_Q_HEADS = 16
N_KV_HEADS = 4
HEAD_DIM = 64
Q_GROUP = N_Q_HEADS // N_KV_HEADS
ATTN_WIDTH = N_Q_HEADS * HEAD_DIM
KV_WIDTH = N_KV_HEADS * HEAD_DIM
ROPE_THETA = 10000.0
QK_EPS = 1e-6
OFF_Q = FOURIER_WIDTH
OFF_K = OFF_Q + ATTN_WIDTH
OFF_V = OFF_K + KV_WIDTH
OFF_G = OFF_V + KV_WIDTH
N_EXPERTS = 256
TOP_K = 8
N_EXPERT_GROUPS = 8
GROUP_SIZE = N_EXPERTS // N_EXPERT_GROUPS
TOPK_GROUPS = 4
ROUTED_SCALE = 2.5
LN_EPS = 1e-5

LANES = 128
MXU_DIM = 256
VMEM_LIMIT = 56 * 1024 * 1024

MAX_EXP2_RANGE = 100.0

ZERO_CHUNK = 128
EXPERT_BLOCK = 1152
SC_WINDOW = 128
COMBINE_CHUNKS = 4
TOKEN_ROWS = 4

NT_DIMS = (((1,), (1,)), ((), ()))


def _dot(a, b):
    return jnp.dot(a, b, preferred_element_type=F32)


def _dot_nt(a, b):
    return lax.dot_general(a, b, NT_DIMS, preferred_element_type=F32)


def _sigmoid(x):
    return 1.0 / (1.0 + jnp.exp(-x))


def _params(*sem):
    return pltpu.CompilerParams(dimension_semantics=sem, vmem_limit_bytes=VMEM_LIMIT)


def _tile(n, pref):
    t = min(n, pref)
    assert n % t == 0, (n, t)
    return t


def _rope_tables(seq):
    lane = np.arange(MXU_DIM)
    d = lane % HEAD_DIM
    sub = d % 32
    j = sub % 16
    t = np.arange(seq)[:, None]
    pos = np.where(d[None, :] < 32, t // GRID_W, t % GRID_W).astype(np.float64)
    freq = ROPE_THETA ** (-(j.astype(np.float64)) / 16.0)
    ang = pos * freq[None, :]
    cos = np.cos(ang)
    sin = np.sin(ang) * np.where(sub < 16, -1.0, 1.0)[None, :]
    return jnp.asarray(cos, F32), jnp.asarray(sin, F32)


def _head_mean_matrix():
    i = np.arange(MXU_DIM)
    m = (i[:, None] // HEAD_DIM == i[None, :] // HEAD_DIM).astype(np.float64) / HEAD_DIM
    return jnp.asarray(m, BF16)


def _dft_tables(seq):
    c = np.arange(FOURIER_GROUP_DIM)
    ang_c = 2.0 * np.pi * ((c[:, None] * c[None, :]) % FOURIER_GROUP_DIM) / FOURIER_GROUP_DIM
    sc = 1.0 / math.sqrt(FOURIER_GROUP_DIM)
    eye = np.eye(MXU_DIM // FOURIER_GROUP_DIM)
    block = np.concatenate([np.kron(eye, np.cos(ang_c) * sc), np.kron(eye, np.sin(ang_c) * sc)], axis=1)
    chan = np.stack([block] * (FOURIER_WIDTH // MXU_DIM))
    s = np.arange(seq)
    ang_s = 2.0 * np.pi * ((s[:, None] * s[None, :]) % seq) / seq
    ssc = 1.0 / math.sqrt(seq)
    seqm = np.concatenate([np.cos(ang_s) * ssc, -np.sin(ang_s) * ssc], axis=1)
    return jnp.asarray(chan, BF16), jnp.asarray(seqm, BF16)


def _norm_rope(z, gain, mean_mat, cos, sin, lo_mask):
    ms = _dot((z * z).astype(BF16), mean_mat)
    y = z * lax.rsqrt(ms + QK_EPS) * gain
    outs = []
    for c in range(MXU_DIM // LANES):
        yc = y[:, c * LANES:(c + 1) * LANES]
        up = pltpu.roll(yc, LANES - 16, 1)
        dn = pltpu.roll(yc, 16, 1)
        partner = jnp.where(lo_mask, up, dn)
        sl = slice(c * LANES, (c + 1) * LANES)
        outs.append(yc * cos[:, sl] + partner * sin[:, sl])
    return jnp.concatenate(outs, axis=1)


def _inproj_kernel(x_ref, w_ref, bg_ref, gq_ref, gk_ref, mm_ref, cos_ref, sin_ref,
                   u_ref, q_ref, k_ref, v_ref, g_ref):
    xb = x_ref[...].astype(BF16)

    lane = lax.broadcasted_iota(jnp.int32, (1, LANES), 1)
    lo_mask = (lane & 16) == 0
    mean_mat = mm_ref[...]
    cos = cos_ref[...]
    sin = sin_ref[...]
    heads = MXU_DIM // HEAD_DIM

    def put_u(z):
        u_ref[...] = z.astype(BF16)

    def put_q(c, z):
        q = _norm_rope(z, gq_ref[...], mean_mat, cos, sin, lo_mask).astype(BF16)
        for j in range(heads):
            q_ref[0, c * heads + j] = q[:, j * HEAD_DIM:(j + 1) * HEAD_DIM]

    def put_k(z):
        k = _norm_rope(z, gk_ref[...], mean_mat, cos, sin, lo_mask).astype(BF16)
        for j in range(N_KV_HEADS):
            k_ref[0, j] = k[:, j * HEAD_DIM:(j + 1) * HEAD_DIM]

    def put_v(z):
        vt = z.T.astype(BF16)
        for j in range(N_KV_HEADS):
            v_ref[0, j] = vt[j * HEAD_DIM:(j + 1) * HEAD_DIM, :]

    def put_g(lo, hi, z):
        g_ref[:, lo:hi] = _sigmoid(z + bg_ref[:, lo:hi]).astype(BF16)

    stages = [((0, OFF_Q), put_u)]
    for c in range(ATTN_WIDTH // MXU_DIM):
        stages.append(((OFF_Q + c * MXU_DIM, OFF_Q + (c + 1) * MXU_DIM), functools.partial(put_q, c)))
    stages.append(((OFF_K, OFF_V), put_k))
    stages.append(((OFF_V, OFF_G), put_v))
    gw = 512
    for lo in range(0, w_ref.shape[1] - OFF_G, gw):
        stages.append(((OFF_G + lo, OFF_G + lo + gw), functools.partial(put_g, lo, lo + gw)))

    z_next = _dot(xb, w_ref[:, stages[0][0][0]:stages[0][0][1]])
    for s, (_, put) in enumerate(stages):
        z = z_next
        if s + 1 < len(stages):
            lo, hi = stages[s + 1][0]
            z_next = _dot(xb, w_ref[:, lo:hi])
        put(z)


def _inproj(x2, w_in, b_gate, q_g, k_g, batch, seq):
    n, d = x2.shape
    tm = _tile(seq, 512)
    spb = seq // tm
    in_width = w_in.shape[1]
    gate_w = in_width - OFF_G
    cos, sin = _rope_tables(seq)
    mean_mat = _head_mean_matrix()
    scale = HEAD_DIM ** -0.5 * math.log2(math.e)
    gq = jnp.tile(q_g.astype(F32) * scale, MXU_DIM // HEAD_DIM)[None, :]
    gk = jnp.tile(k_g.astype(F32), MXU_DIM // HEAD_DIM)[None, :]
    score_bound = (HEAD_DIM * jnp.max(jnp.abs(gq)) * jnp.max(jnp.abs(gk))).reshape(1)
    const = lambda i: (0, 0)
    outs = pl.pallas_call(
        _inproj_kernel,
        grid=(n // tm,),
        in_specs=[
            pl.BlockSpec((tm, d), lambda i: (i, 0)),
            pl.BlockSpec((d, in_width), const),
            pl.BlockSpec((1, gate_w), const),
            pl.BlockSpec((1, MXU_DIM), const),
            pl.BlockSpec((1, MXU_DIM), const),
            pl.BlockSpec((MXU_DIM, MXU_DIM), const),
            pl.BlockSpec((tm, MXU_DIM), lambda i: (i % spb, 0)),
            pl.BlockSpec((tm, MXU_DIM), lambda i: (i % spb, 0)),
        ],
        out_specs=[
            pl.BlockSpec((tm, FOURIER_WIDTH), lambda i: (i, 0)),
            pl.BlockSpec((1, N_Q_HEADS, tm, HEAD_DIM), lambda i: (i // spb, 0, i % spb, 0)),
            pl.BlockSpec((1, N_KV_HEADS, tm, HEAD_DIM), lambda i: (i // spb, 0, i % spb, 0)),
            pl.BlockSpec((1, N_KV_HEADS, HEAD_DIM, tm), lambda i: (i // spb, 0, 0, i % spb)),
            pl.BlockSpec((tm, gate_w), lambda i: (i, 0)),
        ],
        out_shape=[
            jax.ShapeDtypeStruct((n, FOURIER_WIDTH), BF16),
            jax.ShapeDtypeStruct((batch, N_Q_HEADS, seq, HEAD_DIM), BF16),
            jax.ShapeDtypeStruct((batch, N_KV_HEADS, seq, HEAD_DIM), BF16),
            jax.ShapeDtypeStruct((batch, N_KV_HEADS, HEAD_DIM, seq), BF16),
            jax.ShapeDtypeStruct((n, gate_w), BF16),
        ],
        compiler_params=_params("parallel"),
        name="inproj",
    )(x2, w_in.astype(BF16), b_gate.astype(F32)[None, :], gq, gk, mean_mat, cos, sin)
    return (*outs, score_bound)


def _fourier_kernel(u_ref, chan_ref, seqm_ref, wp_ref, g_ref, o_ref, ab_ref):
    seq = u_ref.shape[1]

    @pl.when(pl.program_id(1) == 0)
    def _():
        for h in range(FOURIER_WIDTH // MXU_DIM):
            cols = slice(h * MXU_DIM, (h + 1) * MXU_DIM)
            ab = _dot(u_ref[0, :, cols], chan_ref[h])
            ab_ref[0:seq, cols] = ab[:, 0:MXU_DIM].astype(BF16)
            ab_ref[seq:2 * seq, cols] = ab[:, MXU_DIM:].astype(BF16)

    f = _dot(seqm_ref[...], ab_ref[...]).astype(BF16)
    y = _dot(f, wp_ref[...])
    o_ref[0] = (g_ref[0].astype(F32) * y).astype(BF16)


def _fourier(u3, g3, w_four_proj):
    batch, seq, _ = u3.shape
    d = w_four_proj.shape[1]
    tr = _tile(seq, 512)
    chan, seqm = _dft_tables(seq)
    return pl.pallas_call(
        _fourier_kernel,
        grid=(batch, seq // tr),
        in_specs=[
            pl.BlockSpec((1, seq, FOURIER_WIDTH), lambda b, r: (b, 0, 0)),
            pl.BlockSpec(chan.shape, lambda b, r: (0, 0, 0)),
            pl.BlockSpec((tr, 2 * seq), lambda b, r: (r, 0)),
            pl.BlockSpec((FOURIER_WIDTH, d), lambda b, r: (0, 0)),
            pl.BlockSpec((1, tr, d), lambda b, r: (b, r, 0)),
        ],
        out_specs=pl.BlockSpec((1, tr, d), lambda b, r: (b, r, 0)),
        out_shape=jax.ShapeDtypeStruct((batch, seq, d), BF16),
        scratch_shapes=[pltpu.VMEM((2 * seq, FOURIER_WIDTH), BF16)],
        compiler_params=_params("parallel", "arbitrary"),
        name="fourier",
    )(u3, chan, seqm, w_four_proj.astype(BF16), g3)


def _attention_kernel(bounded, sb_ref, q_ref, k_ref, vt_ref, o_ref, vone_ref):
    seq = k_ref.shape[2]

    @pl.when(pl.program_id(2) == 0)
    def _():
        vone_ref[0:HEAD_DIM, :] = vt_ref[0, 0]
        vone_ref[HEAD_DIM:, :] = jnp.ones((HEAD_DIM, seq), BF16)

    k = k_ref[0, 0]
    vone = vone_ref[...]
    outs = []
    st_next = _dot_nt(k, q_ref[0, 0])
    for g in range(Q_GROUP):
        st = st_next
        if g + 1 < Q_GROUP:
            st_next = _dot_nt(k, q_ref[0, g + 1])
        if bounded:
            m = sb_ref[0]
        else:
            m = jnp.max(st, axis=0, keepdims=True)
        pt = jnp.exp2(st - m).astype(BF16)
        ol = _dot(vone, pt)
        ot = ol[0:HEAD_DIM, :] / ol[HEAD_DIM:HEAD_DIM + 1, :]
        outs.append(ot.T.astype(BF16))
    o_ref[0] = jnp.concatenate(outs, axis=1)


def _attention(q4, k4, vt4, score_bound):
    batch, _, seq, _ = q4.shape
    tq = _tile(seq, 2048)

    def call(bounded):
        return pl.pallas_call(
            functools.partial(_attention_kernel, bounded),
            grid_spec=pltpu.PrefetchScalarGridSpec(
                num_scalar_prefetch=1,
                grid=(batch, N_KV_HEADS, seq // tq),
                in_specs=[
                    pl.BlockSpec((1, Q_GROUP, tq, HEAD_DIM), lambda b, h, i, sb: (b, h, i, 0)),
                    pl.BlockSpec((1, 1, seq, HEAD_DIM), lambda b, h, i, sb: (b, h, 0, 0)),
                    pl.BlockSpec((1, 1, HEAD_DIM, seq), lambda b, h, i, sb: (b, h, 0, 0)),
                ],
                out_specs=pl.BlockSpec((1, tq, Q_GROUP * HEAD_DIM), lambda b, h, i, sb: (b, i, h)),
                scratch_shapes=[pltpu.VMEM((2 * HEAD_DIM, seq), BF16)],
            ),
            out_shape=jax.ShapeDtypeStruct((batch, seq, ATTN_WIDTH), BF16),
            compiler_params=_params("parallel", "parallel", "arbitrary"),
            name="attention_bounded" if bounded else "attention",
        )(score_bound, q4, k4, vt4)

    return lax.cond(2.0 * score_bound[0] <= MAX_EXP2_RANGE,
                    lambda: call(True), lambda: call(False))


def _layer_norm(h, g, b):
    mu = jnp.mean(h, axis=-1, keepdims=True)
    c = h - mu
    var = jnp.mean(c * c, axis=-1, keepdims=True)
    return c * lax.rsqrt(var + LN_EPS) * g + b


def _sorted_word_row(dest, s):
    return (dest // EXPERT_BLOCK * TOKEN_ROWS + s) * EXPERT_BLOCK + dest % EXPERT_BLOCK


def _pack_words(val):
    half = val.shape[1] // 2
    assert half == TOKEN_ROWS * LANES
    bits = lax.bitcast_convert_type(val.astype(BF16).astype(F32), jnp.uint32)
    words = (bits[:, :half] >> 16) | bits[:, half:]
    return [words[:, s * LANES:(s + 1) * LANES] for s in range(TOKEN_ROWS)]


def _unpack_words(words):
    lo = [lax.bitcast_convert_type(w << 16, F32) for w in words]
    hi = [lax.bitcast_convert_type(w & jnp.uint32(0xFFFF0000), F32) for w in words]
    return jnp.concatenate(lo + hi, axis=1)


def _mix_kernel(alpha, o_ref, mf_ref, g_ref, x_ref, wap_ref, wo_ref, lg_ref, lb_ref,
                x1_ref, x1w_ref):
    hm = o_ref.shape[0] // 2
    rows = [pl.ds(h * hm, hm) for h in range(2)]
    ys = [_dot(o_ref[r, :], wap_ref[...]) for r in rows]
    merged = [(mf_ref[r, :].astype(F32) + g_ref[r, :].astype(F32) * y).astype(BF16)
              for r, y in zip(rows, ys)]
    mixes = [_dot(m, wo_ref[...]) for m in merged]
    for r, mix in zip(rows, mixes):
        x1 = _layer_norm(alpha * x_ref[r, :] + mix, lg_ref[...], lb_ref[...])
        x1_ref[r, :] = x1
        for s, w in enumerate(_pack_words(x1)):
            x1w_ref[s, r, :] = w


def _mix(o2, mf2, g2, x2, w_attn_proj, w_o, ln_g, ln_b, alpha):
    n, d = x2.shape
    assert d == 2 * TOKEN_ROWS * LANES
    tm = _tile(n, 512)
    const = lambda i: (0, 0)
    return pl.pallas_call(
        functools.partial(_mix_kernel, alpha),
        grid=(n // tm,),
        in_specs=[
            pl.BlockSpec((tm, ATTN_WIDTH), lambda i: (i, 0)),
            pl.BlockSpec((tm, d), lambda i: (i, 0)),
            pl.BlockSpec((tm, d), lambda i: (i, 1)),
            pl.BlockSpec((tm, d), lambda i: (i, 0)),
            pl.BlockSpec((ATTN_WIDTH, d), const),
            pl.BlockSpec((d, d), const),
            pl.BlockSpec((1, d), const),
            pl.BlockSpec((1, d), const),
        ],
        out_specs=[
            pl.BlockSpec((tm, d), lambda i: (i, 0)),
            pl.BlockSpec((TOKEN_ROWS, tm, LANES), lambda i: (0, i, 0)),
        ],
        out_shape=[
            jax.ShapeDtypeStruct((n, d), F32),
            jax.ShapeDtypeStruct((TOKEN_ROWS, n, LANES), jnp.uint32),
        ],
        compiler_params=_params("parallel"),
        name="mix",
    )(o2, mf2, g2, x2, w_attn_proj.astype(BF16), w_o.astype(BF16),
      ln_g.astype(F32)[None, :], ln_b.astype(F32)[None, :])


def _route_kernel(x_ref, wh_ref, wl_ref, eb_ref, tri_ref,
                  eidx_ref, rank_ref, w_ref, cnt_ref, carry_ref):
    tm = x_ref.shape[0]

    @pl.when(pl.program_id(0) == 0)
    def _():
        carry_ref[...] = jnp.zeros_like(carry_ref)

    x = x_ref[...]
    xh = x.astype(BF16)
    xl = (x - xh.astype(F32)).astype(BF16)
    wh = wh_ref[...]
    logits = _dot_nt(wh, xh) + _dot_nt(wh, xl) + _dot_nt(wl_ref[...], xh)
    scores = _sigmoid(logits)
    biased = scores + eb_ref[:, 0:1]
    neg = -jnp.inf

    sub_iota = lax.broadcasted_iota(jnp.int32, (GROUP_SIZE, tm), 0).astype(F32)
    gs = []
    for g in range(N_EXPERT_GROUPS):
        blk = biased[g * GROUP_SIZE:(g + 1) * GROUP_SIZE, :]
        m1 = jnp.max(blk, axis=0, keepdims=True)
        a1 = jnp.min(jnp.where(blk == m1, sub_iota, float(GROUP_SIZE)), axis=0, keepdims=True)
        m2 = jnp.max(jnp.where(sub_iota == a1, neg, blk), axis=0, keepdims=True)
        gs.append(m1 + m2)

    masked = []
    for g in range(N_EXPERT_GROUPS):
        beat = jnp.zeros((1, tm), F32)
        for h in range(N_EXPERT_GROUPS):
            if h == g:
                continue
            wins = (gs[h] >= gs[g]) if h < g else (gs[h] > gs[g])
            beat = beat + jnp.where(wins, 1.0, 0.0)
        keep = beat < float(TOPK_GROUPS)
        blk = biased[g * GROUP_SIZE:(g + 1) * GROUP_SIZE, :]
        masked.append(jnp.where(keep, blk, neg))
    masked = jnp.concatenate(masked, axis=0)

    e_iota = lax.broadcasted_iota(jnp.int32, (N_EXPERTS, tm), 0).astype(F32)
    sel = jnp.zeros((N_EXPERTS, tm), F32)
    idxs, ws = [], []
    for _ in range(TOP_K):
        mx = jnp.max(masked, axis=0, keepdims=True)
        idx = jnp.min(jnp.where(masked == mx, e_iota, float(N_EXPERTS)), axis=0, keepdims=True)
        hit = e_iota == idx
        masked = jnp.where(hit, neg, masked)
        sel = jnp.where(hit, 1.0, sel)
        idxs.append(idx)
        ws.append(jnp.sum(jnp.where(hit, scores, 0.0), axis=0, keepdims=True))

    carry = carry_ref[...]
    selb = sel.astype(BF16)
    prefix = _dot(selb, tri_ref[...])
    rank_all = prefix + jnp.concatenate([carry] * (tm // LANES), axis=1)
    total = carry + _dot(selb, jnp.ones((tm, LANES), BF16))
    carry_ref[...] = total
    cnt_ref[...] = total

    wsum = ws[0]
    for j in range(1, TOP_K):
        wsum = wsum + ws[j]
    for j in range(TOP_K):
        eidx_ref[j:j + 1, :] = idxs[j].astype(jnp.int32)
        r = jnp.sum(jnp.where(e_iota == idxs[j], rank_all, 0.0), axis=0, keepdims=True)
        rank_ref[j:j + 1, :] = r.astype(jnp.int32)
        w_ref[j:j + 1, :] = ws[j] / wsum * ROUTED_SCALE


def _route(x1, w_router, e_bias):
    n, d = x1.shape
    tm = _tile(n, 512)
    wt = w_router.astype(F32).T
    wh = wt.astype(BF16)
    wl = (wt - wh.astype(F32)).astype(BF16)
    eb = jnp.broadcast_to(e_bias.astype(F32)[:, None], (N_EXPERTS, LANES))
    tri = jnp.asarray(np.triu(np.ones((tm, tm)), k=1), BF16)
    const = lambda i: (0, 0)
    return pl.pallas_call(
        _route_kernel,
        grid=(n // tm,),
        in_specs=[
            pl.BlockSpec((tm, d), lambda i: (i, 0)),
            pl.BlockSpec((N_EXPERTS, d), const),
            pl.BlockSpec((N_EXPERTS, d), const),
            pl.BlockSpec((N_EXPERTS, LANES), const),
            pl.BlockSpec((tm, tm), const),
        ],
        out_specs=[
            pl.BlockSpec((TOP_K, tm), lambda i: (0, i)),
            pl.BlockSpec((TOP_K, tm), lambda i: (0, i)),
            pl.BlockSpec((TOP_K, tm), lambda i: (0, i)),
            pl.BlockSpec((N_EXPERTS, LANES), const),
        ],
        out_shape=[
            jax.ShapeDtypeStruct((TOP_K, n), jnp.int32),
            jax.ShapeDtypeStruct((TOP_K, n), jnp.int32),
            jax.ShapeDtypeStruct((TOP_K, n), F32),
            jax.ShapeDtypeStruct((N_EXPERTS, LANES), F32),
        ],
        scratch_shapes=[pltpu.VMEM((N_EXPERTS, LANES), F32)],
        compiler_params=_params("arbitrary"),
        name="route",
    )(x1, wh, wl, eb, tri)


def _dest_kernel(eidx_ref, rank_ref, ps_ref, dest_ref):
    tm = eidx_ref.shape[1]
    e_iota = lax.broadcasted_iota(jnp.int32, (N_EXPERTS, tm), 0)
    ps = jnp.concatenate([ps_ref[...]] * (tm // LANES), axis=1)
    for j in range(TOP_K):
        hit = e_iota == eidx_ref[j:j + 1, :]
        start = jnp.sum(jnp.where(hit, ps, 0.0), axis=0, keepdims=True)
        dest_ref[j:j + 1, :] = start.astype(jnp.int32) + rank_ref[j:j + 1, :]


def _dest(eidx, rank, pad_start):
    n = eidx.shape[1]
    tm = _tile(n, 512)
    ps = jnp.broadcast_to(pad_start.astype(F32)[:, None], (N_EXPERTS, LANES))
    return pl.pallas_call(
        _dest_kernel,
        grid=(n // tm,),
        in_specs=[
            pl.BlockSpec((TOP_K, tm), lambda i: (0, i)),
            pl.BlockSpec((TOP_K, tm), lambda i: (0, i)),
            pl.BlockSpec((N_EXPERTS, LANES), lambda i: (0, 0)),
        ],
        out_specs=pl.BlockSpec((TOP_K, tm), lambda i: (0, i)),
        out_shape=jax.ShapeDtypeStruct((TOP_K, n), jnp.int32),
        compiler_params=_params("parallel"),
        name="dest",
    )(eidx, rank, ps)


def _tails_kernel(tail_ref, first_ref, xs_ref, zero_ref, sem):
    zero_ref[...] = jnp.zeros_like(zero_ref)

    def piece_copy(e, q, s):
        row0 = (tail_ref[e] * TOKEN_ROWS + s) * EXPERT_BLOCK + q * ZERO_CHUNK
        row0 = pl.multiple_of(row0, ZERO_CHUNK)
        return pltpu.make_async_copy(zero_ref, xs_ref.at[pl.ds(row0, ZERO_CHUNK), :], sem)

    def each_piece(act):
        def body(e, c):
            for q in range(EXPERT_BLOCK // ZERO_CHUNK):
                @pl.when(jnp.logical_and(tail_ref[e] >= 0, q >= first_ref[e]))
                def _():
                    for s in range(TOKEN_ROWS):
                        act(piece_copy(e, q, s))
            return c
        lax.fori_loop(0, N_EXPERTS, body, 0)

    each_piece(lambda cp: cp.start())
    each_piece(lambda cp: cp.wait())


def _sc_scatter_rows(xs_init, src, idx):
    nr = src.shape[0]
    slots = idx.shape[0]
    mesh = plsc.VectorSubcoreMesh(core_axis_name="core", subcore_axis_name="subcore")

    @functools.partial(pl.kernel, mesh=mesh, scratch_types=[], out_type=())
    def scatter(src_hbm, idx_hbm, xs_hbm):
        def window(src_vmem, *idx_vmems):
            for idx_vmem in idx_vmems:
                pltpu.sync_copy(src_vmem, xs_hbm.at[idx_vmem.at[0]])

        pltpu.emit_pipeline(
            window,
            grid=(nr // SC_WINDOW,),
            in_specs=[pl.BlockSpec((SC_WINDOW, LANES), index_map=lambda i: (i, 0))]
                     + [pl.BlockSpec((1, SC_WINDOW), index_map=lambda i, j=j: (j, i))
                        for j in range(slots)],
            out_specs=[],
            core_axis_name=("core", "subcore"),
            dimension_semantics=(pltpu.PARALLEL,),
        )(src_hbm, *([idx_hbm] * slots))

    xs_ref = jax.new_ref(xs_init)
    scatter(src, idx, xs_ref)
    return xs_ref[...]


def _dispatch(tail, dest, x1w, n_rows):
    n = dest.shape[1]
    xs = pl.pallas_call(
        _tails_kernel,
        grid_spec=pltpu.PrefetchScalarGridSpec(
            num_scalar_prefetch=len(tail),
            grid=(1,),
            in_specs=[],
            out_specs=pl.BlockSpec(memory_space=pl.ANY),
            scratch_shapes=[pltpu.VMEM((ZERO_CHUNK, LANES), jnp.uint32), pltpu.SemaphoreType.DMA],
        ),
        out_shape=jax.ShapeDtypeStruct((n_rows * TOKEN_ROWS, LANES), jnp.uint32),
        compiler_params=_params("arbitrary"),
        name="tails",
    )(*tail)
    word_rows = _sorted_word_row(dest[:, None, :], jnp.arange(TOKEN_ROWS, dtype=jnp.int32)[None, :, None])
    word_rows = word_rows.reshape(TOP_K, TOKEN_ROWS * n)
    return _sc_scatter_rows(xs, x1w.reshape(TOKEN_ROWS * n, LANES), word_rows)


def _swiglu(xb, w_in, w_down):
    h = _dot(xb, w_in)
    half = h.shape[1] // 2
    g = h[:, :half]
    act = g * _sigmoid(g) * h[:, half:]
    return _dot(act.astype(BF16), w_down)


def _experts_kernel(be_ref, nu_ref, fresh_ref, slot_ref, nxt_ref, x_ref, wi_hbm, wd_hbm, o_ref,
                    wi_buf, wd_buf, wib_ref, wdb_ref, sem):
    i = pl.program_id(0)
    used = i < nu_ref[0]
    blk = o_ref.shape[0] // TOKEN_ROWS

    def fetch(e, s):
        return (pltpu.make_async_copy(wi_hbm.at[e], wi_buf.at[s], sem.at[0, s]),
                pltpu.make_async_copy(wd_hbm.at[e], wd_buf.at[s], sem.at[1, s]))

    @pl.when(jnp.logical_and(used, fresh_ref[i] == 1))
    def _():
        s = slot_ref[i]

        @pl.when(i == 0)
        def _():
            for cp in fetch(be_ref[i], s):
                cp.start()

        for cp in fetch(be_ref[i], s):
            cp.wait()

        @pl.when(nxt_ref[i] >= 0)
        def _():
            for cp in fetch(nxt_ref[i], 1 - s):
                cp.start(priority=1)

        wib_ref[...] = wi_buf[s].astype(BF16)
        wdb_ref[...] = wd_buf[s].astype(BF16)

    @pl.when(used)
    def _():
        hb = blk // 2
        wi, wd = wib_ref[...], wdb_ref[...]
        plane = lambda s, h: pl.ds(s * blk + h * hb, hb)
        xbs = [_unpack_words([x_ref[plane(s, h), :] for s in range(TOKEN_ROWS)]).astype(BF16)
               for h in range(2)]
        hs = [_dot(xb, wi) for xb in xbs]
        half = wi.shape[1] // 2
        acts = [(h[:, :half] * _sigmoid(h[:, :half]) * h[:, half:]).astype(BF16) for h in hs]
        for h, act in enumerate(acts):
            for s, w in enumerate(_pack_words(_dot(act, wd))):
                o_ref[plane(s, h), :] = w


def _experts(sched, xs, w_e_in, w_e_down):
    n_blocks = sched[0].shape[0]
    _, d, h2 = w_e_in.shape
    hdim = w_e_down.shape[1]
    rows = EXPERT_BLOCK * TOKEN_ROWS

    def x_map(i, be, nu, *_):
        return (jnp.minimum(i, nu[0] - 1), 0)

    return pl.pallas_call(
        _experts_kernel,
        grid_spec=pltpu.PrefetchScalarGridSpec(
            num_scalar_prefetch=len(sched),
            grid=(n_blocks,),
            in_specs=[
                pl.BlockSpec((rows, LANES), x_map),
                pl.BlockSpec(memory_space=pl.ANY),
                pl.BlockSpec(memory_space=pl.ANY),
            ],
            out_specs=pl.BlockSpec((rows, LANES), x_map),
            scratch_shapes=[
                pltpu.VMEM((2, d, h2), F32), pltpu.VMEM((2, hdim, d), F32),
                pltpu.VMEM((d, h2), BF16), pltpu.VMEM((hdim, d), BF16),
                pltpu.SemaphoreType.DMA((2, 2)),
            ],
        ),
        out_shape=jax.ShapeDtypeStruct(xs.shape, jnp.uint32),
        input_output_aliases={len(sched): 0},
        compiler_params=_params("arbitrary"),
        name="experts",
    )(*sched, xs, w_e_in, w_e_down)


def _sc_gather_rows(table, idx):
    ni = idx.shape[0]
    mesh = plsc.VectorSubcoreMesh(core_axis_name="core", subcore_axis_name="subcore")

    @functools.partial(pl.kernel, mesh=mesh, scratch_types=[],
                       out_type=jax.ShapeDtypeStruct((ni, LANES), table.dtype))
    def gather(table_hbm, idx_hbm, out_hbm):
        def window(idx_vmem, out_vmem):
            pltpu.sync_copy(table_hbm.at[idx_vmem.at[0]], out_vmem)

        pltpu.emit_pipeline(
            window,
            grid=(ni // SC_WINDOW,),
            in_specs=[pl.BlockSpec((1, SC_WINDOW), index_map=lambda i: (0, i))],
            out_specs=[pl.BlockSpec((SC_WINDOW, LANES), index_map=lambda i: (i, 0))],
            core_axis_name=("core", "subcore"),
            dimension_semantics=(pltpu.PARALLEL,),
        )(idx_hbm, out_hbm)

    return gather(table, idx.reshape(1, ni))


def _shared_kernel(alpha, x1_ref, wsi_ref, wsd_ref, part_ref):
    x1 = x1_ref[...]
    part_ref[...] = alpha * x1 + _swiglu(x1.astype(BF16), wsi_ref[...], wsd_ref[...])


def _shared(x1, w_sh_in, w_sh_down, alpha):
    n, d = x1.shape
    tm = _tile(n, 512)
    const = lambda i: (0, 0)
    return pl.pallas_call(
        functools.partial(_shared_kernel, alpha),
        grid=(n // tm,),
        in_specs=[
            pl.BlockSpec((tm, d), lambda i: (i, 0)),
            pl.BlockSpec(w_sh_in.shape, const),
            pl.BlockSpec(w_sh_down.shape, const),
        ],
        out_specs=pl.BlockSpec((tm, d), lambda i: (i, 0)),
        out_shape=jax.ShapeDtypeStruct((n, d), F32),
        compiler_params=_params("parallel"),
        name="shared",
    )(x1, w_sh_in.astype(BF16), w_sh_down.astype(BF16))


def _finish_kernel(acc_ref, wt_ref, st_ref, lg_ref, lb_ref, out_ref):
    acc = acc_ref[...]
    for j in range(TOP_K):
        words = [st_ref[s, j] for s in range(TOKEN_ROWS)]
        acc = acc + wt_ref[:, j:j + 1] * _unpack_words(words)
    out_ref[...] = _layer_norm(acc, lg_ref[...], lb_ref[...])


def _combine(dest, part, wt, ln_g, ln_b, os):
    n, d = part.shape
    nc = n // COMBINE_CHUNKS
    tf = _tile(nc, 512)
    steps = nc // tf
    const = lambda i: (0, 0)
    out = part
    for c in range(COMBINE_CHUNKS):
        word_rows = _sorted_word_row(dest[None, :, c * nc:(c + 1) * nc],
                                     jnp.arange(TOKEN_ROWS, dtype=jnp.int32)[:, None, None]).reshape(-1)
        staged = _sc_gather_rows(os, word_rows).reshape(TOKEN_ROWS, TOP_K, nc, LANES)
        tile = lambda i, c=c: (c * steps + i, 0)
        out = pl.pallas_call(
            _finish_kernel,
            grid=(steps,),
            in_specs=[
                pl.BlockSpec((tf, d), tile),
                pl.BlockSpec((tf, TOP_K), tile),
                pl.BlockSpec((TOKEN_ROWS, TOP_K, tf, LANES), lambda i: (0, 0, i, 0)),
                pl.BlockSpec((1, d), const),
                pl.BlockSpec((1, d), const),
            ],
            out_specs=pl.BlockSpec((tf, d), tile),
            out_shape=jax.ShapeDtypeStruct((n, d), F32),
            input_output_aliases={0: 0},
            compiler_params=_params("arbitrary"),
            name="finish",
        )(out, wt, staged, ln_g.astype(F32)[None, :], ln_b.astype(F32)[None, :])
    return out


def _block_layout(counts, n_assign):
    n_blocks = (n_assign + N_EXPERTS * (EXPERT_BLOCK - 1) + EXPERT_BLOCK - 1) // EXPERT_BLOCK
    nblk = (counts + EXPERT_BLOCK - 1) // EXPERT_BLOCK
    blk_end = jnp.cumsum(nblk)
    pad_start = (blk_end - nblk) * EXPERT_BLOCK
    blk_e = jnp.sum(blk_end[None, :] <= jnp.arange(n_blocks, dtype=jnp.int32)[:, None], axis=1)
    blk_e = jnp.minimum(blk_e, N_EXPERTS - 1).astype(jnp.int32)
    n_used = blk_end[-1:].astype(jnp.int32)
    tail_blk = jnp.where(nblk > 0, blk_end - 1, -1).astype(jnp.int32)
    tail_rows = counts - (nblk - 1) * EXPERT_BLOCK
    tail_first = (tail_rows // ZERO_CHUNK).astype(jnp.int32)
    tail = (tail_blk, tail_first)
    has = nblk > 0
    e_ids = jnp.arange(N_EXPERTS, dtype=jnp.int32)
    nxt_ge = lax.cummin(jnp.where(has, e_ids, N_EXPERTS)[::-1])[::-1]
    nxt_e = jnp.concatenate([nxt_ge[1:], jnp.full((1,), N_EXPERTS, jnp.int32)])
    nxt_e = jnp.where(nxt_e < N_EXPERTS, nxt_e, -1)
    slot_e = (jnp.cumsum(has.astype(jnp.int32)) - 1) % 2
    blk_ids = jnp.arange(n_blocks, dtype=jnp.int32)
    fresh = (blk_ids == (blk_end - nblk)[blk_e]).astype(jnp.int32)
    sched = (blk_e, n_used, fresh, slot_e[blk_e].astype(jnp.int32), nxt_e[blk_e].astype(jnp.int32))
    return n_blocks, pad_start.astype(jnp.int32), sched, tail


def _layer(x, w_in, b_gate, q_g, k_g, w_four_proj, w_attn_proj, w_o, ln1_g, ln1_b,
           w_router, e_bias, w_e_in, w_e_down, w_sh_in, w_sh_down, ln2_g, ln2_b, alpha):
    batch, seq, d = x.shape
    n = batch * seq
    x2 = x.reshape(n, d)

    u, q4, k4, vt4, gates, score_bound = _inproj(x2, w_in, b_gate, q_g, k_g, batch, seq)
    mf = _fourier(u.reshape(batch, seq, FOURIER_WIDTH), gates.reshape(batch, seq, -1), w_four_proj)
    o = _attention(q4, k4, vt4, score_bound)
    x1, x1w = _mix(o.reshape(n, ATTN_WIDTH), mf.reshape(n, d), gates, x2,
                   w_attn_proj, w_o, ln1_g, ln1_b, alpha)

    eidx, rank, wts, cnt = _route(x1, w_router, e_bias)
    counts = cnt[:, 0].astype(jnp.int32)
    n_blocks, pad_start, sched, tail = _block_layout(counts, n * TOP_K)
    dest = _dest(eidx, rank, pad_start)

    xs = _dispatch(tail, dest, x1w, n_blocks * EXPERT_BLOCK)
    part = _shared(x1, w_sh_in, w_sh_down, alpha)
    xs, part = lax.optimization_barrier((xs, part))
    os = _experts(sched, xs, w_e_in, w_e_down)
    out = _combine(dest, part, wts.T, ln2_g, ln2_b, os)
    return out.reshape(batch, seq, d)


def kernel(x, w_in, b_gate, q_norm_g, k_norm_g, w_four_proj, w_attn_proj, w_o, ln1_g, ln1_b, w_router, e_bias, w_e_in, w_e_down, w_sh_in, w_sh_down, ln2_g, ln2_b):
    depth = w_in.shape[0]
    alpha = (2 * depth) ** 0.25
    for l in range(depth):
        x = _layer(x, w_in[l], b_gate[l], q_norm_g[l], k_norm_g[l], w_four_proj[l],
                   w_attn_proj[l], w_o[l], ln1_g[l], ln1_b[l], w_router[l], e_bias[l],
                   w_e_in[l], w_e_down[l], w_sh_in[l], w_sh_down[l], ln2_g[l], ln2_b[l], alpha)
    return x
```
